```python
import jax, jax.numpy as jnp
from jax import lax
import numpy as np

D_MODEL = 1024
BATCH = 16
SEQ = 2048
DEPTH = 1
DEC_BATCH = 128
DEC_SEQ = 1
PAST_LEN = 8192
PAGE_SIZE = 128

N_HEADS = 8
HEAD_DIM = 64
N_KV_HEADS = 2
ROT_DIM = HEAD_DIM // 4
ROPE_THETA = 500000.0
IDX_HEADS = 4
IDX_DIM = 64
TOPK_MAX = 256
Q_BLOCK = 128
GM_WIDTH = D_MODEL // 2
GM_GROUPS = 8
GM_GROUP_DIM = GM_WIDTH // GM_GROUPS
CHUNK = 128
N_EXPERT_GROUPS = 4
EXPERTS_PER_GROUP = 8
N_EXPERTS = N_EXPERT_GROUPS * EXPERTS_PER_GROUP
EXPERT_TOP_K = 2
D_EXPERT = D_MODEL // 4
EPS = 1e-6
ATTN_WIDTH = N_HEADS * HEAD_DIM
KV_WIDTH = N_KV_HEADS * HEAD_DIM
IN_SIZES = (ATTN_WIDTH, KV_WIDTH, KV_WIDTH, IDX_HEADS * IDX_DIM, IDX_DIM, IDX_HEADS, GM_WIDTH, GM_WIDTH, 2 * D_MODEL)
D_IN = sum(IN_SIZES)
N_PAGES = PAST_LEN // PAGE_SIZE
N_POOL_PAGES = (DEC_BATCH * N_PAGES * 5) // 4

kernel_name = "hybrid_dsa_gmlp_hmoe_step"


def rms_norm(x, g):
    xf = x.astype(jnp.float32)
    y = xf * lax.rsqrt(jnp.mean(xf * xf, axis=-1, keepdims=True) + EPS)
    return (y * g.astype(jnp.float32)).astype(x.dtype)


def layer_norm(x, g, b):
    xf = x.astype(jnp.float32)
    mu = jnp.mean(xf, axis=-1, keepdims=True)
    var = jnp.mean(jnp.square(xf - mu), axis=-1, keepdims=True)
    return ((xf - mu) * lax.rsqrt(var + EPS) * g.astype(jnp.float32) + b.astype(jnp.float32)).astype(x.dtype)


def partial_rope(x, pos):
    half = ROT_DIM // 2
    inv_freq = ROPE_THETA ** (-jnp.arange(half, dtype=jnp.float32) / half)
    ang = pos.astype(jnp.float32)[:, None] * inv_freq[None, :]
    cos = jnp.cos(ang)[:, None, :]
    sin = jnp.sin(ang)[:, None, :]
    xf = x.astype(jnp.float32)
    x1, x2 = xf[..., :half], xf[..., half:ROT_DIM]
    out = jnp.concatenate([x1 * cos - x2 * sin, x2 * cos + x1 * sin, xf[..., ROT_DIM:]], axis=-1)
    return out.astype(x.dtype)


def adaln_terms(c, w_ada, b_ada):
    mod = jax.nn.silu(c) @ w_ada + b_ada
    return [m[:, None, :] for m in jnp.split(mod, 6, axis=-1)]


def modulate(x, g, shift, scale):
    return rms_norm(x, g) * (1.0 + scale) + shift


def project_inputs(h, pos, w_in, q_norm_g, k_norm_g, kidx_norm_g, gm_ln_g, gm_ln_b):
    B, S, _ = h.shape
    offsets = np.cumsum(IN_SIZES)[:-1].tolist()
    q, k, v, qi, ki, wi, u, vg, gates = jnp.split(h @ w_in, offsets, axis=-1)
    q = partial_rope(rms_norm(q.reshape(B, S, N_HEADS, HEAD_DIM), q_norm_g), pos)
    k = partial_rope(rms_norm(k.reshape(B, S, N_KV_HEADS, HEAD_DIM), k_norm_g), pos)
    v = v.reshape(B, S, N_KV_HEADS, HEAD_DIM)
    qi = partial_rope(qi.reshape(B, S, IDX_HEADS, IDX_DIM), pos)
    ki = partial_rope(rms_norm(ki, kidx_norm_g)[:, :, None, :], pos)[:, :, 0, :]
    u = jax.nn.gelu(u)
    vg = layer_norm(jax.nn.gelu(vg), gm_ln_g, gm_ln_b)
    gate_attn, gate_gmlp = jnp.split(jax.nn.sigmoid(gates), 2, axis=-1)
    return q, k, v, qi, ki, wi, u, vg, gate_attn, gate_gmlp


def index_scores(qi, wi, ki):
    dots = jnp.einsum('bqhd,bld->bqhl', qi, ki).astype(jnp.float32) * (IDX_DIM ** -0.5)
    return jnp.einsum('bqhl,bqh->bql', jax.nn.relu(dots), wi.astype(jnp.float32)) * (IDX_HEADS ** -0.5)


def select_keys(scores, q_pos, topk):
    L = scores.shape[-1]
    admissible = jnp.arange(L)[None, None, :] <= q_pos[None, :, None]
    _, idx = lax.top_k(jnp.where(admissible, scores, -jnp.inf), topk)
    valid = idx <= q_pos[None, :, None]
    return idx, valid


def sparse_attend(q, k_sel, v_sel, valid):
    B, Q = q.shape[:2]
    qg = q.reshape(B, Q, N_KV_HEADS, N_HEADS // N_KV_HEADS, HEAD_DIM)
    s = jnp.einsum('bqhgd,bqkhd->bqhgk', qg, k_sel).astype(jnp.float32) * (HEAD_DIM ** -0.5)
    s = jnp.where(valid[:, :, None, None, :], s, -jnp.inf)
    p = jax.nn.softmax(s, axis=-1).astype(v_sel.dtype)
    o = jnp.einsum('bqhgk,bqkhd->bqhgd', p, v_sel)
    return o.reshape(B, Q, ATTN_WIDTH)


def prompt_sparse_attention(q, k, v, qi, wi, ki):
    B, S = q.shape[:2]
    nb = S // Q_BLOCK
    topk = min(TOPK_MAX, S // 4)
    bidx = jnp.arange(B)[:, None, None]

    def to_blocks(a):
        return jnp.moveaxis(a.reshape((B, nb, Q_BLOCK) + a.shape[2:]), 1, 0)

    def block(args):
        q_b, qi_b, wi_b, blk = args
        q_pos = blk * Q_BLOCK + jnp.arange(Q_BLOCK)
        idx, valid = select_keys(index_scores(qi_b, wi_b, ki), q_pos, topk)
        return sparse_attend(q_b, k[bidx, idx], v[bidx, idx], valid)

    out = lax.map(block, (to_blocks(q), to_blocks(qi), to_blocks(wi), jnp.arange(nb)))
    return jnp.moveaxis(out, 0, 1).reshape(B, S, ATTN_WIDTH)


def sample_sparse_attention(q, k_new, v_new, qi, wi, ki_new, cache_k, cache_v, cache_kidx, page_table, layer):
    DB, DS = q.shape[:2]
    L = PAST_LEN + DS
    topk = min(TOPK_MAX, L // 4)
    q_pos = PAST_LEN + jnp.arange(DS)
    ki_past = cache_kidx[layer, page_table].reshape(DB, PAST_LEN, IDX_DIM)
    ki_all = jnp.concatenate([ki_past, ki_new.astype(ki_past.dtype)], axis=1)
    idx, valid = select_keys(index_scores(qi, wi, ki_all), q_pos, topk)
    page = jnp.minimum(idx // PAGE_SIZE, N_PAGES - 1)
    phys = jnp.take_along_axis(page_table, page.reshape(DB, -1), axis=1).reshape(idx.shape)
    off = idx % PAGE_SIZE
    new_i = jnp.clip(idx - PAST_LEN, 0, DS - 1)
    is_new = (idx >= PAST_LEN)[..., None, None]
    bidx = jnp.arange(DB)[:, None, None]
    k_sel = jnp.where(is_new, k_new[bidx, new_i], cache_k[layer, phys, off])
    v_sel = jnp.where(is_new, v_new[bidx, new_i], cache_v[layer, phys, off])
    return sparse_attend(q, k_sel, v_sel, valid)


def chunk_spatial_gating(u, vg, gm_spatial_w, gm_spatial_b):
    B, S, _ = u.shape
    pad = (-S) % CHUNK
    nc = (S + pad) // CHUNK
    vc = jnp.pad(vg, ((0, 0), (0, pad), (0, 0))).reshape(B, nc, CHUNK, GM_GROUPS, GM_GROUP_DIM)
    causal = jnp.tril(jnp.ones((CHUNK, CHUNK), dtype=bool))
    w = jnp.where(causal[None], gm_spatial_w, 0.0).astype(vc.dtype)
    mixed = jnp.einsum('gts,bcsgd->bctgd', w, vc) + gm_spatial_b.T[None, None, :, :, None]
    mixed = mixed.reshape(B, nc * CHUNK, GM_WIDTH)[:, :S]
    return u * mixed


def merge_branches(attn, gm, gate_attn, gate_gmlp, w_proj_attn, w_proj_gmlp, w_out):
    return (gate_attn * (attn @ w_proj_attn) + gate_gmlp * (gm @ w_proj_gmlp)) @ w_out


def hierarchical_moe(h, w_rg, b_rg, w_re, b_re, w_gate, w_up, w_down):
    B, S, D = h.shape
    t = h.reshape(B * S, D)
    g_logits = (t @ w_rg).astype(jnp.float32) + b_rg.astype(jnp.float32)
    g_probs = jax.nn.softmax(g_logits, axis=-1)
    g_sel = jnp.argmax(g_logits, axis=-1)
    g_w = jnp.take_along_axis(g_probs, g_sel[:, None], axis=-1)
    e_logits = ((t @ w_re).astype(jnp.float32) + b_re.astype(jnp.float32)).reshape(-1, N_EXPERT_GROUPS, EXPERTS_PER_GROUP)
    e_logits = jnp.take_along_axis(e_logits, g_sel[:, None, None], axis=1)[:, 0]
    top_p, top_i = lax.top_k(jax.nn.softmax(e_logits, axis=-1), EXPERT_TOP_K)
    top_p = top_p / jnp.sum(top_p, axis=-1, keepdims=True)
    expert_id = g_sel[:, None] * EXPERTS_PER_GROUP + top_i
    gate = jnp.sum(jax.nn.one_hot(expert_id, N_EXPERTS, dtype=jnp.float32) * (g_w * top_p)[..., None], axis=1)
    out = jnp.zeros_like(t)
    for g in range(N_EXPERT_GROUPS):
        sl = slice(g * EXPERTS_PER_GROUP, (g + 1) * EXPERTS_PER_GROUP)
        hid = jax.nn.silu(jnp.einsum('td,edf->tef', t, w_gate[sl])) * jnp.einsum('td,edf->tef', t, w_up[sl])
        hid = hid * gate[:, sl, None].astype(hid.dtype)
        out = out + jnp.einsum('tef,efd->td', hid, w_down[sl])
    return out.reshape(B, S, D)


def setup_inputs(seed: int = 0) -> dict:
    key = jax.random.key(seed)
    ks = iter(jax.random.split(key, 40))
    f32 = jnp.float32

    def nrm(shape, scale):
        return scale * jax.random.normal(next(ks), shape, f32)

    def gain(shape):
        return 1.0 + 0.1 * jax.random.normal(next(ks), shape, f32)

    x_prompt = nrm((BATCH, SEQ, D_MODEL), 1.0)
    x_sample = nrm((DEC_BATCH, DEC_SEQ, D_MODEL), 1.0)
    c_prompt = nrm((BATCH, D_MODEL), 1.0)
    c_sample = nrm((DEC_BATCH, D_MODEL), 1.0)
    cache_k = nrm((DEPTH, N_POOL_PAGES, PAGE_SIZE, N_KV_HEADS, HEAD_DIM), 1.0)
    cache_v = nrm((DEPTH, N_POOL_PAGES, PAGE_SIZE, N_KV_HEADS, HEAD_DIM), 1.0)
    cache_kidx = nrm((DEPTH, N_POOL_PAGES, PAGE_SIZE, IDX_DIM), 1.0)
    perm = jax.random.permutation(next(ks), N_POOL_PAGES)
    page_table = perm[:DEC_BATCH * N_PAGES].reshape(DEC_BATCH, N_PAGES).astype(jnp.int32)
    return {
        "x_prompt": x_prompt,
        "x_sample": x_sample,
        "c_prompt": c_prompt,
        "c_sample": c_sample,
        "cache_k": cache_k,
        "cache_v": cache_v,
        "cache_kidx": cache_kidx,
        "page_table": page_table,
        "w_ada": nrm((DEPTH, D_MODEL, 6 * D_MODEL), 0.5 * D_MODEL ** -0.5),
        "b_ada": nrm((DEPTH, 6 * D_MODEL), 0.02),
        "norm_mix_g": gain((DEPTH, D_MODEL)),
        "norm_ffn_g": gain((DEPTH, D_MODEL)),
        "w_in": nrm((DEPTH, D_MODEL, D_IN), D_MODEL ** -0.5),
        "q_norm_g": gain((DEPTH, HEAD_DIM)),
        "k_norm_g": gain((DEPTH, HEAD_DIM)),
        "kidx_norm_g": gain((DEPTH, IDX_DIM)),
        "gm_ln_g": gain((DEPTH, GM_WIDTH)),
        "gm_ln_b": nrm((DEPTH, GM_WIDTH), 0.02),
        "gm_spatial_w": nrm((DEPTH, GM_GROUPS, CHUNK, CHUNK), CHUNK ** -0.5),
        "gm_spatial_b": gain((DEPTH, GM_GROUPS, CHUNK)),
        "w_proj_attn": nrm((DEPTH, ATTN_WIDTH, D_MODEL), ATTN_WIDTH ** -0.5),
        "w_proj_gmlp": nrm((DEPTH, GM_WIDTH, D_MODEL), GM_WIDTH ** -0.5),
        "w_out": nrm((DEPTH, D_MODEL, D_MODEL), D_MODEL ** -0.5),
        "w_router_group": nrm((DEPTH, D_MODEL, N_EXPERT_GROUPS), D_MODEL ** -0.5),
        "b_router_group": nrm((DEPTH, N_EXPERT_GROUPS), 0.01),
        "w_router_expert": nrm((DEPTH, D_MODEL, N_EXPERTS), D_MODEL ** -0.5),
        "b_router_expert": nrm((DEPTH, N_EXPERTS), 0.01),
        "w_expert_gate": nrm((DEPTH, N_EXPERTS, D_MODEL, D_EXPERT), D_MODEL ** -0.5),
        "w_expert_up": nrm((DEPTH, N_EXPERTS, D_MODEL, D_EXPERT), D_MODEL ** -0.5),
        "w_expert_down": nrm((DEPTH, N_EXPERTS, D_EXPERT, D_MODEL), D_EXPERT ** -0.5),
    }


def reference(x_prompt, x_sample, c_prompt, c_sample, cache_k, cache_v, cache_kidx, page_table,
              w_ada, b_ada, norm_mix_g, norm_ffn_g, w_in, q_norm_g, k_norm_g, kidx_norm_g,
              gm_ln_g, gm_ln_b, gm_spatial_w, gm_spatial_b, w_proj_attn, w_proj_gmlp, w_out,
              w_router_group, b_router_group, w_router_expert, b_router_expert,
              w_expert_gate, w_expert_up, w_expert_down):
    pos_p = jnp.arange(SEQ, dtype=jnp.int32)
    pos_s = PAST_LEN + jnp.arange(DEC_SEQ, dtype=jnp.int32)
    xp, xs = x_prompt, x_sample
    k_p_rows, v_p_rows, ki_p_rows = [], [], []
    k_s_rows, v_s_rows, ki_s_rows, gv_s_rows = [], [], [], []
    for l in range(DEPTH):
        mp = adaln_terms(c_prompt, w_ada[l], b_ada[l])
        ms = adaln_terms(c_sample, w_ada[l], b_ada[l])

        hp = modulate(xp, norm_mix_g[l], mp[0], mp[1])
        q, k, v, qi, ki, wi, u, vg, ga, gb = project_inputs(hp, pos_p, w_in[l], q_norm_g[l], k_norm_g[l],
                                                            kidx_norm_g[l], gm_ln_g[l], gm_ln_b[l])
        attn_p = prompt_sparse_attention(q, k, v, qi, wi, ki)
        gm_p = chunk_spatial_gating(u, vg, gm_spatial_w[l], gm_spatial_b[l])
        xp = xp + mp[2] * merge_branches(attn_p, gm_p, ga, gb, w_proj_attn[l], w_proj_gmlp[l], w_out[l])
        k_p_rows.append(k)
        v_p_rows.append(v)
        ki_p_rows.append(ki)

        hs = modulate(xs, norm_mix_g[l], ms[0], ms[1])
        q, k, v, qi, ki, wi, u, vg, ga, gb = project_inputs(hs, pos_s, w_in[l], q_norm_g[l], k_norm_g[l],
                                                            kidx_norm_g[l], gm_ln_g[l], gm_ln_b[l])
        attn_s = sample_sparse_attention(q, k, v, qi, wi, ki, cache_k, cache_v, cache_kidx, page_table, l)
        gm_s = chunk_spatial_gating(u, vg, gm_spatial_w[l], gm_spatial_b[l])
        xs = xs + ms[2] * merge_branches(attn_s, gm_s, ga, gb, w_proj_attn[l], w_proj_gmlp[l], w_out[l])
        k_s_rows.append(k)
        v_s_rows.append(v)
        ki_s_rows.append(ki)
        gv_s_rows.append(vg)

        xp = xp + mp[5] * hierarchical_moe(modulate(xp, norm_ffn_g[l], mp[3], mp[4]), w_router_group[l],
                                            b_router_group[l], w_router_expert[l], b_router_expert[l],
                                            w_expert_gate[l], w_expert_up[l], w_expert_down[l])
        xs = xs + ms[5] * hierarchical_moe(modulate(xs, norm_ffn_g[l], ms[3], ms[4]), w_router_group[l],
                                            b_router_group[l], w_router_expert[l], b_router_expert[l],
                                            w_expert_gate[l], w_expert_up[l], w_expert_down[l])
    return (xp, xs, jnp.stack(k_p_rows), jnp.stack(v_p_rows), jnp.stack(ki_p_rows),
            jnp.stack(k_s_rows), jnp.stack(v_s_rows), jnp.stack(ki_s_rows), jnp.stack(gv_s_rows))
```

```python
import functools

import numpy as np
import jax
import jax.numpy as jnp
from jax import lax
from jax.experimental import pallas as pl
from jax.experimental.pallas import tpu as pltpu

F32 = jnp.float32
BF16 = jnp.bfloat16

D_MODEL = 1024
N_HEADS = 8
HEAD_DIM = 64
N_KV_HEADS = 2
ROT_DIM = 16
ROPE_THETA = 500000.0
IDX_HEADS = 4
IDX_DIM = 64
TOPK_MAX = 256
Q_BLOCK = 128
GM_WIDTH = 512
GM_GROUPS = 8
CHUNK = 128
N_EXPERT_GROUPS = 4
EXPERTS_PER_GROUP = 8
N_EXPERTS = 32
D_EXPERT = 256
EPS = 1e-6
PAGE_SIZE = 128
LANES = 128

C_Q, C_K, C_V, C_QI, C_KIWI, C_U, C_VG, C_GATE = 0, 512, 640, 768, 1024, 1152, 1664, 2176
D_IN_PAD = 4224
VMEM_LIMIT = 56 * 1024 * 1024


def _cparams(sem):
    return pltpu.CompilerParams(dimension_semantics=sem, vmem_limit_bytes=VMEM_LIMIT)


def _adaln_body(c_ref, w_ref, b_ref, o_ref):
    c = c_ref[...]
    a = c * jax.nn.sigmoid(c)
    o_ref[...] = jnp.dot(a, w_ref[...], preferred_element_type=F32,
                         precision=lax.Precision.HIGHEST) + b_ref[...]


def _adaln(c, w, b):
    r = c.shape[0]
    n = w.shape[1]
    bn = 1536
    return pl.pallas_call(
        _adaln_body,
        grid=(n // bn,),
        in_specs=[pl.BlockSpec((r, D_MODEL), lambda j: (0, 0)),
                  pl.BlockSpec((D_MODEL, bn), lambda j: (0, j)),
                  pl.BlockSpec((1, bn), lambda j: (0, j))],
        out_specs=pl.BlockSpec((r, bn), lambda j: (0, j)),
        out_shape=jax.ShapeDtypeStruct((r, n), F32),
        compiler_params=_cparams(("arbitrary",)),
        name="adaln",
    )(c, w, b.reshape(1, n))


def _rope_table_body(pos_ref, invf_ref, sa_m_ref, sb_m_ref, c_ref, sa_ref, sb_ref):
    ang = pos_ref[...] * invf_ref[...]
    s = jnp.sin(ang)
    c_ref[...] = jnp.cos(ang)
    sa_ref[...] = s * sa_m_ref[...]
    sb_ref[...] = s * sb_m_ref[...]


def _rope_tables(pos):
    half = ROT_DIM // 2
    inv_freq = ROPE_THETA ** (-jnp.arange(half, dtype=F32) / half)
    d = np.arange(LANES) % HEAD_DIM
    invf = jnp.where(jnp.asarray(d < ROT_DIM), inv_freq[d % half], 0.0).reshape(1, LANES)
    sa_m = jnp.asarray(np.where(d < half, -1.0, 0.0), F32).reshape(1, LANES)
    sb_m = jnp.asarray(np.where((d >= half) & (d < ROT_DIM), 1.0, 0.0), F32).reshape(1, LANES)
    r = pos.shape[0]
    return pl.pallas_call(
        _rope_table_body,
        out_shape=[jax.ShapeDtypeStruct((r, LANES), F32)] * 3,
        name="rope_tables",
    )(pos, invf, sa_m, sb_m)


def _rope(y, c, sa, sb):
    return y * c + pltpu.roll(y, LANES - ROT_DIM // 2, 1) * sa + pltpu.roll(y, ROT_DIM // 2, 1) * sb


def _seg_rms(r, seg, gain):
    ss = r * r
    hi = ss.astype(BF16)
    lo = (ss - hi.astype(F32)).astype(BF16)
    tot = jnp.dot(hi, seg, preferred_element_type=F32) + jnp.dot(lo, seg, preferred_element_type=F32)
    return r * lax.rsqrt(tot * (1.0 / HEAD_DIM) + EPS) * gain


def _proj_body(mod3d, x_ref, mod_ref, g_ref, w_ref, c_ref, sa_ref, sb_ref, qg_ref, kg_ref, kig_ref,
               lng_ref, lnb_ref, seg_ref, segki_ref,
               q_o, kf_o, vf_o, kif_o, kb_o, vb_o, qi_o, kib_o, kiwi_o, u_o, vg_o, ga_o, gb_o):
    x = x_ref[...]
    m = mod_ref[0] if mod3d else mod_ref[...]
    shift1 = m[:, 0:D_MODEL]
    scale1 = m[:, D_MODEL:2 * D_MODEL]
    ms = jnp.mean(x * x, axis=-1, keepdims=True)
    y = x * lax.rsqrt(ms + EPS) * g_ref[...]
    h = (y * (1.0 + scale1) + shift1).astype(BF16)
    c, sa, sb = c_ref[...], sa_ref[...], sb_ref[...]
    seg = seg_ref[...]

    def proj(a, b):
        return jnp.dot(h, w_ref[:, a:b], preferred_element_type=F32)

    for g in range(4):
        r = proj(C_Q + g * LANES, C_Q + (g + 1) * LANES)
        yq = _rope(_seg_rms(r, seg, qg_ref[...]), c, sa, sb) * (HEAD_DIM ** -0.5)
        q_o[:, g * LANES:(g + 1) * LANES] = yq.astype(BF16)
    r = proj(C_K, C_K + LANES)
    yk = _rope(_seg_rms(r, seg, kg_ref[...]), c, sa, sb)
    kf_o[...] = yk
    kb_o[...] = yk.astype(BF16)
    r = proj(C_V, C_V + LANES)
    vf_o[...] = r
    vb_o[...] = r.astype(BF16)
    for g in range(2):
        r = proj(C_QI + g * LANES, C_QI + (g + 1) * LANES)
        yqi = _rope(r, c, sa, sb) * (IDX_DIM ** -0.5 * IDX_HEADS ** -0.5)
        qi_o[:, g * LANES:(g + 1) * LANES] = yqi.astype(BF16)
    r = proj(C_KIWI, C_KIWI + LANES)
    yki = _rope(_seg_rms(r, segki_ref[...], kig_ref[...]), c, sa, sb)
    lane = lax.broadcasted_iota(jnp.int32, r.shape, 1)
    kiwi = jnp.where(lane < IDX_DIM, yki, r)
    kiwi_o[...] = kiwi
    kif_o[...] = kiwi[:, 0:IDX_DIM]
    kib_o[...] = kiwi[:, 0:IDX_DIM].astype(BF16)
    r = proj(C_U, C_U + GM_WIDTH)
    u_o[...] = jax.nn.gelu(r).astype(BF16)
    r = proj(C_VG, C_VG + GM_WIDTH)
    gl = jax.nn.gelu(r)
    mu = jnp.mean(gl, axis=-1, keepdims=True)
    dv = gl - mu
    var = jnp.mean(dv * dv, axis=-1, keepdims=True)
    vg_o[...] = (dv * lax.rsqrt(var + EPS) * lng_ref[...] + lnb_ref[...]).astype(vg_o.dtype)
    r = proj(C_GATE, C_GATE + D_MODEL)
    ga_o[...] = jax.nn.sigmoid(r).astype(BF16)
    r = proj(C_GATE + D_MODEL, C_GATE + 2 * D_MODEL)
    gb_o[...] = jax.nn.sigmoid(r).astype(BF16)


def _project(x, mod, tables, consts, w_pad, tm, tiles_per_seq, vg_dtype):
    t = x.shape[0]
    mod3d = mod.ndim == 3
    c_t, sa_t, sb_t = tables
    g_mix, qg, kg, kig, lng, lnb, seg, segki = consts
    row = lambda i: (i, 0)
    fixed = lambda i: (0, 0)
    if mod3d:
        mod_spec = pl.BlockSpec((1, 1, 6 * D_MODEL), lambda i: (i // tiles_per_seq, 0, 0))
        tab_spec = pl.BlockSpec((tm, LANES), lambda i: (i % tiles_per_seq, 0))
    else:
        mod_spec = pl.BlockSpec((tm, 6 * D_MODEL), row)
        tab_spec = pl.BlockSpec((1, LANES), fixed)
    widths = [(512, BF16), (128, F32), (128, F32), (64, F32), (128, BF16), (128, BF16), (256, BF16),
              (64, BF16), (128, F32), (512, BF16), (512, vg_dtype), (1024, BF16), (1024, BF16)]
    return pl.pallas_call(
        functools.partial(_proj_body, mod3d),
        grid=(t // tm,),
        in_specs=[pl.BlockSpec((tm, D_MODEL), row), mod_spec,
                  pl.BlockSpec((1, D_MODEL), fixed),
                  pl.BlockSpec((D_MODEL, D_IN_PAD), fixed),
                  tab_spec, tab_spec, tab_spec,
                  pl.BlockSpec((1, LANES), fixed), pl.BlockSpec((1, LANES), fixed), pl.BlockSpec((1, LANES), fixed),
                  pl.BlockSpec((1, GM_WIDTH), fixed), pl.BlockSpec((1, GM_WIDTH), fixed),
                  pl.BlockSpec((LANES, LANES), fixed), pl.BlockSpec((LANES, LANES), fixed)],
        out_specs=[pl.BlockSpec((tm, w), row) for w, _ in widths],
        out_shape=[jax.ShapeDtypeStruct((t, w), dt) for w, dt in widths],
        compiler_params=_cparams(("arbitrary",)),
        name="project",
    )(x, mod, g_mix, w_pad, c_t, sa_t, sb_t, qg, kg, kig, lng, lnb, seg, segki)


def _select_bias(s_ref, bias_ref, rows, width, topk, tie_check_start=12):
    nb = width // LANES
    kf = float(topk)
    neg, pos = -jnp.inf, jnp.inf

    def blk(j):
        return s_ref[:, j * LANES:(j + 1) * LANES]

    def count_above(t):
        tb = jnp.broadcast_to(t, (rows, LANES))
        acc = jnp.zeros((rows, LANES), F32)
        for j in range(nb):
            acc = acc + jnp.where(blk(j) > tb, 1.0, 0.0)
        return jnp.sum(acc, axis=1, keepdims=True)

    mx = jnp.full((rows, LANES), neg, F32)
    mn = jnp.full((rows, LANES), pos, F32)
    for j in range(nb):
        b = blk(j)
        mx = jnp.maximum(mx, b)
        mn = jnp.minimum(mn, jnp.where(b == neg, pos, b))
    hi0 = jnp.max(mx, axis=1, keepdims=True)
    smin = jnp.min(mn, axis=1, keepdims=True)
    lo0 = smin - jnp.abs(smin) - 1.0
    f_lo0 = count_above(lo0)
    zeros = jnp.zeros((rows, 1), F32)

    def active_of(f_lo, tie):
        return jnp.logical_and(f_lo > kf, tie == 0.0)

    def cond(st):
        _, _, _, f_lo, _, tie = st
        return jnp.max(jnp.where(active_of(f_lo, tie), 1.0, 0.0)) > 0.0

    def body(st):
        it, lo, hi, f_lo, f_hi, tie = st
        active = active_of(f_lo, tie)
        mid = lo + (hi - lo) * 0.5
        stuck = jnp.logical_or(mid <= lo, mid >= hi)
        cnt = count_above(mid)
        ge = cnt >= kf
        up_lo = jnp.logical_and(active, ge)
        up_hi = jnp.logical_and(active, jnp.logical_not(ge))
        lo = jnp.where(up_lo, mid, lo)
        f_lo = jnp.where(up_lo, cnt, f_lo)
        hi = jnp.where(up_hi, mid, hi)
        f_hi = jnp.where(up_hi, cnt, f_hi)
        tie = jnp.where(jnp.logical_and(active, stuck), 1.0, tie)

        def tie_check(_):
            lob = jnp.broadcast_to(lo, (rows, LANES))
            hib = jnp.broadcast_to(hi, (rows, LANES))
            vmx = jnp.full((rows, LANES), neg, F32)
            vmn = jnp.full((rows, LANES), pos, F32)
            for j in range(nb):
                b = blk(j)
                inn = jnp.logical_and(b > lob, b <= hib)
                vmx = jnp.maximum(vmx, jnp.where(inn, b, neg))
                vmn = jnp.minimum(vmn, jnp.where(inn, b, pos))
            one_value = jnp.max(vmx, axis=1, keepdims=True) == jnp.min(vmn, axis=1, keepdims=True)
            return jnp.where(one_value, 1.0, tie)

        tie = lax.cond(it >= tie_check_start, tie_check, lambda _: tie, 0)
        return it + 1, lo, hi, f_lo, f_hi, tie

    _, lo, hi, f_lo, f_hi, _ = lax.while_loop(cond, body, (jnp.int32(0), lo0, hi0, f_lo0, zeros, zeros))
    lob = jnp.broadcast_to(lo, (rows, LANES))
    need_prefix = jnp.max(jnp.where(f_lo > kf, 1.0, 0.0)) > 0.0

    @pl.when(jnp.logical_not(need_prefix))
    def _():
        for j in range(nb):
            bias_ref[:, j * LANES:(j + 1) * LANES] = jnp.where(blk(j) > lob, 0.0, neg)

    @pl.when(need_prefix)
    def _():
        hib = jnp.broadcast_to(hi, (rows, LANES))
        need = kf - f_hi
        ri = lax.broadcasted_iota(jnp.int32, (LANES, LANES), 0)
        ci = lax.broadcasted_iota(jnp.int32, (LANES, LANES), 1)
        upper = jnp.where(ri < ci, 1.0, 0.0).astype(BF16)
        off = jnp.zeros((rows, 1), F32)
        for j in range(nb):
            b = blk(j)
            inn = jnp.logical_and(b > lob, b <= hib)
            innf = jnp.where(inn, 1.0, 0.0)
            before = jnp.dot(innf.astype(BF16), upper, preferred_element_type=F32) + off
            sel = jnp.logical_or(b > hib, jnp.logical_and(inn, before < need))
            bias_ref[:, j * LANES:(j + 1) * LANES] = jnp.where(sel, 0.0, neg)
            off = off + jnp.sum(innf, axis=1, keepdims=True)


_NT = (((1,), (1,)), ((), ()))


def _prompt_attn_body(blk_i, topk, qi_ref, kiwi_ref, kib_ref, q_ref, k_ref, v_ref, o_ref, s_ref, bias_ref):
    width = (blk_i + 1) * Q_BLOCK
    neg = -jnp.inf
    kiwi = kiwi_ref[0]
    chunk = 512
    for c0 in range(0, width, chunk):
        c1 = min(width, c0 + chunk)
        kib = kib_ref[0, c0:c1, :]
        sc = None
        for h in range(IDX_HEADS):
            d = lax.dot_general(qi_ref[0, :, h * IDX_DIM:(h + 1) * IDX_DIM], kib, _NT,
                                preferred_element_type=F32)
            t = jnp.maximum(d, 0.0) * kiwi[:, IDX_DIM + h:IDX_DIM + h + 1]
            sc = t if sc is None else sc + t
        s_ref[:, c0:c1] = sc
    ri = lax.broadcasted_iota(jnp.int32, (Q_BLOCK, Q_BLOCK), 0)
    ci = lax.broadcasted_iota(jnp.int32, (Q_BLOCK, Q_BLOCK), 1)
    d0 = width - Q_BLOCK
    s_ref[:, d0:width] = jnp.where(ci <= ri, s_ref[:, d0:width], neg)
    if width > topk:
        _select_bias(s_ref, bias_ref, Q_BLOCK, width, topk)
    else:
        bias_ref[...] = jnp.where(s_ref[...] == neg, neg, 0.0)
    for h in range(N_HEADS):
        g = h // (N_HEADS // N_KV_HEADS)
        s = lax.dot_general(q_ref[0, :, h * HEAD_DIM:(h + 1) * HEAD_DIM],
                            k_ref[0, :, g * HEAD_DIM:(g + 1) * HEAD_DIM], _NT,
                            preferred_element_type=F32) + bias_ref[...]
        m = jnp.max(s, axis=1, keepdims=True)
        p = jnp.exp(s - m)
        l = jnp.sum(p, axis=1, keepdims=True)
        o = jnp.dot(p.astype(BF16), v_ref[0, :, g * HEAD_DIM:(g + 1) * HEAD_DIM], preferred_element_type=F32)
        o_ref[0, :, h * HEAD_DIM:(h + 1) * HEAD_DIM] = (o / l).astype(BF16)


def _prompt_attention(qi, kiwi, kib, q, kb, vb, topk):
    b, s, _ = q.shape
    outs = []
    for i in range(s // Q_BLOCK):
        width = (i + 1) * Q_BLOCK
        qblk = lambda bb, i=i: (bb, i, 0)
        kall = lambda bb: (bb, 0, 0)
        outs.append(pl.pallas_call(
            functools.partial(_prompt_attn_body, i, topk),
            grid=(b,),
            in_specs=[pl.BlockSpec((1, Q_BLOCK, IDX_HEADS * IDX_DIM), qblk),
                      pl.BlockSpec((1, Q_BLOCK, LANES), qblk),
                      pl.BlockSpec((1, width, IDX_DIM), kall),
                      pl.BlockSpec((1, Q_BLOCK, N_HEADS * HEAD_DIM), qblk),
                      pl.BlockSpec((1, width, N_KV_HEADS * HEAD_DIM), kall),
                      pl.BlockSpec((1, width, N_KV_HEADS * HEAD_DIM), kall)],
            out_specs=pl.BlockSpec((1, Q_BLOCK, N_HEADS * HEAD_DIM), lambda bb: (bb, 0, 0)),
            out_shape=jax.ShapeDtypeStruct((b, Q_BLOCK, N_HEADS * HEAD_DIM), BF16),
            scratch_shapes=[pltpu.VMEM((Q_BLOCK, width), F32), pltpu.VMEM((Q_BLOCK, width), F32)],
            compiler_params=_cparams(("arbitrary",)),
            name=f"prompt_attn_{i}",
        )(qi, kiwi, kib, q, kb, vb))
    return jnp.stack(outs)


def _page_copies(pt_ref, sample, src_hbm, buf, slot, sem, n_pages):
    return [pltpu.make_async_copy(src_hbm.at[pt_ref[sample, p]],
                                  buf.at[slot, pl.ds(p * PAGE_SIZE, PAGE_SIZE)], sem.at[slot])
            for p in range(n_pages)]


def _sample_scores_body(n_pages, pt_ref, kidx_hbm, qi_ref, wi_ref, kin_ref, o_ref, buf, sem):
    s = pl.program_id(0)
    slot = s % 2

    @pl.when(s == 0)
    def _():
        for cp in _page_copies(pt_ref, 0, kidx_hbm, buf, 0, sem, n_pages):
            cp.start()

    @pl.when(s + 1 < pl.num_programs(0))
    def _():
        for cp in _page_copies(pt_ref, s + 1, kidx_hbm, buf, 1 - slot, sem, n_pages):
            cp.start()

    for cp in _page_copies(pt_ref, s, kidx_hbm, buf, slot, sem, n_pages):
        cp.wait()
    qi = qi_ref[0]
    wi = wi_ref[0]
    ki = buf[slot].astype(BF16)
    d = lax.dot_general(qi, ki, _NT, preferred_element_type=F32)
    past = jnp.sum(jnp.maximum(d, 0.0) * wi, axis=0, keepdims=True)
    dn = jnp.sum(qi.astype(F32) * kin_ref[0].astype(F32), axis=1, keepdims=True)
    new = jnp.sum(jnp.maximum(dn, 0.0) * wi, axis=0, keepdims=True)
    lane = lax.broadcasted_iota(jnp.int32, (1, LANES), 1)
    tail = jnp.where(lane == 0, jnp.broadcast_to(new, (1, LANES)), -jnp.inf)
    o_ref[0] = jnp.concatenate([past, tail], axis=1)


def _sample_scores(page_table, kidx_pool, qi8, wi8, ki_new):
    n, n_pages = page_table.shape
    past = n_pages * PAGE_SIZE
    grid_spec = pltpu.PrefetchScalarGridSpec(
        num_scalar_prefetch=1,
        grid=(n,),
        in_specs=[pl.BlockSpec(memory_space=pl.ANY),
                  pl.BlockSpec((1, 8, IDX_DIM), lambda s, pt: (s, 0, 0)),
                  pl.BlockSpec((1, 8, 1), lambda s, pt: (s, 0, 0)),
                  pl.BlockSpec((1, 1, IDX_DIM), lambda s, pt: (s, 0, 0))],
        out_specs=pl.BlockSpec((1, 1, past + LANES), lambda s, pt: (s, 0, 0)),
        scratch_shapes=[pltpu.VMEM((2, past, IDX_DIM), F32), pltpu.SemaphoreType.DMA((2,))],
    )
    return pl.pallas_call(
        functools.partial(_sample_scores_body, n_pages),
        grid_spec=grid_spec,
        out_shape=jax.ShapeDtypeStruct((n, 1, past + LANES), F32),
        compiler_params=_cparams(("arbitrary",)),
        name="sample_scores",
    )(page_table, kidx_pool, qi8, wi8, ki_new)


def _sample_select_body(topk, s_ref, bias_ref):
    rows, width = s_ref.shape
    _select_bias(s_ref, bias_ref, rows, width, topk)


def _sample_select(scores, topk):
    return pl.pallas_call(
        functools.partial(_sample_select_body, topk),
        out_shape=jax.ShapeDtypeStruct(scores.shape, F32),
        compiler_params=pltpu.CompilerParams(vmem_limit_bytes=VMEM_LIMIT),
        name="sample_select",
    )(scores)


def _sample_attn_body(n_pages, pt_ref, k_hbm, v_hbm, q_ref, bias_ref, kn_ref, vn_ref, o_ref, kbuf, vbuf, sem):
    s = pl.program_id(0)
    slot = s % 2
    past = n_pages * PAGE_SIZE

    def copies(sample, sl):
        return (_page_copies(pt_ref, sample, k_hbm, kbuf, sl, sem.at[0], n_pages)
                + _page_copies(pt_ref, sample, v_hbm, vbuf, sl, sem.at[1], n_pages))

    @pl.when(s == 0)
    def _():
        for cp in copies(0, 0):
            cp.start()

    @pl.when(s + 1 < pl.num_programs(0))
    def _():
        for cp in copies(s + 1, 1 - slot):
            cp.start()

    for cp in copies(s, slot):
        cp.wait()
    q = q_ref[0]
    kb = kbuf[slot].astype(BF16)
    vb = vbuf[slot].astype(BF16)
    row = lax.broadcasted_iota(jnp.int32, (N_HEADS, 1), 0)
    first = row < (N_HEADS // N_KV_HEADS)
    s0 = lax.dot_general(q, kb[:, 0:HEAD_DIM], _NT, preferred_element_type=F32)
    s1 = lax.dot_general(q, kb[:, HEAD_DIM:2 * HEAD_DIM], _NT, preferred_element_type=F32)
    bias = bias_ref[0]
    sc = jnp.where(first, s0, s1) + bias[:, 0:past]
    kn = kn_ref[0].astype(F32)
    vn = vn_ref[0].astype(F32)
    kn_sel = jnp.where(first, kn[:, 0:HEAD_DIM], kn[:, HEAD_DIM:2 * HEAD_DIM])
    vn_sel = jnp.where(first, vn[:, 0:HEAD_DIM], vn[:, HEAD_DIM:2 * HEAD_DIM])
    sn = jnp.sum(q.astype(F32) * kn_sel, axis=1, keepdims=True) + bias[:, past:past + 1]
    m = jnp.maximum(jnp.max(sc, axis=1, keepdims=True), sn)
    p = jnp.exp(sc - m)
    pn = jnp.exp(sn - m)
    l = jnp.sum(p, axis=1, keepdims=True) + pn
    pb = p.astype(BF16)
    o0 = jnp.dot(pb, vb[:, 0:HEAD_DIM], preferred_element_type=F32)
    o1 = jnp.dot(pb, vb[:, HEAD_DIM:2 * HEAD_DIM], preferred_element_type=F32)
    o = jnp.where(first, o0, o1) + pn.astype(BF16).astype(F32) * vn_sel
    o_ref[0] = (o / l).astype(BF16)


def _sample_attention(page_table, k_pool, v_pool, q8, bias, k_new, v_new):
    n, n_pages = page_table.shape
    past = n_pages * PAGE_SIZE
    kvw = N_KV_HEADS * HEAD_DIM
    per = lambda s, pt: (s, 0, 0)
    grid_spec = pltpu.PrefetchScalarGridSpec(
        num_scalar_prefetch=1,
        grid=(n,),
        in_specs=[pl.BlockSpec(memory_space=pl.ANY), pl.BlockSpec(memory_space=pl.ANY),
                  pl.BlockSpec((1, N_HEADS, HEAD_DIM), per),
                  pl.BlockSpec((1, 1, past + LANES), per),
                  pl.BlockSpec((1, 1, kvw), per),
                  pl.BlockSpec((1, 1, kvw), per)],
        out_specs=pl.BlockSpec((1, N_HEADS, HEAD_DIM), per),
        scratch_shapes=[pltpu.VMEM((2, past, kvw), F32), pltpu.VMEM((2, past, kvw), F32),
                        pltpu.SemaphoreType.DMA((2, 2))],
    )
    return pl.pallas_call(
        functools.partial(_sample_attn_body, n_pages),
        grid_spec=grid_spec,
        out_shape=jax.ShapeDtypeStruct((n, N_HEADS, HEAD_DIM), BF16),
        compiler_params=_cparams(("arbitrary",)),
        name="sample_attn",
    )(page_table, k_pool, v_pool, q8, bias, k_new, v_new)


def _gmlp_body(n_chunks, u_ref, vg_ref, w_ref, bt_ref, o_ref):
    ri = lax.broadcasted_iota(jnp.int32, (CHUNK, CHUNK), 0)
    ci = lax.broadcasted_iota(jnp.int32, (CHUNK, CHUNK), 1)
    gd = GM_WIDTH // GM_GROUPS
    for g in range(GM_GROUPS):
        wg = jnp.where(ci <= ri, w_ref[g], 0.0).astype(BF16)
        bg = bt_ref[:, g:g + 1]
        for c in range(n_chunks):
            rows = slice(c * CHUNK, (c + 1) * CHUNK)
            cols = slice(g * gd, (g + 1) * gd)
            mixed = jnp.dot(wg, vg_ref[rows, cols], preferred_element_type=F32) + bg
            o_ref[rows, cols] = (u_ref[rows, cols].astype(F32) * mixed).astype(BF16)


def _gmlp_prompt(u, vg, w, bt, tm):
    t = u.shape[0]
    row = lambda i: (i, 0)
    return pl.pallas_call(
        functools.partial(_gmlp_body, tm // CHUNK),
        grid=(t // tm,),
        in_specs=[pl.BlockSpec((tm, GM_WIDTH), row), pl.BlockSpec((tm, GM_WIDTH), row),
                  pl.BlockSpec((GM_GROUPS, CHUNK, CHUNK), lambda i: (0, 0, 0)),
                  pl.BlockSpec((CHUNK, LANES), lambda i: (0, 0))],
        out_specs=pl.BlockSpec((tm, GM_WIDTH), row),
        out_shape=jax.ShapeDtypeStruct((t, GM_WIDTH), BF16),
        compiler_params=_cparams(("arbitrary",)),
        name="gmlp",
    )(u, vg, w, bt)


def _gmlp_first_row_body(u_ref, vg_ref, w0_ref, b0_ref, o_ref):
    o_ref[...] = (u_ref[...].astype(F32) * (vg_ref[...] * w0_ref[...] + b0_ref[...])).astype(BF16)


def _gmlp_sample(u, vg, w0, b0):
    return pl.pallas_call(
        _gmlp_first_row_body,
        out_shape=jax.ShapeDtypeStruct(u.shape, BF16),
        name="gmlp_first_row",
    )(u, vg, w0, b0)


def _merge_body(mod3d, attn_ref, gm_ref, ga_ref, gb_ref, x_ref, mod_ref, wpa_ref, wpg_ref, wo_ref, g2_ref,
                wr_ref, br_ref, x1_o, h2_o, gate_o):
    tm = x_ref.shape[0]
    attn = attn_ref[...].reshape(tm, N_HEADS * HEAD_DIM)
    a = jnp.dot(attn, wpa_ref[...], preferred_element_type=F32)
    g = jnp.dot(gm_ref[...], wpg_ref[...], preferred_element_type=F32)
    merged = ga_ref[...].astype(F32) * a + gb_ref[...].astype(F32) * g
    out = jnp.dot(merged.astype(BF16), wo_ref[...], preferred_element_type=F32)
    m = mod_ref[0] if mod3d else mod_ref[...]
    x1 = x_ref[...] + m[:, 2 * D_MODEL:3 * D_MODEL] * out
    x1_o[...] = x1
    ms = jnp.mean(x1 * x1, axis=-1, keepdims=True)
    y = x1 * lax.rsqrt(ms + EPS) * g2_ref[...]
    h2 = y * (1.0 + m[:, 4 * D_MODEL:5 * D_MODEL]) + m[:, 3 * D_MODEL:4 * D_MODEL]
    hi = h2.astype(BF16)
    h2_o[...] = hi
    lo = (h2 - hi.astype(F32)).astype(BF16)
    r = jnp.dot(hi, wr_ref[...], preferred_element_type=F32) + jnp.dot(lo, wr_ref[...], preferred_element_type=F32)
    logits = r[:, 0:LANES] + r[:, LANES:2 * LANES] + br_ref[...]
    neg = -jnp.inf
    big = jnp.int32(1 << 20)
    lane = lax.broadcasted_iota(jnp.int32, logits.shape, 1)
    is_g = jnp.logical_and(lane >= N_EXPERTS, lane < N_EXPERTS + N_EXPERT_GROUPS)
    gl = jnp.where(is_g, logits, neg)
    gmax = jnp.max(gl, axis=1, keepdims=True)
    g_lane = jnp.min(jnp.where(gl == gmax, lane, big), axis=1, keepdims=True)
    g_w = 1.0 / jnp.sum(jnp.exp(gl - gmax), axis=1, keepdims=True)
    g_sel = g_lane - N_EXPERTS
    in_grp = jnp.logical_and(lane < N_EXPERTS, (lane >> 3) == g_sel)
    el = jnp.where(in_grp, logits, neg)
    m1 = jnp.max(el, axis=1, keepdims=True)
    i1 = jnp.min(jnp.where(el == m1, lane, big), axis=1, keepdims=True)
    el2 = jnp.where(lane == i1, neg, el)
    m2 = jnp.max(el2, axis=1, keepdims=True)
    i2 = jnp.min(jnp.where(el2 == m2, lane, big), axis=1, keepdims=True)
    e2 = jnp.exp(m2 - m1)
    w1 = 1.0 / (1.0 + e2)
    w2 = e2 / (1.0 + e2)
    gate_o[...] = jnp.where(lane == i1, g_w * w1, 0.0) + jnp.where(lane == i2, g_w * w2, 0.0)


def _merge(attn4, gm, ga, gb, x, mod, wpa, wpg, wo, g2, wr, br, tm, tiles_per_seq):
    t = x.shape[0]
    mod3d = mod.ndim == 3
    nb = tm // Q_BLOCK
    row = lambda i: (i, 0)
    fixed = lambda i: (0, 0)
    if mod3d:
        mod_spec = pl.BlockSpec((1, 1, 6 * D_MODEL), lambda i: (i // tiles_per_seq, 0, 0))
    else:
        mod_spec = pl.BlockSpec((tm, 6 * D_MODEL), row)
    attn_spec = pl.BlockSpec((nb, 1, Q_BLOCK, N_HEADS * HEAD_DIM),
                             lambda i: (i % tiles_per_seq, i // tiles_per_seq, 0, 0))
    return pl.pallas_call(
        functools.partial(_merge_body, mod3d),
        grid=(t // tm,),
        in_specs=[attn_spec, pl.BlockSpec((tm, GM_WIDTH), row),
                  pl.BlockSpec((tm, D_MODEL), row), pl.BlockSpec((tm, D_MODEL), row),
                  pl.BlockSpec((tm, D_MODEL), row), mod_spec,
                  pl.BlockSpec((N_HEADS * HEAD_DIM, D_MODEL), fixed), pl.BlockSpec((GM_WIDTH, D_MODEL), fixed),
                  pl.BlockSpec((D_MODEL, D_MODEL), fixed), pl.BlockSpec((1, D_MODEL), fixed),
                  pl.BlockSpec((D_MODEL, 2 * LANES), fixed), pl.BlockSpec((1, LANES), fixed)],
        out_specs=[pl.BlockSpec((tm, D_MODEL), row), pl.BlockSpec((tm, D_MODEL), row),
                   pl.BlockSpec((tm, LANES), row)],
        out_shape=[jax.ShapeDtypeStruct((t, D_MODEL), F32), jax.ShapeDtypeStruct((t, D_MODEL), BF16),
                   jax.ShapeDtypeStruct((t, LANES), F32)],
        compiler_params=_cparams(("arbitrary",)),
        name="merge",
    )(attn4, gm, ga, gb, x, mod, wpa, wpg, wo, g2, wr, br)


def _moe_body(mod3d, h_ref, gate_ref, x1_ref, mod_ref, wgu_ref, wd_ref, o_ref, acc_ref):
    e = pl.program_id(1)

    @pl.when(e == 0)
    def _():
        acc_ref[...] = jnp.zeros_like(acc_ref)

    gu = jnp.dot(h_ref[...], wgu_ref[0], preferred_element_type=F32)
    a = gu[:, 0:D_EXPERT]
    hid = a * jax.nn.sigmoid(a) * gu[:, D_EXPERT:2 * D_EXPERT]
    gate = gate_ref[...]
    lane = lax.broadcasted_iota(jnp.int32, gate.shape, 1)
    ge = jnp.sum(jnp.where(lane == e, gate, 0.0), axis=1, keepdims=True)
    acc_ref[...] += jnp.dot((hid * ge).astype(BF16), wd_ref[0], preferred_element_type=F32)

    @pl.when(e == pl.num_programs(1) - 1)
    def _():
        m = mod_ref[0] if mod3d else mod_ref[...]
        o_ref[...] = x1_ref[...] + m[:, 5 * D_MODEL:6 * D_MODEL] * acc_ref[...]


def _moe(h2, gate, x1, mod, wgu, wd, tm, tiles_per_seq):
    t = h2.shape[0]
    mod3d = mod.ndim == 3
    row = lambda i, e: (i, 0)
    if mod3d:
        mod_spec = pl.BlockSpec((1, 1, 6 * D_MODEL), lambda i, e: (i // tiles_per_seq, 0, 0))
    else:
        mod_spec = pl.BlockSpec((tm, 6 * D_MODEL), row)
    return pl.pallas_call(
        functools.partial(_moe_body, mod3d),
        grid=(t // tm, N_EXPERTS),
        in_specs=[pl.BlockSpec((tm, D_MODEL), row), pl.BlockSpec((tm, LANES), row),
                  pl.BlockSpec((tm, D_MODEL), row), mod_spec,
                  pl.BlockSpec((1, D_MODEL, 2 * D_EXPERT), lambda i, e: (e, 0, 0)),
                  pl.BlockSpec((1, D_EXPERT, D_MODEL), lambda i, e: (e, 0, 0))],
        out_specs=pl.BlockSpec((tm, D_MODEL), row),
        out_shape=jax.ShapeDtypeStruct((t, D_MODEL), F32),
        scratch_shapes=[pltpu.VMEM((tm, D_MODEL), F32)],
        compiler_params=_cparams(("arbitrary", "arbitrary")),
        name="moe",
    )(h2, gate, x1, mod, wgu, wd)


def _pad_lanes(v, fill):
    n = v.shape[-1]
    return jnp.concatenate([v, jnp.full((LANES - n,), fill, v.dtype)]).reshape(1, LANES)


def kernel(x_prompt, x_sample, c_prompt, c_sample, cache_k, cache_v, cache_kidx, page_table, w_ada, b_ada, norm_mix_g, norm_ffn_g, w_in, q_norm_g, k_norm_g, kidx_norm_g, gm_ln_g, gm_ln_b, gm_spatial_w, gm_spatial_b, w_proj_attn, w_proj_gmlp, w_out, w_router_group, b_router_group, w_router_expert, b_router_expert, w_expert_gate, w_expert_up, w_expert_down):
    depth = w_ada.shape[0]
    assert depth == 1
    l = 0
    bp, sp, _ = x_prompt.shape
    bs, ss, _ = x_sample.shape
    assert ss == 1
    n_pages = page_table.shape[1]
    past = n_pages * PAGE_SIZE
    tp = bp * sp

    w = w_in[l]
    zpad = jnp.zeros((D_MODEL, LANES - IDX_DIM - IDX_HEADS), F32)
    w_pad = jnp.concatenate([w[:, 0:1024], w[:, 1024:1088], w[:, 1088:1092], zpad, w[:, 1092:]], axis=1).astype(BF16)
    seg_np = (np.arange(LANES)[:, None] // HEAD_DIM) == (np.arange(LANES)[None, :] // HEAD_DIM)
    seg = jnp.asarray(seg_np, BF16)
    segki = jnp.asarray(seg_np & (np.arange(LANES)[:, None] < IDX_DIM) & (np.arange(LANES)[None, :] < IDX_DIM), BF16)
    consts = (norm_mix_g[l].reshape(1, D_MODEL),
              jnp.tile(q_norm_g[l], 2).reshape(1, LANES), jnp.tile(k_norm_g[l], 2).reshape(1, LANES),
              _pad_lanes(kidx_norm_g[l], 1.0),
              gm_ln_g[l].reshape(1, GM_WIDTH), gm_ln_b[l].reshape(1, GM_WIDTH), seg, segki)
    wpa = w_proj_attn[l].astype(BF16)
    wpg = w_proj_gmlp[l].astype(BF16)
    wo = w_out[l].astype(BF16)
    wr32 = jnp.concatenate([w_router_expert[l], w_router_group[l],
                            jnp.zeros((D_MODEL, LANES - N_EXPERTS - N_EXPERT_GROUPS), F32)], axis=1)
    wr_hi = wr32.astype(BF16)
    wr_lo = (wr32 - wr_hi.astype(F32)).astype(BF16)
    wr = jnp.concatenate([wr_hi, wr_lo], axis=1)
    br = _pad_lanes(jnp.concatenate([b_router_expert[l], b_router_group[l]]), 0.0)
    wgu = jnp.concatenate([w_expert_gate[l], w_expert_up[l]], axis=2).astype(BF16)
    wd = w_expert_down[l].astype(BF16)
    g2 = norm_ffn_g[l].reshape(1, D_MODEL)

    mod = _adaln(jnp.concatenate([c_prompt, c_sample], axis=0), w_ada[l], b_ada[l])
    mod_p = mod[0:bp].reshape(bp, 1, 6 * D_MODEL)
    mod_s = mod[bp:bp + bs]
    pos = jnp.concatenate([jnp.arange(sp, dtype=jnp.int32),
                           jnp.full((8,), past, jnp.int32)]).astype(F32).reshape(sp + 8, 1)
    tabs = _rope_tables(pos)
    tabs_p = tuple(t[0:sp] for t in tabs)
    tabs_s = tuple(t[sp:sp + 1] for t in tabs)

    tm = 512
    tps = sp // tm
    (q, kf, vf, kif, kb, vb, qi, kib, kiwi, u, vg, ga, gb) = _project(
        x_prompt.reshape(tp, D_MODEL), mod_p, tabs_p, consts, w_pad, tm, tps, BF16)
    topk_p = min(TOPK_MAX, sp // 4)
    r3 = lambda a: a.reshape(bp, sp, a.shape[-1])
    attn_p = _prompt_attention(r3(qi), r3(kiwi), r3(kib), r3(q), r3(kb), r3(vb), topk_p)
    bt = jnp.concatenate([gm_spatial_b[l].T, jnp.zeros((CHUNK, LANES - GM_GROUPS), F32)], axis=1)
    gm_p = _gmlp_prompt(u, vg, gm_spatial_w[l], bt, tm)
    x1_p, h2_p, gate_p = _merge(attn_p, gm_p, ga, gb, x_prompt.reshape(tp, D_MODEL), mod_p,
                                wpa, wpg, wo, g2, wr, br, tm, tps)
    y_p = _moe(h2_p, gate_p, x1_p, mod_p, wgu, wd, 1024, sp // 1024)

    (q_s, kf_s, vf_s, kif_s, kb_s, vb_s, qi_s, kib_s, kiwi_s, u_s, vg_s, ga_s, gb_s) = _project(
        x_sample.reshape(bs, D_MODEL), mod_s, tabs_s, consts, w_pad, bs, 1, F32)
    qi8 = jnp.concatenate([qi_s.reshape(bs, IDX_HEADS, IDX_DIM),
                           jnp.zeros((bs, 8 - IDX_HEADS, IDX_DIM), BF16)], axis=1)
    wi8 = jnp.concatenate([kiwi_s[:, IDX_DIM:IDX_DIM + IDX_HEADS],
                           jnp.zeros((bs, 8 - IDX_HEADS), F32)], axis=1).reshape(bs, 8, 1)
    scores = _sample_scores(page_table, cache_kidx[l], qi8, wi8, kib_s.reshape(bs, 1, IDX_DIM))
    topk_s = min(TOPK_MAX, (past + ss) // 4)
    bias = _sample_select(scores.reshape(bs, past + LANES), topk_s).reshape(bs, 1, past + LANES)
    kvw = N_KV_HEADS * HEAD_DIM
    attn_s = _sample_attention(page_table, cache_k[l].reshape(-1, PAGE_SIZE, kvw),
                               cache_v[l].reshape(-1, PAGE_SIZE, kvw),
                               q_s.reshape(bs, N_HEADS, HEAD_DIM), bias,
                               kb_s.reshape(bs, 1, kvw), vb_s.reshape(bs, 1, kvw))
    gd = GM_WIDTH // GM_GROUPS
    w0 = jnp.repeat(gm_spatial_w[l][:, 0, 0], gd).reshape(1, GM_WIDTH)
    b0 = jnp.repeat(gm_spatial_b[l][:, 0], gd).reshape(1, GM_WIDTH)
    gm_s = _gmlp_sample(u_s, vg_s, w0, b0)
    x1_s, h2_s, gate_s = _merge(attn_s.reshape(1, 1, bs, N_HEADS * HEAD_DIM), gm_s, ga_s, gb_s,
                                x_sample.reshape(bs, D_MODEL), mod_s, wpa, wpg, wo, g2, wr, br, bs, 1)
    y_s = _moe(h2_s, gate_s, x1_s, mod_s, wgu, wd, bs, 1)

    return (y_p.reshape(bp, sp, D_MODEL), y_s.reshape(bs, ss, D_MODEL),
            kf.reshape(1, bp, sp, N_KV_HEADS, HEAD_DIM), vf.reshape(1, bp, sp, N_KV_HEADS, HEAD_DIM),
            kif.reshape(1, bp, sp, IDX_DIM),
            kf_s.reshape(1, bs, ss, N_KV_HEADS, HEAD_DIM), vf_s.reshape(1, bs, ss, N_KV_HEADS, HEAD_DIM),
            kif_s.reshape(1, bs, ss, IDX_DIM), vg_s.reshape(1, bs, ss, GM_WIDTH))
```

```python
import functools

import numpy as np
import jax
import jax.numpy as jnp
from jax import lax
from jax.experimental import pallas as pl
from jax.experimental.pallas import tpu as pltpu

F32 = jnp.float32
BF16 = jnp.bfloat16

D_MODEL = 1024
N_HEADS = 8
HEAD_DIM = 64
N_KV_HEADS = 2
ROT_DIM = 16
ROPE_THETA = 500000.0
IDX_HEADS = 4
IDX_DIM = 64
TOPK_MAX = 256
Q_BLOCK = 128
GM_WIDTH = 512
GM_GROUPS = 8
CHUNK = 128
N_EXPERT_GROUPS = 4
EXPERTS_PER_GROUP = 8
N_EXPERTS = 32
D_EXPERT = 256
EPS = 1e-6
PAGE_SIZE = 128
LANES = 128

C_Q, C_K, C_V, C_QI, C_KIWI, C_U, C_VG, C_GATE = 0, 512, 640, 768, 1024, 1152, 1664, 2176
D_IN_PAD = 4224
VMEM_LIMIT = 56 * 1024 * 1024


def _cparams(sem):
    return pltpu.CompilerParams(dimension_semantics=sem, vmem_limit_bytes=VMEM_LIMIT)


def _adaln_body(c_ref, w_ref, b_ref, o_ref):
    c = c_ref[...]
    a = c * jax.nn.sigmoid(c)
    o_ref[...] = jnp.dot(a, w_ref[...], preferred_element_type=F32,
                         precision=lax.Precision.HIGHEST) + b_ref[...]


def _adaln(c, w, b):
    r = c.shape[0]
    n = w.shape[1]
    bn = 1536
    return pl.pallas_call(
        _adaln_body,
        grid=(n // bn,),
        in_specs=[pl.BlockSpec((r, D_MODEL), lambda j: (0, 0)),
                  pl.BlockSpec((D_MODEL, bn), lambda j: (0, j)),
                  pl.BlockSpec((1, bn), lambda j: (0, j))],
        out_specs=pl.BlockSpec((r, bn), lambda j: (0, j)),
        out_shape=jax.ShapeDtypeStruct((r, n), F32),
        compiler_params=_cparams(("arbitrary",)),
        name="adaln",
    )(c, w, b.reshape(1, n))


def _rope_table_body(pos_ref, invf_ref, sa_m_ref, sb_m_ref, c_ref, sa_ref, sb_ref):
    ang = pos_ref[...] * invf_ref[...]
    s = jnp.sin(ang)
    c_ref[...] = jnp.cos(ang)
    sa_ref[...] = s * sa_m_ref[...]
    sb_ref[...] = s * sb_m_ref[...]


def _rope_tables(pos):
    half = ROT_DIM // 2
    inv_freq = ROPE_THETA ** (-jnp.arange(half, dtype=F32) / half)
    d = np.arange(LANES) % HEAD_DIM
    invf = jnp.where(jnp.asarray(d < ROT_DIM), inv_freq[d % half], 0.0).reshape(1, LANES)
    sa_m = jnp.asarray(np.where(d < half, -1.0, 0.0), F32).reshape(1, LANES)
    sb_m = jnp.asarray(np.where((d >= half) & (d < ROT_DIM), 1.0, 0.0), F32).reshape(1, LANES)
    r = pos.shape[0]
    return pl.pallas_call(
        _rope_table_body,
        out_shape=[jax.ShapeDtypeStruct((r, LANES), F32)] * 3,
        name="rope_tables",
    )(pos, invf, sa_m, sb_m)


def _rope(y, c, sa, sb):
    return y * c + pltpu.roll(y, LANES - ROT_DIM // 2, 1) * sa + pltpu.roll(y, ROT_DIM // 2, 1) * sb


def _seg_rms(r, seg, gain):
    ss = r * r
    hi = ss.astype(BF16)
    lo = (ss - hi.astype(F32)).astype(BF16)
    tot = jnp.dot(hi, seg, preferred_element_type=F32) + jnp.dot(lo, seg, preferred_element_type=F32)
    return r * lax.rsqrt(tot * (1.0 / HEAD_DIM) + EPS) * gain


def _proj_body(mod3d, x_ref, mod_ref, g_ref, w_ref, c_ref, sa_ref, sb_ref, qg_ref, kg_ref, kig_ref,
               lng_ref, lnb_ref, seg_ref, segki_ref,
               q_o, kb_o, vb_o, qi_o, kib_o, kiwi_o, u_o, vg_o, ga_o, gb_o, kt_o, vtf_o, kit_o, vt_o):
    x = x_ref[...]
    m = mod_ref[0] if mod3d else mod_ref[...]
    shift1 = m[:, 0:D_MODEL]
    scale1 = m[:, D_MODEL:2 * D_MODEL]
    ms = jnp.mean(x * x, axis=-1, keepdims=True)
    y = x * lax.rsqrt(ms + EPS) * g_ref[...]
    h = (y * (1.0 + scale1) + shift1).astype(BF16)
    c, sa, sb = c_ref[...], sa_ref[...], sb_ref[...]
    seg = seg_ref[...]

    def proj(a, b):
        return jnp.dot(h, w_ref[:, a:b], preferred_element_type=F32)

    for g in range(4):
        r = proj(C_Q + g * LANES, C_Q + (g + 1) * LANES)
        yq = _rope(_seg_rms(r, seg, qg_ref[...]), c, sa, sb) * (HEAD_DIM ** -0.5)
        q_o[:, g * LANES:(g + 1) * LANES] = yq.astype(BF16)
    r = proj(C_K, C_K + LANES)
    yk = _rope(_seg_rms(r, seg, kg_ref[...]), c, sa, sb)
    kt_o[0] = yk.T
    kb_o[...] = yk.astype(BF16)
    r = proj(C_V, C_V + LANES)
    r_t = r.T
    vtf_o[0] = r_t
    vb_o[...] = r.astype(BF16)
    vt_o[0] = r_t.astype(BF16)
    for g in range(2):
        r = proj(C_QI + g * LANES, C_QI + (g + 1) * LANES)
        yqi = _rope(r, c, sa, sb) * (IDX_DIM ** -0.5 * IDX_HEADS ** -0.5)
        qi_o[:, g * LANES:(g + 1) * LANES] = yqi.astype(BF16)
    r = proj(C_KIWI, C_KIWI + LANES)
    yki = _rope(_seg_rms(r, segki_ref[...], kig_ref[...]), c, sa, sb)
    lane = lax.broadcasted_iota(jnp.int32, r.shape, 1)
    kiwi = jnp.where(lane < IDX_DIM, yki, r)
    kiwi_o[...] = kiwi
    kit_o[0] = kiwi.T[0:IDX_DIM, :]
    kib_o[...] = kiwi[:, 0:IDX_DIM].astype(BF16)
    r = proj(C_U, C_U + GM_WIDTH)
    u_o[...] = jax.nn.gelu(r).astype(BF16)
    r = proj(C_VG, C_VG + GM_WIDTH)
    gl = jax.nn.gelu(r)
    mu = jnp.mean(gl, axis=-1, keepdims=True)
    dv = gl - mu
    var = jnp.mean(dv * dv, axis=-1, keepdims=True)
    vg_o[...] = (dv * lax.rsqrt(var + EPS) * lng_ref[...] + lnb_ref[...]).astype(vg_o.dtype)
    r = proj(C_GATE, C_GATE + D_MODEL)
    ga_o[...] = jax.nn.sigmoid(r).astype(BF16)
    r = proj(C_GATE + D_MODEL, C_GATE + 2 * D_MODEL)
    gb_o[...] = jax.nn.sigmoid(r).astype(BF16)


def _project(x, mod, tables, consts, w_pad, tm, tiles_per_seq, vg_dtype):
    t = x.shape[0]
    mod3d = mod.ndim == 3
    c_t, sa_t, sb_t = tables
    g_mix, qg, kg, kig, lng, lnb, seg, segki = consts
    row = lambda i: (i, 0)
    fixed = lambda i: (0, 0)
    if mod3d:
        mod_spec = pl.BlockSpec((1, 1, 6 * D_MODEL), lambda i: (i // tiles_per_seq, 0, 0))
        tab_spec = pl.BlockSpec((tm, LANES), lambda i: (i % tiles_per_seq, 0))
    else:
        mod_spec = pl.BlockSpec((tm, 6 * D_MODEL), row)
        tab_spec = pl.BlockSpec((1, LANES), fixed)
    widths = [(512, BF16), (128, BF16), (128, BF16), (256, BF16),
              (64, BF16), (128, F32), (512, BF16), (512, vg_dtype), (1024, BF16), (1024, BF16)]
    n_seq, seq = t // (tm * tiles_per_seq), tm * tiles_per_seq
    t_idx = lambda i: (i // tiles_per_seq, 0, i % tiles_per_seq)
    t_outs = [(LANES, F32), (LANES, F32), (IDX_DIM, F32), (LANES, BF16)]
    return pl.pallas_call(
        functools.partial(_proj_body, mod3d),
        grid=(t // tm,),
        in_specs=[pl.BlockSpec((tm, D_MODEL), row), mod_spec,
                  pl.BlockSpec((1, D_MODEL), fixed),
                  pl.BlockSpec((D_MODEL, D_IN_PAD), fixed),
                  tab_spec, tab_spec, tab_spec,
                  pl.BlockSpec((1, LANES), fixed), pl.BlockSpec((1, LANES), fixed), pl.BlockSpec((1, LANES), fixed),
                  pl.BlockSpec((1, GM_WIDTH), fixed), pl.BlockSpec((1, GM_WIDTH), fixed),
                  pl.BlockSpec((LANES, LANES), fixed), pl.BlockSpec((LANES, LANES), fixed)],
        out_specs=[pl.BlockSpec((tm, w), row) for w, _ in widths]
        + [pl.BlockSpec((1, f, tm), t_idx) for f, _ in t_outs],
        out_shape=[jax.ShapeDtypeStruct((t, w), dt) for w, dt in widths]
        + [jax.ShapeDtypeStruct((n_seq, f, seq), dt) for f, dt in t_outs],
        compiler_params=_cparams(("arbitrary",)),
        name="project",
    )(x, mod, g_mix, w_pad, c_t, sa_t, sb_t, qg, kg, kig, lng, lnb, seg, segki)


def _select_bias(s_ref, bias_ref, rows, width, topk, tie_check_start=16):
    nb = width // LANES
    kf = float(topk)
    neg, pos = -jnp.inf, jnp.inf

    def blk(j):
        return s_ref[:, j * LANES:(j + 1) * LANES]

    def count_above(t):
        tb = jnp.broadcast_to(t, (rows, LANES))
        acc = jnp.zeros((rows, LANES), F32)
        for j in range(nb):
            acc = acc + jnp.where(blk(j) > tb, 1.0, 0.0)
        return jnp.sum(acc, axis=1, keepdims=True)

    mx = jnp.full((rows, LANES), neg, F32)
    mn = jnp.full((rows, LANES), pos, F32)
    for j in range(nb):
        b = blk(j)
        mx = jnp.maximum(mx, b)
        mn = jnp.minimum(mn, jnp.where(b == neg, pos, b))
    hi0 = jnp.max(mx, axis=1, keepdims=True)
    smin = jnp.min(mn, axis=1, keepdims=True)
    lo0 = smin - jnp.abs(smin) - 1.0
    f_lo0 = count_above(lo0)
    zeros = jnp.zeros((rows, 1), F32)

    def active_of(f_lo, tie):
        return jnp.logical_and(f_lo > kf, tie == 0.0)

    def cond(st):
        _, _, _, f_lo, _, tie = st
        return jnp.max(jnp.where(active_of(f_lo, tie), 1.0, 0.0)) > 0.0

    def body(st):
        it, lo, hi, f_lo, f_hi, tie = st
        active = active_of(f_lo, tie)
        mid = lo + (hi - lo) * 0.5
        stuck = jnp.logical_or(mid <= lo, mid >= hi)
        cnt = count_above(mid)
        ge = cnt >= kf
        up_lo = jnp.logical_and(active, ge)
        up_hi = jnp.logical_and(active, jnp.logical_not(ge))
        lo = jnp.where(up_lo, mid, lo)
        f_lo = jnp.where(up_lo, cnt, f_lo)
        hi = jnp.where(up_hi, mid, hi)
        f_hi = jnp.where(up_hi, cnt, f_hi)
        tie = jnp.where(jnp.logical_and(active, stuck), 1.0, tie)

        def tie_check(_):
            lob = jnp.broadcast_to(lo, (rows, LANES))
            hib = jnp.broadcast_to(hi, (rows, LANES))
            vmx = jnp.full((rows, LANES), neg, F32)
            vmn = jnp.full((rows, LANES), pos, F32)
            for j in range(nb):
                b = blk(j)
                inn = jnp.logical_and(b > lob, b <= hib)
                vmx = jnp.maximum(vmx, jnp.where(inn, b, neg))
                vmn = jnp.minimum(vmn, jnp.where(inn, b, pos))
            one_value = jnp.max(vmx, axis=1, keepdims=True) == jnp.min(vmn, axis=1, keepdims=True)
            return jnp.where(one_value, 1.0, tie)

        run_check = jnp.logical_and(it >= tie_check_start, it % 4 == 0)
        tie = lax.cond(run_check, tie_check, lambda _: tie, 0)
        return it + 1, lo, hi, f_lo, f_hi, tie

    _, lo, hi, f_lo, f_hi, _ = lax.while_loop(cond, body, (jnp.int32(0), lo0, hi0, f_lo0, zeros, zeros))
    lob = jnp.broadcast_to(lo, (rows, LANES))
    need_prefix = jnp.max(jnp.where(f_lo > kf, 1.0, 0.0)) > 0.0

    @pl.when(jnp.logical_not(need_prefix))
    def _():
        for j in range(nb):
            bias_ref[:, j * LANES:(j + 1) * LANES] = jnp.where(blk(j) > lob, 0.0, neg)

    @pl.when(need_prefix)
    def _():
        hib = jnp.broadcast_to(hi, (rows, LANES))
        need = kf - f_hi
        ri = lax.broadcasted_iota(jnp.int32, (LANES, LANES), 0)
        ci = lax.broadcasted_iota(jnp.int32, (LANES, LANES), 1)
        upper = jnp.where(ri < ci, 1.0, 0.0).astype(BF16)
        off = jnp.zeros((rows, 1), F32)
        for j in range(nb):
            b = blk(j)
            inn = jnp.logical_and(b > lob, b <= hib)
            innf = jnp.where(inn, 1.0, 0.0)
            before = jnp.dot(innf.astype(BF16), upper, preferred_element_type=F32) + off
            sel = jnp.logical_or(b > hib, jnp.logical_and(inn, before < need))
            bias_ref[:, j * LANES:(j + 1) * LANES] = jnp.where(sel, 0.0, neg)
            off = off + jnp.sum(innf, axis=1, keepdims=True)


_NT = (((1,), (1,)), ((), ()))


SUB = 8


def _sub_reduce(x, op):
    for sh in (4, 2, 1):
        x = op(x, pltpu.roll(x, sh, 0))
    return x


def _select_bias_t(s_ref, bias_ref, width, topk, n_adm, steps_per_check=4, tie_check_from=4):
    rb = 64
    nb = width // rb
    kf = float(topk)
    neg, pos = -jnp.inf, jnp.inf

    def blk(j):
        return s_ref[j * rb:(j + 1) * rb, :]

    def fold(x, op):
        y = x[0:SUB]
        for a in range(1, rb // SUB):
            y = op(y, x[a * SUB:(a + 1) * SUB])
        return _sub_reduce(y, op)

    def tile(v):
        return jnp.concatenate([v] * (rb // SUB), axis=0)

    def count_above(t):
        tb = tile(t)
        acc = jnp.zeros((rb, LANES), F32)
        for j in range(nb):
            acc = acc + jnp.where(blk(j) > tb, 1.0, 0.0)
        return fold(acc, jnp.add)

    mx = jnp.full((rb, LANES), neg, F32)
    mn = jnp.full((rb, LANES), pos, F32)
    for j in range(nb):
        b = blk(j)
        mx = jnp.maximum(mx, b)
        mn = jnp.minimum(mn, jnp.where(b == neg, pos, b))
    hi0 = fold(mx, jnp.maximum)
    smin = fold(mn, jnp.minimum)
    lo0 = smin - jnp.abs(smin) - 1.0
    zeros = jnp.zeros((SUB, LANES), F32)

    def active_of(f_lo, tie):
        return jnp.logical_and(f_lo > kf, tie == 0.0)

    def any_lane(cond):
        return jnp.max(jnp.where(cond, 1.0, 0.0)) > 0.0

    def step(lo, hi, f_lo, f_hi, tie):
        active = active_of(f_lo, tie)
        mid = lo + (hi - lo) * 0.5
        stuck = jnp.logical_or(mid <= lo, mid >= hi)
        cnt = count_above(mid)
        ge = cnt >= kf
        up_lo = jnp.logical_and(active, ge)
        up_hi = jnp.logical_and(active, jnp.logical_not(ge))
        return (jnp.where(up_lo, mid, lo), jnp.where(up_hi, mid, hi), jnp.where(up_lo, cnt, f_lo),
                jnp.where(up_hi, cnt, f_hi), jnp.where(jnp.logical_and(active, stuck), 1.0, tie))

    def cond(st):
        _, _, _, f_lo, _, tie = st
        return any_lane(active_of(f_lo, tie))

    def body(st):
        it, lo, hi, f_lo, f_hi, tie = st
        for _ in range(steps_per_check):
            lo, hi, f_lo, f_hi, tie = step(lo, hi, f_lo, f_hi, tie)

        def tie_check(_):
            lob, hib = tile(lo), tile(hi)
            vmx = jnp.full((rb, LANES), neg, F32)
            vmn = jnp.full((rb, LANES), pos, F32)
            for j in range(nb):
                b = blk(j)
                inn = jnp.logical_and(b > lob, b <= hib)
                vmx = jnp.maximum(vmx, jnp.where(inn, b, neg))
                vmn = jnp.minimum(vmn, jnp.where(inn, b, pos))
            return jnp.where(fold(vmx, jnp.maximum) == fold(vmn, jnp.minimum), 1.0, tie)

        run_check = jnp.logical_and(it + 1 >= tie_check_from, any_lane(active_of(f_lo, tie)))
        tie = lax.cond(run_check, tie_check, lambda _: tie, 0)
        return it + 1, lo, hi, f_lo, f_hi, tie

    _, lo, hi, f_lo, f_hi, _ = lax.while_loop(cond, body, (jnp.int32(0), lo0, hi0, n_adm, zeros, zeros))
    lob = tile(lo)
    need_prefix = any_lane(f_lo > kf)

    @pl.when(jnp.logical_not(need_prefix))
    def _():
        for j in range(nb):
            bias_ref[j * rb:(j + 1) * rb, :] = jnp.where(blk(j) > lob, 0.0, neg)

    @pl.when(need_prefix)
    def _():
        need = (kf - f_hi)[0:1]
        lo1, hi1 = lo[0:1], hi[0:1]
        ri = lax.broadcasted_iota(jnp.int32, (LANES, LANES), 0)
        ci = lax.broadcasted_iota(jnp.int32, (LANES, LANES), 1)
        lower = jnp.where(ci < ri, 1.0, 0.0).astype(BF16)
        off = jnp.zeros((1, LANES), F32)
        for j in range(width // LANES):
            b = s_ref[j * LANES:(j + 1) * LANES, :]
            inn = jnp.logical_and(b > lo1, b <= hi1)
            innf = jnp.where(inn, 1.0, 0.0)
            before = jnp.dot(lower, innf.astype(BF16), preferred_element_type=F32) + off
            sel = jnp.logical_or(b > hi1, jnp.logical_and(inn, before < need))
            bias_ref[j * LANES:(j + 1) * LANES, :] = jnp.where(sel, 0.0, neg)
            off = off + jnp.sum(innf, axis=0, keepdims=True)


def _prompt_attn_body(blk_i, topk, qi_ref, kiwi_ref, kib_ref, q_ref, k_ref, vt_ref, o_ref, s_ref, bias_ref):
    width = (blk_i + 1) * Q_BLOCK
    neg = -jnp.inf
    kiwi_t = kiwi_ref[0].T
    chunk = 512
    for c0 in range(0, width, chunk):
        c1 = min(width, c0 + chunk)
        kib = kib_ref[0, c0:c1, :]
        sc = None
        for h in range(IDX_HEADS):
            d = lax.dot_general(kib, qi_ref[0, :, h * IDX_DIM:(h + 1) * IDX_DIM], _NT,
                                preferred_element_type=F32)
            t = jnp.maximum(d, 0.0) * kiwi_t[IDX_DIM + h:IDX_DIM + h + 1, :]
            sc = t if sc is None else sc + t
        s_ref[c0:c1, :] = sc
    ki = lax.broadcasted_iota(jnp.int32, (Q_BLOCK, Q_BLOCK), 0)
    qj = lax.broadcasted_iota(jnp.int32, (Q_BLOCK, Q_BLOCK), 1)
    d0 = width - Q_BLOCK
    s_ref[d0:width, :] = jnp.where(ki <= qj, s_ref[d0:width, :], neg)
    if width > topk:
        n_adm = (lax.broadcasted_iota(jnp.int32, (SUB, LANES), 1) + (d0 + 1)).astype(F32)
        _select_bias_t(s_ref, bias_ref, width, topk, n_adm)
    else:
        bias_ref[...] = jnp.where(s_ref[...] == neg, neg, 0.0)
    lane = lax.broadcasted_iota(jnp.int32, (Q_BLOCK, LANES), 1)
    hpg = N_HEADS // N_KV_HEADS
    for h in range(N_HEADS):
        g = h // hpg
        qp = q_ref[0, :, (h // 2) * LANES:(h // 2 + 1) * LANES].astype(F32)
        if h % 2 != g:
            qp = pltpu.roll(qp, HEAD_DIM, 1)
        q2 = jnp.where((lane >= HEAD_DIM) == (g == 1), qp, 0.0).astype(BF16)
        s = lax.dot_general(k_ref[0], q2, _NT, preferred_element_type=F32) + bias_ref[...]
        m = jnp.max(s, axis=0, keepdims=True)
        p = jnp.exp(s - m)
        l = jnp.sum(p, axis=0, keepdims=True)
        o_t = jnp.dot(vt_ref[0, g * HEAD_DIM:(g + 1) * HEAD_DIM, :], p.astype(BF16), preferred_element_type=F32)
        o_ref[0, :, h * HEAD_DIM:(h + 1) * HEAD_DIM] = (o_t / l).T.astype(BF16)


def _prompt_attention(qi, kiwi, kib, q, kb, vt, topk):
    b, s, _ = q.shape
    outs = []
    for i in range(s // Q_BLOCK):
        width = (i + 1) * Q_BLOCK
        qblk = lambda bb, i=i: (bb, i, 0)
        kall = lambda bb: (bb, 0, 0)
        outs.append(pl.pallas_call(
            functools.partial(_prompt_attn_body, i, topk),
            grid=(b,),
            in_specs=[pl.BlockSpec((1, Q_BLOCK, IDX_HEADS * IDX_DIM), qblk),
                      pl.BlockSpec((1, Q_BLOCK, LANES), qblk),
                      pl.BlockSpec((1, width, IDX_DIM), kall),
                      pl.BlockSpec((1, Q_BLOCK, N_HEADS * HEAD_DIM), qblk),
                      pl.BlockSpec((1, width, N_KV_HEADS * HEAD_DIM), kall),
                      pl.BlockSpec((1, N_KV_HEADS * HEAD_DIM, width), kall)],
            out_specs=pl.BlockSpec((1, Q_BLOCK, N_HEADS * HEAD_DIM), lambda bb: (bb, 0, 0)),
            out_shape=jax.ShapeDtypeStruct((b, Q_BLOCK, N_HEADS * HEAD_DIM), BF16),
            scratch_shapes=[pltpu.VMEM((width, Q_BLOCK), F32), pltpu.VMEM((width, Q_BLOCK), F32)],
            compiler_params=_cparams(("arbitrary",)),
            name=f"prompt_attn_{i}",
        )(qi, kiwi, kib, q, kb, vt))
    return jnp.stack(outs)


def _page_copies(pt_ref, sample, src_hbm, buf, slot, sem, n_pages):
    return [pltpu.make_async_copy(src_hbm.at[pt_ref[sample, p]],
                                  buf.at[slot, :, pl.ds(p * PAGE_SIZE, PAGE_SIZE)], sem.at[slot])
            for p in range(n_pages)]


def _sample_scores_body(n_pages, pt_ref, kidx_hbm, qi_ref, wi_ref, kin_ref, o_ref, buf, sem):
    s = pl.program_id(0)
    slot = s % 2

    @pl.when(s == 0)
    def _():
        for cp in _page_copies(pt_ref, 0, kidx_hbm, buf, 0, sem, n_pages):
            cp.start()

    @pl.when(s + 1 < pl.num_programs(0))
    def _():
        for cp in _page_copies(pt_ref, s + 1, kidx_hbm, buf, 1 - slot, sem, n_pages):
            cp.start()

    for cp in _page_copies(pt_ref, s, kidx_hbm, buf, slot, sem, n_pages):
        cp.wait()
    qi = qi_ref[0]
    wi = wi_ref[0]
    ki_t = buf[slot].astype(BF16)
    d = jnp.dot(qi, ki_t, preferred_element_type=F32)
    past = jnp.sum(jnp.maximum(d, 0.0) * wi, axis=0, keepdims=True)
    dn = jnp.sum(qi.astype(F32) * kin_ref[0].astype(F32), axis=1, keepdims=True)
    new = jnp.sum(jnp.maximum(dn, 0.0) * wi, axis=0, keepdims=True)
    lane = lax.broadcasted_iota(jnp.int32, (1, LANES), 1)
    tail = jnp.where(lane == 0, jnp.broadcast_to(new, (1, LANES)), -jnp.inf)
    o_ref[0] = jnp.concatenate([past, tail], axis=1)


def _sample_scores(page_table, kidx_pool, qi8, wi8, ki_new):
    n, n_pages = page_table.shape
    past = n_pages * PAGE_SIZE
    grid_spec = pltpu.PrefetchScalarGridSpec(
        num_scalar_prefetch=1,
        grid=(n,),
        in_specs=[pl.BlockSpec(memory_space=pl.ANY),
                  pl.BlockSpec((1, 8, IDX_DIM), lambda s, pt: (s, 0, 0)),
                  pl.BlockSpec((1, 8, 1), lambda s, pt: (s, 0, 0)),
                  pl.BlockSpec((1, 1, IDX_DIM), lambda s, pt: (s, 0, 0))],
        out_specs=pl.BlockSpec((1, 1, past + LANES), lambda s, pt: (s, 0, 0)),
        scratch_shapes=[pltpu.VMEM((2, IDX_DIM, past), F32), pltpu.SemaphoreType.DMA((2,))],
    )
    return pl.pallas_call(
        functools.partial(_sample_scores_body, n_pages),
        grid_spec=grid_spec,
        out_shape=jax.ShapeDtypeStruct((n, 1, past + LANES), F32),
        compiler_params=_cparams(("arbitrary",)),
        name="sample_scores",
    )(page_table, kidx_pool, qi8, wi8, ki_new)


def _sample_select_body(topk, s_ref, bias_ref):
    rows, width = s_ref.shape
    _select_bias(s_ref, bias_ref, rows, width, topk)


def _sample_select(scores, topk):
    return pl.pallas_call(
        functools.partial(_sample_select_body, topk),
        out_shape=jax.ShapeDtypeStruct(scores.shape, F32),
        compiler_params=pltpu.CompilerParams(vmem_limit_bytes=VMEM_LIMIT),
        name="sample_select",
    )(scores)


def _sample_attn_body(n_pages, pt_ref, k_hbm, v_hbm, q_ref, bias_ref, kn_ref, vn_ref, o_ref, kbuf, vbuf, sem):
    s = pl.program_id(0)
    slot = s % 2
    past = n_pages * PAGE_SIZE

    def copies(sample, sl):
        return (_page_copies(pt_ref, sample, k_hbm, kbuf, sl, sem.at[0], n_pages)
                + _page_copies(pt_ref, sample, v_hbm, vbuf, sl, sem.at[1], n_pages))

    @pl.when(s == 0)
    def _():
        for cp in copies(0, 0):
            cp.start()

    @pl.when(s + 1 < pl.num_programs(0))
    def _():
        for cp in copies(s + 1, 1 - slot):
            cp.start()

    for cp in copies(s, slot):
        cp.wait()
    q2 = q_ref[0]
    k_t = kbuf[slot].astype(BF16)
    v_t = vbuf[slot].astype(BF16)
    row = lax.broadcasted_iota(jnp.int32, (N_HEADS, 1), 0)
    first = row < (N_HEADS // N_KV_HEADS)
    bias = bias_ref[0]
    sc = jnp.dot(q2, k_t, preferred_element_type=F32) + bias[:, 0:past]
    sn = jnp.sum(q2.astype(F32) * kn_ref[0].astype(F32), axis=1, keepdims=True) + bias[:, past:past + 1]
    m = jnp.maximum(jnp.max(sc, axis=1, keepdims=True), sn)
    p = jnp.exp(sc - m)
    pn = jnp.exp(sn - m)
    l = jnp.sum(p, axis=1, keepdims=True) + pn
    o2 = lax.dot_general(p.astype(BF16), v_t, _NT, preferred_element_type=F32)
    o2 = o2 + pn.astype(BF16).astype(F32) * vn_ref[0].astype(F32)
    o = jnp.where(first, o2[:, 0:HEAD_DIM], o2[:, HEAD_DIM:2 * HEAD_DIM])
    o_ref[0] = (o / l).astype(BF16)


def _sample_attention(page_table, k_pool, v_pool, q8, bias, k_new, v_new):
    n, n_pages = page_table.shape
    past = n_pages * PAGE_SIZE
    kvw = N_KV_HEADS * HEAD_DIM
    per = lambda s, pt: (s, 0, 0)
    grid_spec = pltpu.PrefetchScalarGridSpec(
        num_scalar_prefetch=1,
        grid=(n,),
        in_specs=[pl.BlockSpec(memory_space=pl.ANY), pl.BlockSpec(memory_space=pl.ANY),
                  pl.BlockSpec((1, N_HEADS, kvw), per),
                  pl.BlockSpec((1, 1, past + LANES), per),
                  pl.BlockSpec((1, 1, kvw), per),
                  pl.BlockSpec((1, 1, kvw), per)],
        out_specs=pl.BlockSpec((1, N_HEADS, HEAD_DIM), per),
        scratch_shapes=[pltpu.VMEM((2, kvw, past), F32), pltpu.VMEM((2, kvw, past), F32),
                        pltpu.SemaphoreType.DMA((2, 2))],
    )
    return pl.pallas_call(
        functools.partial(_sample_attn_body, n_pages),
        grid_spec=grid_spec,
        out_shape=jax.ShapeDtypeStruct((n, N_HEADS, HEAD_DIM), BF16),
        compiler_params=_cparams(("arbitrary",)),
        name="sample_attn",
    )(page_table, k_pool, v_pool, q8, bias, k_new, v_new)


def _gmlp_body(n_chunks, u_ref, vg_ref, w_ref, bt_ref, o_ref):
    ri = lax.broadcasted_iota(jnp.int32, (CHUNK, CHUNK), 0)
    ci = lax.broadcasted_iota(jnp.int32, (CHUNK, CHUNK), 1)
    gd = GM_WIDTH // GM_GROUPS
    for g in range(GM_GROUPS):
        wg = jnp.where(ci <= ri, w_ref[g], 0.0).astype(BF16)
        bg = bt_ref[:, g:g + 1]
        for c in range(n_chunks):
            rows = slice(c * CHUNK, (c + 1) * CHUNK)
            cols = slice(g * gd, (g + 1) * gd)
            mixed = jnp.dot(wg, vg_ref[rows, cols], preferred_element_type=F32) + bg
            o_ref[rows, cols] = (u_ref[rows, cols].astype(F32) * mixed).astype(BF16)


def _gmlp_prompt(u, vg, w, bt, tm):
    t = u.shape[0]
    row = lambda i: (i, 0)
    return pl.pallas_call(
        functools.partial(_gmlp_body, tm // CHUNK),
        grid=(t // tm,),
        in_specs=[pl.BlockSpec((tm, GM_WIDTH), row), pl.BlockSpec((tm, GM_WIDTH), row),
                  pl.BlockSpec((GM_GROUPS, CHUNK, CHUNK), lambda i: (0, 0, 0)),
                  pl.BlockSpec((CHUNK, LANES), lambda i: (0, 0))],
        out_specs=pl.BlockSpec((tm, GM_WIDTH), row),
        out_shape=jax.ShapeDtypeStruct((t, GM_WIDTH), BF16),
        compiler_params=_cparams(("arbitrary",)),
        name="gmlp",
    )(u, vg, w, bt)


def _gmlp_first_row_body(u_ref, vg_ref, w0_ref, b0_ref, o_ref):
    o_ref[...] = (u_ref[...].astype(F32) * (vg_ref[...] * w0_ref[...] + b0_ref[...])).astype(BF16)


def _gmlp_sample(u, vg, w0, b0):
    return pl.pallas_call(
        _gmlp_first_row_body,
        out_shape=jax.ShapeDtypeStruct(u.shape, BF16),
        name="gmlp_first_row",
    )(u, vg, w0, b0)


def _merge_body(mod3d, attn_ref, gm_ref, ga_ref, gb_ref, x_ref, mod_ref, wpa_ref, wpg_ref, wo_ref, g2_ref,
                wr_ref, br_ref, x1_o, h2_o, gate_o):
    tm = x_ref.shape[0]
    attn = attn_ref[...].reshape(tm, N_HEADS * HEAD_DIM)
    a = jnp.dot(attn, wpa_ref[...], preferred_element_type=F32)
    g = jnp.dot(gm_ref[...], wpg_ref[...], preferred_element_type=F32)
    merged = ga_ref[...].astype(F32) * a + gb_ref[...].astype(F32) * g
    out = jnp.dot(merged.astype(BF16), wo_ref[...], preferred_element_type=F32)
    m = mod_ref[0] if mod3d else mod_ref[...]
    x1 = x_ref[...] + m[:, 2 * D_MODEL:3 * D_MODEL] * out
    x1_o[...] = x1
    ms = jnp.mean(x1 * x1, axis=-1, keepdims=True)
    y = x1 * lax.rsqrt(ms + EPS) * g2_ref[...]
    h2 = y * (1.0 + m[:, 4 * D_MODEL:5 * D_MODEL]) + m[:, 3 * D_MODEL:4 * D_MODEL]
    hi = h2.astype(BF16)
    h2_o[...] = hi
    lo = (h2 - hi.astype(F32)).astype(BF16)
    r = jnp.dot(hi, wr_ref[...], preferred_element_type=F32) + jnp.dot(lo, wr_ref[...], preferred_element_type=F32)
    logits = r[:, 0:LANES] + r[:, LANES:2 * LANES] + br_ref[...]
    neg = -jnp.inf
    big = jnp.int32(1 << 20)
    lane = lax.broadcasted_iota(jnp.int32, logits.shape, 1)
    is_g = jnp.logical_and(lane >= N_EXPERTS, lane < N_EXPERTS + N_EXPERT_GROUPS)
    gl = jnp.where(is_g, logits, neg)
    gmax = jnp.max(gl, axis=1, keepdims=True)
    g_lane = jnp.min(jnp.where(gl == gmax, lane, big), axis=1, keepdims=True)
    g_w = 1.0 / jnp.sum(jnp.exp(gl - gmax), axis=1, keepdims=True)
    g_sel = g_lane - N_EXPERTS
    in_grp = jnp.logical_and(lane < N_EXPERTS, (lane >> 3) == g_sel)
    el = jnp.where(in_grp, logits, neg)
    m1 = jnp.max(el, axis=1, keepdims=True)
    i1 = jnp.min(jnp.where(el == m1, lane, big), axis=1, keepdims=True)
    el2 = jnp.where(lane == i1, neg, el)
    m2 = jnp.max(el2, axis=1, keepdims=True)
    i2 = jnp.min(jnp.where(el2 == m2, lane, big), axis=1, keepdims=True)
    e2 = jnp.exp(m2 - m1)
    w1 = 1.0 / (1.0 + e2)
    w2 = e2 / (1.0 + e2)
    gate_o[...] = jnp.where(lane == i1, g_w * w1, 0.0) + jnp.where(lane == i2, g_w * w2, 0.0)


def _merge(attn4, gm, ga, gb, x, mod, wpa, wpg, wo, g2, wr, br, tm, tiles_per_seq):
    t = x.shape[0]
    mod3d = mod.ndim == 3
    nb = tm // Q_BLOCK
    row = lambda i: (i, 0)
    fixed = lambda i: (0, 0)
    if mod3d:
        mod_spec = pl.BlockSpec((1, 1, 6 * D_MODEL), lambda i: (i // tiles_per_seq, 0, 0))
    else:
        mod_spec = pl.BlockSpec((tm, 6 * D_MODEL), row)
    attn_spec = pl.BlockSpec((nb, 1, Q_BLOCK, N_HEADS * HEAD_DIM),
                             lambda i: (i % tiles_per_seq, i // tiles_per_seq, 0, 0))
    return pl.pallas_call(
        functools.partial(_merge_body, mod3d),
        grid=(t // tm,),
        in_specs=[attn_spec, pl.BlockSpec((tm, GM_WIDTH), row),
                  pl.BlockSpec((tm, D_MODEL), row), pl.BlockSpec((tm, D_MODEL), row),
                  pl.BlockSpec((tm, D_MODEL), row), mod_spec,
                  pl.BlockSpec((N_HEADS * HEAD_DIM, D_MODEL), fixed), pl.BlockSpec((GM_WIDTH, D_MODEL), fixed),
                  pl.BlockSpec((D_MODEL, D_MODEL), fixed), pl.BlockSpec((1, D_MODEL), fixed),
                  pl.BlockSpec((D_MODEL, 2 * LANES), fixed), pl.BlockSpec((1, LANES), fixed)],
        out_specs=[pl.BlockSpec((tm, D_MODEL), row), pl.BlockSpec((tm, D_MODEL), row),
                   pl.BlockSpec((tm, LANES), row)],
        out_shape=[jax.ShapeDtypeStruct((t, D_MODEL), F32), jax.ShapeDtypeStruct((t, D_MODEL), BF16),
                   jax.ShapeDtypeStruct((t, LANES), F32)],
        compiler_params=_cparams(("arbitrary",)),
        name="merge",
    )(attn4, gm, ga, gb, x, mod, wpa, wpg, wo, g2, wr, br)


def _moe_body(mod3d, h_ref, gate_ref, x1_ref, mod_ref, wgu_ref, wd_ref, o_ref, acc_ref):
    e = pl.program_id(1)

    @pl.when(e == 0)
    def _():
        acc_ref[...] = jnp.zeros_like(acc_ref)

    gu = jnp.dot(h_ref[...], wgu_ref[0], preferred_element_type=F32)
    a = gu[:, 0:D_EXPERT]
    hid = a * jax.nn.sigmoid(a) * gu[:, D_EXPERT:2 * D_EXPERT]
    gate = gate_ref[...]
    lane = lax.broadcasted_iota(jnp.int32, gate.shape, 1)
    ge = jnp.sum(jnp.where(lane == e, gate, 0.0), axis=1, keepdims=True)
    acc_ref[...] += jnp.dot((hid * ge).astype(BF16), wd_ref[0], preferred_element_type=F32)

    @pl.when(e == pl.num_programs(1) - 1)
    def _():
        m = mod_ref[0] if mod3d else mod_ref[...]
        o_ref[...] = x1_ref[...] + m[:, 5 * D_MODEL:6 * D_MODEL] * acc_ref[...]


def _moe(h2, gate, x1, mod, wgu, wd, tm, tiles_per_seq):
    t = h2.shape[0]
    mod3d = mod.ndim == 3
    row = lambda i, e: (i, 0)
    if mod3d:
        mod_spec = pl.BlockSpec((1, 1, 6 * D_MODEL), lambda i, e: (i // tiles_per_seq, 0, 0))
    else:
        mod_spec = pl.BlockSpec((tm, 6 * D_MODEL), row)
    return pl.pallas_call(
        functools.partial(_moe_body, mod3d),
        grid=(t // tm, N_EXPERTS),
        in_specs=[pl.BlockSpec((tm, D_MODEL), row), pl.BlockSpec((tm, LANES), row),
                  pl.BlockSpec((tm, D_MODEL), row), mod_spec,
                  pl.BlockSpec((1, D_MODEL, 2 * D_EXPERT), lambda i, e: (e, 0, 0)),
                  pl.BlockSpec((1, D_EXPERT, D_MODEL), lambda i, e: (e, 0, 0))],
        out_specs=pl.BlockSpec((tm, D_MODEL), row),
        out_shape=jax.ShapeDtypeStruct((t, D_MODEL), F32),
        scratch_shapes=[pltpu.VMEM((tm, D_MODEL), F32)],
        compiler_params=_cparams(("arbitrary", "arbitrary")),
        name="moe",
    )(h2, gate, x1, mod, wgu, wd)


def _pad_lanes(v, fill):
    n = v.shape[-1]
    return jnp.concatenate([v, jnp.full((LANES - n,), fill, v.dtype)]).reshape(1, LANES)


def kernel(x_prompt, x_sample, c_prompt, c_sample, cache_k, cache_v, cache_kidx, page_table, w_ada, b_ada, norm_mix_g, norm_ffn_g, w_in, q_norm_g, k_norm_g, kidx_norm_g, gm_ln_g, gm_ln_b, gm_spatial_w, gm_spatial_b, w_proj_attn, w_proj_gmlp, w_out, w_router_group, b_router_group, w_router_expert, b_router_expert, w_expert_gate, w_expert_up, w_expert_down):
    depth = w_ada.shape[0]
    assert depth == 1
    l = 0
    bp, sp, _ = x_prompt.shape
    bs, ss, _ = x_sample.shape
    assert ss == 1
    n_pages = page_table.shape[1]
    past = n_pages * PAGE_SIZE
    tp = bp * sp

    w = w_in[l]
    zpad = jnp.zeros((D_MODEL, LANES - IDX_DIM - IDX_HEADS), F32)
    w_pad = jnp.concatenate([w[:, 0:1024], w[:, 1024:1088], w[:, 1088:1092], zpad, w[:, 1092:]], axis=1).astype(BF16)
    seg_np = (np.arange(LANES)[:, None] // HEAD_DIM) == (np.arange(LANES)[None, :] // HEAD_DIM)
    seg = jnp.asarray(seg_np, BF16)
    segki = jnp.asarray(seg_np & (np.arange(LANES)[:, None] < IDX_DIM) & (np.arange(LANES)[None, :] < IDX_DIM), BF16)
    consts = (norm_mix_g[l].reshape(1, D_MODEL),
              jnp.tile(q_norm_g[l], 2).reshape(1, LANES), jnp.tile(k_norm_g[l], 2).reshape(1, LANES),
              _pad_lanes(kidx_norm_g[l], 1.0),
              gm_ln_g[l].reshape(1, GM_WIDTH), gm_ln_b[l].reshape(1, GM_WIDTH), seg, segki)
    wpa = w_proj_attn[l].astype(BF16)
    wpg = w_proj_gmlp[l].astype(BF16)
    wo = w_out[l].astype(BF16)
    wr32 = jnp.concatenate([w_router_expert[l], w_router_group[l],
                            jnp.zeros((D_MODEL, LANES - N_EXPERTS - N_EXPERT_GROUPS), F32)], axis=1)
    wr_hi = wr32.astype(BF16)
    wr_lo = (wr32 - wr_hi.astype(F32)).astype(BF16)
    wr = jnp.concatenate([wr_hi, wr_lo], axis=1)
    br = _pad_lanes(jnp.concatenate([b_router_expert[l], b_router_group[l]]), 0.0)
    wgu = jnp.concatenate([w_expert_gate[l], w_expert_up[l]], axis=2).astype(BF16)
    wd = w_expert_down[l].astype(BF16)
    g2 = norm_ffn_g[l].reshape(1, D_MODEL)

    mod = _adaln(jnp.concatenate([c_prompt, c_sample], axis=0), w_ada[l], b_ada[l])
    mod_p = mod[0:bp].reshape(bp, 1, 6 * D_MODEL)
    mod_s = mod[bp:bp + bs]
    pos = jnp.concatenate([jnp.arange(sp, dtype=jnp.int32),
                           jnp.full((8,), past, jnp.int32)]).astype(F32).reshape(sp + 8, 1)
    tabs = _rope_tables(pos)
    tabs_p = tuple(t[0:sp] for t in tabs)
    tabs_s = tuple(t[sp:sp + 1] for t in tabs)

    tm = 512
    tps = sp // tm
    (q, kb, vb, qi, kib, kiwi, u, vg, ga, gb, kt_f, vt_f, kit_f, vt) = _project(
        x_prompt.reshape(tp, D_MODEL), mod_p, tabs_p, consts, w_pad, tm, tps, BF16)
    topk_p = min(TOPK_MAX, sp // 4)
    r3 = lambda a: a.reshape(bp, sp, a.shape[-1])
    attn_p = _prompt_attention(r3(qi), r3(kiwi), r3(kib), r3(q), r3(kb), vt, topk_p)
    bt = jnp.concatenate([gm_spatial_b[l].T, jnp.zeros((CHUNK, LANES - GM_GROUPS), F32)], axis=1)
    gm_p = _gmlp_prompt(u, vg, gm_spatial_w[l], bt, tm)
    x1_p, h2_p, gate_p = _merge(attn_p, gm_p, ga, gb, x_prompt.reshape(tp, D_MODEL), mod_p,
                                wpa, wpg, wo, g2, wr, br, tm, tps)
    y_p = _moe(h2_p, gate_p, x1_p, mod_p, wgu, wd, 1024, sp // 1024)

    (q_s, kb_s, vb_s, qi_s, kib_s, kiwi_s, u_s, vg_s, ga_s, gb_s, kt_s, vt_s, kit_s, _) = _project(
        x_sample.reshape(bs, D_MODEL), mod_s, tabs_s, consts, w_pad, bs, 1, F32)
    qi8 = jnp.concatenate([qi_s.reshape(bs, IDX_HEADS, IDX_DIM),
                           jnp.zeros((bs, 8 - IDX_HEADS, IDX_DIM), BF16)], axis=1)
    wi8 = jnp.concatenate([kiwi_s[:, IDX_DIM:IDX_DIM + IDX_HEADS],
                           jnp.zeros((bs, 8 - IDX_HEADS), F32)], axis=1).reshape(bs, 8, 1)
    kidx_t = jnp.transpose(cache_kidx[l], (0, 2, 1))
    scores = _sample_scores(page_table, kidx_t, qi8, wi8, kib_s.reshape(bs, 1, IDX_DIM))
    topk_s = min(TOPK_MAX, (past + ss) // 4)
    bias = _sample_select(scores.reshape(bs, past + LANES), topk_s).reshape(bs, 1, past + LANES)
    kvw = N_KV_HEADS * HEAD_DIM
    k_t = jnp.transpose(cache_k[l], (0, 2, 3, 1)).reshape(-1, kvw, PAGE_SIZE)
    v_t = jnp.transpose(cache_v[l], (0, 2, 3, 1)).reshape(-1, kvw, PAGE_SIZE)
    q3 = q_s.reshape(bs, N_HEADS, HEAD_DIM)
    zq = jnp.zeros_like(q3)
    in_first = (jnp.arange(N_HEADS) < N_HEADS // N_KV_HEADS)[None, :, None]
    q2_s = jnp.where(in_first, jnp.concatenate([q3, zq], axis=2), jnp.concatenate([zq, q3], axis=2))
    attn_s = _sample_attention(page_table, k_t, v_t, q2_s, bias,
                               kb_s.reshape(bs, 1, kvw), vb_s.reshape(bs, 1, kvw))
    gd = GM_WIDTH // GM_GROUPS
    w0 = jnp.repeat(gm_spatial_w[l][:, 0, 0], gd).reshape(1, GM_WIDTH)
    b0 = jnp.repeat(gm_spatial_b[l][:, 0], gd).reshape(1, GM_WIDTH)
    gm_s = _gmlp_sample(u_s, vg_s, w0, b0)
    x1_s, h2_s, gate_s = _merge(attn_s.reshape(1, 1, bs, N_HEADS * HEAD_DIM), gm_s, ga_s, gb_s,
                                x_sample.reshape(bs, D_MODEL), mod_s, wpa, wpg, wo, g2, wr, br, bs, 1)
    y_s = _moe(h2_s, gate_s, x1_s, mod_s, wgu, wd, bs, 1)

    def rows_kv(a_t, n, s):
        return jnp.transpose(a_t.reshape(n, N_KV_HEADS, HEAD_DIM, s), (0, 3, 1, 2))[None]

    def rows_ki(a_t):
        return jnp.transpose(a_t, (0, 2, 1))[None]

    return (y_p.reshape(bp, sp, D_MODEL), y_s.reshape(bs, ss, D_MODEL),
            rows_kv(kt_f, bp, sp), rows_kv(vt_f, bp, sp), rows_ki(kit_f),
            rows_kv(kt_s, 1, bs).reshape(1, bs, ss, N_KV_HEADS, HEAD_DIM),
            rows_kv(vt_s, 1, bs).reshape(1, bs, ss, N_KV_HEADS, HEAD_DIM),
            rows_ki(kit_s).reshape(1, bs, ss, IDX_DIM), vg_s.reshape(1, bs, ss, GM_WIDTH))
```

```python
import functools

import numpy as np
import jax
import jax.numpy as jnp
from jax import lax
from jax.experimental import pallas as pl
from jax.experimental.pallas import tpu as pltpu

F32 = jnp.float32
BF16 = jnp.bfloat16
U32 = jnp.uint32

D_MODEL = 1024
N_HEADS = 8
HEAD_DIM = 64
N_KV_HEADS = 2
ROT_DIM = 16
ROPE_THETA = 500000.0
IDX_HEADS = 4
IDX_DIM = 64
TOPK_MAX = 256
Q_BLOCK = 128
GM_WIDTH = 512
GM_GROUPS = 8
CHUNK = 128
N_EXPERT_GROUPS = 4
EXPERTS_PER_GROUP = 8
N_EXPERTS = 32
D_EXPERT = 256
EPS = 1e-6
PAGE_SIZE = 128
LANES = 128
SUB = 8

C_Q, C_K, C_V, C_QI, C_KIWI, C_U, C_VG, C_GATE = 0, 512, 640, 768, 1024, 1152, 1664, 2176
D_IN_PAD = 4224
VMEM_LIMIT = 56 * 1024 * 1024

N_PAIRS = EXPERTS_PER_GROUP * (EXPERTS_PER_GROUP - 1) // 2
N_CLASSES = N_EXPERT_GROUPS * N_PAIRS
ROW_TILE = 128
HALF = D_MODEL // 2
ROW_WORDS = HALF + LANES
MOVE_TILE = 256
HI_MASK = 0xFFFF0000


def _cparams(sem):
    return pltpu.CompilerParams(dimension_semantics=sem, vmem_limit_bytes=VMEM_LIMIT)


def _pack_halves(x):
    n = x.shape[1] // 2
    hi = lax.bitcast_convert_type(x[:, 0:n], U32) & jnp.uint32(HI_MASK)
    lo = lax.bitcast_convert_type(x[:, n:2 * n], U32) >> 16
    return hi | lo


def _unpack_halves(u):
    hi = lax.bitcast_convert_type(u & jnp.uint32(HI_MASK), F32)
    lo = lax.bitcast_convert_type(u << 16, F32)
    return jnp.concatenate([hi, lo], axis=1)


def _adaln_body(c_ref, w_ref, b_ref, o_ref):
    c = c_ref[...]
    a = c * jax.nn.sigmoid(c)
    o_ref[...] = jnp.dot(a, w_ref[...], preferred_element_type=F32,
                         precision=lax.Precision.HIGHEST) + b_ref[...]


def _adaln(c, w, b):
    r = c.shape[0]
    n = w.shape[1]
    bn = 1536
    return pl.pallas_call(
        _adaln_body,
        grid=(n // bn,),
        in_specs=[pl.BlockSpec((r, D_MODEL), lambda j: (0, 0)),
                  pl.BlockSpec((D_MODEL, bn), lambda j: (0, j)),
                  pl.BlockSpec((1, bn), lambda j: (0, j))],
        out_specs=pl.BlockSpec((r, bn), lambda j: (0, j)),
        out_shape=jax.ShapeDtypeStruct((r, n), F32),
        compiler_params=_cparams(("arbitrary",)),
        name="adaln",
    )(c, w, b.reshape(1, n))


def _rope_table_body(pos_ref, invf_ref, sa_m_ref, sb_m_ref, c_ref, sa_ref, sb_ref):
    ang = pos_ref[...] * invf_ref[...]
    s = jnp.sin(ang)
    c_ref[...] = jnp.cos(ang)
    sa_ref[...] = s * sa_m_ref[...]
    sb_ref[...] = s * sb_m_ref[...]


def _rope_tables(pos):
    half = ROT_DIM // 2
    inv_freq = ROPE_THETA ** (-jnp.arange(half, dtype=F32) / half)
    d = np.arange(LANES) % HEAD_DIM
    invf = jnp.where(jnp.asarray(d < ROT_DIM), inv_freq[d % half], 0.0).reshape(1, LANES)
    sa_m = jnp.asarray(np.where(d < half, -1.0, 0.0), F32).reshape(1, LANES)
    sb_m = jnp.asarray(np.where((d >= half) & (d < ROT_DIM), 1.0, 0.0), F32).reshape(1, LANES)
    r = pos.shape[0]
    return pl.pallas_call(
        _rope_table_body,
        out_shape=[jax.ShapeDtypeStruct((r, LANES), F32)] * 3,
        name="rope_tables",
    )(pos, invf, sa_m, sb_m)


def _rope(y, c, sa, sb):
    return y * c + pltpu.roll(y, LANES - ROT_DIM // 2, 1) * sa + pltpu.roll(y, ROT_DIM // 2, 1) * sb


def _seg_rms(r, seg, gain):
    ss = r * r
    hi = ss.astype(BF16)
    lo = (ss - hi.astype(F32)).astype(BF16)
    tot = jnp.dot(hi, seg, preferred_element_type=F32) + jnp.dot(lo, seg, preferred_element_type=F32)
    return r * lax.rsqrt(tot * (1.0 / HEAD_DIM) + EPS) * gain


def _proj_body(mod3d, x_ref, mod_ref, g_ref, w_ref, c_ref, sa_ref, sb_ref, qg_ref, kg_ref, kig_ref,
               lng_ref, lnb_ref, seg_ref, segki_ref,
               q_o, kb_o, vb_o, qi_o, kib_o, kiwi_o, u_o, vg_o, ga_o, gb_o, kt_o, vtf_o, kit_o, vt_o):
    x = x_ref[...]
    m = mod_ref[0] if mod3d else mod_ref[...]
    shift1 = m[:, 0:D_MODEL]
    scale1 = m[:, D_MODEL:2 * D_MODEL]
    ms = jnp.mean(x * x, axis=-1, keepdims=True)
    y = x * lax.rsqrt(ms + EPS) * g_ref[...]
    h = (y * (1.0 + scale1) + shift1).astype(BF16)
    c, sa, sb = c_ref[...], sa_ref[...], sb_ref[...]
    seg = seg_ref[...]

    def proj(a, b):
        return jnp.dot(h, w_ref[:, a:b], preferred_element_type=F32)

    for g in range(4):
        r = proj(C_Q + g * LANES, C_Q + (g + 1) * LANES)
        yq = _rope(_seg_rms(r, seg, qg_ref[...]), c, sa, sb) * (HEAD_DIM ** -0.5)
        q_o[:, g * LANES:(g + 1) * LANES] = yq.astype(BF16)
    r = proj(C_K, C_K + LANES)
    yk = _rope(_seg_rms(r, seg, kg_ref[...]), c, sa, sb)
    kt_o[0] = yk.T
    kb_o[...] = yk.astype(BF16)
    r = proj(C_V, C_V + LANES)
    r_t = r.T
    vtf_o[0] = r_t
    vb_o[...] = r.astype(BF16)
    vt_o[0] = r_t.astype(BF16)
    for g in range(2):
        r = proj(C_QI + g * LANES, C_QI + (g + 1) * LANES)
        yqi = _rope(r, c, sa, sb) * (IDX_DIM ** -0.5 * IDX_HEADS ** -0.5)
        qi_o[:, g * LANES:(g + 1) * LANES] = yqi.astype(BF16)
    r = proj(C_KIWI, C_KIWI + LANES)
    yki = _rope(_seg_rms(r, segki_ref[...], kig_ref[...]), c, sa, sb)
    lane = lax.broadcasted_iota(jnp.int32, r.shape, 1)
    kiwi = jnp.where(lane < IDX_DIM, yki, r)
    kiwi_o[...] = kiwi
    kit_o[0] = kiwi.T[0:IDX_DIM, :]
    kib_o[...] = kiwi[:, 0:IDX_DIM].astype(BF16)
    r = proj(C_U, C_U + GM_WIDTH)
    u_o[...] = jax.nn.gelu(r).astype(BF16)
    r = proj(C_VG, C_VG + GM_WIDTH)
    gl = jax.nn.gelu(r)
    mu = jnp.mean(gl, axis=-1, keepdims=True)
    dv = gl - mu
    var = jnp.mean(dv * dv, axis=-1, keepdims=True)
    vg_o[...] = (dv * lax.rsqrt(var + EPS) * lng_ref[...] + lnb_ref[...]).astype(vg_o.dtype)
    r = proj(C_GATE, C_GATE + D_MODEL)
    ga_o[...] = jax.nn.sigmoid(r).astype(BF16)
    r = proj(C_GATE + D_MODEL, C_GATE + 2 * D_MODEL)
    gb_o[...] = jax.nn.sigmoid(r).astype(BF16)


def _project(x, mod, tables, consts, w_pad, tm, tiles_per_seq, vg_dtype):
    t = x.shape[0]
    mod3d = mod.ndim == 3
    c_t, sa_t, sb_t = tables
    g_mix, qg, kg, kig, lng, lnb, seg, segki = consts
    row = lambda i: (i, 0)
    fixed = lambda i: (0, 0)
    if mod3d:
        mod_spec = pl.BlockSpec((1, 1, 6 * D_MODEL), lambda i: (i // tiles_per_seq, 0, 0))
        tab_spec = pl.BlockSpec((tm, LANES), lambda i: (i % tiles_per_seq, 0))
    else:
        mod_spec = pl.BlockSpec((tm, 6 * D_MODEL), row)
        tab_spec = pl.BlockSpec((1, LANES), fixed)
    widths = [(512, BF16), (128, BF16), (128, BF16), (256, BF16),
              (64, BF16), (128, F32), (512, BF16), (512, vg_dtype), (1024, BF16), (1024, BF16)]
    n_seq, seq = t // (tm * tiles_per_seq), tm * tiles_per_seq
    t_idx = lambda i: (i // tiles_per_seq, 0, i % tiles_per_seq)
    t_outs = [(LANES, F32), (LANES, F32), (IDX_DIM, F32), (LANES, BF16)]
    return pl.pallas_call(
        functools.partial(_proj_body, mod3d),
        grid=(t // tm,),
        in_specs=[pl.BlockSpec((tm, D_MODEL), row), mod_spec,
                  pl.BlockSpec((1, D_MODEL), fixed),
                  pl.BlockSpec((D_MODEL, D_IN_PAD), fixed),
                  tab_spec, tab_spec, tab_spec,
                  pl.BlockSpec((1, LANES), fixed), pl.BlockSpec((1, LANES), fixed), pl.BlockSpec((1, LANES), fixed),
                  pl.BlockSpec((1, GM_WIDTH), fixed), pl.BlockSpec((1, GM_WIDTH), fixed),
                  pl.BlockSpec((LANES, LANES), fixed), pl.BlockSpec((LANES, LANES), fixed)],
        out_specs=[pl.BlockSpec((tm, w), row) for w, _ in widths]
        + [pl.BlockSpec((1, f, tm), t_idx) for f, _ in t_outs],
        out_shape=[jax.ShapeDtypeStruct((t, w), dt) for w, dt in widths]
        + [jax.ShapeDtypeStruct((n_seq, f, seq), dt) for f, dt in t_outs],
        compiler_params=_cparams(("arbitrary",)),
        name="project",
    )(x, mod, g_mix, w_pad, c_t, sa_t, sb_t, qg, kg, kig, lng, lnb, seg, segki)


def _select_bias(s_ref, bias_ref, rows, width, topk, tie_check_start=16):
    nb = width // LANES
    kf = float(topk)
    neg, pos = -jnp.inf, jnp.inf

    def blk(j):
        return s_ref[:, j * LANES:(j + 1) * LANES]

    def count_above(t):
        tb = jnp.broadcast_to(t, (rows, LANES))
        acc = jnp.zeros((rows, LANES), F32)
        for j in range(nb):
            acc = acc + jnp.where(blk(j) > tb, 1.0, 0.0)
        return jnp.sum(acc, axis=1, keepdims=True)

    mx = jnp.full((rows, LANES), neg, F32)
    mn = jnp.full((rows, LANES), pos, F32)
    for j in range(nb):
        b = blk(j)
        mx = jnp.maximum(mx, b)
        mn = jnp.minimum(mn, jnp.where(b == neg, pos, b))
    hi0 = jnp.max(mx, axis=1, keepdims=True)
    smin = jnp.min(mn, axis=1, keepdims=True)
    lo0 = smin - jnp.abs(smin) - 1.0
    f_lo0 = count_above(lo0)
    zeros = jnp.zeros((rows, 1), F32)

    def active_of(f_lo, tie):
        return jnp.logical_and(f_lo > kf, tie == 0.0)

    def cond(st):
        _, _, _, f_lo, _, tie = st
        return jnp.max(jnp.where(active_of(f_lo, tie), 1.0, 0.0)) > 0.0

    def body(st):
        it, lo, hi, f_lo, f_hi, tie = st
        active = active_of(f_lo, tie)
        mid = lo + (hi - lo) * 0.5
        stuck = jnp.logical_or(mid <= lo, mid >= hi)
        cnt = count_above(mid)
        ge = cnt >= kf
        up_lo = jnp.logical_and(active, ge)
        up_hi = jnp.logical_and(active, jnp.logical_not(ge))
        lo = jnp.where(up_lo, mid, lo)
        f_lo = jnp.where(up_lo, cnt, f_lo)
        hi = jnp.where(up_hi, mid, hi)
        f_hi = jnp.where(up_hi, cnt, f_hi)
        tie = jnp.where(jnp.logical_and(active, stuck), 1.0, tie)

        def tie_check(_):
            lob = jnp.broadcast_to(lo, (rows, LANES))
            hib = jnp.broadcast_to(hi, (rows, LANES))
            vmx = jnp.full((rows, LANES), neg, F32)
            vmn = jnp.full((rows, LANES), pos, F32)
            for j in range(nb):
                b = blk(j)
                inn = jnp.logical_and(b > lob, b <= hib)
                vmx = jnp.maximum(vmx, jnp.where(inn, b, neg))
                vmn = jnp.minimum(vmn, jnp.where(inn, b, pos))
            one_value = jnp.max(vmx, axis=1, keepdims=True) == jnp.min(vmn, axis=1, keepdims=True)
            return jnp.where(one_value, 1.0, tie)

        run_check = jnp.logical_and(it >= tie_check_start, it % 4 == 0)
        tie = lax.cond(run_check, tie_check, lambda _: tie, 0)
        return it + 1, lo, hi, f_lo, f_hi, tie

    _, lo, hi, f_lo, f_hi, _ = lax.while_loop(cond, body, (jnp.int32(0), lo0, hi0, f_lo0, zeros, zeros))
    lob = jnp.broadcast_to(lo, (rows, LANES))
    need_prefix = jnp.max(jnp.where(f_lo > kf, 1.0, 0.0)) > 0.0

    @pl.when(jnp.logical_not(need_prefix))
    def _():
        for j in range(nb):
            bias_ref[:, j * LANES:(j + 1) * LANES] = jnp.where(blk(j) > lob, 0.0, neg)

    @pl.when(need_prefix)
    def _():
        hib = jnp.broadcast_to(hi, (rows, LANES))
        need = kf - f_hi
        ri = lax.broadcasted_iota(jnp.int32, (LANES, LANES), 0)
        ci = lax.broadcasted_iota(jnp.int32, (LANES, LANES), 1)
        upper = jnp.where(ri < ci, 1.0, 0.0).astype(BF16)
        off = jnp.zeros((rows, 1), F32)
        for j in range(nb):
            b = blk(j)
            inn = jnp.logical_and(b > lob, b <= hib)
            innf = jnp.where(inn, 1.0, 0.0)
            before = jnp.dot(innf.astype(BF16), upper, preferred_element_type=F32) + off
            sel = jnp.logical_or(b > hib, jnp.logical_and(inn, before < need))
            bias_ref[:, j * LANES:(j + 1) * LANES] = jnp.where(sel, 0.0, neg)
            off = off + jnp.sum(innf, axis=1, keepdims=True)


_NT = (((1,), (1,)), ((), ()))


def _sub_reduce(x, op):
    for sh in (4, 2, 1):
        x = op(x, pltpu.roll(x, sh, 0))
    return x


def _select_bias_t(s_ref, bias_ref, width, topk, n_adm, steps_per_check=4, tie_check_from=4):
    rb = 64
    nb = width // rb
    kf = float(topk)
    neg, pos = -jnp.inf, jnp.inf

    def blk(j):
        return s_ref[j * rb:(j + 1) * rb, :]

    def fold(x, op):
        y = x[0:SUB]
        for a in range(1, rb // SUB):
            y = op(y, x[a * SUB:(a + 1) * SUB])
        return _sub_reduce(y, op)

    def tile(v):
        return jnp.concatenate([v] * (rb // SUB), axis=0)

    def count_above(t):
        tb = tile(t)
        acc = jnp.zeros((rb, LANES), F32)
        for j in range(nb):
            acc = acc + jnp.where(blk(j) > tb, 1.0, 0.0)
        return fold(acc, jnp.add)

    mx = jnp.full((rb, LANES), neg, F32)
    mn = jnp.full((rb, LANES), pos, F32)
    for j in range(nb):
        b = blk(j)
        mx = jnp.maximum(mx, b)
        mn = jnp.minimum(mn, jnp.where(b == neg, pos, b))
    hi0 = fold(mx, jnp.maximum)
    smin = fold(mn, jnp.minimum)
    lo0 = smin - jnp.abs(smin) - 1.0
    zeros = jnp.zeros((SUB, LANES), F32)

    def active_of(f_lo, tie):
        return jnp.logical_and(f_lo > kf, tie == 0.0)

    def any_lane(cond):
        return jnp.max(jnp.where(cond, 1.0, 0.0)) > 0.0

    def step(lo, hi, f_lo, f_hi, tie):
        active = active_of(f_lo, tie)
        mid = lo + (hi - lo) * 0.5
        stuck = jnp.logical_or(mid <= lo, mid >= hi)
        cnt = count_above(mid)
        ge = cnt >= kf
        up_lo = jnp.logical_and(active, ge)
        up_hi = jnp.logical_and(active, jnp.logical_not(ge))
        return (jnp.where(up_lo, mid, lo), jnp.where(up_hi, mid, hi), jnp.where(up_lo, cnt, f_lo),
                jnp.where(up_hi, cnt, f_hi), jnp.where(jnp.logical_and(active, stuck), 1.0, tie))

    def cond(st):
        _, _, _, f_lo, _, tie = st
        return any_lane(active_of(f_lo, tie))

    def body(st):
        it, lo, hi, f_lo, f_hi, tie = st
        for _ in range(steps_per_check):
            lo, hi, f_lo, f_hi, tie = step(lo, hi, f_lo, f_hi, tie)

        def tie_check(_):
            lob, hib = tile(lo), tile(hi)
            vmx = jnp.full((rb, LANES), neg, F32)
            vmn = jnp.full((rb, LANES), pos, F32)
            for j in range(nb):
                b = blk(j)
                inn = jnp.logical_and(b > lob, b <= hib)
                vmx = jnp.maximum(vmx, jnp.where(inn, b, neg))
                vmn = jnp.minimum(vmn, jnp.where(inn, b, pos))
            return jnp.where(fold(vmx, jnp.maximum) == fold(vmn, jnp.minimum), 1.0, tie)

        run_check = jnp.logical_and(it + 1 >= tie_check_from, any_lane(active_of(f_lo, tie)))
        tie = lax.cond(run_check, tie_check, lambda _: tie, 0)
        return it + 1, lo, hi, f_lo, f_hi, tie

    _, lo, hi, f_lo, f_hi, _ = lax.while_loop(cond, body, (jnp.int32(0), lo0, hi0, n_adm, zeros, zeros))
    lob = tile(lo)
    need_prefix = any_lane(f_lo > kf)

    @pl.when(jnp.logical_not(need_prefix))
    def _():
        for j in range(nb):
            bias_ref[j * rb:(j + 1) * rb, :] = jnp.where(blk(j) > lob, 0.0, neg)

    @pl.when(need_prefix)
    def _():
        need = (kf - f_hi)[0:1]
        lo1, hi1 = lo[0:1], hi[0:1]
        ri = lax.broadcasted_iota(jnp.int32, (LANES, LANES), 0)
        ci = lax.broadcasted_iota(jnp.int32, (LANES, LANES), 1)
        lower = jnp.where(ci < ri, 1.0, 0.0).astype(BF16)
        off = jnp.zeros((1, LANES), F32)
        for j in range(width // LANES):
            b = s_ref[j * LANES:(j + 1) * LANES, :]
            inn = jnp.logical_and(b > lo1, b <= hi1)
            innf = jnp.where(inn, 1.0, 0.0)
            before = jnp.dot(lower, innf.astype(BF16), preferred_element_type=F32) + off
            sel = jnp.logical_or(b > hi1, jnp.logical_and(inn, before < need))
            bias_ref[j * LANES:(j + 1) * LANES, :] = jnp.where(sel, 0.0, neg)
            off = off + jnp.sum(innf, axis=0, keepdims=True)


def _prompt_attn_body(blk_i, topk, qi_ref, kiwi_ref, kib_ref, q_ref, k_ref, vt_ref, o_ref, s_ref, bias_ref):
    width = (blk_i + 1) * Q_BLOCK
    neg = -jnp.inf
    kiwi_t = kiwi_ref[0].T
    chunk = 512
    for c0 in range(0, width, chunk):
        c1 = min(width, c0 + chunk)
        kib = kib_ref[0, c0:c1, :]
        sc = None
        for h in range(IDX_HEADS):
            d = lax.dot_general(kib, qi_ref[0, :, h * IDX_DIM:(h + 1) * IDX_DIM], _NT,
                                preferred_element_type=F32)
            t = jnp.maximum(d, 0.0) * kiwi_t[IDX_DIM + h:IDX_DIM + h + 1, :]
            sc = t if sc is None else sc + t
        s_ref[c0:c1, :] = sc
    ki = lax.broadcasted_iota(jnp.int32, (Q_BLOCK, Q_BLOCK), 0)
    qj = lax.broadcasted_iota(jnp.int32, (Q_BLOCK, Q_BLOCK), 1)
    d0 = width - Q_BLOCK
    s_ref[d0:width, :] = jnp.where(ki <= qj, s_ref[d0:width, :], neg)
    if width > topk:
        n_adm = (lax.broadcasted_iota(jnp.int32, (SUB, LANES), 1) + (d0 + 1)).astype(F32)
        _select_bias_t(s_ref, bias_ref, width, topk, n_adm)
    else:
        bias_ref[...] = jnp.where(s_ref[...] == neg, neg, 0.0)
    lane = lax.broadcasted_iota(jnp.int32, (Q_BLOCK, LANES), 1)
    hpg = N_HEADS // N_KV_HEADS
    for h in range(N_HEADS):
        g = h // hpg
        qp = q_ref[0, :, (h // 2) * LANES:(h // 2 + 1) * LANES].astype(F32)
        if h % 2 != g:
            qp = pltpu.roll(qp, HEAD_DIM, 1)
        q2 = jnp.where((lane >= HEAD_DIM) == (g == 1), qp, 0.0).astype(BF16)
        s = lax.dot_general(k_ref[0], q2, _NT, preferred_element_type=F32) + bias_ref[...]
        m = jnp.max(s, axis=0, keepdims=True)
        p = jnp.exp(s - m)
        l = jnp.sum(p, axis=0, keepdims=True)
        o_t = jnp.dot(vt_ref[0, g * HEAD_DIM:(g + 1) * HEAD_DIM, :], p.astype(BF16), preferred_element_type=F32)
        o_ref[0, :, h * HEAD_DIM:(h + 1) * HEAD_DIM] = (o_t / l).T.astype(BF16)


def _prompt_attention(qi, kiwi, kib, q, kb, vt, topk):
    b, s, _ = q.shape
    outs = []
    for i in range(s // Q_BLOCK):
        width = (i + 1) * Q_BLOCK
        qblk = lambda bb, i=i: (bb, i, 0)
        kall = lambda bb: (bb, 0, 0)
        outs.append(pl.pallas_call(
            functools.partial(_prompt_attn_body, i, topk),
            grid=(b,),
            in_specs=[pl.BlockSpec((1, Q_BLOCK, IDX_HEADS * IDX_DIM), qblk),
                      pl.BlockSpec((1, Q_BLOCK, LANES), qblk),
                      pl.BlockSpec((1, width, IDX_DIM), kall),
                      pl.BlockSpec((1, Q_BLOCK, N_HEADS * HEAD_DIM), qblk),
                      pl.BlockSpec((1, width, N_KV_HEADS * HEAD_DIM), kall),
                      pl.BlockSpec((1, N_KV_HEADS * HEAD_DIM, width), kall)],
            out_specs=pl.BlockSpec((1, Q_BLOCK, N_HEADS * HEAD_DIM), lambda bb: (bb, 0, 0)),
            out_shape=jax.ShapeDtypeStruct((b, Q_BLOCK, N_HEADS * HEAD_DIM), BF16),
            scratch_shapes=[pltpu.VMEM((width, Q_BLOCK), F32), pltpu.VMEM((width, Q_BLOCK), F32)],
            compiler_params=_cparams(("arbitrary",)),
            name=f"prompt_attn_{i}",
        )(qi, kiwi, kib, q, kb, vt))
    return jnp.stack(outs)


def _page_copies(pt_ref, sample, src_hbm, buf, slot, sem, n_pages):
    return [pltpu.make_async_copy(src_hbm.at[pt_ref[sample, p]],
                                  buf.at[slot, :, pl.ds(p * PAGE_SIZE, PAGE_SIZE)], sem.at[slot])
            for p in range(n_pages)]


def _sample_scores_body(n_pages, pt_ref, kidx_hbm, qi_ref, wi_ref, kin_ref, o_ref, buf, sem):
    s = pl.program_id(0)
    slot = s % 2

    @pl.when(s == 0)
    def _():
        for cp in _page_copies(pt_ref, 0, kidx_hbm, buf, 0, sem, n_pages):
            cp.start()

    @pl.when(s + 1 < pl.num_programs(0))
    def _():
        for cp in _page_copies(pt_ref, s + 1, kidx_hbm, buf, 1 - slot, sem, n_pages):
            cp.start()

    for cp in _page_copies(pt_ref, s, kidx_hbm, buf, slot, sem, n_pages):
        cp.wait()
    qi = qi_ref[0]
    wi = wi_ref[0]
    ki_t = buf[slot].astype(BF16)
    d = jnp.dot(qi, ki_t, preferred_element_type=F32)
    past = jnp.sum(jnp.maximum(d, 0.0) * wi, axis=0, keepdims=True)
    dn = jnp.sum(qi.astype(F32) * kin_ref[0].astype(F32), axis=1, keepdims=True)
    new = jnp.sum(jnp.maximum(dn, 0.0) * wi, axis=0, keepdims=True)
    lane = lax.broadcasted_iota(jnp.int32, (1, LANES), 1)
    tail = jnp.where(lane == 0, jnp.broadcast_to(new, (1, LANES)), -jnp.inf)
    o_ref[0] = jnp.concatenate([past, tail], axis=1)


def _sample_scores(page_table, kidx_pool, qi8, wi8, ki_new):
    n, n_pages = page_table.shape
    past = n_pages * PAGE_SIZE
    grid_spec = pltpu.PrefetchScalarGridSpec(
        num_scalar_prefetch=1,
        grid=(n,),
        in_specs=[pl.BlockSpec(memory_space=pl.ANY),
                  pl.BlockSpec((1, 8, IDX_DIM), lambda s, pt: (s, 0, 0)),
                  pl.BlockSpec((1, 8, 1), lambda s, pt: (s, 0, 0)),
                  pl.BlockSpec((1, 1, IDX_DIM), lambda s, pt: (s, 0, 0))],
        out_specs=pl.BlockSpec((1, 1, past + LANES), lambda s, pt: (s, 0, 0)),
        scratch_shapes=[pltpu.VMEM((2, IDX_DIM, past), F32), pltpu.SemaphoreType.DMA((2,))],
    )
    return pl.pallas_call(
        functools.partial(_sample_scores_body, n_pages),
        grid_spec=grid_spec,
        out_shape=jax.ShapeDtypeStruct((n, 1, past + LANES), F32),
        compiler_params=_cparams(("arbitrary",)),
        name="sample_scores",
    )(page_table, kidx_pool, qi8, wi8, ki_new)


def _sample_select_body(topk, s_ref, bias_ref):
    rows, width = s_ref.shape
    _select_bias(s_ref, bias_ref, rows, width, topk)


def _sample_select(scores, topk):
    return pl.pallas_call(
        functools.partial(_sample_select_body, topk),
        out_shape=jax.ShapeDtypeStruct(scores.shape, F32),
        compiler_params=pltpu.CompilerParams(vmem_limit_bytes=VMEM_LIMIT),
        name="sample_select",
    )(scores)


def _sample_attn_body(n_pages, pt_ref, k_hbm, v_hbm, q_ref, bias_ref, kn_ref, vn_ref, o_ref, kbuf, vbuf, sem):
    s = pl.program_id(0)
    slot = s % 2
    past = n_pages * PAGE_SIZE

    def copies(sample, sl):
        return (_page_copies(pt_ref, sample, k_hbm, kbuf, sl, sem.at[0], n_pages)
                + _page_copies(pt_ref, sample, v_hbm, vbuf, sl, sem.at[1], n_pages))

    @pl.when(s == 0)
    def _():
        for cp in copies(0, 0):
            cp.start()

    @pl.when(s + 1 < pl.num_programs(0))
    def _():
        for cp in copies(s + 1, 1 - slot):
            cp.start()

    for cp in copies(s, slot):
        cp.wait()
    q2 = q_ref[0]
    k_t = kbuf[slot].astype(BF16)
    v_t = vbuf[slot].astype(BF16)
    row = lax.broadcasted_iota(jnp.int32, (N_HEADS, 1), 0)
    first = row < (N_HEADS // N_KV_HEADS)
    bias = bias_ref[0]
    sc = jnp.dot(q2, k_t, preferred_element_type=F32) + bias[:, 0:past]
    sn = jnp.sum(q2.astype(F32) * kn_ref[0].astype(F32), axis=1, keepdims=True) + bias[:, past:past + 1]
    m = jnp.maximum(jnp.max(sc, axis=1, keepdims=True), sn)
    p = jnp.exp(sc - m)
    pn = jnp.exp(sn - m)
    l = jnp.sum(p, axis=1, keepdims=True) + pn
    o2 = lax.dot_general(p.astype(BF16), v_t, _NT, preferred_element_type=F32)
    o2 = o2 + pn.astype(BF16).astype(F32) * vn_ref[0].astype(F32)
    o = jnp.where(first, o2[:, 0:HEAD_DIM], o2[:, HEAD_DIM:2 * HEAD_DIM])
    o_ref[0] = (o / l).astype(BF16)


def _sample_attention(page_table, k_pool, v_pool, q8, bias, k_new, v_new):
    n, n_pages = page_table.shape
    past = n_pages * PAGE_SIZE
    kvw = N_KV_HEADS * HEAD_DIM
    per = lambda s, pt: (s, 0, 0)
    grid_spec = pltpu.PrefetchScalarGridSpec(
        num_scalar_prefetch=1,
        grid=(n,),
        in_specs=[pl.BlockSpec(memory_space=pl.ANY), pl.BlockSpec(memory_space=pl.ANY),
                  pl.BlockSpec((1, N_HEADS, kvw), per),
                  pl.BlockSpec((1, 1, past + LANES), per),
                  pl.BlockSpec((1, 1, kvw), per),
                  pl.BlockSpec((1, 1, kvw), per)],
        out_specs=pl.BlockSpec((1, N_HEADS, HEAD_DIM), per),
        scratch_shapes=[pltpu.VMEM((2, kvw, past), F32), pltpu.VMEM((2, kvw, past), F32),
                        pltpu.SemaphoreType.DMA((2, 2))],
    )
    return pl.pallas_call(
        functools.partial(_sample_attn_body, n_pages),
        grid_spec=grid_spec,
        out_shape=jax.ShapeDtypeStruct((n, N_HEADS, HEAD_DIM), BF16),
        compiler_params=_cparams(("arbitrary",)),
        name="sample_attn",
    )(page_table, k_pool, v_pool, q8, bias, k_new, v_new)


def _gmlp_body(n_chunks, u_ref, vg_ref, w_ref, bt_ref, o_ref):
    ri = lax.broadcasted_iota(jnp.int32, (CHUNK, CHUNK), 0)
    ci = lax.broadcasted_iota(jnp.int32, (CHUNK, CHUNK), 1)
    gd = GM_WIDTH // GM_GROUPS
    for g in range(GM_GROUPS):
        wg = jnp.where(ci <= ri, w_ref[g], 0.0).astype(BF16)
        bg = bt_ref[:, g:g + 1]
        for c in range(n_chunks):
            rows = slice(c * CHUNK, (c + 1) * CHUNK)
            cols = slice(g * gd, (g + 1) * gd)
            mixed = jnp.dot(wg, vg_ref[rows, cols], preferred_element_type=F32) + bg
            o_ref[rows, cols] = (u_ref[rows, cols].astype(F32) * mixed).astype(BF16)


def _gmlp_prompt(u, vg, w, bt, tm):
    t = u.shape[0]
    row = lambda i: (i, 0)
    return pl.pallas_call(
        functools.partial(_gmlp_body, tm // CHUNK),
        grid=(t // tm,),
        in_specs=[pl.BlockSpec((tm, GM_WIDTH), row), pl.BlockSpec((tm, GM_WIDTH), row),
                  pl.BlockSpec((GM_GROUPS, CHUNK, CHUNK), lambda i: (0, 0, 0)),
                  pl.BlockSpec((CHUNK, LANES), lambda i: (0, 0))],
        out_specs=pl.BlockSpec((tm, GM_WIDTH), row),
        out_shape=jax.ShapeDtypeStruct((t, GM_WIDTH), BF16),
        compiler_params=_cparams(("arbitrary",)),
        name="gmlp",
    )(u, vg, w, bt)


def _gmlp_first_row_body(u_ref, vg_ref, w0_ref, b0_ref, o_ref):
    o_ref[...] = (u_ref[...].astype(F32) * (vg_ref[...] * w0_ref[...] + b0_ref[...])).astype(BF16)


def _gmlp_sample(u, vg, w0, b0):
    return pl.pallas_call(
        _gmlp_first_row_body,
        out_shape=jax.ShapeDtypeStruct(u.shape, BF16),
        name="gmlp_first_row",
    )(u, vg, w0, b0)


def _merge_body(mod3d, dispatch, attn_ref, gm_ref, ga_ref, gb_ref, x_ref, mod_ref, wpa_ref, wpg_ref, wo_ref,
                g2_ref, wr_ref, br_ref, x1_o, a_o, b_o):
    tm = x_ref.shape[0]
    attn = attn_ref[...].reshape(tm, N_HEADS * HEAD_DIM)
    a = jnp.dot(attn, wpa_ref[...], preferred_element_type=F32)
    g = jnp.dot(gm_ref[...], wpg_ref[...], preferred_element_type=F32)
    merged = ga_ref[...].astype(F32) * a + gb_ref[...].astype(F32) * g
    out = jnp.dot(merged.astype(BF16), wo_ref[...], preferred_element_type=F32)
    m = mod_ref[0] if mod3d else mod_ref[...]
    x1 = x_ref[...] + m[:, 2 * D_MODEL:3 * D_MODEL] * out
    x1_o[...] = x1
    ms = jnp.mean(x1 * x1, axis=-1, keepdims=True)
    y = x1 * lax.rsqrt(ms + EPS) * g2_ref[...]
    h2 = y * (1.0 + m[:, 4 * D_MODEL:5 * D_MODEL]) + m[:, 3 * D_MODEL:4 * D_MODEL]
    hi = h2.astype(BF16)
    lo = (h2 - hi.astype(F32)).astype(BF16)
    r = jnp.dot(hi, wr_ref[...], preferred_element_type=F32) + jnp.dot(lo, wr_ref[...], preferred_element_type=F32)
    logits = r[:, 0:LANES] + r[:, LANES:2 * LANES] + br_ref[...]
    neg = -jnp.inf
    big = jnp.int32(1 << 20)
    lane = lax.broadcasted_iota(jnp.int32, logits.shape, 1)
    is_g = jnp.logical_and(lane >= N_EXPERTS, lane < N_EXPERTS + N_EXPERT_GROUPS)
    gl = jnp.where(is_g, logits, neg)
    gmax = jnp.max(gl, axis=1, keepdims=True)
    g_lane = jnp.min(jnp.where(gl == gmax, lane, big), axis=1, keepdims=True)
    g_w = 1.0 / jnp.sum(jnp.exp(gl - gmax), axis=1, keepdims=True)
    g_sel = g_lane - N_EXPERTS
    in_grp = jnp.logical_and(lane < N_EXPERTS, (lane >> 3) == g_sel)
    el = jnp.where(in_grp, logits, neg)
    m1 = jnp.max(el, axis=1, keepdims=True)
    i1 = jnp.min(jnp.where(el == m1, lane, big), axis=1, keepdims=True)
    el2 = jnp.where(lane == i1, neg, el)
    m2 = jnp.max(el2, axis=1, keepdims=True)
    i2 = jnp.min(jnp.where(el2 == m2, lane, big), axis=1, keepdims=True)
    e2 = jnp.exp(m2 - m1)
    w1 = g_w / (1.0 + e2)
    w2 = g_w * e2 / (1.0 + e2)
    if not dispatch:
        a_o[...] = hi
        b_o[...] = jnp.where(lane == i1, w1, 0.0) + jnp.where(lane == i2, w2, 0.0)
        return
    low_first = i1 < i2
    ea = jnp.where(low_first, i1, i2) - g_sel * EXPERTS_PER_GROUP
    eb = jnp.where(low_first, i2, i1) - g_sel * EXPERTS_PER_GROUP
    cls = g_sel * N_PAIRS + ((ea * (2 * EXPERTS_PER_GROUP - 1 - ea)) >> 1) + (eb - ea - 1)
    b_o[...] = jnp.broadcast_to(cls, (tm, LANES))
    wa = jnp.where(low_first, w1, w2)
    wb = jnp.where(low_first, w2, w1)
    a_o[:, 0:HALF] = _pack_halves(hi.astype(F32))
    a_o[:, HALF:ROW_WORDS] = lax.bitcast_convert_type(
        jnp.where(lane == 0, wa, jnp.where(lane == 1, wb, 0.0)), U32)


def _merge(attn4, gm, ga, gb, x, mod, wpa, wpg, wo, g2, wr, br, tm, tiles_per_seq, dispatch):
    t = x.shape[0]
    mod3d = mod.ndim == 3
    nb = tm // Q_BLOCK
    row = lambda i: (i, 0)
    fixed = lambda i: (0, 0)
    if mod3d:
        mod_spec = pl.BlockSpec((1, 1, 6 * D_MODEL), lambda i: (i // tiles_per_seq, 0, 0))
    else:
        mod_spec = pl.BlockSpec((tm, 6 * D_MODEL), row)
    attn_spec = pl.BlockSpec((nb, 1, Q_BLOCK, N_HEADS * HEAD_DIM),
                             lambda i: (i % tiles_per_seq, i // tiles_per_seq, 0, 0))
    if dispatch:
        extra = [(ROW_WORDS, U32), (LANES, jnp.int32)]
    else:
        extra = [(D_MODEL, BF16), (LANES, F32)]
    return pl.pallas_call(
        functools.partial(_merge_body, mod3d, dispatch),
        grid=(t // tm,),
        in_specs=[attn_spec, pl.BlockSpec((tm, GM_WIDTH), row),
                  pl.BlockSpec((tm, D_MODEL), row), pl.BlockSpec((tm, D_MODEL), row),
                  pl.BlockSpec((tm, D_MODEL), row), mod_spec,
                  pl.BlockSpec((N_HEADS * HEAD_DIM, D_MODEL), fixed), pl.BlockSpec((GM_WIDTH, D_MODEL), fixed),
                  pl.BlockSpec((D_MODEL, D_MODEL), fixed), pl.BlockSpec((1, D_MODEL), fixed),
                  pl.BlockSpec((D_MODEL, 2 * LANES), fixed), pl.BlockSpec((1, LANES), fixed)],
        out_specs=[pl.BlockSpec((tm, D_MODEL), row)] + [pl.BlockSpec((tm, w), row) for w, _ in extra],
        out_shape=[jax.ShapeDtypeStruct((t, D_MODEL), F32)] + [jax.ShapeDtypeStruct((t, w), dt) for w, dt in extra],
        compiler_params=_cparams(("arbitrary",)),
        name="merge",
    )(attn4, gm, ga, gb, x, mod, wpa, wpg, wo, g2, wr, br)


def _moe_body(mod3d, h_ref, gate_ref, x1_ref, mod_ref, wgu_ref, wd_ref, o_ref, acc_ref):
    e = pl.program_id(1)

    @pl.when(e == 0)
    def _():
        acc_ref[...] = jnp.zeros_like(acc_ref)

    gu = jnp.dot(h_ref[...], wgu_ref[0], preferred_element_type=F32)
    a = gu[:, 0:D_EXPERT]
    hid = a * jax.nn.sigmoid(a) * gu[:, D_EXPERT:2 * D_EXPERT]
    gate = gate_ref[...]
    lane = lax.broadcasted_iota(jnp.int32, gate.shape, 1)
    ge = jnp.sum(jnp.where(lane == e, gate, 0.0), axis=1, keepdims=True)
    acc_ref[...] += jnp.dot((hid * ge).astype(BF16), wd_ref[0], preferred_element_type=F32)

    @pl.when(e == pl.num_programs(1) - 1)
    def _():
        m = mod_ref[0] if mod3d else mod_ref[...]
        o_ref[...] = x1_ref[...] + m[:, 5 * D_MODEL:6 * D_MODEL] * acc_ref[...]


def _moe(h2, gate, x1, mod, wgu, wd, tm, tiles_per_seq):
    t = h2.shape[0]
    mod3d = mod.ndim == 3
    row = lambda i, e: (i, 0)
    if mod3d:
        mod_spec = pl.BlockSpec((1, 1, 6 * D_MODEL), lambda i, e: (i // tiles_per_seq, 0, 0))
    else:
        mod_spec = pl.BlockSpec((tm, 6 * D_MODEL), row)
    return pl.pallas_call(
        functools.partial(_moe_body, mod3d),
        grid=(t // tm, N_EXPERTS),
        in_specs=[pl.BlockSpec((tm, D_MODEL), row), pl.BlockSpec((tm, LANES), row),
                  pl.BlockSpec((tm, D_MODEL), row), mod_spec,
                  pl.BlockSpec((1, D_MODEL, 2 * D_EXPERT), lambda i, e: (e, 0, 0)),
                  pl.BlockSpec((1, D_EXPERT, D_MODEL), lambda i, e: (e, 0, 0))],
        out_specs=pl.BlockSpec((tm, D_MODEL), row),
        out_shape=jax.ShapeDtypeStruct((t, D_MODEL), F32),
        scratch_shapes=[pltpu.VMEM((tm, D_MODEL), F32)],
        compiler_params=_cparams(("arbitrary", "arbitrary")),
        name="moe",
    )(h2, gate, x1, mod, wgu, wd)


def _rank_body(cls_ref, rank_o, cnt_o, carry):
    i = pl.program_id(0)

    @pl.when(i == 0)
    def _():
        carry[...] = jnp.zeros_like(carry)

    tm = cls_ref.shape[0]
    lane = lax.broadcasted_iota(jnp.int32, (tm, LANES), 1)
    hit = lane == cls_ref[...]
    onehot = jnp.where(hit, 1.0, 0.0)
    ri = lax.broadcasted_iota(jnp.int32, (tm, tm), 0)
    ci = lax.broadcasted_iota(jnp.int32, (tm, tm), 1)
    earlier = jnp.where(ci < ri, 1.0, 0.0).astype(BF16)
    before = jnp.dot(earlier, onehot.astype(BF16), preferred_element_type=F32) + carry[0:1, :]
    rank = jnp.sum(jnp.where(hit, before, 0.0), axis=1, keepdims=True)
    rank_o[...] = jnp.broadcast_to(rank, (tm, LANES)).astype(jnp.int32)
    carry[...] = carry[...] + jnp.sum(onehot, axis=0, keepdims=True)
    cnt_o[...] = carry[...]


def _class_ranks(cls, tm):
    t = cls.shape[0]
    return pl.pallas_call(
        _rank_body,
        grid=(t // tm,),
        in_specs=[pl.BlockSpec((tm, LANES), lambda i: (i, 0))],
        out_specs=[pl.BlockSpec((tm, LANES), lambda i: (i, 0)), pl.BlockSpec((SUB, LANES), lambda i: (0, 0))],
        out_shape=[jax.ShapeDtypeStruct((t, LANES), jnp.int32), jax.ShapeDtypeStruct((SUB, LANES), F32)],
        scratch_shapes=[pltpu.VMEM((SUB, LANES), F32)],
        compiler_params=_cparams(("arbitrary",)),
        name="class_ranks",
    )(cls)


INVERT_CHUNK = 4096


def _invert_body(dest_ref, src_o):
    i = pl.program_id(0)

    @pl.when(i == 0)
    def _():
        src_o[...] = jnp.zeros_like(src_o)

    def put(r, carry):
        src_o[pl.ds(dest_ref[0, 0, r], 1), :] = jnp.full((1, LANES), i * INVERT_CHUNK + r, jnp.int32)
        return carry

    lax.fori_loop(0, INVERT_CHUNK, put, 0, unroll=8)


def _invert(dest, n_sorted):
    t = dest.shape[0]
    return pl.pallas_call(
        _invert_body,
        grid=(t // INVERT_CHUNK,),
        in_specs=[pl.BlockSpec((1, 1, INVERT_CHUNK), lambda i: (i, 0, 0), memory_space=pltpu.SMEM)],
        out_specs=pl.BlockSpec((n_sorted, LANES), lambda i: (0, 0)),
        out_shape=jax.ShapeDtypeStruct((n_sorted, LANES), jnp.int32),
        compiler_params=_cparams(("arbitrary",)),
        name="invert_slots",
    )(dest.reshape(t // INVERT_CHUNK, 1, INVERT_CHUNK))


def _tile_fetches(src_ref, rows_hbm, buf, slot, sem):
    return [pltpu.make_async_copy(rows_hbm.at[pl.ds(src_ref[0, 0, r], 1)], buf.at[slot, pl.ds(r, 1)], sem.at[slot])
            for r in range(ROW_TILE)]


def _experts_body(ea_ref, eb_ref, used_ref, scur_ref, snext_ref, rows_hbm, wgu_a_ref, wgu_b_ref, wd_a_ref,
                  wd_b_ref, o_ref, buf, sem):
    j = pl.program_id(0)
    slot = j % 2
    n_used = used_ref[0]

    @pl.when(j == 0)
    def _():
        for cp in _tile_fetches(scur_ref, rows_hbm, buf, 0, sem):
            cp.start()

    @pl.when(j + 1 < n_used)
    def _():
        for cp in _tile_fetches(snext_ref, rows_hbm, buf, 1 - slot, sem):
            cp.start()

    @pl.when(jnp.logical_or(j < n_used, j == 0))
    def _():
        for cp in _tile_fetches(scur_ref, rows_hbm, buf, slot, sem):
            cp.wait()

    @pl.when(j < n_used)
    def _():
        xs = buf[slot]
        x = _unpack_halves(xs[:, 0:HALF]).astype(BF16)
        wts = lax.bitcast_convert_type(xs[:, HALF:ROW_WORDS], F32)

        def hidden(wgu_ref, wgt):
            gu = jnp.dot(x, wgu_ref[0], preferred_element_type=F32)
            a = gu[:, 0:D_EXPERT]
            return (a * jax.nn.sigmoid(a) * gu[:, D_EXPERT:2 * D_EXPERT] * wgt).astype(BF16)

        y = (jnp.dot(hidden(wgu_a_ref, wts[:, 0:1]), wd_a_ref[0], preferred_element_type=F32)
             + jnp.dot(hidden(wgu_b_ref, wts[:, 1:2]), wd_b_ref[0], preferred_element_type=F32))
        o_ref[...] = _pack_halves(y.astype(BF16).astype(F32))

    @pl.when(j >= n_used)
    def _():
        o_ref[...] = jnp.zeros_like(o_ref)


def _experts(ea_t, eb_t, n_used, src3, rows, wgu, wd):
    n_tiles = src3.shape[0]
    w_a = lambda j, ea, eb, nu: (ea[j], 0, 0)
    w_b = lambda j, ea, eb, nu: (eb[j], 0, 0)
    smem_blk = lambda f: pl.BlockSpec((1, 1, ROW_TILE), f, memory_space=pltpu.SMEM)
    grid_spec = pltpu.PrefetchScalarGridSpec(
        num_scalar_prefetch=3,
        grid=(n_tiles,),
        in_specs=[smem_blk(lambda j, ea, eb, nu: (j, 0, 0)),
                  smem_blk(lambda j, ea, eb, nu: (jnp.minimum(j + 1, n_tiles - 1), 0, 0)),
                  pl.BlockSpec(memory_space=pl.ANY),
                  pl.BlockSpec((1, D_MODEL, 2 * D_EXPERT), w_a), pl.BlockSpec((1, D_MODEL, 2 * D_EXPERT), w_b),
                  pl.BlockSpec((1, D_EXPERT, D_MODEL), w_a), pl.BlockSpec((1, D_EXPERT, D_MODEL), w_b)],
        out_specs=pl.BlockSpec((ROW_TILE, HALF), lambda j, ea, eb, nu: (j, 0)),
        scratch_shapes=[pltpu.VMEM((2, ROW_TILE, ROW_WORDS), U32), pltpu.SemaphoreType.DMA((2,))],
    )
    return pl.pallas_call(
        _experts_body,
        grid_spec=grid_spec,
        out_shape=jax.ShapeDtypeStruct((n_tiles * ROW_TILE, HALF), U32),
        compiler_params=_cparams(("arbitrary",)),
        name="experts",
    )(ea_t, eb_t, n_used, src3, src3, rows, wgu, wgu, wd, wd)


def _row_fetches(dest_ref, ys_hbm, buf, slot, sem):
    return [pltpu.make_async_copy(ys_hbm.at[pl.ds(dest_ref[0, 0, r], 1)], buf.at[slot, pl.ds(r, 1)], sem.at[slot])
            for r in range(MOVE_TILE)]


def _combine_body(mod3d, dcur_ref, dnext_ref, ys_hbm, x1_ref, mod_ref, o_ref, buf, sem):
    i = pl.program_id(0)
    slot = i % 2

    @pl.when(i == 0)
    def _():
        for cp in _row_fetches(dcur_ref, ys_hbm, buf, 0, sem):
            cp.start()

    @pl.when(i + 1 < pl.num_programs(0))
    def _():
        for cp in _row_fetches(dnext_ref, ys_hbm, buf, 1 - slot, sem):
            cp.start()

    for cp in _row_fetches(dcur_ref, ys_hbm, buf, slot, sem):
        cp.wait()
    m = mod_ref[0] if mod3d else mod_ref[...]
    o_ref[...] = x1_ref[...] + m[:, 5 * D_MODEL:6 * D_MODEL] * _unpack_halves(buf[slot])


def _combine(dest3, ys, x1, mod, tiles_per_seq):
    steps = dest3.shape[0]
    t = x1.shape[0]
    mod3d = mod.ndim == 3
    row = lambda i: (i, 0)
    if mod3d:
        mod_spec = pl.BlockSpec((1, 1, 6 * D_MODEL), lambda i: (i // tiles_per_seq, 0, 0))
    else:
        mod_spec = pl.BlockSpec((MOVE_TILE, 6 * D_MODEL), row)
    smem_blk = lambda f: pl.BlockSpec((1, 1, MOVE_TILE), f, memory_space=pltpu.SMEM)
    return pl.pallas_call(
        functools.partial(_combine_body, mod3d),
        grid=(steps,),
        in_specs=[smem_blk(lambda i: (i, 0, 0)), smem_blk(lambda i: (jnp.minimum(i + 1, steps - 1), 0, 0)),
                  pl.BlockSpec(memory_space=pl.ANY),
                  pl.BlockSpec((MOVE_TILE, D_MODEL), row), mod_spec],
        out_specs=pl.BlockSpec((MOVE_TILE, D_MODEL), row),
        out_shape=jax.ShapeDtypeStruct((t, D_MODEL), F32),
        scratch_shapes=[pltpu.VMEM((2, MOVE_TILE, HALF), U32), pltpu.SemaphoreType.DMA((2,))],
        compiler_params=_cparams(("arbitrary",)),
        name="combine",
    )(dest3, dest3, ys, x1, mod)


def _class_expert_tables():
    ea, eb = [], []
    for g in range(N_EXPERT_GROUPS):
        for a in range(EXPERTS_PER_GROUP):
            for b in range(a + 1, EXPERTS_PER_GROUP):
                ea.append(g * EXPERTS_PER_GROUP + a)
                eb.append(g * EXPERTS_PER_GROUP + b)
    return np.asarray(ea, np.int32), np.asarray(eb, np.int32)


def _moe_dispatched(rows, cls, x1, mod, wgu, wd, tiles_per_seq):
    t = x1.shape[0]
    rank, counts = _class_ranks(cls, 512)
    counts = counts[0, 0:N_CLASSES].astype(jnp.int32)
    padded = ((counts + ROW_TILE - 1) // ROW_TILE) * ROW_TILE
    seg_end = jnp.cumsum(padded)
    seg_start = seg_end - padded
    dest = seg_start[cls[:, 0]] + rank[:, 0]
    dest3 = dest.reshape(t // MOVE_TILE, 1, MOVE_TILE)
    n_sorted = t + N_CLASSES * ROW_TILE
    tile_row0 = jnp.arange(n_sorted // ROW_TILE, dtype=jnp.int32) * ROW_TILE
    tile_cls = jnp.minimum(jnp.searchsorted(seg_end, tile_row0, side="right"), N_CLASSES - 1)
    ea_np, eb_np = _class_expert_tables()
    ea_t = jnp.asarray(ea_np)[tile_cls]
    eb_t = jnp.asarray(eb_np)[tile_cls]
    n_used = (seg_end[N_CLASSES - 1] // ROW_TILE).astype(jnp.int32).reshape(1)
    src = _invert(dest, n_sorted)[:, 0]
    ys = _experts(ea_t, eb_t, n_used, src.reshape(n_sorted // ROW_TILE, 1, ROW_TILE), rows, wgu, wd)
    return _combine(dest3, ys, x1, mod, tiles_per_seq)


def _pad_lanes(v, fill):
    n = v.shape[-1]
    return jnp.concatenate([v, jnp.full((LANES - n,), fill, v.dtype)]).reshape(1, LANES)


def kernel(x_prompt, x_sample, c_prompt, c_sample, cache_k, cache_v, cache_kidx, page_table, w_ada, b_ada, norm_mix_g, norm_ffn_g, w_in, q_norm_g, k_norm_g, kidx_norm_g, gm_ln_g, gm_ln_b, gm_spatial_w, gm_spatial_b, w_proj_attn, w_proj_gmlp, w_out, w_router_group, b_router_group, w_router_expert, b_router_expert, w_expert_gate, w_expert_up, w_expert_down):
    depth = w_ada.shape[0]
    assert depth == 1
    l = 0
    bp, sp, _ = x_prompt.shape
    bs, ss, _ = x_sample.shape
    assert ss == 1
    n_pages = page_table.shape[1]
    past = n_pages * PAGE_SIZE
    tp = bp * sp

    w = w_in[l]
    zpad = jnp.zeros((D_MODEL, LANES - IDX_DIM - IDX_HEADS), F32)
    w_pad = jnp.concatenate([w[:, 0:1024], w[:, 1024:1088], w[:, 1088:1092], zpad, w[:, 1092:]], axis=1).astype(BF16)
    seg_np = (np.arange(LANES)[:, None] // HEAD_DIM) == (np.arange(LANES)[None, :] // HEAD_DIM)
    seg = jnp.asarray(seg_np, BF16)
    segki = jnp.asarray(seg_np & (np.arange(LANES)[:, None] < IDX_DIM) & (np.arange(LANES)[None, :] < IDX_DIM), BF16)
    consts = (norm_mix_g[l].reshape(1, D_MODEL),
              jnp.tile(q_norm_g[l], 2).reshape(1, LANES), jnp.tile(k_norm_g[l], 2).reshape(1, LANES),
              _pad_lanes(kidx_norm_g[l], 1.0),
              gm_ln_g[l].reshape(1, GM_WIDTH), gm_ln_b[l].reshape(1, GM_WIDTH), seg, segki)
    wpa = w_proj_attn[l].astype(BF16)
    wpg = w_proj_gmlp[l].astype(BF16)
    wo = w_out[l].astype(BF16)
    wr32 = jnp.concatenate([w_router_expert[l], w_router_group[l],
                            jnp.zeros((D_MODEL, LANES - N_EXPERTS - N_EXPERT_GROUPS), F32)], axis=1)
    wr_hi = wr32.astype(BF16)
    wr_lo = (wr32 - wr_hi.astype(F32)).astype(BF16)
    wr = jnp.concatenate([wr_hi, wr_lo], axis=1)
    br = _pad_lanes(jnp.concatenate([b_router_expert[l], b_router_group[l]]), 0.0)
    wgu = jnp.concatenate([w_expert_gate[l], w_expert_up[l]], axis=2).astype(BF16)
    wd = w_expert_down[l].astype(BF16)
    g2 = norm_ffn_g[l].reshape(1, D_MODEL)

    mod = _adaln(jnp.concatenate([c_prompt, c_sample], axis=0), w_ada[l], b_ada[l])
    mod_p = mod[0:bp].reshape(bp, 1, 6 * D_MODEL)
    mod_s = mod[bp:bp + bs]
    pos = jnp.concatenate([jnp.arange(sp, dtype=jnp.int32),
                           jnp.full((8,), past, jnp.int32)]).astype(F32).reshape(sp + 8, 1)
    tabs = _rope_tables(pos)
    tabs_p = tuple(t[0:sp] for t in tabs)
    tabs_s = tuple(t[sp:sp + 1] for t in tabs)

    tm = 512
    tps = sp // tm
    (q, kb, vb, qi, kib, kiwi, u, vg, ga, gb, kt_f, vt_f, kit_f, vt) = _project(
        x_prompt.reshape(tp, D_MODEL), mod_p, tabs_p, consts, w_pad, tm, tps, BF16)
    topk_p = min(TOPK_MAX, sp // 4)
    r3 = lambda a: a.reshape(bp, sp, a.shape[-1])
    attn_p = _prompt_attention(r3(qi), r3(kiwi), r3(kib), r3(q), r3(kb), vt, topk_p)
    bt = jnp.concatenate([gm_spatial_b[l].T, jnp.zeros((CHUNK, LANES - GM_GROUPS), F32)], axis=1)
    gm_p = _gmlp_prompt(u, vg, gm_spatial_w[l], bt, tm)
    x1_p, rows_p, cls_p = _merge(attn_p, gm_p, ga, gb, x_prompt.reshape(tp, D_MODEL), mod_p,
                                 wpa, wpg, wo, g2, wr, br, tm, tps, True)
    y_p = _moe_dispatched(rows_p, cls_p, x1_p, mod_p, wgu, wd, sp // MOVE_TILE)

    (q_s, kb_s, vb_s, qi_s, kib_s, kiwi_s, u_s, vg_s, ga_s, gb_s, kt_s, vt_s, kit_s, _) = _project(
        x_sample.reshape(bs, D_MODEL), mod_s, tabs_s, consts, w_pad, bs, 1, F32)
    qi8 = jnp.concatenate([qi_s.reshape(bs, IDX_HEADS, IDX_DIM),
                           jnp.zeros((bs, 8 - IDX_HEADS, IDX_DIM), BF16)], axis=1)
    wi8 = jnp.concatenate([kiwi_s[:, IDX_DIM:IDX_DIM + IDX_HEADS],
                           jnp.zeros((bs, 8 - IDX_HEADS), F32)], axis=1).reshape(bs, 8, 1)
    kidx_t = jnp.transpose(cache_kidx[l], (0, 2, 1))
    scores = _sample_scores(page_table, kidx_t, qi8, wi8, kib_s.reshape(bs, 1, IDX_DIM))
    topk_s = min(TOPK_MAX, (past + ss) // 4)
    bias = _sample_select(scores.reshape(bs, past + LANES), topk_s).reshape(bs, 1, past + LANES)
    kvw = N_KV_HEADS * HEAD_DIM
    k_t = jnp.transpose(cache_k[l], (0, 2, 3, 1)).reshape(-1, kvw, PAGE_SIZE)
    v_t = jnp.transpose(cache_v[l], (0, 2, 3, 1)).reshape(-1, kvw, PAGE_SIZE)
    q3 = q_s.reshape(bs, N_HEADS, HEAD_DIM)
    zq = jnp.zeros_like(q3)
    in_first = (jnp.arange(N_HEADS) < N_HEADS // N_KV_HEADS)[None, :, None]
    q2_s = jnp.where(in_first, jnp.concatenate([q3, zq], axis=2), jnp.concatenate([zq, q3], axis=2))
    attn_s = _sample_attention(page_table, k_t, v_t, q2_s, bias,
                               kb_s.reshape(bs, 1, kvw), vb_s.reshape(bs, 1, kvw))
    gd = GM_WIDTH // GM_GROUPS
    w0 = jnp.repeat(gm_spatial_w[l][:, 0, 0], gd).reshape(1, GM_WIDTH)
    b0 = jnp.repeat(gm_spatial_b[l][:, 0], gd).reshape(1, GM_WIDTH)
    gm_s = _gmlp_sample(u_s, vg_s, w0, b0)
    x1_s, h2_s, gate_s = _merge(attn_s.reshape(1, 1, bs, N_HEADS * HEAD_DIM), gm_s, ga_s, gb_s,
                                x_sample.reshape(bs, D_MODEL), mod_s, wpa, wpg, wo, g2, wr, br, bs, 1, False)
    y_s = _moe(h2_s, gate_s, x1_s, mod_s, wgu, wd, bs, 1)

    def rows_kv(a_t, n, s):
        return jnp.transpose(a_t.reshape(n, N_KV_HEADS, HEAD_DIM, s), (0, 3, 1, 2))[None]

    def rows_ki(a_t):
        return jnp.transpose(a_t, (0, 2, 1))[None]

    return (y_p.reshape(bp, sp, D_MODEL), y_s.reshape(bs, ss, D_MODEL),
            rows_kv(kt_f, bp, sp), rows_kv(vt_f, bp, sp), rows_ki(kit_f),
            rows_kv(kt_s, 1, bs).reshape(1, bs, ss, N_KV_HEADS, HEAD_DIM),
            rows_kv(vt_s, 1, bs).reshape(1, bs, ss, N_KV_HEADS, HEAD_DIM),
            rows_ki(kit_s).reshape(1, bs, ss, IDX_DIM), vg_s.reshape(1, bs, ss, GM_WIDTH))
```

```python
import functools

import numpy as np
import jax
import jax.numpy as jnp
from jax import lax
from jax.experimental import pallas as pl
from jax.experimental.pallas import tpu as pltpu

F32 = jnp.float32
BF16 = jnp.bfloat16
U32 = jnp.uint32

D_MODEL = 1024
N_HEADS = 8
HEAD_DIM = 64
N_KV_HEADS = 2
ROT_DIM = 16
ROPE_THETA = 500000.0
IDX_HEADS = 4
IDX_DIM = 64
TOPK_MAX = 256
Q_BLOCK = 128
GM_WIDTH = 512
GM_GROUPS = 8
CHUNK = 128
N_EXPERT_GROUPS = 4
EXPERTS_PER_GROUP = 8
N_EXPERTS = 32
D_EXPERT = 256
EPS = 1e-6
PAGE_SIZE = 128
LANES = 128
SUB = 8

C_Q, C_K, C_V, C_QI, C_KIWI, C_U, C_VG, C_GATE = 0, 512, 640, 768, 1024, 1152, 1664, 2176
D_IN_PAD = 4224
VMEM_LIMIT = 56 * 1024 * 1024

N_PAIRS = EXPERTS_PER_GROUP * (EXPERTS_PER_GROUP - 1) // 2
N_CLASSES = N_EXPERT_GROUPS * N_PAIRS
ROW_TILE = 128
HALF = D_MODEL // 2
ROW_WORDS = HALF + LANES
MOVE_TILE = 256
HI_MASK = 0xFFFF0000


def _cparams(sem):
    return pltpu.CompilerParams(dimension_semantics=sem, vmem_limit_bytes=VMEM_LIMIT)


def _pack_halves(x):
    n = x.shape[1] // 2
    hi = lax.bitcast_convert_type(x[:, 0:n], U32) & jnp.uint32(HI_MASK)
    lo = lax.bitcast_convert_type(x[:, n:2 * n], U32) >> 16
    return hi | lo


def _unpack_halves(u):
    hi = lax.bitcast_convert_type(u & jnp.uint32(HI_MASK), F32)
    lo = lax.bitcast_convert_type(u << 16, F32)
    return jnp.concatenate([hi, lo], axis=1)


def _adaln_body(c_ref, w_ref, b_ref, o_ref):
    c = c_ref[...]
    a = c * jax.nn.sigmoid(c)
    o_ref[...] = jnp.dot(a, w_ref[...], preferred_element_type=F32,
                         precision=lax.Precision.HIGHEST) + b_ref[...]


def _adaln(c, w, b):
    r = c.shape[0]
    n = w.shape[1]
    bn = 1536
    return pl.pallas_call(
        _adaln_body,
        grid=(n // bn,),
        in_specs=[pl.BlockSpec((r, D_MODEL), lambda j: (0, 0)),
                  pl.BlockSpec((D_MODEL, bn), lambda j: (0, j)),
                  pl.BlockSpec((1, bn), lambda j: (0, j))],
        out_specs=pl.BlockSpec((r, bn), lambda j: (0, j)),
        out_shape=jax.ShapeDtypeStruct((r, n), F32),
        compiler_params=_cparams(("arbitrary",)),
        name="adaln",
    )(c, w, b.reshape(1, n))


def _rope_table_body(pos_ref, invf_ref, sa_m_ref, sb_m_ref, c_ref, sa_ref, sb_ref):
    ang = pos_ref[...] * invf_ref[...]
    s = jnp.sin(ang)
    c_ref[...] = jnp.cos(ang)
    sa_ref[...] = s * sa_m_ref[...]
    sb_ref[...] = s * sb_m_ref[...]


def _rope_tables(pos):
    half = ROT_DIM // 2
    inv_freq = ROPE_THETA ** (-jnp.arange(half, dtype=F32) / half)
    d = np.arange(LANES) % HEAD_DIM
    invf = jnp.where(jnp.asarray(d < ROT_DIM), inv_freq[d % half], 0.0).reshape(1, LANES)
    sa_m = jnp.asarray(np.where(d < half, -1.0, 0.0), F32).reshape(1, LANES)
    sb_m = jnp.asarray(np.where((d >= half) & (d < ROT_DIM), 1.0, 0.0), F32).reshape(1, LANES)
    r = pos.shape[0]
    return pl.pallas_call(
        _rope_table_body,
        out_shape=[jax.ShapeDtypeStruct((r, LANES), F32)] * 3,
        name="rope_tables",
    )(pos, invf, sa_m, sb_m)


def _rope(y, c, sa, sb):
    return y * c + pltpu.roll(y, LANES - ROT_DIM // 2, 1) * sa + pltpu.roll(y, ROT_DIM // 2, 1) * sb


def _seg_rms(r, seg, gain):
    ss = r * r
    hi = ss.astype(BF16)
    lo = (ss - hi.astype(F32)).astype(BF16)
    tot = jnp.dot(hi, seg, preferred_element_type=F32) + jnp.dot(lo, seg, preferred_element_type=F32)
    return r * lax.rsqrt(tot * (1.0 / HEAD_DIM) + EPS) * gain


def _proj_body(mod3d, x_ref, mod_ref, g_ref, w_ref, c_ref, sa_ref, sb_ref, qg_ref, kg_ref, kig_ref,
               lng_ref, lnb_ref, seg_ref, segki_ref,
               q_o, kb_o, vb_o, qi_o, kib_o, kiwi_o, u_o, vg_o, ga_o, gb_o, kt_o, vtf_o, kit_o, vt_o):
    x = x_ref[...]
    m = mod_ref[0] if mod3d else mod_ref[...]
    shift1 = m[:, 0:D_MODEL]
    scale1 = m[:, D_MODEL:2 * D_MODEL]
    ms = jnp.mean(x * x, axis=-1, keepdims=True)
    y = x * lax.rsqrt(ms + EPS) * g_ref[...]
    h = (y * (1.0 + scale1) + shift1).astype(BF16)
    c, sa, sb = c_ref[...], sa_ref[...], sb_ref[...]
    seg = seg_ref[...]

    def proj(a, b):
        return jnp.dot(h, w_ref[:, a:b], preferred_element_type=F32)

    for g in range(4):
        r = proj(C_Q + g * LANES, C_Q + (g + 1) * LANES)
        yq = _rope(_seg_rms(r, seg, qg_ref[...]), c, sa, sb) * (HEAD_DIM ** -0.5)
        q_o[:, g * LANES:(g + 1) * LANES] = yq.astype(BF16)
    r = proj(C_K, C_K + LANES)
    yk = _rope(_seg_rms(r, seg, kg_ref[...]), c, sa, sb)
    kt_o[0] = yk.T
    kb_o[...] = yk.astype(BF16)
    r = proj(C_V, C_V + LANES)
    r_t = r.T
    vtf_o[0] = r_t
    vb_o[...] = r.astype(BF16)
    vt_o[0] = r_t.astype(BF16)
    for g in range(2):
        r = proj(C_QI + g * LANES, C_QI + (g + 1) * LANES)
        yqi = _rope(r, c, sa, sb) * (IDX_DIM ** -0.5 * IDX_HEADS ** -0.5)
        qi_o[:, g * LANES:(g + 1) * LANES] = yqi.astype(BF16)
    r = proj(C_KIWI, C_KIWI + LANES)
    yki = _rope(_seg_rms(r, segki_ref[...], kig_ref[...]), c, sa, sb)
    lane = lax.broadcasted_iota(jnp.int32, r.shape, 1)
    kiwi = jnp.where(lane < IDX_DIM, yki, r)
    kiwi_o[...] = kiwi
    kit_o[0] = kiwi.T[0:IDX_DIM, :]
    kib_o[...] = kiwi[:, 0:IDX_DIM].astype(BF16)
    r = proj(C_U, C_U + GM_WIDTH)
    u_o[...] = jax.nn.gelu(r).astype(BF16)
    r = proj(C_VG, C_VG + GM_WIDTH)
    gl = jax.nn.gelu(r)
    mu = jnp.mean(gl, axis=-1, keepdims=True)
    dv = gl - mu
    var = jnp.mean(dv * dv, axis=-1, keepdims=True)
    vg_o[...] = (dv * lax.rsqrt(var + EPS) * lng_ref[...] + lnb_ref[...]).astype(vg_o.dtype)
    r = proj(C_GATE, C_GATE + D_MODEL)
    ga_o[...] = jax.nn.sigmoid(r).astype(BF16)
    r = proj(C_GATE + D_MODEL, C_GATE + 2 * D_MODEL)
    gb_o[...] = jax.nn.sigmoid(r).astype(BF16)


def _project(x, mod, tables, consts, w_pad, tm, tiles_per_seq, vg_dtype):
    t = x.shape[0]
    mod3d = mod.ndim == 3
    c_t, sa_t, sb_t = tables
    g_mix, qg, kg, kig, lng, lnb, seg, segki = consts
    row = lambda i: (i, 0)
    fixed = lambda i: (0, 0)
    if mod3d:
        mod_spec = pl.BlockSpec((1, 1, 6 * D_MODEL), lambda i: (i // tiles_per_seq, 0, 0))
        tab_spec = pl.BlockSpec((tm, LANES), lambda i: (i % tiles_per_seq, 0))
    else:
        mod_spec = pl.BlockSpec((tm, 6 * D_MODEL), row)
        tab_spec = pl.BlockSpec((1, LANES), fixed)
    widths = [(512, BF16), (128, BF16), (128, BF16), (256, BF16),
              (64, BF16), (128, F32), (512, BF16), (512, vg_dtype), (1024, BF16), (1024, BF16)]
    n_seq, seq = t // (tm * tiles_per_seq), tm * tiles_per_seq
    t_idx = lambda i: (i // tiles_per_seq, 0, i % tiles_per_seq)
    t_outs = [(LANES, F32), (LANES, F32), (IDX_DIM, F32), (LANES, BF16)]
    return pl.pallas_call(
        functools.partial(_proj_body, mod3d),
        grid=(t // tm,),
        in_specs=[pl.BlockSpec((tm, D_MODEL), row), mod_spec,
                  pl.BlockSpec((1, D_MODEL), fixed),
                  pl.BlockSpec((D_MODEL, D_IN_PAD), fixed),
                  tab_spec, tab_spec, tab_spec,
                  pl.BlockSpec((1, LANES), fixed), pl.BlockSpec((1, LANES), fixed), pl.BlockSpec((1, LANES), fixed),
                  pl.BlockSpec((1, GM_WIDTH), fixed), pl.BlockSpec((1, GM_WIDTH), fixed),
                  pl.BlockSpec((LANES, LANES), fixed), pl.BlockSpec((LANES, LANES), fixed)],
        out_specs=[pl.BlockSpec((tm, w), row) for w, _ in widths]
        + [pl.BlockSpec((1, f, tm), t_idx) for f, _ in t_outs],
        out_shape=[jax.ShapeDtypeStruct((t, w), dt) for w, dt in widths]
        + [jax.ShapeDtypeStruct((n_seq, f, seq), dt) for f, dt in t_outs],
        compiler_params=_cparams(("arbitrary",)),
        name="project",
    )(x, mod, g_mix, w_pad, c_t, sa_t, sb_t, qg, kg, kig, lng, lnb, seg, segki)


def _select_bias(s_ref, bias_ref, rows, width, topk, tie_check_start=16):
    nb = width // LANES
    kf = float(topk)
    neg, pos = -jnp.inf, jnp.inf

    def blk(j):
        return s_ref[:, j * LANES:(j + 1) * LANES]

    def count_above(t):
        tb = jnp.broadcast_to(t, (rows, LANES))
        acc = jnp.zeros((rows, LANES), F32)
        for j in range(nb):
            acc = acc + jnp.where(blk(j) > tb, 1.0, 0.0)
        return jnp.sum(acc, axis=1, keepdims=True)

    mx = jnp.full((rows, LANES), neg, F32)
    mn = jnp.full((rows, LANES), pos, F32)
    for j in range(nb):
        b = blk(j)
        mx = jnp.maximum(mx, b)
        mn = jnp.minimum(mn, jnp.where(b == neg, pos, b))
    hi0 = jnp.max(mx, axis=1, keepdims=True)
    smin = jnp.min(mn, axis=1, keepdims=True)
    lo0 = smin - jnp.abs(smin) - 1.0
    f_lo0 = count_above(lo0)
    zeros = jnp.zeros((rows, 1), F32)

    def active_of(f_lo, tie):
        return jnp.logical_and(f_lo > kf, tie == 0.0)

    def cond(st):
        _, _, _, f_lo, _, tie = st
        return jnp.max(jnp.where(active_of(f_lo, tie), 1.0, 0.0)) > 0.0

    def body(st):
        it, lo, hi, f_lo, f_hi, tie = st
        active = active_of(f_lo, tie)
        mid = lo + (hi - lo) * 0.5
        stuck = jnp.logical_or(mid <= lo, mid >= hi)
        cnt = count_above(mid)
        ge = cnt >= kf
        up_lo = jnp.logical_and(active, ge)
        up_hi = jnp.logical_and(active, jnp.logical_not(ge))
        lo = jnp.where(up_lo, mid, lo)
        f_lo = jnp.where(up_lo, cnt, f_lo)
        hi = jnp.where(up_hi, mid, hi)
        f_hi = jnp.where(up_hi, cnt, f_hi)
        tie = jnp.where(jnp.logical_and(active, stuck), 1.0, tie)

        def tie_check(_):
            lob = jnp.broadcast_to(lo, (rows, LANES))
            hib = jnp.broadcast_to(hi, (rows, LANES))
            vmx = jnp.full((rows, LANES), neg, F32)
            vmn = jnp.full((rows, LANES), pos, F32)
            for j in range(nb):
                b = blk(j)
                inn = jnp.logical_and(b > lob, b <= hib)
                vmx = jnp.maximum(vmx, jnp.where(inn, b, neg))
                vmn = jnp.minimum(vmn, jnp.where(inn, b, pos))
            one_value = jnp.max(vmx, axis=1, keepdims=True) == jnp.min(vmn, axis=1, keepdims=True)
            return jnp.where(one_value, 1.0, tie)

        run_check = jnp.logical_and(it >= tie_check_start, it % 4 == 0)
        tie = lax.cond(run_check, tie_check, lambda _: tie, 0)
        return it + 1, lo, hi, f_lo, f_hi, tie

    _, lo, hi, f_lo, f_hi, _ = lax.while_loop(cond, body, (jnp.int32(0), lo0, hi0, f_lo0, zeros, zeros))
    lob = jnp.broadcast_to(lo, (rows, LANES))
    need_prefix = jnp.max(jnp.where(f_lo > kf, 1.0, 0.0)) > 0.0

    @pl.when(jnp.logical_not(need_prefix))
    def _():
        for j in range(nb):
            bias_ref[:, j * LANES:(j + 1) * LANES] = jnp.where(blk(j) > lob, 0.0, neg)

    @pl.when(need_prefix)
    def _():
        hib = jnp.broadcast_to(hi, (rows, LANES))
        need = kf - f_hi
        ri = lax.broadcasted_iota(jnp.int32, (LANES, LANES), 0)
        ci = lax.broadcasted_iota(jnp.int32, (LANES, LANES), 1)
        upper = jnp.where(ri < ci, 1.0, 0.0).astype(BF16)
        off = jnp.zeros((rows, 1), F32)
        for j in range(nb):
            b = blk(j)
            inn = jnp.logical_and(b > lob, b <= hib)
            innf = jnp.where(inn, 1.0, 0.0)
            before = jnp.dot(innf.astype(BF16), upper, preferred_element_type=F32) + off
            sel = jnp.logical_or(b > hib, jnp.logical_and(inn, before < need))
            bias_ref[:, j * LANES:(j + 1) * LANES] = jnp.where(sel, 0.0, neg)
            off = off + jnp.sum(innf, axis=1, keepdims=True)


_NT = (((1,), (1,)), ((), ()))


def _sub_reduce(x, op):
    for sh in (4, 2, 1):
        x = op(x, pltpu.roll(x, sh, 0))
    return x


def _select_bias_t(s_ref, bias_ref, width, topk, n_adm, steps_per_check=4, tie_check_from=4):
    rb = 64
    nb = width // rb
    kf = float(topk)
    neg, pos = -jnp.inf, jnp.inf

    def blk(j):
        return s_ref[j * rb:(j + 1) * rb, :]

    def fold(x, op):
        y = x[0:SUB]
        for a in range(1, rb // SUB):
            y = op(y, x[a * SUB:(a + 1) * SUB])
        return _sub_reduce(y, op)

    def tile(v):
        return jnp.concatenate([v] * (rb // SUB), axis=0)

    def count_above(t):
        tb = tile(t)
        acc = jnp.zeros((rb, LANES), F32)
        for j in range(nb):
            acc = acc + jnp.where(blk(j) > tb, 1.0, 0.0)
        return fold(acc, jnp.add)

    mx = jnp.full((rb, LANES), neg, F32)
    mn = jnp.full((rb, LANES), pos, F32)
    for j in range(nb):
        b = blk(j)
        mx = jnp.maximum(mx, b)
        mn = jnp.minimum(mn, jnp.where(b == neg, pos, b))
    hi0 = fold(mx, jnp.maximum)
    smin = fold(mn, jnp.minimum)
    lo0 = smin - jnp.abs(smin) - 1.0
    zeros = jnp.zeros((SUB, LANES), F32)

    def active_of(f_lo, tie):
        return jnp.logical_and(f_lo > kf, tie == 0.0)

    def any_lane(cond):
        return jnp.max(jnp.where(cond, 1.0, 0.0)) > 0.0

    def step(lo, hi, f_lo, f_hi, tie, interpolate):
        active = active_of(f_lo, tie)
        if interpolate:
            frac = jnp.clip((f_lo - kf + 0.5) / jnp.maximum(f_lo - f_hi, 1.0), 1.0 / 16, 15.0 / 16)
        else:
            frac = 0.5
        mid = lo + (hi - lo) * frac
        stuck = jnp.logical_or(mid <= lo, mid >= hi)
        cnt = count_above(mid)
        ge = cnt >= kf
        up_lo = jnp.logical_and(active, ge)
        up_hi = jnp.logical_and(active, jnp.logical_not(ge))
        return (jnp.where(up_lo, mid, lo), jnp.where(up_hi, mid, hi), jnp.where(up_lo, cnt, f_lo),
                jnp.where(up_hi, cnt, f_hi), jnp.where(jnp.logical_and(active, stuck), 1.0, tie))

    def cond(st):
        _, _, _, f_lo, _, tie = st
        return any_lane(active_of(f_lo, tie))

    def body(st):
        it, lo, hi, f_lo, f_hi, tie = st
        for k in range(steps_per_check):
            lo, hi, f_lo, f_hi, tie = step(lo, hi, f_lo, f_hi, tie, k % 2 == 0)

        def tie_check(_):
            lob, hib = tile(lo), tile(hi)
            vmx = jnp.full((rb, LANES), neg, F32)
            vmn = jnp.full((rb, LANES), pos, F32)
            for j in range(nb):
                b = blk(j)
                inn = jnp.logical_and(b > lob, b <= hib)
                vmx = jnp.maximum(vmx, jnp.where(inn, b, neg))
                vmn = jnp.minimum(vmn, jnp.where(inn, b, pos))
            return jnp.where(fold(vmx, jnp.maximum) == fold(vmn, jnp.minimum), 1.0, tie)

        run_check = jnp.logical_and(it + 1 >= tie_check_from, any_lane(active_of(f_lo, tie)))
        tie = lax.cond(run_check, tie_check, lambda _: tie, 0)
        return it + 1, lo, hi, f_lo, f_hi, tie

    _, lo, hi, f_lo, f_hi, _ = lax.while_loop(cond, body, (jnp.int32(0), lo0, hi0, n_adm, zeros, zeros))
    lob = tile(lo)
    need_prefix = any_lane(f_lo > kf)

    @pl.when(jnp.logical_not(need_prefix))
    def _():
        for j in range(nb):
            bias_ref[j * rb:(j + 1) * rb, :] = jnp.where(blk(j) > lob, 0.0, neg)

    @pl.when(need_prefix)
    def _():
        need = (kf - f_hi)[0:1]
        lo1, hi1 = lo[0:1], hi[0:1]
        ri = lax.broadcasted_iota(jnp.int32, (LANES, LANES), 0)
        ci = lax.broadcasted_iota(jnp.int32, (LANES, LANES), 1)
        lower = jnp.where(ci < ri, 1.0, 0.0).astype(BF16)
        off = jnp.zeros((1, LANES), F32)
        for j in range(width // LANES):
            b = s_ref[j * LANES:(j + 1) * LANES, :]
            inn = jnp.logical_and(b > lo1, b <= hi1)
            innf = jnp.where(inn, 1.0, 0.0)
            before = jnp.dot(lower, innf.astype(BF16), preferred_element_type=F32) + off
            sel = jnp.logical_or(b > hi1, jnp.logical_and(inn, before < need))
            bias_ref[j * LANES:(j + 1) * LANES, :] = jnp.where(sel, 0.0, neg)
            off = off + jnp.sum(innf, axis=0, keepdims=True)


def _prompt_attn_body(blk_i, topk, qi_ref, kiwi_ref, kib_ref, q_ref, k_ref, vt_ref, o_ref, s_ref, bias_ref):
    width = (blk_i + 1) * Q_BLOCK
    neg = -jnp.inf
    kiwi_t = kiwi_ref[0].T
    chunk = 512
    for c0 in range(0, width, chunk):
        c1 = min(width, c0 + chunk)
        kib = kib_ref[0, c0:c1, :]
        sc = None
        for h in range(IDX_HEADS):
            d = lax.dot_general(kib, qi_ref[0, :, h * IDX_DIM:(h + 1) * IDX_DIM], _NT,
                                preferred_element_type=F32)
            t = jnp.maximum(d, 0.0) * kiwi_t[IDX_DIM + h:IDX_DIM + h + 1, :]
            sc = t if sc is None else sc + t
        s_ref[c0:c1, :] = sc
    ki = lax.broadcasted_iota(jnp.int32, (Q_BLOCK, Q_BLOCK), 0)
    qj = lax.broadcasted_iota(jnp.int32, (Q_BLOCK, Q_BLOCK), 1)
    d0 = width - Q_BLOCK
    s_ref[d0:width, :] = jnp.where(ki <= qj, s_ref[d0:width, :], neg)
    if width > topk:
        n_adm = (lax.broadcasted_iota(jnp.int32, (SUB, LANES), 1) + (d0 + 1)).astype(F32)
        _select_bias_t(s_ref, bias_ref, width, topk, n_adm)
    else:
        bias_ref[...] = jnp.where(s_ref[...] == neg, neg, 0.0)
    lane = lax.broadcasted_iota(jnp.int32, (Q_BLOCK, LANES), 1)
    hpg = N_HEADS // N_KV_HEADS
    for h in range(N_HEADS):
        g = h // hpg
        qp = q_ref[0, :, (h // 2) * LANES:(h // 2 + 1) * LANES].astype(F32)
        if h % 2 != g:
            qp = pltpu.roll(qp, HEAD_DIM, 1)
        q2 = jnp.where((lane >= HEAD_DIM) == (g == 1), qp, 0.0).astype(BF16)
        s = lax.dot_general(k_ref[0], q2, _NT, preferred_element_type=F32) + bias_ref[...]
        m = jnp.max(s, axis=0, keepdims=True)
        p = jnp.exp(s - m)
        l = jnp.sum(p, axis=0, keepdims=True)
        o_t = jnp.dot(vt_ref[0, g * HEAD_DIM:(g + 1) * HEAD_DIM, :], p.astype(BF16), preferred_element_type=F32)
        o_ref[0, :, h * HEAD_DIM:(h + 1) * HEAD_DIM] = (o_t / l).T.astype(BF16)


def _prompt_attention(qi, kiwi, kib, q, kb, vt, topk):
    b, s, _ = q.shape
    outs = []
    for i in range(s // Q_BLOCK):
        width = (i + 1) * Q_BLOCK
        qblk = lambda bb, i=i: (bb, i, 0)
        kall = lambda bb: (bb, 0, 0)
        outs.append(pl.pallas_call(
            functools.partial(_prompt_attn_body, i, topk),
            grid=(b,),
            in_specs=[pl.BlockSpec((1, Q_BLOCK, IDX_HEADS * IDX_DIM), qblk),
                      pl.BlockSpec((1, Q_BLOCK, LANES), qblk),
                      pl.BlockSpec((1, width, IDX_DIM), kall),
                      pl.BlockSpec((1, Q_BLOCK, N_HEADS * HEAD_DIM), qblk),
                      pl.BlockSpec((1, width, N_KV_HEADS * HEAD_DIM), kall),
                      pl.BlockSpec((1, N_KV_HEADS * HEAD_DIM, width), kall)],
            out_specs=pl.BlockSpec((1, Q_BLOCK, N_HEADS * HEAD_DIM), lambda bb: (bb, 0, 0)),
            out_shape=jax.ShapeDtypeStruct((b, Q_BLOCK, N_HEADS * HEAD_DIM), BF16),
            scratch_shapes=[pltpu.VMEM((width, Q_BLOCK), F32), pltpu.VMEM((width, Q_BLOCK), F32)],
            compiler_params=_cparams(("arbitrary",)),
            name=f"prompt_attn_{i}",
        )(qi, kiwi, kib, q, kb, vt))
    return jnp.stack(outs)


def _page_copies(pt_ref, sample, src_hbm, buf, slot, sem, n_pages):
    return [pltpu.make_async_copy(src_hbm.at[pt_ref[sample, p]],
                                  buf.at[slot, :, pl.ds(p * PAGE_SIZE, PAGE_SIZE)], sem.at[slot])
            for p in range(n_pages)]


def _sample_scores_body(n_pages, pt_ref, kidx_hbm, qi_ref, wi_ref, kin_ref, o_ref, buf, sem):
    s = pl.program_id(0)
    slot = s % 2

    @pl.when(s == 0)
    def _():
        for cp in _page_copies(pt_ref, 0, kidx_hbm, buf, 0, sem, n_pages):
            cp.start()

    @pl.when(s + 1 < pl.num_programs(0))
    def _():
        for cp in _page_copies(pt_ref, s + 1, kidx_hbm, buf, 1 - slot, sem, n_pages):
            cp.start()

    for cp in _page_copies(pt_ref, s, kidx_hbm, buf, slot, sem, n_pages):
        cp.wait()
    qi = qi_ref[0]
    wi = wi_ref[0]
    ki_t = buf[slot].astype(BF16)
    d = jnp.dot(qi, ki_t, preferred_element_type=F32)
    past = jnp.sum(jnp.maximum(d, 0.0) * wi, axis=0, keepdims=True)
    dn = jnp.sum(qi.astype(F32) * kin_ref[0].astype(F32), axis=1, keepdims=True)
    new = jnp.sum(jnp.maximum(dn, 0.0) * wi, axis=0, keepdims=True)
    lane = lax.broadcasted_iota(jnp.int32, (1, LANES), 1)
    tail = jnp.where(lane == 0, jnp.broadcast_to(new, (1, LANES)), -jnp.inf)
    o_ref[0] = jnp.concatenate([past, tail], axis=1)


def _sample_scores(page_table, kidx_pool, qi8, wi8, ki_new):
    n, n_pages = page_table.shape
    past = n_pages * PAGE_SIZE
    grid_spec = pltpu.PrefetchScalarGridSpec(
        num_scalar_prefetch=1,
        grid=(n,),
        in_specs=[pl.BlockSpec(memory_space=pl.ANY),
                  pl.BlockSpec((1, 8, IDX_DIM), lambda s, pt: (s, 0, 0)),
                  pl.BlockSpec((1, 8, 1), lambda s, pt: (s, 0, 0)),
                  pl.BlockSpec((1, 1, IDX_DIM), lambda s, pt: (s, 0, 0))],
        out_specs=pl.BlockSpec((1, 1, past + LANES), lambda s, pt: (s, 0, 0)),
        scratch_shapes=[pltpu.VMEM((2, IDX_DIM, past), F32), pltpu.SemaphoreType.DMA((2,))],
    )
    return pl.pallas_call(
        functools.partial(_sample_scores_body, n_pages),
        grid_spec=grid_spec,
        out_shape=jax.ShapeDtypeStruct((n, 1, past + LANES), F32),
        compiler_params=_cparams(("arbitrary",)),
        name="sample_scores",
    )(page_table, kidx_pool, qi8, wi8, ki_new)


def _sample_select_body(topk, s_ref, bias_ref):
    rows, width = s_ref.shape
    _select_bias(s_ref, bias_ref, rows, width, topk)


def _sample_select(scores, topk):
    return pl.pallas_call(
        functools.partial(_sample_select_body, topk),
        out_shape=jax.ShapeDtypeStruct(scores.shape, F32),
        compiler_params=pltpu.CompilerParams(vmem_limit_bytes=VMEM_LIMIT),
        name="sample_select",
    )(scores)


def _sample_attn_body(n_pages, pt_ref, k_hbm, v_hbm, q_ref, bias_ref, kn_ref, vn_ref, o_ref, kbuf, vbuf, sem):
    s = pl.program_id(0)
    slot = s % 2
    past = n_pages * PAGE_SIZE

    def copies(sample, sl):
        return (_page_copies(pt_ref, sample, k_hbm, kbuf, sl, sem.at[0], n_pages)
                + _page_copies(pt_ref, sample, v_hbm, vbuf, sl, sem.at[1], n_pages))

    @pl.when(s == 0)
    def _():
        for cp in copies(0, 0):
            cp.start()

    @pl.when(s + 1 < pl.num_programs(0))
    def _():
        for cp in copies(s + 1, 1 - slot):
            cp.start()

    for cp in copies(s, slot):
        cp.wait()
    q2 = q_ref[0]
    k_t = kbuf[slot].astype(BF16)
    v_t = vbuf[slot].astype(BF16)
    row = lax.broadcasted_iota(jnp.int32, (N_HEADS, 1), 0)
    first = row < (N_HEADS // N_KV_HEADS)
    bias = bias_ref[0]
    sc = jnp.dot(q2, k_t, preferred_element_type=F32) + bias[:, 0:past]
    sn = jnp.sum(q2.astype(F32) * kn_ref[0].astype(F32), axis=1, keepdims=True) + bias[:, past:past + 1]
    m = jnp.maximum(jnp.max(sc, axis=1, keepdims=True), sn)
    p = jnp.exp(sc - m)
    pn = jnp.exp(sn - m)
    l = jnp.sum(p, axis=1, keepdims=True) + pn
    o2 = lax.dot_general(p.astype(BF16), v_t, _NT, preferred_element_type=F32)
    o2 = o2 + pn.astype(BF16).astype(F32) * vn_ref[0].astype(F32)
    o = jnp.where(first, o2[:, 0:HEAD_DIM], o2[:, HEAD_DIM:2 * HEAD_DIM])
    o_ref[0] = (o / l).astype(BF16)


def _sample_attention(page_table, k_pool, v_pool, q8, bias, k_new, v_new):
    n, n_pages = page_table.shape
    past = n_pages * PAGE_SIZE
    kvw = N_KV_HEADS * HEAD_DIM
    per = lambda s, pt: (s, 0, 0)
    grid_spec = pltpu.PrefetchScalarGridSpec(
        num_scalar_prefetch=1,
        grid=(n,),
        in_specs=[pl.BlockSpec(memory_space=pl.ANY), pl.BlockSpec(memory_space=pl.ANY),
                  pl.BlockSpec((1, N_HEADS, kvw), per),
                  pl.BlockSpec((1, 1, past + LANES), per),
                  pl.BlockSpec((1, 1, kvw), per),
                  pl.BlockSpec((1, 1, kvw), per)],
        out_specs=pl.BlockSpec((1, N_HEADS, HEAD_DIM), per),
        scratch_shapes=[pltpu.VMEM((2, kvw, past), F32), pltpu.VMEM((2, kvw, past), F32),
                        pltpu.SemaphoreType.DMA((2, 2))],
    )
    return pl.pallas_call(
        functools.partial(_sample_attn_body, n_pages),
        grid_spec=grid_spec,
        out_shape=jax.ShapeDtypeStruct((n, N_HEADS, HEAD_DIM), BF16),
        compiler_params=_cparams(("arbitrary",)),
        name="sample_attn",
    )(page_table, k_pool, v_pool, q8, bias, k_new, v_new)


def _gmlp_body(n_chunks, u_ref, vg_ref, w_ref, bt_ref, o_ref):
    ri = lax.broadcasted_iota(jnp.int32, (CHUNK, CHUNK), 0)
    ci = lax.broadcasted_iota(jnp.int32, (CHUNK, CHUNK), 1)
    gd = GM_WIDTH // GM_GROUPS
    for g in range(GM_GROUPS):
        wg = jnp.where(ci <= ri, w_ref[g], 0.0).astype(BF16)
        bg = bt_ref[:, g:g + 1]
        for c in range(n_chunks):
            rows = slice(c * CHUNK, (c + 1) * CHUNK)
            cols = slice(g * gd, (g + 1) * gd)
            mixed = jnp.dot(wg, vg_ref[rows, cols], preferred_element_type=F32) + bg
            o_ref[rows, cols] = (u_ref[rows, cols].astype(F32) * mixed).astype(BF16)


def _gmlp_prompt(u, vg, w, bt, tm):
    t = u.shape[0]
    row = lambda i: (i, 0)
    return pl.pallas_call(
        functools.partial(_gmlp_body, tm // CHUNK),
        grid=(t // tm,),
        in_specs=[pl.BlockSpec((tm, GM_WIDTH), row), pl.BlockSpec((tm, GM_WIDTH), row),
                  pl.BlockSpec((GM_GROUPS, CHUNK, CHUNK), lambda i: (0, 0, 0)),
                  pl.BlockSpec((CHUNK, LANES), lambda i: (0, 0))],
        out_specs=pl.BlockSpec((tm, GM_WIDTH), row),
        out_shape=jax.ShapeDtypeStruct((t, GM_WIDTH), BF16),
        compiler_params=_cparams(("arbitrary",)),
        name="gmlp",
    )(u, vg, w, bt)


def _gmlp_first_row_body(u_ref, vg_ref, w0_ref, b0_ref, o_ref):
    o_ref[...] = (u_ref[...].astype(F32) * (vg_ref[...] * w0_ref[...] + b0_ref[...])).astype(BF16)


def _gmlp_sample(u, vg, w0, b0):
    return pl.pallas_call(
        _gmlp_first_row_body,
        out_shape=jax.ShapeDtypeStruct(u.shape, BF16),
        name="gmlp_first_row",
    )(u, vg, w0, b0)


def _merge_body(mod3d, dispatch, attn_ref, gm_ref, ga_ref, gb_ref, x_ref, mod_ref, wpa_ref, wpg_ref, wo_ref,
                g2_ref, wr_ref, br_ref, x1_o, a_o, b_o):
    tm = x_ref.shape[0]
    attn = attn_ref[...].reshape(tm, N_HEADS * HEAD_DIM)
    a = jnp.dot(attn, wpa_ref[...], preferred_element_type=F32)
    g = jnp.dot(gm_ref[...], wpg_ref[...], preferred_element_type=F32)
    merged = ga_ref[...].astype(F32) * a + gb_ref[...].astype(F32) * g
    out = jnp.dot(merged.astype(BF16), wo_ref[...], preferred_element_type=F32)
    m = mod_ref[0] if mod3d else mod_ref[...]
    x1 = x_ref[...] + m[:, 2 * D_MODEL:3 * D_MODEL] * out
    x1_o[...] = x1
    ms = jnp.mean(x1 * x1, axis=-1, keepdims=True)
    y = x1 * lax.rsqrt(ms + EPS) * g2_ref[...]
    h2 = y * (1.0 + m[:, 4 * D_MODEL:5 * D_MODEL]) + m[:, 3 * D_MODEL:4 * D_MODEL]
    hi = h2.astype(BF16)
    lo = (h2 - hi.astype(F32)).astype(BF16)
    r = jnp.dot(hi, wr_ref[...], preferred_element_type=F32) + jnp.dot(lo, wr_ref[...], preferred_element_type=F32)
    logits = r[:, 0:LANES] + r[:, LANES:2 * LANES] + br_ref[...]
    neg = -jnp.inf
    big = jnp.int32(1 << 20)
    lane = lax.broadcasted_iota(jnp.int32, logits.shape, 1)
    is_g = jnp.logical_and(lane >= N_EXPERTS, lane < N_EXPERTS + N_EXPERT_GROUPS)
    gl = jnp.where(is_g, logits, neg)
    gmax = jnp.max(gl, axis=1, keepdims=True)
    g_lane = jnp.min(jnp.where(gl == gmax, lane, big), axis=1, keepdims=True)
    g_w = 1.0 / jnp.sum(jnp.exp(gl - gmax), axis=1, keepdims=True)
    g_sel = g_lane - N_EXPERTS
    in_grp = jnp.logical_and(lane < N_EXPERTS, (lane >> 3) == g_sel)
    el = jnp.where(in_grp, logits, neg)
    m1 = jnp.max(el, axis=1, keepdims=True)
    i1 = jnp.min(jnp.where(el == m1, lane, big), axis=1, keepdims=True)
    el2 = jnp.where(lane == i1, neg, el)
    m2 = jnp.max(el2, axis=1, keepdims=True)
    i2 = jnp.min(jnp.where(el2 == m2, lane, big), axis=1, keepdims=True)
    e2 = jnp.exp(m2 - m1)
    w1 = g_w / (1.0 + e2)
    w2 = g_w * e2 / (1.0 + e2)
    if not dispatch:
        a_o[...] = hi
        b_o[...] = jnp.where(lane == i1, w1, 0.0) + jnp.where(lane == i2, w2, 0.0)
        return
    low_first = i1 < i2
    ea = jnp.where(low_first, i1, i2) - g_sel * EXPERTS_PER_GROUP
    eb = jnp.where(low_first, i2, i1) - g_sel * EXPERTS_PER_GROUP
    cls = g_sel * N_PAIRS + ((ea * (2 * EXPERTS_PER_GROUP - 1 - ea)) >> 1) + (eb - ea - 1)
    b_o[...] = jnp.broadcast_to(cls, (tm, LANES))
    wa = jnp.where(low_first, w1, w2)
    wb = jnp.where(low_first, w2, w1)
    a_o[:, 0:HALF] = _pack_halves(hi.astype(F32))
    a_o[:, HALF:ROW_WORDS] = lax.bitcast_convert_type(
        jnp.where(lane == 0, wa, jnp.where(lane == 1, wb, 0.0)), U32)


def _merge(attn4, gm, ga, gb, x, mod, wpa, wpg, wo, g2, wr, br, tm, tiles_per_seq, dispatch):
    t = x.shape[0]
    mod3d = mod.ndim == 3
    nb = tm // Q_BLOCK
    row = lambda i: (i, 0)
    fixed = lambda i: (0, 0)
    if mod3d:
        mod_spec = pl.BlockSpec((1, 1, 6 * D_MODEL), lambda i: (i // tiles_per_seq, 0, 0))
    else:
        mod_spec = pl.BlockSpec((tm, 6 * D_MODEL), row)
    attn_spec = pl.BlockSpec((nb, 1, Q_BLOCK, N_HEADS * HEAD_DIM),
                             lambda i: (i % tiles_per_seq, i // tiles_per_seq, 0, 0))
    if dispatch:
        extra = [(ROW_WORDS, U32), (LANES, jnp.int32)]
    else:
        extra = [(D_MODEL, BF16), (LANES, F32)]
    return pl.pallas_call(
        functools.partial(_merge_body, mod3d, dispatch),
        grid=(t // tm,),
        in_specs=[attn_spec, pl.BlockSpec((tm, GM_WIDTH), row),
                  pl.BlockSpec((tm, D_MODEL), row), pl.BlockSpec((tm, D_MODEL), row),
                  pl.BlockSpec((tm, D_MODEL), row), mod_spec,
                  pl.BlockSpec((N_HEADS * HEAD_DIM, D_MODEL), fixed), pl.BlockSpec((GM_WIDTH, D_MODEL), fixed),
                  pl.BlockSpec((D_MODEL, D_MODEL), fixed), pl.BlockSpec((1, D_MODEL), fixed),
                  pl.BlockSpec((D_MODEL, 2 * LANES), fixed), pl.BlockSpec((1, LANES), fixed)],
        out_specs=[pl.BlockSpec((tm, D_MODEL), row)] + [pl.BlockSpec((tm, w), row) for w, _ in extra],
        out_shape=[jax.ShapeDtypeStruct((t, D_MODEL), F32)] + [jax.ShapeDtypeStruct((t, w), dt) for w, dt in extra],
        compiler_params=_cparams(("arbitrary",)),
        name="merge",
    )(attn4, gm, ga, gb, x, mod, wpa, wpg, wo, g2, wr, br)


def _moe_body(mod3d, h_ref, gate_ref, x1_ref, mod_ref, wgu_ref, wd_ref, o_ref, acc_ref):
    e = pl.program_id(1)

    @pl.when(e == 0)
    def _():
        acc_ref[...] = jnp.zeros_like(acc_ref)

    gu = jnp.dot(h_ref[...], wgu_ref[0], preferred_element_type=F32)
    a = gu[:, 0:D_EXPERT]
    hid = a * jax.nn.sigmoid(a) * gu[:, D_EXPERT:2 * D_EXPERT]
    gate = gate_ref[...]
    lane = lax.broadcasted_iota(jnp.int32, gate.shape, 1)
    ge = jnp.sum(jnp.where(lane == e, gate, 0.0), axis=1, keepdims=True)
    acc_ref[...] += jnp.dot((hid * ge).astype(BF16), wd_ref[0], preferred_element_type=F32)

    @pl.when(e == pl.num_programs(1) - 1)
    def _():
        m = mod_ref[0] if mod3d else mod_ref[...]
        o_ref[...] = x1_ref[...] + m[:, 5 * D_MODEL:6 * D_MODEL] * acc_ref[...]


def _moe(h2, gate, x1, mod, wgu, wd, tm, tiles_per_seq):
    t = h2.shape[0]
    mod3d = mod.ndim == 3
    row = lambda i, e: (i, 0)
    if mod3d:
        mod_spec = pl.BlockSpec((1, 1, 6 * D_MODEL), lambda i, e: (i // tiles_per_seq, 0, 0))
    else:
        mod_spec = pl.BlockSpec((tm, 6 * D_MODEL), row)
    return pl.pallas_call(
        functools.partial(_moe_body, mod3d),
        grid=(t // tm, N_EXPERTS),
        in_specs=[pl.BlockSpec((tm, D_MODEL), row), pl.BlockSpec((tm, LANES), row),
                  pl.BlockSpec((tm, D_MODEL), row), mod_spec,
                  pl.BlockSpec((1, D_MODEL, 2 * D_EXPERT), lambda i, e: (e, 0, 0)),
                  pl.BlockSpec((1, D_EXPERT, D_MODEL), lambda i, e: (e, 0, 0))],
        out_specs=pl.BlockSpec((tm, D_MODEL), row),
        out_shape=jax.ShapeDtypeStruct((t, D_MODEL), F32),
        scratch_shapes=[pltpu.VMEM((tm, D_MODEL), F32)],
        compiler_params=_cparams(("arbitrary", "arbitrary")),
        name="moe",
    )(h2, gate, x1, mod, wgu, wd)


def _slots_body(cls_ref, slot_o, segend_o, carry, seg_start):
    sweep = pl.program_id(0)
    i = pl.program_id(1)
    tm = cls_ref.shape[0]
    lane = lax.broadcasted_iota(jnp.int32, (tm, LANES), 1)
    hit = lane == cls_ref[...]
    onehot = jnp.where(hit, 1.0, 0.0)

    @pl.when(jnp.logical_and(sweep == 0, i == 0))
    def _():
        carry[...] = jnp.zeros_like(carry)

    @pl.when(sweep == 0)
    def _():
        carry[...] = carry[...] + jnp.sum(onehot, axis=0, keepdims=True)
        slot_o[...] = jnp.zeros_like(slot_o)

        @pl.when(i == pl.num_programs(1) - 1)
        def _():
            padded = jnp.floor((carry[...] + (ROW_TILE - 1)) * (1.0 / ROW_TILE)) * ROW_TILE
            ri = lax.broadcasted_iota(jnp.int32, (LANES, LANES), 0)
            ci = lax.broadcasted_iota(jnp.int32, (LANES, LANES), 1)
            upto = jnp.where(ri <= ci, 1.0, 0.0)
            seg_end = jnp.dot(padded, upto, preferred_element_type=F32, precision=lax.Precision.HIGHEST)
            segend_o[...] = seg_end
            seg_start[...] = seg_end - padded
            carry[...] = jnp.zeros_like(carry)

    @pl.when(sweep == 1)
    def _():
        ri = lax.broadcasted_iota(jnp.int32, (tm, tm), 0)
        ci = lax.broadcasted_iota(jnp.int32, (tm, tm), 1)
        earlier = jnp.where(ci < ri, 1.0, 0.0).astype(BF16)
        before = (jnp.dot(earlier, onehot.astype(BF16), preferred_element_type=F32)
                  + carry[0:1, :] + seg_start[0:1, :])
        slot = jnp.sum(jnp.where(hit, before, 0.0), axis=1, keepdims=True)
        slot_o[...] = jnp.broadcast_to(slot, (tm, LANES)).astype(jnp.int32)
        carry[...] = carry[...] + jnp.sum(onehot, axis=0, keepdims=True)


def _class_slots(cls, tm):
    t = cls.shape[0]
    return pl.pallas_call(
        _slots_body,
        grid=(2, t // tm),
        in_specs=[pl.BlockSpec((tm, LANES), lambda s, i: (i, 0))],
        out_specs=[pl.BlockSpec((tm, LANES), lambda s, i: (i * s, 0)),
                   pl.BlockSpec((SUB, LANES), lambda s, i: (0, 0))],
        out_shape=[jax.ShapeDtypeStruct((t, LANES), jnp.int32), jax.ShapeDtypeStruct((SUB, LANES), F32)],
        scratch_shapes=[pltpu.VMEM((SUB, LANES), F32), pltpu.VMEM((SUB, LANES), F32)],
        compiler_params=_cparams(("arbitrary", "arbitrary")),
        name="class_slots",
    )(cls)


INVERT_CHUNK = 4096


def _invert_body(dest_ref, src_o):
    i = pl.program_id(0)

    @pl.when(i == 0)
    def _():
        src_o[...] = jnp.zeros_like(src_o)

    def put(r, carry):
        src_o[pl.ds(dest_ref[0, 0, r], 1), :] = jnp.full((1, LANES), i * INVERT_CHUNK + r, jnp.int32)
        return carry

    lax.fori_loop(0, INVERT_CHUNK, put, 0, unroll=8)


def _invert(dest, n_sorted):
    t = dest.shape[0]
    return pl.pallas_call(
        _invert_body,
        grid=(t // INVERT_CHUNK,),
        in_specs=[pl.BlockSpec((1, 1, INVERT_CHUNK), lambda i: (i, 0, 0), memory_space=pltpu.SMEM)],
        out_specs=pl.BlockSpec((n_sorted, LANES), lambda i: (0, 0)),
        out_shape=jax.ShapeDtypeStruct((n_sorted, LANES), jnp.int32),
        compiler_params=_cparams(("arbitrary",)),
        name="invert_slots",
    )(dest.reshape(t // INVERT_CHUNK, 1, INVERT_CHUNK))


def _tile_fetches(src_ref, rows_hbm, buf, slot, sem):
    return [pltpu.make_async_copy(rows_hbm.at[pl.ds(src_ref[0, 0, r], 1)], buf.at[slot, pl.ds(r, 1)], sem.at[slot])
            for r in range(ROW_TILE)]


FETCH_AHEAD = 2


def _experts_body(ea_ref, eb_ref, used_ref, s0_ref, s1_ref, s2_ref, rows_hbm, wgu_a_ref, wgu_b_ref, wd_a_ref,
                  wd_b_ref, o_ref, buf, sem):
    j = pl.program_id(0)
    n_buf = FETCH_AHEAD + 1
    slot = j % n_buf
    n_used = used_ref[0]

    @pl.when(j == 0)
    def _():
        for cp in _tile_fetches(s0_ref, rows_hbm, buf, 0, sem):
            cp.start()

    @pl.when(jnp.logical_and(j == 0, 1 < n_used))
    def _():
        for cp in _tile_fetches(s1_ref, rows_hbm, buf, 1, sem):
            cp.start()

    @pl.when(j + FETCH_AHEAD < n_used)
    def _():
        for cp in _tile_fetches(s2_ref, rows_hbm, buf, (j + FETCH_AHEAD) % n_buf, sem):
            cp.start()

    @pl.when(jnp.logical_or(j < n_used, j == 0))
    def _():
        for cp in _tile_fetches(s0_ref, rows_hbm, buf, slot, sem):
            cp.wait()

    @pl.when(j < n_used)
    def _():
        xs = buf[slot]
        x = _unpack_halves(xs[:, 0:HALF]).astype(BF16)
        wts = lax.bitcast_convert_type(xs[:, HALF:ROW_WORDS], F32)

        def hidden(wgu_ref, wgt):
            gu = jnp.dot(x, wgu_ref[0], preferred_element_type=F32)
            a = gu[:, 0:D_EXPERT]
            return (a * jax.nn.sigmoid(a) * gu[:, D_EXPERT:2 * D_EXPERT] * wgt).astype(BF16)

        y = (jnp.dot(hidden(wgu_a_ref, wts[:, 0:1]), wd_a_ref[0], preferred_element_type=F32)
             + jnp.dot(hidden(wgu_b_ref, wts[:, 1:2]), wd_b_ref[0], preferred_element_type=F32))
        o_ref[...] = _pack_halves(y.astype(BF16).astype(F32))

    @pl.when(j >= n_used)
    def _():
        o_ref[...] = jnp.zeros_like(o_ref)


def _experts(ea_t, eb_t, n_used, src3, rows, wgu, wd):
    n_tiles = src3.shape[0]
    w_a = lambda j, ea, eb, nu: (ea[j], 0, 0)
    w_b = lambda j, ea, eb, nu: (eb[j], 0, 0)
    smem_blk = lambda f: pl.BlockSpec((1, 1, ROW_TILE), f, memory_space=pltpu.SMEM)
    grid_spec = pltpu.PrefetchScalarGridSpec(
        num_scalar_prefetch=3,
        grid=(n_tiles,),
        in_specs=[smem_blk(lambda j, ea, eb, nu: (j, 0, 0)),
                  smem_blk(lambda j, ea, eb, nu: (jnp.minimum(j + 1, n_tiles - 1), 0, 0)),
                  smem_blk(lambda j, ea, eb, nu: (jnp.minimum(j + FETCH_AHEAD, n_tiles - 1), 0, 0)),
                  pl.BlockSpec(memory_space=pl.ANY),
                  pl.BlockSpec((1, D_MODEL, 2 * D_EXPERT), w_a), pl.BlockSpec((1, D_MODEL, 2 * D_EXPERT), w_b),
                  pl.BlockSpec((1, D_EXPERT, D_MODEL), w_a), pl.BlockSpec((1, D_EXPERT, D_MODEL), w_b)],
        out_specs=pl.BlockSpec((ROW_TILE, HALF), lambda j, ea, eb, nu: (j, 0)),
        scratch_shapes=[pltpu.VMEM((FETCH_AHEAD + 1, ROW_TILE, ROW_WORDS), U32),
                        pltpu.SemaphoreType.DMA((FETCH_AHEAD + 1,))],
    )
    return pl.pallas_call(
        _experts_body,
        grid_spec=grid_spec,
        out_shape=jax.ShapeDtypeStruct((n_tiles * ROW_TILE, HALF), U32),
        compiler_params=_cparams(("arbitrary",)),
        name="experts",
    )(ea_t, eb_t, n_used, src3, src3, src3, rows, wgu, wgu, wd, wd)


def _row_fetches(dest_ref, ys_hbm, buf, slot, sem):
    return [pltpu.make_async_copy(ys_hbm.at[pl.ds(dest_ref[0, 0, r], 1)], buf.at[slot, pl.ds(r, 1)], sem.at[slot])
            for r in range(MOVE_TILE)]


def _combine_body(mod3d, dcur_ref, dnext_ref, ys_hbm, x1_ref, mod_ref, o_ref, buf, sem):
    i = pl.program_id(0)
    slot = i % 2

    @pl.when(i == 0)
    def _():
        for cp in _row_fetches(dcur_ref, ys_hbm, buf, 0, sem):
            cp.start()

    @pl.when(i + 1 < pl.num_programs(0))
    def _():
        for cp in _row_fetches(dnext_ref, ys_hbm, buf, 1 - slot, sem):
            cp.start()

    for cp in _row_fetches(dcur_ref, ys_hbm, buf, slot, sem):
        cp.wait()
    m = mod_ref[0] if mod3d else mod_ref[...]
    o_ref[...] = x1_ref[...] + m[:, 5 * D_MODEL:6 * D_MODEL] * _unpack_halves(buf[slot])


def _combine(dest3, ys, x1, mod, tiles_per_seq):
    steps = dest3.shape[0]
    t = x1.shape[0]
    mod3d = mod.ndim == 3
    row = lambda i: (i, 0)
    if mod3d:
        mod_spec = pl.BlockSpec((1, 1, 6 * D_MODEL), lambda i: (i // tiles_per_seq, 0, 0))
    else:
        mod_spec = pl.BlockSpec((MOVE_TILE, 6 * D_MODEL), row)
    smem_blk = lambda f: pl.BlockSpec((1, 1, MOVE_TILE), f, memory_space=pltpu.SMEM)
    return pl.pallas_call(
        functools.partial(_combine_body, mod3d),
        grid=(steps,),
        in_specs=[smem_blk(lambda i: (i, 0, 0)), smem_blk(lambda i: (jnp.minimum(i + 1, steps - 1), 0, 0)),
                  pl.BlockSpec(memory_space=pl.ANY),
                  pl.BlockSpec((MOVE_TILE, D_MODEL), row), mod_spec],
        out_specs=pl.BlockSpec((MOVE_TILE, D_MODEL), row),
        out_shape=jax.ShapeDtypeStruct((t, D_MODEL), F32),
        scratch_shapes=[pltpu.VMEM((2, MOVE_TILE, HALF), U32), pltpu.SemaphoreType.DMA((2,))],
        compiler_params=_cparams(("arbitrary",)),
        name="combine",
    )(dest3, dest3, ys, x1, mod)


def _class_expert_tables():
    ea, eb = [], []
    for g in range(N_EXPERT_GROUPS):
        for a in range(EXPERTS_PER_GROUP):
            for b in range(a + 1, EXPERTS_PER_GROUP):
                ea.append(g * EXPERTS_PER_GROUP + a)
                eb.append(g * EXPERTS_PER_GROUP + b)
    return np.asarray(ea, np.int32), np.asarray(eb, np.int32)


def _moe_dispatched(rows, cls, x1, mod, wgu, wd, tiles_per_seq):
    t = x1.shape[0]
    slots, seg_end8 = _class_slots(cls, 512)
    seg_end = seg_end8[0, 0:N_CLASSES].astype(jnp.int32)
    dest = slots[:, 0]
    dest3 = dest.reshape(t // MOVE_TILE, 1, MOVE_TILE)
    n_sorted = t + N_CLASSES * ROW_TILE
    tile_row0 = jnp.arange(n_sorted // ROW_TILE, dtype=jnp.int32) * ROW_TILE
    tile_cls = jnp.minimum(jnp.searchsorted(seg_end, tile_row0, side="right"), N_CLASSES - 1)
    ea_np, eb_np = _class_expert_tables()
    ea_t = jnp.asarray(ea_np)[tile_cls]
    eb_t = jnp.asarray(eb_np)[tile_cls]
    n_used = (seg_end[N_CLASSES - 1] // ROW_TILE).astype(jnp.int32).reshape(1)
    src = _invert(dest, n_sorted)[:, 0]
    ys = _experts(ea_t, eb_t, n_used, src.reshape(n_sorted // ROW_TILE, 1, ROW_TILE), rows, wgu, wd)
    return _combine(dest3, ys, x1, mod, tiles_per_seq)


def _pad_lanes(v, fill):
    n = v.shape[-1]
    return jnp.concatenate([v, jnp.full((LANES - n,), fill, v.dtype)]).reshape(1, LANES)


def kernel(x_prompt, x_sample, c_prompt, c_sample, cache_k, cache_v, cache_kidx, page_table, w_ada, b_ada, norm_mix_g, norm_ffn_g, w_in, q_norm_g, k_norm_g, kidx_norm_g, gm_ln_g, gm_ln_b, gm_spatial_w, gm_spatial_b, w_proj_attn, w_proj_gmlp, w_out, w_router_group, b_router_group, w_router_expert, b_router_expert, w_expert_gate, w_expert_up, w_expert_down):
    depth = w_ada.shape[0]
    assert depth == 1
    l = 0
    bp, sp, _ = x_prompt.shape
    bs, ss, _ = x_sample.shape
    assert ss == 1
    n_pages = page_table.shape[1]
    past = n_pages * PAGE_SIZE
    tp = bp * sp

    w = w_in[l]
    zpad = jnp.zeros((D_MODEL, LANES - IDX_DIM - IDX_HEADS), F32)
    w_pad = jnp.concatenate([w[:, 0:1024], w[:, 1024:1088], w[:, 1088:1092], zpad, w[:, 1092:]], axis=1).astype(BF16)
    seg_np = (np.arange(LANES)[:, None] // HEAD_DIM) == (np.arange(LANES)[None, :] // HEAD_DIM)
    seg = jnp.asarray(seg_np, BF16)
    segki = jnp.asarray(seg_np & (np.arange(LANES)[:, None] < IDX_DIM) & (np.arange(LANES)[None, :] < IDX_DIM), BF16)
    consts = (norm_mix_g[l].reshape(1, D_MODEL),
              jnp.tile(q_norm_g[l], 2).reshape(1, LANES), jnp.tile(k_norm_g[l], 2).reshape(1, LANES),
              _pad_lanes(kidx_norm_g[l], 1.0),
              gm_ln_g[l].reshape(1, GM_WIDTH), gm_ln_b[l].reshape(1, GM_WIDTH), seg, segki)
    wpa = w_proj_attn[l].astype(BF16)
    wpg = w_proj_gmlp[l].astype(BF16)
    wo = w_out[l].astype(BF16)
    wr32 = jnp.concatenate([w_router_expert[l], w_router_group[l],
                            jnp.zeros((D_MODEL, LANES - N_EXPERTS - N_EXPERT_GROUPS), F32)], axis=1)
    wr_hi = wr32.astype(BF16)
    wr_lo = (wr32 - wr_hi.astype(F32)).astype(BF16)
    wr = jnp.concatenate([wr_hi, wr_lo], axis=1)
    br = _pad_lanes(jnp.concatenate([b_router_expert[l], b_router_group[l]]), 0.0)
    wgu = jnp.concatenate([w_expert_gate[l], w_expert_up[l]], axis=2).astype(BF16)
    wd = w_expert_down[l].astype(BF16)
    g2 = norm_ffn_g[l].reshape(1, D_MODEL)

    mod = _adaln(jnp.concatenate([c_prompt, c_sample], axis=0), w_ada[l], b_ada[l])
    mod_p = mod[0:bp].reshape(bp, 1, 6 * D_MODEL)
    mod_s = mod[bp:bp + bs]
    pos = jnp.concatenate([jnp.arange(sp, dtype=jnp.int32),
                           jnp.full((8,), past, jnp.int32)]).astype(F32).reshape(sp + 8, 1)
    tabs = _rope_tables(pos)
    tabs_p = tuple(t[0:sp] for t in tabs)
    tabs_s = tuple(t[sp:sp + 1] for t in tabs)

    tm = 512
    tps = sp // tm
    (q, kb, vb, qi, kib, kiwi, u, vg, ga, gb, kt_f, vt_f, kit_f, vt) = _project(
        x_prompt.reshape(tp, D_MODEL), mod_p, tabs_p, consts, w_pad, tm, tps, BF16)
    topk_p = min(TOPK_MAX, sp // 4)
    r3 = lambda a: a.reshape(bp, sp, a.shape[-1])
    attn_p = _prompt_attention(r3(qi), r3(kiwi), r3(kib), r3(q), r3(kb), vt, topk_p)
    bt = jnp.concatenate([gm_spatial_b[l].T, jnp.zeros((CHUNK, LANES - GM_GROUPS), F32)], axis=1)
    gm_p = _gmlp_prompt(u, vg, gm_spatial_w[l], bt, tm)
    x1_p, rows_p, cls_p = _merge(attn_p, gm_p, ga, gb, x_prompt.reshape(tp, D_MODEL), mod_p,
                                 wpa, wpg, wo, g2, wr, br, tm, tps, True)
    y_p = _moe_dispatched(rows_p, cls_p, x1_p, mod_p, wgu, wd, sp // MOVE_TILE)

    (q_s, kb_s, vb_s, qi_s, kib_s, kiwi_s, u_s, vg_s, ga_s, gb_s, kt_s, vt_s, kit_s, _) = _project(
        x_sample.reshape(bs, D_MODEL), mod_s, tabs_s, consts, w_pad, bs, 1, F32)
    qi8 = jnp.concatenate([qi_s.reshape(bs, IDX_HEADS, IDX_DIM),
                           jnp.zeros((bs, 8 - IDX_HEADS, IDX_DIM), BF16)], axis=1)
    wi8 = jnp.concatenate([kiwi_s[:, IDX_DIM:IDX_DIM + IDX_HEADS],
                           jnp.zeros((bs, 8 - IDX_HEADS), F32)], axis=1).reshape(bs, 8, 1)
    kidx_t = jnp.transpose(cache_kidx[l], (0, 2, 1))
    scores = _sample_scores(page_table, kidx_t, qi8, wi8, kib_s.reshape(bs, 1, IDX_DIM))
    topk_s = min(TOPK_MAX, (past + ss) // 4)
    bias = _sample_select(scores.reshape(bs, past + LANES), topk_s).reshape(bs, 1, past + LANES)
    kvw = N_KV_HEADS * HEAD_DIM
    k_t = jnp.transpose(cache_k[l], (0, 2, 3, 1)).reshape(-1, kvw, PAGE_SIZE)
    v_t = jnp.transpose(cache_v[l], (0, 2, 3, 1)).reshape(-1, kvw, PAGE_SIZE)
    q3 = q_s.reshape(bs, N_HEADS, HEAD_DIM)
    zq = jnp.zeros_like(q3)
    in_first = (jnp.arange(N_HEADS) < N_HEADS // N_KV_HEADS)[None, :, None]
    q2_s = jnp.where(in_first, jnp.concatenate([q3, zq], axis=2), jnp.concatenate([zq, q3], axis=2))
    attn_s = _sample_attention(page_table, k_t, v_t, q2_s, bias,
                               kb_s.reshape(bs, 1, kvw), vb_s.reshape(bs, 1, kvw))
    gd = GM_WIDTH // GM_GROUPS
    w0 = jnp.repeat(gm_spatial_w[l][:, 0, 0], gd).reshape(1, GM_WIDTH)
    b0 = jnp.repeat(gm_spatial_b[l][:, 0], gd).reshape(1, GM_WIDTH)
    gm_s = _gmlp_sample(u_s, vg_s, w0, b0)
    x1_s, h2_s, gate_s = _merge(attn_s.reshape(1, 1, bs, N_HEADS * HEAD_DIM), gm_s, ga_s, gb_s,
                                x_sample.reshape(bs, D_MODEL), mod_s, wpa, wpg, wo, g2, wr, br, bs, 1, False)
    y_s = _moe(h2_s, gate_s, x1_s, mod_s, wgu, wd, bs, 1)

    def rows_kv(a_t, n, s):
        return jnp.transpose(a_t.reshape(n, N_KV_HEADS, HEAD_DIM, s), (0, 3, 1, 2))[None]

    def rows_ki(a_t):
        return jnp.transpose(a_t, (0, 2, 1))[None]

    return (y_p.reshape(bp, sp, D_MODEL), y_s.reshape(bs, ss, D_MODEL),
            rows_kv(kt_f, bp, sp), rows_kv(vt_f, bp, sp), rows_ki(kit_f),
            rows_kv(kt_s, 1, bs).reshape(1, bs, ss, N_KV_HEADS, HEAD_DIM),
            rows_kv(vt_s, 1, bs).reshape(1, bs, ss, N_KV_HEADS, HEAD_DIM),
            rows_ki(kit_s).reshape(1, bs, ss, IDX_DIM), vg_s.reshape(1, bs, ss, GM_WIDTH))
```

```python
import functools

import numpy as np
import jax
import jax.numpy as jnp
from jax import lax
from jax.experimental import pallas as pl
from jax.experimental.pallas import tpu as pltpu

F32 = jnp.float32
BF16 = jnp.bfloat16
U32 = jnp.uint32

D_MODEL = 1024
N_HEADS = 8
HEAD_DIM = 64
N_KV_HEADS = 2
ROT_DIM = 16
ROPE_THETA = 500000.0
IDX_HEADS = 4
IDX_DIM = 64
TOPK_MAX = 256
Q_BLOCK = 128
GM_WIDTH = 512
GM_GROUPS = 8
CHUNK = 128
N_EXPERT_GROUPS = 4
EXPERTS_PER_GROUP = 8
N_EXPERTS = 32
D_EXPERT = 256
EPS = 1e-6
PAGE_SIZE = 128
LANES = 128
SUB = 8

C_Q, C_K, C_V, C_QI, C_KIWI, C_U, C_VG, C_GATE = 0, 512, 640, 768, 1024, 1152, 1664, 2176
D_IN_PAD = 4224
VMEM_LIMIT = 56 * 1024 * 1024

N_PAIRS = EXPERTS_PER_GROUP * (EXPERTS_PER_GROUP - 1) // 2
N_CLASSES = N_EXPERT_GROUPS * N_PAIRS
ROW_TILE = 128
HALF = D_MODEL // 2
FEAT_SUB = HALF // LANES
MOVE_TILE = 256
HI_MASK = 0xFFFF0000


def _cparams(sem):
    return pltpu.CompilerParams(dimension_semantics=sem, vmem_limit_bytes=VMEM_LIMIT)


def _pack_halves(x):
    n = x.shape[1] // 2
    hi = lax.bitcast_convert_type(x[:, 0:n], U32) & jnp.uint32(HI_MASK)
    lo = lax.bitcast_convert_type(x[:, n:2 * n], U32) >> 16
    return hi | lo


def _unpack_halves(u):
    hi = lax.bitcast_convert_type(u & jnp.uint32(HI_MASK), F32)
    lo = lax.bitcast_convert_type(u << 16, F32)
    return jnp.concatenate([hi, lo], axis=1)


def _store_token_tiles(ref, words, extra):
    n = words.shape[0]
    for s in range(FEAT_SUB):
        ref[pl.ds(s, n, stride=SUB), :] = words[:, s * LANES:(s + 1) * LANES]
    zero = jnp.zeros((n, LANES), U32)
    ref[pl.ds(FEAT_SUB, n, stride=SUB), :] = zero if extra is None else extra
    for s in range(FEAT_SUB + 1, SUB):
        ref[pl.ds(s, n, stride=SUB), :] = zero


def _load_token_words(ref, n):
    return jnp.concatenate([ref[pl.ds(s, n, stride=SUB), :] for s in range(FEAT_SUB)], axis=1)


def _adaln_body(c_ref, w_ref, b_ref, o_ref):
    c = c_ref[...]
    a = c * jax.nn.sigmoid(c)
    o_ref[...] = jnp.dot(a, w_ref[...], preferred_element_type=F32,
                         precision=lax.Precision.HIGHEST) + b_ref[...]


def _adaln(c, w, b):
    r = c.shape[0]
    n = w.shape[1]
    bn = 1536
    return pl.pallas_call(
        _adaln_body,
        grid=(n // bn,),
        in_specs=[pl.BlockSpec((r, D_MODEL), lambda j: (0, 0)),
                  pl.BlockSpec((D_MODEL, bn), lambda j: (0, j)),
                  pl.BlockSpec((1, bn), lambda j: (0, j))],
        out_specs=pl.BlockSpec((r, bn), lambda j: (0, j)),
        out_shape=jax.ShapeDtypeStruct((r, n), F32),
        compiler_params=_cparams(("arbitrary",)),
        name="adaln",
    )(c, w, b.reshape(1, n))


def _rope_table_body(pos_ref, invf_ref, sa_m_ref, sb_m_ref, c_ref, sa_ref, sb_ref):
    ang = pos_ref[...] * invf_ref[...]
    s = jnp.sin(ang)
    c_ref[...] = jnp.cos(ang)
    sa_ref[...] = s * sa_m_ref[...]
    sb_ref[...] = s * sb_m_ref[...]


def _rope_tables(pos):
    half = ROT_DIM // 2
    inv_freq = ROPE_THETA ** (-jnp.arange(half, dtype=F32) / half)
    d = np.arange(LANES) % HEAD_DIM
    invf = jnp.where(jnp.asarray(d < ROT_DIM), inv_freq[d % half], 0.0).reshape(1, LANES)
    sa_m = jnp.asarray(np.where(d < half, -1.0, 0.0), F32).reshape(1, LANES)
    sb_m = jnp.asarray(np.where((d >= half) & (d < ROT_DIM), 1.0, 0.0), F32).reshape(1, LANES)
    r = pos.shape[0]
    return pl.pallas_call(
        _rope_table_body,
        out_shape=[jax.ShapeDtypeStruct((r, LANES), F32)] * 3,
        name="rope_tables",
    )(pos, invf, sa_m, sb_m)


def _rope(y, c, sa, sb):
    return y * c + pltpu.roll(y, LANES - ROT_DIM // 2, 1) * sa + pltpu.roll(y, ROT_DIM // 2, 1) * sb


def _seg_rms(r, seg, gain):
    ss = r * r
    hi = ss.astype(BF16)
    lo = (ss - hi.astype(F32)).astype(BF16)
    tot = jnp.dot(hi, seg, preferred_element_type=F32) + jnp.dot(lo, seg, preferred_element_type=F32)
    return r * lax.rsqrt(tot * (1.0 / HEAD_DIM) + EPS) * gain


def _proj_body(mod3d, x_ref, mod_ref, g_ref, w_ref, c_ref, sa_ref, sb_ref, qg_ref, kg_ref, kig_ref,
               lng_ref, lnb_ref, seg_ref, segki_ref,
               q_o, kb_o, vb_o, qi_o, kib_o, kiwi_o, u_o, vg_o, ga_o, gb_o, kt_o, vtf_o, kit_o, vt_o):
    x = x_ref[...]
    m = mod_ref[0] if mod3d else mod_ref[...]
    shift1 = m[:, 0:D_MODEL]
    scale1 = m[:, D_MODEL:2 * D_MODEL]
    ms = jnp.mean(x * x, axis=-1, keepdims=True)
    y = x * lax.rsqrt(ms + EPS) * g_ref[...]
    h = (y * (1.0 + scale1) + shift1).astype(BF16)
    c, sa, sb = c_ref[...], sa_ref[...], sb_ref[...]
    seg = seg_ref[...]

    def proj(a, b):
        return jnp.dot(h, w_ref[:, a:b], preferred_element_type=F32)

    for g2 in range(2):
        r2 = proj(C_Q + g2 * 2 * LANES, C_Q + (g2 + 1) * 2 * LANES)
        for g in (2 * g2, 2 * g2 + 1):
            r = r2[:, (g % 2) * LANES:(g % 2 + 1) * LANES]
            yq = _rope(_seg_rms(r, seg, qg_ref[...]), c, sa, sb) * (HEAD_DIM ** -0.5)
            q_o[:, g * LANES:(g + 1) * LANES] = yq.astype(BF16)
    r_kv = proj(C_K, C_V + LANES)
    yk = _rope(_seg_rms(r_kv[:, 0:LANES], seg, kg_ref[...]), c, sa, sb)
    kt_o[0] = yk.T
    kb_o[...] = yk.astype(BF16)
    r = r_kv[:, LANES:2 * LANES]
    r_t = r.T
    vtf_o[0] = r_t
    vb_o[...] = r.astype(BF16)
    vt_o[0] = r_t.astype(BF16)
    r2 = proj(C_QI, C_QI + 2 * LANES)
    for g in range(2):
        yqi = _rope(r2[:, g * LANES:(g + 1) * LANES], c, sa, sb) * (IDX_DIM ** -0.5 * IDX_HEADS ** -0.5)
        qi_o[:, g * LANES:(g + 1) * LANES] = yqi.astype(BF16)
    r = proj(C_KIWI, C_KIWI + LANES)
    yki = _rope(_seg_rms(r, segki_ref[...], kig_ref[...]), c, sa, sb)
    lane = lax.broadcasted_iota(jnp.int32, r.shape, 1)
    kiwi = jnp.where(lane < IDX_DIM, yki, r)
    kiwi_o[...] = kiwi
    kit_o[0] = kiwi.T[0:IDX_DIM, :]
    kib_o[...] = kiwi[:, 0:IDX_DIM].astype(BF16)
    r = proj(C_U, C_U + GM_WIDTH)
    u_o[...] = jax.nn.gelu(r).astype(BF16)
    r = proj(C_VG, C_VG + GM_WIDTH)
    gl = jax.nn.gelu(r)
    mu = jnp.mean(gl, axis=-1, keepdims=True)
    dv = gl - mu
    var = jnp.mean(dv * dv, axis=-1, keepdims=True)
    vg_o[...] = (dv * lax.rsqrt(var + EPS) * lng_ref[...] + lnb_ref[...]).astype(vg_o.dtype)
    r = proj(C_GATE, C_GATE + D_MODEL)
    ga_o[...] = jax.nn.sigmoid(r).astype(BF16)
    r = proj(C_GATE + D_MODEL, C_GATE + 2 * D_MODEL)
    gb_o[...] = jax.nn.sigmoid(r).astype(BF16)


def _project(x, mod, tables, consts, w_pad, tm, tiles_per_seq, vg_dtype):
    t = x.shape[0]
    mod3d = mod.ndim == 3
    c_t, sa_t, sb_t = tables
    g_mix, qg, kg, kig, lng, lnb, seg, segki = consts
    row = lambda i: (i, 0)
    fixed = lambda i: (0, 0)
    if mod3d:
        mod_spec = pl.BlockSpec((1, 1, 6 * D_MODEL), lambda i: (i // tiles_per_seq, 0, 0))
        tab_spec = pl.BlockSpec((tm, LANES), lambda i: (i % tiles_per_seq, 0))
    else:
        mod_spec = pl.BlockSpec((tm, 6 * D_MODEL), row)
        tab_spec = pl.BlockSpec((1, LANES), fixed)
    widths = [(512, BF16), (128, BF16), (128, BF16), (256, BF16),
              (64, BF16), (128, F32), (512, BF16), (512, vg_dtype), (1024, BF16), (1024, BF16)]
    n_seq, seq = t // (tm * tiles_per_seq), tm * tiles_per_seq
    t_idx = lambda i: (i // tiles_per_seq, 0, i % tiles_per_seq)
    t_outs = [(LANES, F32), (LANES, F32), (IDX_DIM, F32), (LANES, BF16)]
    return pl.pallas_call(
        functools.partial(_proj_body, mod3d),
        grid=(t // tm,),
        in_specs=[pl.BlockSpec((tm, D_MODEL), row), mod_spec,
                  pl.BlockSpec((1, D_MODEL), fixed),
                  pl.BlockSpec((D_MODEL, D_IN_PAD), fixed),
                  tab_spec, tab_spec, tab_spec,
                  pl.BlockSpec((1, LANES), fixed), pl.BlockSpec((1, LANES), fixed), pl.BlockSpec((1, LANES), fixed),
                  pl.BlockSpec((1, GM_WIDTH), fixed), pl.BlockSpec((1, GM_WIDTH), fixed),
                  pl.BlockSpec((LANES, LANES), fixed), pl.BlockSpec((LANES, LANES), fixed)],
        out_specs=[pl.BlockSpec((tm, w), row) for w, _ in widths]
        + [pl.BlockSpec((1, f, tm), t_idx) for f, _ in t_outs],
        out_shape=[jax.ShapeDtypeStruct((t, w), dt) for w, dt in widths]
        + [jax.ShapeDtypeStruct((n_seq, f, seq), dt) for f, dt in t_outs],
        compiler_params=_cparams(("arbitrary",)),
        name="project",
    )(x, mod, g_mix, w_pad, c_t, sa_t, sb_t, qg, kg, kig, lng, lnb, seg, segki)


def _select_bias(s_ref, bias_ref, rows, width, topk, tie_check_start=16):
    nb = width // LANES
    kf = float(topk)
    neg, pos = -jnp.inf, jnp.inf

    def blk(j):
        return s_ref[:, j * LANES:(j + 1) * LANES]

    def count_above(t):
        tb = jnp.broadcast_to(t, (rows, LANES))
        acc = jnp.zeros((rows, LANES), F32)
        for j in range(nb):
            acc = acc + jnp.where(blk(j) > tb, 1.0, 0.0)
        return jnp.sum(acc, axis=1, keepdims=True)

    mx = jnp.full((rows, LANES), neg, F32)
    mn = jnp.full((rows, LANES), pos, F32)
    for j in range(nb):
        b = blk(j)
        mx = jnp.maximum(mx, b)
        mn = jnp.minimum(mn, jnp.where(b == neg, pos, b))
    hi0 = jnp.max(mx, axis=1, keepdims=True)
    smin = jnp.min(mn, axis=1, keepdims=True)
    lo0 = smin - jnp.abs(smin) - 1.0
    f_lo0 = count_above(lo0)
    zeros = jnp.zeros((rows, 1), F32)

    def active_of(f_lo, tie):
        return jnp.logical_and(f_lo > kf, tie == 0.0)

    def cond(st):
        _, _, _, f_lo, _, tie = st
        return jnp.max(jnp.where(active_of(f_lo, tie), 1.0, 0.0)) > 0.0

    def body(st):
        it, lo, hi, f_lo, f_hi, tie = st
        active = active_of(f_lo, tie)
        mid = lo + (hi - lo) * 0.5
        stuck = jnp.logical_or(mid <= lo, mid >= hi)
        cnt = count_above(mid)
        ge = cnt >= kf
        up_lo = jnp.logical_and(active, ge)
        up_hi = jnp.logical_and(active, jnp.logical_not(ge))
        lo = jnp.where(up_lo, mid, lo)
        f_lo = jnp.where(up_lo, cnt, f_lo)
        hi = jnp.where(up_hi, mid, hi)
        f_hi = jnp.where(up_hi, cnt, f_hi)
        tie = jnp.where(jnp.logical_and(active, stuck), 1.0, tie)

        def tie_check(_):
            lob = jnp.broadcast_to(lo, (rows, LANES))
            hib = jnp.broadcast_to(hi, (rows, LANES))
            vmx = jnp.full((rows, LANES), neg, F32)
            vmn = jnp.full((rows, LANES), pos, F32)
            for j in range(nb):
                b = blk(j)
                inn = jnp.logical_and(b > lob, b <= hib)
                vmx = jnp.maximum(vmx, jnp.where(inn, b, neg))
                vmn = jnp.minimum(vmn, jnp.where(inn, b, pos))
            one_value = jnp.max(vmx, axis=1, keepdims=True) == jnp.min(vmn, axis=1, keepdims=True)
            return jnp.where(one_value, 1.0, tie)

        run_check = jnp.logical_and(it >= tie_check_start, it % 4 == 0)
        tie = lax.cond(run_check, tie_check, lambda _: tie, 0)
        return it + 1, lo, hi, f_lo, f_hi, tie

    _, lo, hi, f_lo, f_hi, _ = lax.while_loop(cond, body, (jnp.int32(0), lo0, hi0, f_lo0, zeros, zeros))
    lob = jnp.broadcast_to(lo, (rows, LANES))
    need_prefix = jnp.max(jnp.where(f_lo > kf, 1.0, 0.0)) > 0.0

    @pl.when(jnp.logical_not(need_prefix))
    def _():
        for j in range(nb):
            bias_ref[:, j * LANES:(j + 1) * LANES] = jnp.where(blk(j) > lob, 0.0, neg)

    @pl.when(need_prefix)
    def _():
        hib = jnp.broadcast_to(hi, (rows, LANES))
        need = kf - f_hi
        ri = lax.broadcasted_iota(jnp.int32, (LANES, LANES), 0)
        ci = lax.broadcasted_iota(jnp.int32, (LANES, LANES), 1)
        upper = jnp.where(ri < ci, 1.0, 0.0).astype(BF16)
        off = jnp.zeros((rows, 1), F32)
        for j in range(nb):
            b = blk(j)
            inn = jnp.logical_and(b > lob, b <= hib)
            innf = jnp.where(inn, 1.0, 0.0)
            before = jnp.dot(innf.astype(BF16), upper, preferred_element_type=F32) + off
            sel = jnp.logical_or(b > hib, jnp.logical_and(inn, before < need))
            bias_ref[:, j * LANES:(j + 1) * LANES] = jnp.where(sel, 0.0, neg)
            off = off + jnp.sum(innf, axis=1, keepdims=True)


_NT = (((1,), (1,)), ((), ()))


def _sub_reduce(x, op):
    for sh in (4, 2, 1):
        x = op(x, pltpu.roll(x, sh, 0))
    return x


def _select_bias_t(s_ref, bias_ref, width, topk, n_adm, steps_per_check=4, tie_check_from=4):
    rb = 64
    nb = width // rb
    kf = float(topk)
    neg, pos = -jnp.inf, jnp.inf

    def blk(j):
        return s_ref[j * rb:(j + 1) * rb, :]

    def fold(x, op):
        y = x[0:SUB]
        for a in range(1, rb // SUB):
            y = op(y, x[a * SUB:(a + 1) * SUB])
        return _sub_reduce(y, op)

    def tile(v):
        return jnp.concatenate([v] * (rb // SUB), axis=0)

    def count_above(t):
        tb = tile(t)
        acc = jnp.zeros((rb, LANES), F32)
        for j in range(nb):
            acc = acc + jnp.where(blk(j) > tb, 1.0, 0.0)
        return fold(acc, jnp.add)

    mx = jnp.full((rb, LANES), neg, F32)
    mn = jnp.full((rb, LANES), pos, F32)
    for j in range(nb):
        b = blk(j)
        mx = jnp.maximum(mx, b)
        mn = jnp.minimum(mn, jnp.where(b == neg, pos, b))
    hi0 = fold(mx, jnp.maximum)
    smin = fold(mn, jnp.minimum)
    lo0 = smin - jnp.abs(smin) - 1.0
    zeros = jnp.zeros((SUB, LANES), F32)

    def active_of(f_lo, tie):
        return jnp.logical_and(f_lo > kf, tie == 0.0)

    def any_lane(cond):
        return jnp.max(jnp.where(cond, 1.0, 0.0)) > 0.0

    def step(lo, hi, f_lo, f_hi, tie):
        active = active_of(f_lo, tie)
        mid = lo + (hi - lo) * 0.5
        stuck = jnp.logical_or(mid <= lo, mid >= hi)
        cnt = count_above(mid)
        ge = cnt >= kf
        up_lo = jnp.logical_and(active, ge)
        up_hi = jnp.logical_and(active, jnp.logical_not(ge))
        return (jnp.where(up_lo, mid, lo), jnp.where(up_hi, mid, hi), jnp.where(up_lo, cnt, f_lo),
                jnp.where(up_hi, cnt, f_hi), jnp.where(jnp.logical_and(active, stuck), 1.0, tie))

    def cond(st):
        _, _, _, f_lo, _, tie = st
        return any_lane(active_of(f_lo, tie))

    def body(st):
        it, lo, hi, f_lo, f_hi, tie = st
        for _ in range(steps_per_check):
            lo, hi, f_lo, f_hi, tie = step(lo, hi, f_lo, f_hi, tie)

        def tie_check(_):
            lob, hib = tile(lo), tile(hi)
            vmx = jnp.full((rb, LANES), neg, F32)
            vmn = jnp.full((rb, LANES), pos, F32)
            for j in range(nb):
                b = blk(j)
                inn = jnp.logical_and(b > lob, b <= hib)
                vmx = jnp.maximum(vmx, jnp.where(inn, b, neg))
                vmn = jnp.minimum(vmn, jnp.where(inn, b, pos))
            return jnp.where(fold(vmx, jnp.maximum) == fold(vmn, jnp.minimum), 1.0, tie)

        run_check = jnp.logical_and(it + 1 >= tie_check_from, any_lane(active_of(f_lo, tie)))
        tie = lax.cond(run_check, tie_check, lambda _: tie, 0)
        return it + 1, lo, hi, f_lo, f_hi, tie

    _, lo, hi, f_lo, f_hi, _ = lax.while_loop(cond, body, (jnp.int32(0), lo0, hi0, n_adm, zeros, zeros))
    lob = tile(lo)
    need_prefix = any_lane(f_lo > kf)

    @pl.when(jnp.logical_not(need_prefix))
    def _():
        for j in range(nb):
            bias_ref[j * rb:(j + 1) * rb, :] = jnp.where(blk(j) > lob, 0.0, neg)

    @pl.when(need_prefix)
    def _():
        need = (kf - f_hi)[0:1]
        lo1, hi1 = lo[0:1], hi[0:1]
        ri = lax.broadcasted_iota(jnp.int32, (LANES, LANES), 0)
        ci = lax.broadcasted_iota(jnp.int32, (LANES, LANES), 1)
        lower = jnp.where(ci < ri, 1.0, 0.0).astype(BF16)
        off = jnp.zeros((1, LANES), F32)
        for j in range(width // LANES):
            b = s_ref[j * LANES:(j + 1) * LANES, :]
            inn = jnp.logical_and(b > lo1, b <= hi1)
            innf = jnp.where(inn, 1.0, 0.0)
            before = jnp.dot(lower, innf.astype(BF16), preferred_element_type=F32) + off
            sel = jnp.logical_or(b > hi1, jnp.logical_and(inn, before < need))
            bias_ref[j * LANES:(j + 1) * LANES, :] = jnp.where(sel, 0.0, neg)
            off = off + jnp.sum(innf, axis=0, keepdims=True)


def _prompt_attn_body(blk_i, topk, qi_ref, kiwi_ref, kib_ref, q_ref, k_ref, vt_ref, o_ref, s_ref, bias_ref):
    width = (blk_i + 1) * Q_BLOCK
    neg = -jnp.inf
    kiwi_t = kiwi_ref[0].T
    chunk = 512
    for c0 in range(0, width, chunk):
        c1 = min(width, c0 + chunk)
        kib = kib_ref[0, c0:c1, :]
        sc = None
        for h in range(IDX_HEADS):
            d = lax.dot_general(kib, qi_ref[0, :, h * IDX_DIM:(h + 1) * IDX_DIM], _NT,
                                preferred_element_type=F32)
            t = jnp.maximum(d, 0.0) * kiwi_t[IDX_DIM + h:IDX_DIM + h + 1, :]
            sc = t if sc is None else sc + t
        s_ref[c0:c1, :] = sc
    ki = lax.broadcasted_iota(jnp.int32, (Q_BLOCK, Q_BLOCK), 0)
    qj = lax.broadcasted_iota(jnp.int32, (Q_BLOCK, Q_BLOCK), 1)
    d0 = width - Q_BLOCK
    s_ref[d0:width, :] = jnp.where(ki <= qj, s_ref[d0:width, :], neg)
    if width > topk:
        n_adm = (lax.broadcasted_iota(jnp.int32, (SUB, LANES), 1) + (d0 + 1)).astype(F32)
        _select_bias_t(s_ref, bias_ref, width, topk, n_adm)
    else:
        bias_ref[...] = jnp.where(s_ref[...] == neg, neg, 0.0)
    lane = lax.broadcasted_iota(jnp.int32, (Q_BLOCK, LANES), 1)
    hpg = N_HEADS // N_KV_HEADS
    for h in range(N_HEADS):
        g = h // hpg
        qp = q_ref[0, :, (h // 2) * LANES:(h // 2 + 1) * LANES].astype(F32)
        if h % 2 != g:
            qp = pltpu.roll(qp, HEAD_DIM, 1)
        q2 = jnp.where((lane >= HEAD_DIM) == (g == 1), qp, 0.0).astype(BF16)
        s = lax.dot_general(k_ref[0], q2, _NT, preferred_element_type=F32) + bias_ref[...]
        m = jnp.max(s, axis=0, keepdims=True)
        p = jnp.exp(s - m)
        l = jnp.sum(p, axis=0, keepdims=True)
        o_t = jnp.dot(vt_ref[0, g * HEAD_DIM:(g + 1) * HEAD_DIM, :], p.astype(BF16), preferred_element_type=F32)
        o_ref[0, :, h * HEAD_DIM:(h + 1) * HEAD_DIM] = (o_t / l).T.astype(BF16)


def _prompt_attention(qi, kiwi, kib, q, kb, vt, topk):
    b, s, _ = q.shape
    outs = []
    for i in range(s // Q_BLOCK):
        width = (i + 1) * Q_BLOCK
        qblk = lambda bb, i=i: (bb, i, 0)
        kall = lambda bb: (bb, 0, 0)
        outs.append(pl.pallas_call(
            functools.partial(_prompt_attn_body, i, topk),
            grid=(b,),
            in_specs=[pl.BlockSpec((1, Q_BLOCK, IDX_HEADS * IDX_DIM), qblk),
                      pl.BlockSpec((1, Q_BLOCK, LANES), qblk),
                      pl.BlockSpec((1, width, IDX_DIM), kall),
                      pl.BlockSpec((1, Q_BLOCK, N_HEADS * HEAD_DIM), qblk),
                      pl.BlockSpec((1, width, N_KV_HEADS * HEAD_DIM), kall),
                      pl.BlockSpec((1, N_KV_HEADS * HEAD_DIM, width), kall)],
            out_specs=pl.BlockSpec((1, Q_BLOCK, N_HEADS * HEAD_DIM), lambda bb: (bb, 0, 0)),
            out_shape=jax.ShapeDtypeStruct((b, Q_BLOCK, N_HEADS * HEAD_DIM), BF16),
            scratch_shapes=[pltpu.VMEM((width, Q_BLOCK), F32), pltpu.VMEM((width, Q_BLOCK), F32)],
            compiler_params=_cparams(("arbitrary",)),
            name=f"prompt_attn_{i}",
        )(qi, kiwi, kib, q, kb, vt))
    return jnp.stack(outs)


def _page_copies(pt_ref, sample, src_hbm, buf, slot, sem, n_pages):
    return [pltpu.make_async_copy(src_hbm.at[pt_ref[sample, p]],
                                  buf.at[slot, :, pl.ds(p * PAGE_SIZE, PAGE_SIZE)], sem.at[slot])
            for p in range(n_pages)]


def _sample_scores_body(n_pages, pt_ref, kidx_hbm, qi_ref, wi_ref, kin_ref, o_ref, buf, sem):
    s = pl.program_id(0)
    slot = s % 2

    @pl.when(s == 0)
    def _():
        for cp in _page_copies(pt_ref, 0, kidx_hbm, buf, 0, sem, n_pages):
            cp.start()

    @pl.when(s + 1 < pl.num_programs(0))
    def _():
        for cp in _page_copies(pt_ref, s + 1, kidx_hbm, buf, 1 - slot, sem, n_pages):
            cp.start()

    for cp in _page_copies(pt_ref, s, kidx_hbm, buf, slot, sem, n_pages):
        cp.wait()
    qi = qi_ref[0]
    wi = wi_ref[0]
    ki_t = buf[slot].astype(BF16)
    d = jnp.dot(qi, ki_t, preferred_element_type=F32)
    past = jnp.sum(jnp.maximum(d, 0.0) * wi, axis=0, keepdims=True)
    dn = jnp.sum(qi.astype(F32) * kin_ref[0].astype(F32), axis=1, keepdims=True)
    new = jnp.sum(jnp.maximum(dn, 0.0) * wi, axis=0, keepdims=True)
    lane = lax.broadcasted_iota(jnp.int32, (1, LANES), 1)
    tail = jnp.where(lane == 0, jnp.broadcast_to(new, (1, LANES)), -jnp.inf)
    o_ref[0] = jnp.concatenate([past, tail], axis=1)


def _sample_scores(page_table, kidx_pool, qi8, wi8, ki_new):
    n, n_pages = page_table.shape
    past = n_pages * PAGE_SIZE
    grid_spec = pltpu.PrefetchScalarGridSpec(
        num_scalar_prefetch=1,
        grid=(n,),
        in_specs=[pl.BlockSpec(memory_space=pl.ANY),
                  pl.BlockSpec((1, 8, IDX_DIM), lambda s, pt: (s, 0, 0)),
                  pl.BlockSpec((1, 8, 1), lambda s, pt: (s, 0, 0)),
                  pl.BlockSpec((1, 1, IDX_DIM), lambda s, pt: (s, 0, 0))],
        out_specs=pl.BlockSpec((1, 1, past + LANES), lambda s, pt: (s, 0, 0)),
        scratch_shapes=[pltpu.VMEM((2, IDX_DIM, past), F32), pltpu.SemaphoreType.DMA((2,))],
    )
    return pl.pallas_call(
        functools.partial(_sample_scores_body, n_pages),
        grid_spec=grid_spec,
        out_shape=jax.ShapeDtypeStruct((n, 1, past + LANES), F32),
        compiler_params=_cparams(("arbitrary",)),
        name="sample_scores",
    )(page_table, kidx_pool, qi8, wi8, ki_new)


def _sample_select_body(topk, s_ref, bias_ref):
    rows, width = s_ref.shape
    _select_bias(s_ref, bias_ref, rows, width, topk)


def _sample_select(scores, topk):
    return pl.pallas_call(
        functools.partial(_sample_select_body, topk),
        out_shape=jax.ShapeDtypeStruct(scores.shape, F32),
        compiler_params=pltpu.CompilerParams(vmem_limit_bytes=VMEM_LIMIT),
        name="sample_select",
    )(scores)


def _sample_attn_body(n_pages, pt_ref, k_hbm, v_hbm, q_ref, bias_ref, kn_ref, vn_ref, o_ref, kbuf, vbuf, sem):
    s = pl.program_id(0)
    slot = s % 2
    past = n_pages * PAGE_SIZE

    def copies(sample, sl):
        return (_page_copies(pt_ref, sample, k_hbm, kbuf, sl, sem.at[0], n_pages)
                + _page_copies(pt_ref, sample, v_hbm, vbuf, sl, sem.at[1], n_pages))

    @pl.when(s == 0)
    def _():
        for cp in copies(0, 0):
            cp.start()

    @pl.when(s + 1 < pl.num_programs(0))
    def _():
        for cp in copies(s + 1, 1 - slot):
            cp.start()

    for cp in copies(s, slot):
        cp.wait()
    q2 = q_ref[0]
    k_t = kbuf[slot].astype(BF16)
    v_t = vbuf[slot].astype(BF16)
    row = lax.broadcasted_iota(jnp.int32, (N_HEADS, 1), 0)
    first = row < (N_HEADS // N_KV_HEADS)
    bias = bias_ref[0]
    sc = jnp.dot(q2, k_t, preferred_element_type=F32) + bias[:, 0:past]
    sn = jnp.sum(q2.astype(F32) * kn_ref[0].astype(F32), axis=1, keepdims=True) + bias[:, past:past + 1]
    m = jnp.maximum(jnp.max(sc, axis=1, keepdims=True), sn)
    p = jnp.exp(sc - m)
    pn = jnp.exp(sn - m)
    l = jnp.sum(p, axis=1, keepdims=True) + pn
    o2 = lax.dot_general(p.astype(BF16), v_t, _NT, preferred_element_type=F32)
    o2 = o2 + pn.astype(BF16).astype(F32) * vn_ref[0].astype(F32)
    o = jnp.where(first, o2[:, 0:HEAD_DIM], o2[:, HEAD_DIM:2 * HEAD_DIM])
    o_ref[0] = (o / l).astype(BF16)


def _sample_attention(page_table, k_pool, v_pool, q8, bias, k_new, v_new):
    n, n_pages = page_table.shape
    past = n_pages * PAGE_SIZE
    kvw = N_KV_HEADS * HEAD_DIM
    per = lambda s, pt: (s, 0, 0)
    grid_spec = pltpu.PrefetchScalarGridSpec(
        num_scalar_prefetch=1,
        grid=(n,),
        in_specs=[pl.BlockSpec(memory_space=pl.ANY), pl.BlockSpec(memory_space=pl.ANY),
                  pl.BlockSpec((1, N_HEADS, kvw), per),
                  pl.BlockSpec((1, 1, past + LANES), per),
                  pl.BlockSpec((1, 1, kvw), per),
                  pl.BlockSpec((1, 1, kvw), per)],
        out_specs=pl.BlockSpec((1, N_HEADS, HEAD_DIM), per),
        scratch_shapes=[pltpu.VMEM((2, kvw, past), F32), pltpu.VMEM((2, kvw, past), F32),
                        pltpu.SemaphoreType.DMA((2, 2))],
    )
    return pl.pallas_call(
        functools.partial(_sample_attn_body, n_pages),
        grid_spec=grid_spec,
        out_shape=jax.ShapeDtypeStruct((n, N_HEADS, HEAD_DIM), BF16),
        compiler_params=_cparams(("arbitrary",)),
        name="sample_attn",
    )(page_table, k_pool, v_pool, q8, bias, k_new, v_new)


def _gmlp_body(n_chunks, u_ref, vg_ref, w_ref, bt_ref, o_ref):
    ri = lax.broadcasted_iota(jnp.int32, (CHUNK, CHUNK), 0)
    ci = lax.broadcasted_iota(jnp.int32, (CHUNK, CHUNK), 1)
    gd = GM_WIDTH // GM_GROUPS
    for g in range(GM_GROUPS):
        wg = jnp.where(ci <= ri, w_ref[g], 0.0).astype(BF16)
        bg = bt_ref[:, g:g + 1]
        for c in range(n_chunks):
            rows = slice(c * CHUNK, (c + 1) * CHUNK)
            cols = slice(g * gd, (g + 1) * gd)
            mixed = jnp.dot(wg, vg_ref[rows, cols], preferred_element_type=F32) + bg
            o_ref[rows, cols] = (u_ref[rows, cols].astype(F32) * mixed).astype(BF16)


def _gmlp_prompt(u, vg, w, bt, tm):
    t = u.shape[0]
    row = lambda i: (i, 0)
    return pl.pallas_call(
        functools.partial(_gmlp_body, tm // CHUNK),
        grid=(t // tm,),
        in_specs=[pl.BlockSpec((tm, GM_WIDTH), row), pl.BlockSpec((tm, GM_WIDTH), row),
                  pl.BlockSpec((GM_GROUPS, CHUNK, CHUNK), lambda i: (0, 0, 0)),
                  pl.BlockSpec((CHUNK, LANES), lambda i: (0, 0))],
        out_specs=pl.BlockSpec((tm, GM_WIDTH), row),
        out_shape=jax.ShapeDtypeStruct((t, GM_WIDTH), BF16),
        compiler_params=_cparams(("arbitrary",)),
        name="gmlp",
    )(u, vg, w, bt)


def _gmlp_first_row_body(u_ref, vg_ref, w0_ref, b0_ref, o_ref):
    o_ref[...] = (u_ref[...].astype(F32) * (vg_ref[...] * w0_ref[...] + b0_ref[...])).astype(BF16)


def _gmlp_sample(u, vg, w0, b0):
    return pl.pallas_call(
        _gmlp_first_row_body,
        out_shape=jax.ShapeDtypeStruct(u.shape, BF16),
        name="gmlp_first_row",
    )(u, vg, w0, b0)


def _merge_body(mod3d, dispatch, attn_ref, gm_ref, ga_ref, gb_ref, x_ref, mod_ref, wpa_ref, wpg_ref, wo_ref,
                g2_ref, wr_ref, br_ref, x1_o, a_o, b_o):
    tm = x_ref.shape[0]
    attn = attn_ref[...].reshape(tm, N_HEADS * HEAD_DIM)
    a = jnp.dot(attn, wpa_ref[...], preferred_element_type=F32)
    g = jnp.dot(gm_ref[...], wpg_ref[...], preferred_element_type=F32)
    merged = ga_ref[...].astype(F32) * a + gb_ref[...].astype(F32) * g
    out = jnp.dot(merged.astype(BF16), wo_ref[...], preferred_element_type=F32)
    m = mod_ref[0] if mod3d else mod_ref[...]
    x1 = x_ref[...] + m[:, 2 * D_MODEL:3 * D_MODEL] * out
    x1_o[...] = x1
    ms = jnp.mean(x1 * x1, axis=-1, keepdims=True)
    y = x1 * lax.rsqrt(ms + EPS) * g2_ref[...]
    h2 = y * (1.0 + m[:, 4 * D_MODEL:5 * D_MODEL]) + m[:, 3 * D_MODEL:4 * D_MODEL]
    hi = h2.astype(BF16)
    lo = (h2 - hi.astype(F32)).astype(BF16)
    r = jnp.dot(hi, wr_ref[...], preferred_element_type=F32) + jnp.dot(lo, wr_ref[...], preferred_element_type=F32)
    logits = r[:, 0:LANES] + r[:, LANES:2 * LANES] + br_ref[...]
    neg = -jnp.inf
    big = jnp.int32(1 << 20)
    lane = lax.broadcasted_iota(jnp.int32, logits.shape, 1)
    is_g = jnp.logical_and(lane >= N_EXPERTS, lane < N_EXPERTS + N_EXPERT_GROUPS)
    gl = jnp.where(is_g, logits, neg)
    gmax = jnp.max(gl, axis=1, keepdims=True)
    g_lane = jnp.min(jnp.where(gl == gmax, lane, big), axis=1, keepdims=True)
    g_w = 1.0 / jnp.sum(jnp.exp(gl - gmax), axis=1, keepdims=True)
    g_sel = g_lane - N_EXPERTS
    in_grp = jnp.logical_and(lane < N_EXPERTS, (lane >> 3) == g_sel)
    el = jnp.where(in_grp, logits, neg)
    m1 = jnp.max(el, axis=1, keepdims=True)
    i1 = jnp.min(jnp.where(el == m1, lane, big), axis=1, keepdims=True)
    el2 = jnp.where(lane == i1, neg, el)
    m2 = jnp.max(el2, axis=1, keepdims=True)
    i2 = jnp.min(jnp.where(el2 == m2, lane, big), axis=1, keepdims=True)
    e2 = jnp.exp(m2 - m1)
    w1 = g_w / (1.0 + e2)
    w2 = g_w * e2 / (1.0 + e2)
    if not dispatch:
        a_o[...] = hi
        b_o[...] = jnp.where(lane == i1, w1, 0.0) + jnp.where(lane == i2, w2, 0.0)
        return
    low_first = i1 < i2
    ea = jnp.where(low_first, i1, i2) - g_sel * EXPERTS_PER_GROUP
    eb = jnp.where(low_first, i2, i1) - g_sel * EXPERTS_PER_GROUP
    cls = g_sel * N_PAIRS + ((ea * (2 * EXPERTS_PER_GROUP - 1 - ea)) >> 1) + (eb - ea - 1)
    b_o[...] = jnp.broadcast_to(cls, (tm, LANES))
    wa = jnp.where(low_first, w1, w2)
    wb = jnp.where(low_first, w2, w1)
    _store_token_tiles(a_o, _pack_halves(hi.astype(F32)), lax.bitcast_convert_type(
        jnp.where(lane == 0, wa, jnp.where(lane == 1, wb, 0.0)), U32))


def _merge(attn4, gm, ga, gb, x, mod, wpa, wpg, wo, g2, wr, br, tm, tiles_per_seq, dispatch):
    t = x.shape[0]
    mod3d = mod.ndim == 3
    nb = tm // Q_BLOCK
    row = lambda i: (i, 0)
    fixed = lambda i: (0, 0)
    if mod3d:
        mod_spec = pl.BlockSpec((1, 1, 6 * D_MODEL), lambda i: (i // tiles_per_seq, 0, 0))
    else:
        mod_spec = pl.BlockSpec((tm, 6 * D_MODEL), row)
    attn_spec = pl.BlockSpec((nb, 1, Q_BLOCK, N_HEADS * HEAD_DIM),
                             lambda i: (i % tiles_per_seq, i // tiles_per_seq, 0, 0))
    if dispatch:
        extra = [(SUB, LANES, U32), (1, LANES, jnp.int32)]
    else:
        extra = [(1, D_MODEL, BF16), (1, LANES, F32)]
    return pl.pallas_call(
        functools.partial(_merge_body, mod3d, dispatch),
        grid=(t // tm,),
        in_specs=[attn_spec, pl.BlockSpec((tm, GM_WIDTH), row),
                  pl.BlockSpec((tm, D_MODEL), row), pl.BlockSpec((tm, D_MODEL), row),
                  pl.BlockSpec((tm, D_MODEL), row), mod_spec,
                  pl.BlockSpec((N_HEADS * HEAD_DIM, D_MODEL), fixed), pl.BlockSpec((GM_WIDTH, D_MODEL), fixed),
                  pl.BlockSpec((D_MODEL, D_MODEL), fixed), pl.BlockSpec((1, D_MODEL), fixed),
                  pl.BlockSpec((D_MODEL, 2 * LANES), fixed), pl.BlockSpec((1, LANES), fixed)],
        out_specs=[pl.BlockSpec((tm, D_MODEL), row)] + [pl.BlockSpec((tm * r, w), row) for r, w, _ in extra],
        out_shape=[jax.ShapeDtypeStruct((t, D_MODEL), F32)]
        + [jax.ShapeDtypeStruct((t * r, w), dt) for r, w, dt in extra],
        compiler_params=_cparams(("arbitrary",)),
        name="merge",
    )(attn4, gm, ga, gb, x, mod, wpa, wpg, wo, g2, wr, br)


def _moe_body(mod3d, h_ref, gate_ref, x1_ref, mod_ref, wgu_ref, wd_ref, o_ref, acc_ref):
    e = pl.program_id(1)

    @pl.when(e == 0)
    def _():
        acc_ref[...] = jnp.zeros_like(acc_ref)

    gu = jnp.dot(h_ref[...], wgu_ref[0], preferred_element_type=F32)
    a = gu[:, 0:D_EXPERT]
    hid = a * jax.nn.sigmoid(a) * gu[:, D_EXPERT:2 * D_EXPERT]
    gate = gate_ref[...]
    lane = lax.broadcasted_iota(jnp.int32, gate.shape, 1)
    ge = jnp.sum(jnp.where(lane == e, gate, 0.0), axis=1, keepdims=True)
    acc_ref[...] += jnp.dot((hid * ge).astype(BF16), wd_ref[0], preferred_element_type=F32)

    @pl.when(e == pl.num_programs(1) - 1)
    def _():
        m = mod_ref[0] if mod3d else mod_ref[...]
        o_ref[...] = x1_ref[...] + m[:, 5 * D_MODEL:6 * D_MODEL] * acc_ref[...]


def _moe(h2, gate, x1, mod, wgu, wd, tm, tiles_per_seq):
    t = h2.shape[0]
    mod3d = mod.ndim == 3
    row = lambda i, e: (i, 0)
    if mod3d:
        mod_spec = pl.BlockSpec((1, 1, 6 * D_MODEL), lambda i, e: (i // tiles_per_seq, 0, 0))
    else:
        mod_spec = pl.BlockSpec((tm, 6 * D_MODEL), row)
    return pl.pallas_call(
        functools.partial(_moe_body, mod3d),
        grid=(t // tm, N_EXPERTS),
        in_specs=[pl.BlockSpec((tm, D_MODEL), row), pl.BlockSpec((tm, LANES), row),
                  pl.BlockSpec((tm, D_MODEL), row), mod_spec,
                  pl.BlockSpec((1, D_MODEL, 2 * D_EXPERT), lambda i, e: (e, 0, 0)),
                  pl.BlockSpec((1, D_EXPERT, D_MODEL), lambda i, e: (e, 0, 0))],
        out_specs=pl.BlockSpec((tm, D_MODEL), row),
        out_shape=jax.ShapeDtypeStruct((t, D_MODEL), F32),
        scratch_shapes=[pltpu.VMEM((tm, D_MODEL), F32)],
        compiler_params=_cparams(("arbitrary", "arbitrary")),
        name="moe",
    )(h2, gate, x1, mod, wgu, wd)


def _slots_body(cls_ref, slot_o, segend_o, carry, seg_start):
    sweep = pl.program_id(0)
    i = pl.program_id(1)
    tm = cls_ref.shape[0]
    lane = lax.broadcasted_iota(jnp.int32, (tm, LANES), 1)
    hit = lane == cls_ref[...]
    onehot = jnp.where(hit, 1.0, 0.0)

    @pl.when(jnp.logical_and(sweep == 0, i == 0))
    def _():
        carry[...] = jnp.zeros_like(carry)

    @pl.when(sweep == 0)
    def _():
        carry[...] = carry[...] + jnp.sum(onehot, axis=0, keepdims=True)
        slot_o[...] = jnp.zeros_like(slot_o)

        @pl.when(i == pl.num_programs(1) - 1)
        def _():
            padded = jnp.floor((carry[...] + (ROW_TILE - 1)) * (1.0 / ROW_TILE)) * ROW_TILE
            ri = lax.broadcasted_iota(jnp.int32, (LANES, LANES), 0)
            ci = lax.broadcasted_iota(jnp.int32, (LANES, LANES), 1)
            upto = jnp.where(ri <= ci, 1.0, 0.0)
            seg_end = jnp.dot(padded, upto, preferred_element_type=F32, precision=lax.Precision.HIGHEST)
            segend_o[...] = seg_end
            seg_start[...] = seg_end - padded
            carry[...] = jnp.zeros_like(carry)

    @pl.when(sweep == 1)
    def _():
        ri = lax.broadcasted_iota(jnp.int32, (tm, tm), 0)
        ci = lax.broadcasted_iota(jnp.int32, (tm, tm), 1)
        earlier = jnp.where(ci < ri, 1.0, 0.0).astype(BF16)
        before = (jnp.dot(earlier, onehot.astype(BF16), preferred_element_type=F32)
                  + carry[0:1, :] + seg_start[0:1, :])
        slot = jnp.sum(jnp.where(hit, before, 0.0), axis=1, keepdims=True)
        slot_o[...] = jnp.broadcast_to(slot, (tm, LANES)).astype(jnp.int32)
        carry[...] = carry[...] + jnp.sum(onehot, axis=0, keepdims=True)


def _class_slots(cls, tm):
    t = cls.shape[0]
    return pl.pallas_call(
        _slots_body,
        grid=(2, t // tm),
        in_specs=[pl.BlockSpec((tm, LANES), lambda s, i: (i, 0))],
        out_specs=[pl.BlockSpec((tm, LANES), lambda s, i: (i * s, 0)),
                   pl.BlockSpec((SUB, LANES), lambda s, i: (0, 0))],
        out_shape=[jax.ShapeDtypeStruct((t, LANES), jnp.int32), jax.ShapeDtypeStruct((SUB, LANES), F32)],
        scratch_shapes=[pltpu.VMEM((SUB, LANES), F32), pltpu.VMEM((SUB, LANES), F32)],
        compiler_params=_cparams(("arbitrary", "arbitrary")),
        name="class_slots",
    )(cls)


INVERT_CHUNK = 4096


def _invert_body(dest_ref, src_o):
    i = pl.program_id(0)

    @pl.when(i == 0)
    def _():
        src_o[...] = jnp.zeros_like(src_o)

    def put(r, carry):
        src_o[pl.ds(dest_ref[0, 0, r], 1), :] = jnp.full((1, LANES), i * INVERT_CHUNK + r, jnp.int32)
        return carry

    lax.fori_loop(0, INVERT_CHUNK, put, 0, unroll=8)


def _invert(dest, n_sorted):
    t = dest.shape[0]
    return pl.pallas_call(
        _invert_body,
        grid=(t // INVERT_CHUNK,),
        in_specs=[pl.BlockSpec((1, 1, INVERT_CHUNK), lambda i: (i, 0, 0), memory_space=pltpu.SMEM)],
        out_specs=pl.BlockSpec((n_sorted, LANES), lambda i: (0, 0)),
        out_shape=jax.ShapeDtypeStruct((n_sorted, LANES), jnp.int32),
        compiler_params=_cparams(("arbitrary",)),
        name="invert_slots",
    )(dest.reshape(t // INVERT_CHUNK, 1, INVERT_CHUNK))


def _token_fetches(idx_ref, tiles_hbm, buf, slot, sem, n):
    return [pltpu.make_async_copy(tiles_hbm.at[pl.ds(pl.multiple_of(idx_ref[0, 0, r] * SUB, SUB), SUB)],
                                  buf.at[slot, pl.ds(r * SUB, SUB)], sem.at[slot])
            for r in range(n)]


def _tile_fetches(src_ref, rows_hbm, buf, slot, sem):
    return _token_fetches(src_ref, rows_hbm, buf, slot, sem, ROW_TILE)


FETCH_AHEAD = 2


def _experts_body(ea_ref, eb_ref, used_ref, s0_ref, s1_ref, s2_ref, rows_hbm, wgu_a_ref, wgu_b_ref, wd_a_ref,
                  wd_b_ref, o_ref, buf, sem):
    j = pl.program_id(0)
    n_buf = FETCH_AHEAD + 1
    slot = j % n_buf
    n_used = used_ref[0]

    @pl.when(j == 0)
    def _():
        for cp in _tile_fetches(s0_ref, rows_hbm, buf, 0, sem):
            cp.start()

    @pl.when(jnp.logical_and(j == 0, 1 < n_used))
    def _():
        for cp in _tile_fetches(s1_ref, rows_hbm, buf, 1, sem):
            cp.start()

    @pl.when(j + FETCH_AHEAD < n_used)
    def _():
        for cp in _tile_fetches(s2_ref, rows_hbm, buf, (j + FETCH_AHEAD) % n_buf, sem):
            cp.start()

    @pl.when(jnp.logical_or(j < n_used, j == 0))
    def _():
        for cp in _tile_fetches(s0_ref, rows_hbm, buf, slot, sem):
            cp.wait()

    @pl.when(j < n_used)
    def _():
        tiles = buf.at[slot]
        x = _unpack_halves(_load_token_words(tiles, ROW_TILE)).astype(BF16)
        wts = lax.bitcast_convert_type(tiles[pl.ds(FEAT_SUB, ROW_TILE, stride=SUB), :], F32)

        def hidden(wgu_ref, wgt):
            gu = jnp.dot(x, wgu_ref[0], preferred_element_type=F32)
            a = gu[:, 0:D_EXPERT]
            return (a * jax.nn.sigmoid(a) * gu[:, D_EXPERT:2 * D_EXPERT] * wgt).astype(BF16)

        y = (jnp.dot(hidden(wgu_a_ref, wts[:, 0:1]), wd_a_ref[0], preferred_element_type=F32)
             + jnp.dot(hidden(wgu_b_ref, wts[:, 1:2]), wd_b_ref[0], preferred_element_type=F32))
        _store_token_tiles(o_ref, _pack_halves(y.astype(BF16).astype(F32)), None)

    @pl.when(j >= n_used)
    def _():
        o_ref[...] = jnp.zeros_like(o_ref)


def _experts(ea_t, eb_t, n_used, src3, rows, wgu, wd):
    n_tiles = src3.shape[0]
    w_a = lambda j, ea, eb, nu: (ea[j], 0, 0)
    w_b = lambda j, ea, eb, nu: (eb[j], 0, 0)
    smem_blk = lambda f: pl.BlockSpec((1, 1, ROW_TILE), f, memory_space=pltpu.SMEM)
    grid_spec = pltpu.PrefetchScalarGridSpec(
        num_scalar_prefetch=3,
        grid=(n_tiles,),
        in_specs=[smem_blk(lambda j, ea, eb, nu: (j, 0, 0)),
                  smem_blk(lambda j, ea, eb, nu: (jnp.minimum(j + 1, n_tiles - 1), 0, 0)),
                  smem_blk(lambda j, ea, eb, nu: (jnp.minimum(j + FETCH_AHEAD, n_tiles - 1), 0, 0)),
                  pl.BlockSpec(memory_space=pl.ANY),
                  pl.BlockSpec((1, D_MODEL, 2 * D_EXPERT), w_a), pl.BlockSpec((1, D_MODEL, 2 * D_EXPERT), w_b),
                  pl.BlockSpec((1, D_EXPERT, D_MODEL), w_a), pl.BlockSpec((1, D_EXPERT, D_MODEL), w_b)],
        out_specs=pl.BlockSpec((ROW_TILE * SUB, LANES), lambda j, ea, eb, nu: (j, 0)),
        scratch_shapes=[pltpu.VMEM((FETCH_AHEAD + 1, ROW_TILE * SUB, LANES), U32),
                        pltpu.SemaphoreType.DMA((FETCH_AHEAD + 1,))],
    )
    return pl.pallas_call(
        _experts_body,
        grid_spec=grid_spec,
        out_shape=jax.ShapeDtypeStruct((n_tiles * ROW_TILE * SUB, LANES), U32),
        compiler_params=_cparams(("arbitrary",)),
        name="experts",
    )(ea_t, eb_t, n_used, src3, src3, src3, rows, wgu, wgu, wd, wd)


def _row_fetches(dest_ref, ys_hbm, buf, slot, sem):
    return _token_fetches(dest_ref, ys_hbm, buf, slot, sem, MOVE_TILE)


def _combine_body(mod3d, dcur_ref, dnext_ref, ys_hbm, x1_ref, mod_ref, o_ref, buf, sem):
    i = pl.program_id(0)
    slot = i % 2

    @pl.when(i == 0)
    def _():
        for cp in _row_fetches(dcur_ref, ys_hbm, buf, 0, sem):
            cp.start()

    @pl.when(i + 1 < pl.num_programs(0))
    def _():
        for cp in _row_fetches(dnext_ref, ys_hbm, buf, 1 - slot, sem):
            cp.start()

    for cp in _row_fetches(dcur_ref, ys_hbm, buf, slot, sem):
        cp.wait()
    m = mod_ref[0] if mod3d else mod_ref[...]
    o_ref[...] = x1_ref[...] + m[:, 5 * D_MODEL:6 * D_MODEL] * _unpack_halves(_load_token_words(buf.at[slot], MOVE_TILE))


def _combine(dest3, ys, x1, mod, tiles_per_seq):
    steps = dest3.shape[0]
    t = x1.shape[0]
    mod3d = mod.ndim == 3
    row = lambda i: (i, 0)
    if mod3d:
        mod_spec = pl.BlockSpec((1, 1, 6 * D_MODEL), lambda i: (i // tiles_per_seq, 0, 0))
    else:
        mod_spec = pl.BlockSpec((MOVE_TILE, 6 * D_MODEL), row)
    smem_blk = lambda f: pl.BlockSpec((1, 1, MOVE_TILE), f, memory_space=pltpu.SMEM)
    return pl.pallas_call(
        functools.partial(_combine_body, mod3d),
        grid=(steps,),
        in_specs=[smem_blk(lambda i: (i, 0, 0)), smem_blk(lambda i: (jnp.minimum(i + 1, steps - 1), 0, 0)),
                  pl.BlockSpec(memory_space=pl.ANY),
                  pl.BlockSpec((MOVE_TILE, D_MODEL), row), mod_spec],
        out_specs=pl.BlockSpec((MOVE_TILE, D_MODEL), row),
        out_shape=jax.ShapeDtypeStruct((t, D_MODEL), F32),
        scratch_shapes=[pltpu.VMEM((2, MOVE_TILE * SUB, LANES), U32), pltpu.SemaphoreType.DMA((2,))],
        compiler_params=_cparams(("arbitrary",)),
        name="combine",
    )(dest3, dest3, ys, x1, mod)


def _class_expert_tables():
    ea, eb = [], []
    for g in range(N_EXPERT_GROUPS):
        for a in range(EXPERTS_PER_GROUP):
            for b in range(a + 1, EXPERTS_PER_GROUP):
                ea.append(g * EXPERTS_PER_GROUP + a)
                eb.append(g * EXPERTS_PER_GROUP + b)
    return np.asarray(ea, np.int32), np.asarray(eb, np.int32)


def _moe_dispatched(rows, cls, x1, mod, wgu, wd, tiles_per_seq):
    t = x1.shape[0]
    slots, seg_end8 = _class_slots(cls, 1024)
    seg_end = seg_end8[0, 0:N_CLASSES].astype(jnp.int32)
    dest = slots[:, 0]
    dest3 = dest.reshape(t // MOVE_TILE, 1, MOVE_TILE)
    n_sorted = t + N_CLASSES * ROW_TILE
    tile_row0 = jnp.arange(n_sorted // ROW_TILE, dtype=jnp.int32) * ROW_TILE
    tile_cls = jnp.minimum(jnp.searchsorted(seg_end, tile_row0, side="right"), N_CLASSES - 1)
    ea_np, eb_np = _class_expert_tables()
    ea_t = jnp.asarray(ea_np)[tile_cls]
    eb_t = jnp.asarray(eb_np)[tile_cls]
    n_used = (seg_end[N_CLASSES - 1] // ROW_TILE).astype(jnp.int32).reshape(1)
    src = _invert(dest, n_sorted)[:, 0]
    ys = _experts(ea_t, eb_t, n_used, src.reshape(n_sorted // ROW_TILE, 1, ROW_TILE), rows, wgu, wd)
    return _combine(dest3, ys, x1, mod, tiles_per_seq)


def _pad_lanes(v, fill):
    n = v.shape[-1]
    return jnp.concatenate([v, jnp.full((LANES - n,), fill, v.dtype)]).reshape(1, LANES)


def kernel(x_prompt, x_sample, c_prompt, c_sample, cache_k, cache_v, cache_kidx, page_table, w_ada, b_ada, norm_mix_g, norm_ffn_g, w_in, q_norm_g, k_norm_g, kidx_norm_g, gm_ln_g, gm_ln_b, gm_spatial_w, gm_spatial_b, w_proj_attn, w_proj_gmlp, w_out, w_router_group, b_router_group, w_router_expert, b_router_expert, w_expert_gate, w_expert_up, w_expert_down):
    depth = w_ada.shape[0]
    assert depth == 1
    l = 0
    bp, sp, _ = x_prompt.shape
    bs, ss, _ = x_sample.shape
    assert ss == 1
    n_pages = page_table.shape[1]
    past = n_pages * PAGE_SIZE
    tp = bp * sp

    w = w_in[l]
    zpad = jnp.zeros((D_MODEL, LANES - IDX_DIM - IDX_HEADS), F32)
    w_pad = jnp.concatenate([w[:, 0:1024], w[:, 1024:1088], w[:, 1088:1092], zpad, w[:, 1092:]], axis=1).astype(BF16)
    seg_np = (np.arange(LANES)[:, None] // HEAD_DIM) == (np.arange(LANES)[None, :] // HEAD_DIM)
    seg = jnp.asarray(seg_np, BF16)
    segki = jnp.asarray(seg_np & (np.arange(LANES)[:, None] < IDX_DIM) & (np.arange(LANES)[None, :] < IDX_DIM), BF16)
    consts = (norm_mix_g[l].reshape(1, D_MODEL),
              jnp.tile(q_norm_g[l], 2).reshape(1, LANES), jnp.tile(k_norm_g[l], 2).reshape(1, LANES),
              _pad_lanes(kidx_norm_g[l], 1.0),
              gm_ln_g[l].reshape(1, GM_WIDTH), gm_ln_b[l].reshape(1, GM_WIDTH), seg, segki)
    wpa = w_proj_attn[l].astype(BF16)
    wpg = w_proj_gmlp[l].astype(BF16)
    wo = w_out[l].astype(BF16)
    wr32 = jnp.concatenate([w_router_expert[l], w_router_group[l],
                            jnp.zeros((D_MODEL, LANES - N_EXPERTS - N_EXPERT_GROUPS), F32)], axis=1)
    wr_hi = wr32.astype(BF16)
    wr_lo = (wr32 - wr_hi.astype(F32)).astype(BF16)
    wr = jnp.concatenate([wr_hi, wr_lo], axis=1)
    br = _pad_lanes(jnp.concatenate([b_router_expert[l], b_router_group[l]]), 0.0)
    wgu = jnp.concatenate([w_expert_gate[l], w_expert_up[l]], axis=2).astype(BF16)
    wd = w_expert_down[l].astype(BF16)
    g2 = norm_ffn_g[l].reshape(1, D_MODEL)

    mod = _adaln(jnp.concatenate([c_prompt, c_sample], axis=0), w_ada[l], b_ada[l])
    mod_p = mod[0:bp].reshape(bp, 1, 6 * D_MODEL)
    mod_s = mod[bp:bp + bs]
    pos = jnp.concatenate([jnp.arange(sp, dtype=jnp.int32),
                           jnp.full((8,), past, jnp.int32)]).astype(F32).reshape(sp + 8, 1)
    tabs = _rope_tables(pos)
    tabs_p = tuple(t[0:sp] for t in tabs)
    tabs_s = tuple(t[sp:sp + 1] for t in tabs)

    tm = 512
    tps = sp // tm
    (q, kb, vb, qi, kib, kiwi, u, vg, ga, gb, kt_f, vt_f, kit_f, vt) = _project(
        x_prompt.reshape(tp, D_MODEL), mod_p, tabs_p, consts, w_pad, tm, tps, BF16)
    topk_p = min(TOPK_MAX, sp // 4)
    r3 = lambda a: a.reshape(bp, sp, a.shape[-1])
    attn_p = _prompt_attention(r3(qi), r3(kiwi), r3(kib), r3(q), r3(kb), vt, topk_p)
    bt = jnp.concatenate([gm_spatial_b[l].T, jnp.zeros((CHUNK, LANES - GM_GROUPS), F32)], axis=1)
    gm_p = _gmlp_prompt(u, vg, gm_spatial_w[l], bt, tm)
    x1_p, rows_p, cls_p = _merge(attn_p, gm_p, ga, gb, x_prompt.reshape(tp, D_MODEL), mod_p,
                                 wpa, wpg, wo, g2, wr, br, tm, tps, True)
    y_p = _moe_dispatched(rows_p, cls_p, x1_p, mod_p, wgu, wd, sp // MOVE_TILE)

    (q_s, kb_s, vb_s, qi_s, kib_s, kiwi_s, u_s, vg_s, ga_s, gb_s, kt_s, vt_s, kit_s, _) = _project(
        x_sample.reshape(bs, D_MODEL), mod_s, tabs_s, consts, w_pad, bs, 1, F32)
    qi8 = jnp.concatenate([qi_s.reshape(bs, IDX_HEADS, IDX_DIM),
                           jnp.zeros((bs, 8 - IDX_HEADS, IDX_DIM), BF16)], axis=1)
    wi8 = jnp.concatenate([kiwi_s[:, IDX_DIM:IDX_DIM + IDX_HEADS],
                           jnp.zeros((bs, 8 - IDX_HEADS), F32)], axis=1).reshape(bs, 8, 1)
    kidx_t = jnp.transpose(cache_kidx[l], (0, 2, 1))
    scores = _sample_scores(page_table, kidx_t, qi8, wi8, kib_s.reshape(bs, 1, IDX_DIM))
    topk_s = min(TOPK_MAX, (past + ss) // 4)
    bias = _sample_select(scores.reshape(bs, past + LANES), topk_s).reshape(bs, 1, past + LANES)
    kvw = N_KV_HEADS * HEAD_DIM
    k_t = jnp.transpose(cache_k[l], (0, 2, 3, 1)).reshape(-1, kvw, PAGE_SIZE)
    v_t = jnp.transpose(cache_v[l], (0, 2, 3, 1)).reshape(-1, kvw, PAGE_SIZE)
    q3 = q_s.reshape(bs, N_HEADS, HEAD_DIM)
    zq = jnp.zeros_like(q3)
    in_first = (jnp.arange(N_HEADS) < N_HEADS // N_KV_HEADS)[None, :, None]
    q2_s = jnp.where(in_first, jnp.concatenate([q3, zq], axis=2), jnp.concatenate([zq, q3], axis=2))
    attn_s = _sample_attention(page_table, k_t, v_t, q2_s, bias,
                               kb_s.reshape(bs, 1, kvw), vb_s.reshape(bs, 1, kvw))
    gd = GM_WIDTH // GM_GROUPS
    w0 = jnp.repeat(gm_spatial_w[l][:, 0, 0], gd).reshape(1, GM_WIDTH)
    b0 = jnp.repeat(gm_spatial_b[l][:, 0], gd).reshape(1, GM_WIDTH)
    gm_s = _gmlp_sample(u_s, vg_s, w0, b0)
    x1_s, h2_s, gate_s = _merge(attn_s.reshape(1, 1, bs, N_HEADS * HEAD_DIM), gm_s, ga_s, gb_s,
                                x_sample.reshape(bs, D_MODEL), mod_s, wpa, wpg, wo, g2, wr, br, bs, 1, False)
    y_s = _moe(h2_s, gate_s, x1_s, mod_s, wgu, wd, bs, 1)

    def rows_kv(a_t, n, s):
        return jnp.transpose(a_t.reshape(n, N_KV_HEADS, HEAD_DIM, s), (0, 3, 1, 2))[None]

    def rows_ki(a_t):
        return jnp.transpose(a_t, (0, 2, 1))[None]

    return (y_p.reshape(bp, sp, D_MODEL), y_s.reshape(bs, ss, D_MODEL),
            rows_kv(kt_f, bp, sp), rows_kv(vt_f, bp, sp), rows_ki(kit_f),
            rows_kv(kt_s, 1, bs).reshape(1, bs, ss, N_KV_HEADS, HEAD_DIM),
            rows_kv(vt_s, 1, bs).reshape(1, bs, ss, N_KV_HEADS, HEAD_DIM),
            rows_ki(kit_s).reshape(1, bs, ss, IDX_DIM), vg_s.reshape(1, bs, ss, GM_WIDTH))
```

```python
import functools

import numpy as np
import jax
import jax.numpy as jnp
from jax import lax
from jax.experimental import pallas as pl
from jax.experimental.pallas import tpu as pltpu

F32 = jnp.float32
BF16 = jnp.bfloat16
U32 = jnp.uint32

D_MODEL = 1024
N_HEADS = 8
HEAD_DIM = 64
N_KV_HEADS = 2
ROT_DIM = 16
ROPE_THETA = 500000.0
IDX_HEADS = 4
IDX_DIM = 64
TOPK_MAX = 256
Q_BLOCK = 128
GM_WIDTH = 512
GM_GROUPS = 8
CHUNK = 128
N_EXPERT_GROUPS = 4
EXPERTS_PER_GROUP = 8
N_EXPERTS = 32
D_EXPERT = 256
EPS = 1e-6
PAGE_SIZE = 128
LANES = 128
SUB = 8

C_Q, C_K, C_V, C_QI, C_KIWI, C_U, C_VG, C_GATE = 0, 512, 640, 768, 1024, 1152, 1664, 2176
D_IN_PAD = 4224
VMEM_LIMIT = 56 * 1024 * 1024

N_PAIRS = EXPERTS_PER_GROUP * (EXPERTS_PER_GROUP - 1) // 2
N_CLASSES = N_EXPERT_GROUPS * N_PAIRS
ROW_TILE = 128
HALF = D_MODEL // 2
FEAT_SUB = HALF // LANES
MOVE_TILE = 256
HI_MASK = 0xFFFF0000


def _cparams(sem):
    return pltpu.CompilerParams(dimension_semantics=sem, vmem_limit_bytes=VMEM_LIMIT)


def _pack_halves(x):
    n = x.shape[1] // 2
    hi = lax.bitcast_convert_type(x[:, 0:n], U32) & jnp.uint32(HI_MASK)
    lo = lax.bitcast_convert_type(x[:, n:2 * n], U32) >> 16
    return hi | lo


def _unpack_halves(u):
    hi = lax.bitcast_convert_type(u & jnp.uint32(HI_MASK), F32)
    lo = lax.bitcast_convert_type(u << 16, F32)
    return jnp.concatenate([hi, lo], axis=1)


def _store_token_tiles(ref, words, extra):
    n = words.shape[0]
    for s in range(FEAT_SUB):
        ref[pl.ds(s, n, stride=SUB), :] = words[:, s * LANES:(s + 1) * LANES]
    zero = jnp.zeros((n, LANES), U32)
    ref[pl.ds(FEAT_SUB, n, stride=SUB), :] = zero if extra is None else extra
    for s in range(FEAT_SUB + 1, SUB):
        ref[pl.ds(s, n, stride=SUB), :] = zero


def _load_token_words(ref, n):
    return jnp.concatenate([ref[pl.ds(s, n, stride=SUB), :] for s in range(FEAT_SUB)], axis=1)


def _adaln_body(c_ref, w_ref, b_ref, o_ref):
    c = c_ref[...]
    a = c * jax.nn.sigmoid(c)
    o_ref[...] = jnp.dot(a, w_ref[...], preferred_element_type=F32,
                         precision=lax.Precision.HIGHEST) + b_ref[...]


def _adaln(c, w, b):
    r = c.shape[0]
    n = w.shape[1]
    bn = 1536
    return pl.pallas_call(
        _adaln_body,
        grid=(n // bn,),
        in_specs=[pl.BlockSpec((r, D_MODEL), lambda j: (0, 0)),
                  pl.BlockSpec((D_MODEL, bn), lambda j: (0, j)),
                  pl.BlockSpec((1, bn), lambda j: (0, j))],
        out_specs=pl.BlockSpec((r, bn), lambda j: (0, j)),
        out_shape=jax.ShapeDtypeStruct((r, n), F32),
        compiler_params=_cparams(("arbitrary",)),
        name="adaln",
    )(c, w, b.reshape(1, n))


def _rope_table_body(pos_ref, invf_ref, sa_m_ref, sb_m_ref, c_ref, sa_ref, sb_ref):
    ang = pos_ref[...] * invf_ref[...]
    s = jnp.sin(ang)
    c_ref[...] = jnp.cos(ang)
    sa_ref[...] = s * sa_m_ref[...]
    sb_ref[...] = s * sb_m_ref[...]


def _rope_tables(pos):
    half = ROT_DIM // 2
    inv_freq = ROPE_THETA ** (-jnp.arange(half, dtype=F32) / half)
    d = np.arange(LANES) % HEAD_DIM
    invf = jnp.where(jnp.asarray(d < ROT_DIM), inv_freq[d % half], 0.0).reshape(1, LANES)
    sa_m = jnp.asarray(np.where(d < half, -1.0, 0.0), F32).reshape(1, LANES)
    sb_m = jnp.asarray(np.where((d >= half) & (d < ROT_DIM), 1.0, 0.0), F32).reshape(1, LANES)
    r = pos.shape[0]
    return pl.pallas_call(
        _rope_table_body,
        out_shape=[jax.ShapeDtypeStruct((r, LANES), F32)] * 3,
        name="rope_tables",
    )(pos, invf, sa_m, sb_m)


def _rope(y, c, sa, sb):
    return y * c + pltpu.roll(y, LANES - ROT_DIM // 2, 1) * sa + pltpu.roll(y, ROT_DIM // 2, 1) * sb


def _seg_rms(r, seg, gain):
    ss = r * r
    hi = ss.astype(BF16)
    lo = (ss - hi.astype(F32)).astype(BF16)
    tot = jnp.dot(hi, seg, preferred_element_type=F32) + jnp.dot(lo, seg, preferred_element_type=F32)
    return r * lax.rsqrt(tot * (1.0 / HEAD_DIM) + EPS) * gain


def _proj_body(mod3d, x_ref, mod_ref, g_ref, w_ref, c_ref, sa_ref, sb_ref, qg_ref, kg_ref, kig_ref,
               lng_ref, lnb_ref, seg_ref, segki_ref,
               q_o, kb_o, vb_o, qi_o, kib_o, kiwi_o, u_o, vg_o, ga_o, gb_o, kt_o, vtf_o, kit_o, vt_o):
    x = x_ref[...]
    m = mod_ref[0] if mod3d else mod_ref[...]
    shift1 = m[:, 0:D_MODEL]
    scale1 = m[:, D_MODEL:2 * D_MODEL]
    ms = jnp.mean(x * x, axis=-1, keepdims=True)
    y = x * lax.rsqrt(ms + EPS) * g_ref[...]
    h = (y * (1.0 + scale1) + shift1).astype(BF16)
    c, sa, sb = c_ref[...], sa_ref[...], sb_ref[...]
    seg = seg_ref[...]

    def proj(a, b):
        return jnp.dot(h, w_ref[:, a:b], preferred_element_type=F32)

    for g2 in range(2):
        r2 = proj(C_Q + g2 * 2 * LANES, C_Q + (g2 + 1) * 2 * LANES)
        for g in (2 * g2, 2 * g2 + 1):
            r = r2[:, (g % 2) * LANES:(g % 2 + 1) * LANES]
            yq = _rope(_seg_rms(r, seg, qg_ref[...]), c, sa, sb) * (HEAD_DIM ** -0.5)
            q_o[:, g * LANES:(g + 1) * LANES] = yq.astype(BF16)
    r_kv = proj(C_K, C_V + LANES)
    yk = _rope(_seg_rms(r_kv[:, 0:LANES], seg, kg_ref[...]), c, sa, sb)
    kt_o[0] = yk.T
    kb_o[...] = yk.astype(BF16)
    r = r_kv[:, LANES:2 * LANES]
    r_t = r.T
    vtf_o[0] = r_t
    vb_o[...] = r.astype(BF16)
    vt_o[0] = r_t.astype(BF16)
    r2 = proj(C_QI, C_QI + 2 * LANES)
    for g in range(2):
        yqi = _rope(r2[:, g * LANES:(g + 1) * LANES], c, sa, sb) * (IDX_DIM ** -0.5 * IDX_HEADS ** -0.5)
        qi_o[:, g * LANES:(g + 1) * LANES] = yqi.astype(BF16)
    r = proj(C_KIWI, C_KIWI + LANES)
    yki = _rope(_seg_rms(r, segki_ref[...], kig_ref[...]), c, sa, sb)
    lane = lax.broadcasted_iota(jnp.int32, r.shape, 1)
    kiwi = jnp.where(lane < IDX_DIM, yki, r)
    kiwi_o[...] = kiwi
    kit_o[0] = kiwi.T[0:IDX_DIM, :]
    kib_o[...] = kiwi[:, 0:IDX_DIM].astype(BF16)
    r = proj(C_U, C_U + GM_WIDTH)
    u_o[...] = jax.nn.gelu(r).astype(BF16)
    r = proj(C_VG, C_VG + GM_WIDTH)
    gl = jax.nn.gelu(r)
    mu = jnp.mean(gl, axis=-1, keepdims=True)
    dv = gl - mu
    var = jnp.mean(dv * dv, axis=-1, keepdims=True)
    vg_o[...] = (dv * lax.rsqrt(var + EPS) * lng_ref[...] + lnb_ref[...]).astype(vg_o.dtype)
    r = proj(C_GATE, C_GATE + D_MODEL)
    ga_o[...] = jax.nn.sigmoid(r).astype(BF16)
    r = proj(C_GATE + D_MODEL, C_GATE + 2 * D_MODEL)
    gb_o[...] = jax.nn.sigmoid(r).astype(BF16)


def _project(x, mod, tables, consts, w_pad, tm, tiles_per_seq, vg_dtype):
    t = x.shape[0]
    mod3d = mod.ndim == 3
    c_t, sa_t, sb_t = tables
    g_mix, qg, kg, kig, lng, lnb, seg, segki = consts
    row = lambda i: (i, 0)
    fixed = lambda i: (0, 0)
    if mod3d:
        mod_spec = pl.BlockSpec((1, 1, 6 * D_MODEL), lambda i: (i // tiles_per_seq, 0, 0))
        tab_spec = pl.BlockSpec((tm, LANES), lambda i: (i % tiles_per_seq, 0))
    else:
        mod_spec = pl.BlockSpec((tm, 6 * D_MODEL), row)
        tab_spec = pl.BlockSpec((1, LANES), fixed)
    widths = [(512, BF16), (128, BF16), (128, BF16), (256, BF16),
              (64, BF16), (128, F32), (512, BF16), (512, vg_dtype), (1024, BF16), (1024, BF16)]
    n_seq, seq = t // (tm * tiles_per_seq), tm * tiles_per_seq
    t_idx = lambda i: (i // tiles_per_seq, 0, i % tiles_per_seq)
    t_outs = [(LANES, F32), (LANES, F32), (IDX_DIM, F32), (LANES, BF16)]
    return pl.pallas_call(
        functools.partial(_proj_body, mod3d),
        grid=(t // tm,),
        in_specs=[pl.BlockSpec((tm, D_MODEL), row), mod_spec,
                  pl.BlockSpec((1, D_MODEL), fixed),
                  pl.BlockSpec((D_MODEL, D_IN_PAD), fixed),
                  tab_spec, tab_spec, tab_spec,
                  pl.BlockSpec((1, LANES), fixed), pl.BlockSpec((1, LANES), fixed), pl.BlockSpec((1, LANES), fixed),
                  pl.BlockSpec((1, GM_WIDTH), fixed), pl.BlockSpec((1, GM_WIDTH), fixed),
                  pl.BlockSpec((LANES, LANES), fixed), pl.BlockSpec((LANES, LANES), fixed)],
        out_specs=[pl.BlockSpec((tm, w), row) for w, _ in widths]
        + [pl.BlockSpec((1, f, tm), t_idx) for f, _ in t_outs],
        out_shape=[jax.ShapeDtypeStruct((t, w), dt) for w, dt in widths]
        + [jax.ShapeDtypeStruct((n_seq, f, seq), dt) for f, dt in t_outs],
        compiler_params=_cparams(("arbitrary",)),
        name="project",
    )(x, mod, g_mix, w_pad, c_t, sa_t, sb_t, qg, kg, kig, lng, lnb, seg, segki)


def _select_bias(s_ref, bias_ref, rows, width, topk, tie_check_start=16):
    nb = width // LANES
    kf = float(topk)
    neg, pos = -jnp.inf, jnp.inf

    def blk(j):
        return s_ref[:, j * LANES:(j + 1) * LANES]

    def count_above(t):
        tb = jnp.broadcast_to(t, (rows, LANES))
        acc = jnp.zeros((rows, LANES), F32)
        for j in range(nb):
            acc = acc + jnp.where(blk(j) > tb, 1.0, 0.0)
        return jnp.sum(acc, axis=1, keepdims=True)

    mx = jnp.full((rows, LANES), neg, F32)
    mn = jnp.full((rows, LANES), pos, F32)
    for j in range(nb):
        b = blk(j)
        mx = jnp.maximum(mx, b)
        mn = jnp.minimum(mn, jnp.where(b == neg, pos, b))
    hi0 = jnp.max(mx, axis=1, keepdims=True)
    smin = jnp.min(mn, axis=1, keepdims=True)
    lo0 = smin - jnp.abs(smin) - 1.0
    f_lo0 = count_above(lo0)
    zeros = jnp.zeros((rows, 1), F32)

    def active_of(f_lo, tie):
        return jnp.logical_and(f_lo > kf, tie == 0.0)

    def cond(st):
        _, _, _, f_lo, _, tie = st
        return jnp.max(jnp.where(active_of(f_lo, tie), 1.0, 0.0)) > 0.0

    def body(st):
        it, lo, hi, f_lo, f_hi, tie = st
        active = active_of(f_lo, tie)
        mid = lo + (hi - lo) * 0.5
        stuck = jnp.logical_or(mid <= lo, mid >= hi)
        cnt = count_above(mid)
        ge = cnt >= kf
        up_lo = jnp.logical_and(active, ge)
        up_hi = jnp.logical_and(active, jnp.logical_not(ge))
        lo = jnp.where(up_lo, mid, lo)
        f_lo = jnp.where(up_lo, cnt, f_lo)
        hi = jnp.where(up_hi, mid, hi)
        f_hi = jnp.where(up_hi, cnt, f_hi)
        tie = jnp.where(jnp.logical_and(active, stuck), 1.0, tie)

        def tie_check(_):
            lob = jnp.broadcast_to(lo, (rows, LANES))
            hib = jnp.broadcast_to(hi, (rows, LANES))
            vmx = jnp.full((rows, LANES), neg, F32)
            vmn = jnp.full((rows, LANES), pos, F32)
            for j in range(nb):
                b = blk(j)
                inn = jnp.logical_and(b > lob, b <= hib)
                vmx = jnp.maximum(vmx, jnp.where(inn, b, neg))
                vmn = jnp.minimum(vmn, jnp.where(inn, b, pos))
            one_value = jnp.max(vmx, axis=1, keepdims=True) == jnp.min(vmn, axis=1, keepdims=True)
            return jnp.where(one_value, 1.0, tie)

        run_check = jnp.logical_and(it >= tie_check_start, it % 4 == 0)
        tie = lax.cond(run_check, tie_check, lambda _: tie, 0)
        return it + 1, lo, hi, f_lo, f_hi, tie

    _, lo, hi, f_lo, f_hi, _ = lax.while_loop(cond, body, (jnp.int32(0), lo0, hi0, f_lo0, zeros, zeros))
    lob = jnp.broadcast_to(lo, (rows, LANES))
    need_prefix = jnp.max(jnp.where(f_lo > kf, 1.0, 0.0)) > 0.0

    @pl.when(jnp.logical_not(need_prefix))
    def _():
        for j in range(nb):
            bias_ref[:, j * LANES:(j + 1) * LANES] = jnp.where(blk(j) > lob, 0.0, neg)

    @pl.when(need_prefix)
    def _():
        hib = jnp.broadcast_to(hi, (rows, LANES))
        need = kf - f_hi
        ri = lax.broadcasted_iota(jnp.int32, (LANES, LANES), 0)
        ci = lax.broadcasted_iota(jnp.int32, (LANES, LANES), 1)
        upper = jnp.where(ri < ci, 1.0, 0.0).astype(BF16)
        off = jnp.zeros((rows, 1), F32)
        for j in range(nb):
            b = blk(j)
            inn = jnp.logical_and(b > lob, b <= hib)
            innf = jnp.where(inn, 1.0, 0.0)
            before = jnp.dot(innf.astype(BF16), upper, preferred_element_type=F32) + off
            sel = jnp.logical_or(b > hib, jnp.logical_and(inn, before < need))
            bias_ref[:, j * LANES:(j + 1) * LANES] = jnp.where(sel, 0.0, neg)
            off = off + jnp.sum(innf, axis=1, keepdims=True)


_NT = (((1,), (1,)), ((), ()))


def _sub_reduce(x, op):
    for sh in (4, 2, 1):
        x = op(x, pltpu.roll(x, sh, 0))
    return x


def _select_bias_t(s_ref, bias_ref, width, topk, n_adm, steps_per_check=4, tie_check_from=4):
    rb = 64
    nb = width // rb
    kf = float(topk)
    neg, pos = -jnp.inf, jnp.inf

    def blk(j):
        return s_ref[j * rb:(j + 1) * rb, :]

    def fold(x, op):
        y = x[0:SUB]
        for a in range(1, rb // SUB):
            y = op(y, x[a * SUB:(a + 1) * SUB])
        return _sub_reduce(y, op)

    def tile(v):
        return jnp.concatenate([v] * (rb // SUB), axis=0)

    def count_above(t):
        tb = tile(t)
        acc = jnp.zeros((rb, LANES), F32)
        for j in range(nb):
            acc = acc + jnp.where(blk(j) > tb, 1.0, 0.0)
        return fold(acc, jnp.add)

    mx = jnp.full((rb, LANES), neg, F32)
    mn = jnp.full((rb, LANES), pos, F32)
    for j in range(nb):
        b = blk(j)
        mx = jnp.maximum(mx, b)
        mn = jnp.minimum(mn, jnp.where(b == neg, pos, b))
    hi0 = fold(mx, jnp.maximum)
    smin = fold(mn, jnp.minimum)
    lo0 = smin - jnp.abs(smin) - 1.0
    zeros = jnp.zeros((SUB, LANES), F32)

    def active_of(f_lo, tie):
        return jnp.logical_and(f_lo > kf, tie == 0.0)

    def any_lane(cond):
        return jnp.max(jnp.where(cond, 1.0, 0.0)) > 0.0

    def step(lo, hi, f_lo, f_hi, tie):
        active = active_of(f_lo, tie)
        mid = lo + (hi - lo) * 0.5
        stuck = jnp.logical_or(mid <= lo, mid >= hi)
        cnt = count_above(mid)
        ge = cnt >= kf
        up_lo = jnp.logical_and(active, ge)
        up_hi = jnp.logical_and(active, jnp.logical_not(ge))
        return (jnp.where(up_lo, mid, lo), jnp.where(up_hi, mid, hi), jnp.where(up_lo, cnt, f_lo),
                jnp.where(up_hi, cnt, f_hi), jnp.where(jnp.logical_and(active, stuck), 1.0, tie))

    def cond(st):
        _, _, _, f_lo, _, tie = st
        return any_lane(active_of(f_lo, tie))

    def body(st):
        it, lo, hi, f_lo, f_hi, tie = st
        for _ in range(steps_per_check):
            lo, hi, f_lo, f_hi, tie = step(lo, hi, f_lo, f_hi, tie)

        def tie_check(_):
            lob, hib = tile(lo), tile(hi)
            vmx = jnp.full((rb, LANES), neg, F32)
            vmn = jnp.full((rb, LANES), pos, F32)
            for j in range(nb):
                b = blk(j)
                inn = jnp.logical_and(b > lob, b <= hib)
                vmx = jnp.maximum(vmx, jnp.where(inn, b, neg))
                vmn = jnp.minimum(vmn, jnp.where(inn, b, pos))
            return jnp.where(fold(vmx, jnp.maximum) == fold(vmn, jnp.minimum), 1.0, tie)

        run_check = jnp.logical_and(it + 1 >= tie_check_from, any_lane(active_of(f_lo, tie)))
        tie = lax.cond(run_check, tie_check, lambda _: tie, 0)
        return it + 1, lo, hi, f_lo, f_hi, tie

    _, lo, hi, f_lo, f_hi, _ = lax.while_loop(cond, body, (jnp.int32(0), lo0, hi0, n_adm, zeros, zeros))
    lob = tile(lo)
    need_prefix = any_lane(f_lo > kf)

    @pl.when(jnp.logical_not(need_prefix))
    def _():
        for j in range(nb):
            bias_ref[j * rb:(j + 1) * rb, :] = jnp.where(blk(j) > lob, 0.0, neg)

    @pl.when(need_prefix)
    def _():
        need = (kf - f_hi)[0:1]
        lo1, hi1 = lo[0:1], hi[0:1]
        ri = lax.broadcasted_iota(jnp.int32, (LANES, LANES), 0)
        ci = lax.broadcasted_iota(jnp.int32, (LANES, LANES), 1)
        lower = jnp.where(ci < ri, 1.0, 0.0).astype(BF16)
        off = jnp.zeros((1, LANES), F32)
        for j in range(width // LANES):
            b = s_ref[j * LANES:(j + 1) * LANES, :]
            inn = jnp.logical_and(b > lo1, b <= hi1)
            innf = jnp.where(inn, 1.0, 0.0)
            before = jnp.dot(lower, innf.astype(BF16), preferred_element_type=F32) + off
            sel = jnp.logical_or(b > hi1, jnp.logical_and(inn, before < need))
            bias_ref[j * LANES:(j + 1) * LANES, :] = jnp.where(sel, 0.0, neg)
            off = off + jnp.sum(innf, axis=0, keepdims=True)


def _prompt_attn_body(blk_i, topk, qi_ref, kiwi_ref, kib_ref, q_ref, k_ref, vt_ref, o_ref, s_ref, bias_ref):
    width = (blk_i + 1) * Q_BLOCK
    neg = -jnp.inf
    kiwi_t = kiwi_ref[0].T
    chunk = 512
    for c0 in range(0, width, chunk):
        c1 = min(width, c0 + chunk)
        kib = kib_ref[0, c0:c1, :]
        sc = None
        for h in range(IDX_HEADS):
            d = lax.dot_general(kib, qi_ref[0, :, h * IDX_DIM:(h + 1) * IDX_DIM], _NT,
                                preferred_element_type=F32)
            t = jnp.maximum(d, 0.0) * kiwi_t[IDX_DIM + h:IDX_DIM + h + 1, :]
            sc = t if sc is None else sc + t
        s_ref[c0:c1, :] = sc
    ki = lax.broadcasted_iota(jnp.int32, (Q_BLOCK, Q_BLOCK), 0)
    qj = lax.broadcasted_iota(jnp.int32, (Q_BLOCK, Q_BLOCK), 1)
    d0 = width - Q_BLOCK
    s_ref[d0:width, :] = jnp.where(ki <= qj, s_ref[d0:width, :], neg)
    if width > topk:
        n_adm = (lax.broadcasted_iota(jnp.int32, (SUB, LANES), 1) + (d0 + 1)).astype(F32)
        _select_bias_t(s_ref, bias_ref, width, topk, n_adm)
    else:
        bias_ref[...] = jnp.where(s_ref[...] == neg, neg, 0.0)
    lane = lax.broadcasted_iota(jnp.int32, (Q_BLOCK, LANES), 1)
    hpg = N_HEADS // N_KV_HEADS
    for h in range(N_HEADS):
        g = h // hpg
        qp = q_ref[0, :, (h // 2) * LANES:(h // 2 + 1) * LANES].astype(F32)
        if h % 2 != g:
            qp = pltpu.roll(qp, HEAD_DIM, 1)
        q2 = jnp.where((lane >= HEAD_DIM) == (g == 1), qp, 0.0).astype(BF16)
        s = lax.dot_general(k_ref[0], q2, _NT, preferred_element_type=F32) + bias_ref[...]
        m = jnp.max(s, axis=0, keepdims=True)
        p = jnp.exp(s - m)
        l = jnp.sum(p, axis=0, keepdims=True)
        o_t = jnp.dot(vt_ref[0, g * HEAD_DIM:(g + 1) * HEAD_DIM, :], p.astype(BF16), preferred_element_type=F32)
        o_ref[0, :, h * HEAD_DIM:(h + 1) * HEAD_DIM] = (o_t / l).T.astype(BF16)


def _prompt_attention(qi, kiwi, kib, q, kb, vt, topk):
    b, s, _ = q.shape
    outs = []
    for i in range(s // Q_BLOCK):
        width = (i + 1) * Q_BLOCK
        qblk = lambda bb, i=i: (bb, i, 0)
        kall = lambda bb: (bb, 0, 0)
        outs.append(pl.pallas_call(
            functools.partial(_prompt_attn_body, i, topk),
            grid=(b,),
            in_specs=[pl.BlockSpec((1, Q_BLOCK, IDX_HEADS * IDX_DIM), qblk),
                      pl.BlockSpec((1, Q_BLOCK, LANES), qblk),
                      pl.BlockSpec((1, width, IDX_DIM), kall),
                      pl.BlockSpec((1, Q_BLOCK, N_HEADS * HEAD_DIM), qblk),
                      pl.BlockSpec((1, width, N_KV_HEADS * HEAD_DIM), kall),
                      pl.BlockSpec((1, N_KV_HEADS * HEAD_DIM, width), kall)],
            out_specs=pl.BlockSpec((1, Q_BLOCK, N_HEADS * HEAD_DIM), lambda bb: (bb, 0, 0)),
            out_shape=jax.ShapeDtypeStruct((b, Q_BLOCK, N_HEADS * HEAD_DIM), BF16),
            scratch_shapes=[pltpu.VMEM((width, Q_BLOCK), F32), pltpu.VMEM((width, Q_BLOCK), F32)],
            compiler_params=_cparams(("arbitrary",)),
            name=f"prompt_attn_{i}",
        )(qi, kiwi, kib, q, kb, vt))
    return jnp.stack(outs)


def _page_copies(pt_ref, sample, src_hbm, buf, slot, sem, n_pages):
    return [pltpu.make_async_copy(src_hbm.at[pt_ref[sample, p]],
                                  buf.at[slot, :, pl.ds(p * PAGE_SIZE, PAGE_SIZE)], sem.at[slot])
            for p in range(n_pages)]


def _sample_scores_body(n_pages, pt_ref, kidx_hbm, qi_ref, wi_ref, kin_ref, o_ref, buf, sem):
    s = pl.program_id(0)
    slot = s % 2

    @pl.when(s == 0)
    def _():
        _start_all(_page_copies(pt_ref, 0, kidx_hbm, buf, 0, sem, n_pages))

    @pl.when(s + 1 < pl.num_programs(0))
    def _():
        _start_all(_page_copies(pt_ref, s + 1, kidx_hbm, buf, 1 - slot, sem, n_pages))

    for cp in _page_copies(pt_ref, s, kidx_hbm, buf, slot, sem, n_pages):
        cp.wait()
    qi = qi_ref[0]
    wi = wi_ref[0]
    ki_t = buf[slot].astype(BF16)
    d = jnp.dot(qi, ki_t, preferred_element_type=F32)
    past = jnp.sum(jnp.maximum(d, 0.0) * wi, axis=0, keepdims=True)
    dn = jnp.sum(qi.astype(F32) * kin_ref[0].astype(F32), axis=1, keepdims=True)
    new = jnp.sum(jnp.maximum(dn, 0.0) * wi, axis=0, keepdims=True)
    lane = lax.broadcasted_iota(jnp.int32, (1, LANES), 1)
    tail = jnp.where(lane == 0, jnp.broadcast_to(new, (1, LANES)), -jnp.inf)
    o_ref[0] = jnp.concatenate([past, tail], axis=1)


def _sample_scores(page_table, kidx_pool, qi8, wi8, ki_new):
    n, n_pages = page_table.shape
    past = n_pages * PAGE_SIZE
    grid_spec = pltpu.PrefetchScalarGridSpec(
        num_scalar_prefetch=1,
        grid=(n,),
        in_specs=[pl.BlockSpec(memory_space=pl.ANY),
                  pl.BlockSpec((1, 8, IDX_DIM), lambda s, pt: (s, 0, 0)),
                  pl.BlockSpec((1, 8, 1), lambda s, pt: (s, 0, 0)),
                  pl.BlockSpec((1, 1, IDX_DIM), lambda s, pt: (s, 0, 0))],
        out_specs=pl.BlockSpec((1, 1, past + LANES), lambda s, pt: (s, 0, 0)),
        scratch_shapes=[pltpu.VMEM((2, IDX_DIM, past), F32), pltpu.SemaphoreType.DMA((2,))],
    )
    return pl.pallas_call(
        functools.partial(_sample_scores_body, n_pages),
        grid_spec=grid_spec,
        out_shape=jax.ShapeDtypeStruct((n, 1, past + LANES), F32),
        compiler_params=_cparams(("arbitrary",)),
        name="sample_scores",
    )(page_table, kidx_pool, qi8, wi8, ki_new)


def _sample_select_body(topk, s_ref, bias_ref):
    rows, width = s_ref.shape
    _select_bias(s_ref, bias_ref, rows, width, topk)


def _sample_select(scores, topk):
    return pl.pallas_call(
        functools.partial(_sample_select_body, topk),
        out_shape=jax.ShapeDtypeStruct(scores.shape, F32),
        compiler_params=pltpu.CompilerParams(vmem_limit_bytes=VMEM_LIMIT),
        name="sample_select",
    )(scores)


def _sample_attn_body(n_pages, pt_ref, k_hbm, v_hbm, q_ref, bias_ref, kn_ref, vn_ref, o_ref, kbuf, vbuf, sem):
    s = pl.program_id(0)
    slot = s % 2
    past = n_pages * PAGE_SIZE

    def copies(sample, sl):
        return (_page_copies(pt_ref, sample, k_hbm, kbuf, sl, sem.at[0], n_pages)
                + _page_copies(pt_ref, sample, v_hbm, vbuf, sl, sem.at[1], n_pages))

    @pl.when(s == 0)
    def _():
        for cp in copies(0, 0):
            cp.start()

    @pl.when(s + 1 < pl.num_programs(0))
    def _():
        for cp in copies(s + 1, 1 - slot):
            cp.start()

    for cp in copies(s, slot):
        cp.wait()
    q2 = q_ref[0]
    k_t = kbuf[slot].astype(BF16)
    v_t = vbuf[slot].astype(BF16)
    row = lax.broadcasted_iota(jnp.int32, (N_HEADS, 1), 0)
    first = row < (N_HEADS // N_KV_HEADS)
    bias = bias_ref[0]
    sc = jnp.dot(q2, k_t, preferred_element_type=F32) + bias[:, 0:past]
    sn = jnp.sum(q2.astype(F32) * kn_ref[0].astype(F32), axis=1, keepdims=True) + bias[:, past:past + 1]
    m = jnp.maximum(jnp.max(sc, axis=1, keepdims=True), sn)
    p = jnp.exp(sc - m)
    pn = jnp.exp(sn - m)
    l = jnp.sum(p, axis=1, keepdims=True) + pn
    o2 = lax.dot_general(p.astype(BF16), v_t, _NT, preferred_element_type=F32)
    o2 = o2 + pn.astype(BF16).astype(F32) * vn_ref[0].astype(F32)
    o = jnp.where(first, o2[:, 0:HEAD_DIM], o2[:, HEAD_DIM:2 * HEAD_DIM])
    o_ref[0] = (o / l).astype(BF16)


def _sample_attention(page_table, k_pool, v_pool, q8, bias, k_new, v_new):
    n, n_pages = page_table.shape
    past = n_pages * PAGE_SIZE
    kvw = N_KV_HEADS * HEAD_DIM
    per = lambda s, pt: (s, 0, 0)
    grid_spec = pltpu.PrefetchScalarGridSpec(
        num_scalar_prefetch=1,
        grid=(n,),
        in_specs=[pl.BlockSpec(memory_space=pl.ANY), pl.BlockSpec(memory_space=pl.ANY),
                  pl.BlockSpec((1, N_HEADS, kvw), per),
                  pl.BlockSpec((1, 1, past + LANES), per),
                  pl.BlockSpec((1, 1, kvw), per),
                  pl.BlockSpec((1, 1, kvw), per)],
        out_specs=pl.BlockSpec((1, N_HEADS, HEAD_DIM), per),
        scratch_shapes=[pltpu.VMEM((2, kvw, past), F32), pltpu.VMEM((2, kvw, past), F32),
                        pltpu.SemaphoreType.DMA((2, 2))],
    )
    return pl.pallas_call(
        functools.partial(_sample_attn_body, n_pages),
        grid_spec=grid_spec,
        out_shape=jax.ShapeDtypeStruct((n, N_HEADS, HEAD_DIM), BF16),
        compiler_params=_cparams(("arbitrary",)),
        name="sample_attn",
    )(page_table, k_pool, v_pool, q8, bias, k_new, v_new)


def _gmlp_body(n_chunks, u_ref, vg_ref, w_ref, bt_ref, o_ref):
    ri = lax.broadcasted_iota(jnp.int32, (CHUNK, CHUNK), 0)
    ci = lax.broadcasted_iota(jnp.int32, (CHUNK, CHUNK), 1)
    gd = GM_WIDTH // GM_GROUPS
    for g in range(GM_GROUPS):
        wg = jnp.where(ci <= ri, w_ref[g], 0.0).astype(BF16)
        bg = bt_ref[:, g:g + 1]
        for c in range(n_chunks):
            rows = slice(c * CHUNK, (c + 1) * CHUNK)
            cols = slice(g * gd, (g + 1) * gd)
            mixed = jnp.dot(wg, vg_ref[rows, cols], preferred_element_type=F32) + bg
            o_ref[rows, cols] = (u_ref[rows, cols].astype(F32) * mixed).astype(BF16)


def _gmlp_prompt(u, vg, w, bt, tm):
    t = u.shape[0]
    row = lambda i: (i, 0)
    return pl.pallas_call(
        functools.partial(_gmlp_body, tm // CHUNK),
        grid=(t // tm,),
        in_specs=[pl.BlockSpec((tm, GM_WIDTH), row), pl.BlockSpec((tm, GM_WIDTH), row),
                  pl.BlockSpec((GM_GROUPS, CHUNK, CHUNK), lambda i: (0, 0, 0)),
                  pl.BlockSpec((CHUNK, LANES), lambda i: (0, 0))],
        out_specs=pl.BlockSpec((tm, GM_WIDTH), row),
        out_shape=jax.ShapeDtypeStruct((t, GM_WIDTH), BF16),
        compiler_params=_cparams(("arbitrary",)),
        name="gmlp",
    )(u, vg, w, bt)


def _gmlp_first_row_body(u_ref, vg_ref, w0_ref, b0_ref, o_ref):
    o_ref[...] = (u_ref[...].astype(F32) * (vg_ref[...] * w0_ref[...] + b0_ref[...])).astype(BF16)


def _gmlp_sample(u, vg, w0, b0):
    return pl.pallas_call(
        _gmlp_first_row_body,
        out_shape=jax.ShapeDtypeStruct(u.shape, BF16),
        name="gmlp_first_row",
    )(u, vg, w0, b0)


def _merge_body(mod3d, dispatch, attn_ref, gm_ref, ga_ref, gb_ref, x_ref, mod_ref, wpa_ref, wpg_ref, wo_ref,
                g2_ref, wr_ref, br_ref, x1_o, a_o, b_o):
    tm = x_ref.shape[0]
    attn = attn_ref[...].reshape(tm, N_HEADS * HEAD_DIM)
    a = jnp.dot(attn, wpa_ref[...], preferred_element_type=F32)
    g = jnp.dot(gm_ref[...], wpg_ref[...], preferred_element_type=F32)
    merged = ga_ref[...].astype(F32) * a + gb_ref[...].astype(F32) * g
    out = jnp.dot(merged.astype(BF16), wo_ref[...], preferred_element_type=F32)
    m = mod_ref[0] if mod3d else mod_ref[...]
    x1 = x_ref[...] + m[:, 2 * D_MODEL:3 * D_MODEL] * out
    x1_o[...] = x1
    ms = jnp.mean(x1 * x1, axis=-1, keepdims=True)
    y = x1 * lax.rsqrt(ms + EPS) * g2_ref[...]
    h2 = y * (1.0 + m[:, 4 * D_MODEL:5 * D_MODEL]) + m[:, 3 * D_MODEL:4 * D_MODEL]
    hi = h2.astype(BF16)
    lo = (h2 - hi.astype(F32)).astype(BF16)
    r = jnp.dot(hi, wr_ref[...], preferred_element_type=F32) + jnp.dot(lo, wr_ref[...], preferred_element_type=F32)
    logits = r[:, 0:LANES] + r[:, LANES:2 * LANES] + br_ref[...]
    neg = -jnp.inf
    big = jnp.int32(1 << 20)
    lane = lax.broadcasted_iota(jnp.int32, logits.shape, 1)
    is_g = jnp.logical_and(lane >= N_EXPERTS, lane < N_EXPERTS + N_EXPERT_GROUPS)
    gl = jnp.where(is_g, logits, neg)
    gmax = jnp.max(gl, axis=1, keepdims=True)
    g_lane = jnp.min(jnp.where(gl == gmax, lane, big), axis=1, keepdims=True)
    g_w = 1.0 / jnp.sum(jnp.exp(gl - gmax), axis=1, keepdims=True)
    g_sel = g_lane - N_EXPERTS
    in_grp = jnp.logical_and(lane < N_EXPERTS, (lane >> 3) == g_sel)
    el = jnp.where(in_grp, logits, neg)
    m1 = jnp.max(el, axis=1, keepdims=True)
    i1 = jnp.min(jnp.where(el == m1, lane, big), axis=1, keepdims=True)
    el2 = jnp.where(lane == i1, neg, el)
    m2 = jnp.max(el2, axis=1, keepdims=True)
    i2 = jnp.min(jnp.where(el2 == m2, lane, big), axis=1, keepdims=True)
    e2 = jnp.exp(m2 - m1)
    w1 = g_w / (1.0 + e2)
    w2 = g_w * e2 / (1.0 + e2)
    if not dispatch:
        a_o[...] = hi
        b_o[...] = jnp.where(lane == i1, w1, 0.0) + jnp.where(lane == i2, w2, 0.0)
        return
    low_first = i1 < i2
    ea = jnp.where(low_first, i1, i2) - g_sel * EXPERTS_PER_GROUP
    eb = jnp.where(low_first, i2, i1) - g_sel * EXPERTS_PER_GROUP
    cls = g_sel * N_PAIRS + ((ea * (2 * EXPERTS_PER_GROUP - 1 - ea)) >> 1) + (eb - ea - 1)
    b_o[...] = jnp.broadcast_to(cls, (tm, LANES))
    wa = jnp.where(low_first, w1, w2)
    wb = jnp.where(low_first, w2, w1)
    _store_token_tiles(a_o, _pack_halves(hi.astype(F32)), lax.bitcast_convert_type(
        jnp.where(lane == 0, wa, jnp.where(lane == 1, wb, 0.0)), U32))


def _merge(attn4, gm, ga, gb, x, mod, wpa, wpg, wo, g2, wr, br, tm, tiles_per_seq, dispatch):
    t = x.shape[0]
    mod3d = mod.ndim == 3
    nb = tm // Q_BLOCK
    row = lambda i: (i, 0)
    fixed = lambda i: (0, 0)
    if mod3d:
        mod_spec = pl.BlockSpec((1, 1, 6 * D_MODEL), lambda i: (i // tiles_per_seq, 0, 0))
    else:
        mod_spec = pl.BlockSpec((tm, 6 * D_MODEL), row)
    attn_spec = pl.BlockSpec((nb, 1, Q_BLOCK, N_HEADS * HEAD_DIM),
                             lambda i: (i % tiles_per_seq, i // tiles_per_seq, 0, 0))
    if dispatch:
        extra = [(SUB, LANES, U32), (1, LANES, jnp.int32)]
    else:
        extra = [(1, D_MODEL, BF16), (1, LANES, F32)]
    return pl.pallas_call(
        functools.partial(_merge_body, mod3d, dispatch),
        grid=(t // tm,),
        in_specs=[attn_spec, pl.BlockSpec((tm, GM_WIDTH), row),
                  pl.BlockSpec((tm, D_MODEL), row), pl.BlockSpec((tm, D_MODEL), row),
                  pl.BlockSpec((tm, D_MODEL), row), mod_spec,
                  pl.BlockSpec((N_HEADS * HEAD_DIM, D_MODEL), fixed), pl.BlockSpec((GM_WIDTH, D_MODEL), fixed),
                  pl.BlockSpec((D_MODEL, D_MODEL), fixed), pl.BlockSpec((1, D_MODEL), fixed),
                  pl.BlockSpec((D_MODEL, 2 * LANES), fixed), pl.BlockSpec((1, LANES), fixed)],
        out_specs=[pl.BlockSpec((tm, D_MODEL), row)] + [pl.BlockSpec((tm * r, w), row) for r, w, _ in extra],
        out_shape=[jax.ShapeDtypeStruct((t, D_MODEL), F32)]
        + [jax.ShapeDtypeStruct((t * r, w), dt) for r, w, dt in extra],
        compiler_params=_cparams(("arbitrary",)),
        name="merge",
    )(attn4, gm, ga, gb, x, mod, wpa, wpg, wo, g2, wr, br)


def _moe_body(mod3d, h_ref, gate_ref, x1_ref, mod_ref, wgu_ref, wd_ref, o_ref, acc_ref):
    e = pl.program_id(1)

    @pl.when(e == 0)
    def _():
        acc_ref[...] = jnp.zeros_like(acc_ref)

    gu = jnp.dot(h_ref[...], wgu_ref[0], preferred_element_type=F32)
    a = gu[:, 0:D_EXPERT]
    hid = a * jax.nn.sigmoid(a) * gu[:, D_EXPERT:2 * D_EXPERT]
    gate = gate_ref[...]
    lane = lax.broadcasted_iota(jnp.int32, gate.shape, 1)
    ge = jnp.sum(jnp.where(lane == e, gate, 0.0), axis=1, keepdims=True)
    acc_ref[...] += jnp.dot((hid * ge).astype(BF16), wd_ref[0], preferred_element_type=F32)

    @pl.when(e == pl.num_programs(1) - 1)
    def _():
        m = mod_ref[0] if mod3d else mod_ref[...]
        o_ref[...] = x1_ref[...] + m[:, 5 * D_MODEL:6 * D_MODEL] * acc_ref[...]


def _moe(h2, gate, x1, mod, wgu, wd, tm, tiles_per_seq):
    t = h2.shape[0]
    mod3d = mod.ndim == 3
    row = lambda i, e: (i, 0)
    if mod3d:
        mod_spec = pl.BlockSpec((1, 1, 6 * D_MODEL), lambda i, e: (i // tiles_per_seq, 0, 0))
    else:
        mod_spec = pl.BlockSpec((tm, 6 * D_MODEL), row)
    return pl.pallas_call(
        functools.partial(_moe_body, mod3d),
        grid=(t // tm, N_EXPERTS),
        in_specs=[pl.BlockSpec((tm, D_MODEL), row), pl.BlockSpec((tm, LANES), row),
                  pl.BlockSpec((tm, D_MODEL), row), mod_spec,
                  pl.BlockSpec((1, D_MODEL, 2 * D_EXPERT), lambda i, e: (e, 0, 0)),
                  pl.BlockSpec((1, D_EXPERT, D_MODEL), lambda i, e: (e, 0, 0))],
        out_specs=pl.BlockSpec((tm, D_MODEL), row),
        out_shape=jax.ShapeDtypeStruct((t, D_MODEL), F32),
        scratch_shapes=[pltpu.VMEM((tm, D_MODEL), F32)],
        compiler_params=_cparams(("arbitrary", "arbitrary")),
        name="moe",
    )(h2, gate, x1, mod, wgu, wd)


def _slots_body(cls_ref, slot_o, segend_o, carry, seg_start):
    sweep = pl.program_id(0)
    i = pl.program_id(1)
    tm = cls_ref.shape[0]
    lane = lax.broadcasted_iota(jnp.int32, (tm, LANES), 1)
    hit = lane == cls_ref[...]
    onehot = jnp.where(hit, 1.0, 0.0)

    @pl.when(jnp.logical_and(sweep == 0, i == 0))
    def _():
        carry[...] = jnp.zeros_like(carry)

    @pl.when(sweep == 0)
    def _():
        carry[...] = carry[...] + jnp.sum(onehot, axis=0, keepdims=True)
        slot_o[...] = jnp.zeros_like(slot_o)

        @pl.when(i == pl.num_programs(1) - 1)
        def _():
            padded = jnp.floor((carry[...] + (ROW_TILE - 1)) * (1.0 / ROW_TILE)) * ROW_TILE
            ri = lax.broadcasted_iota(jnp.int32, (LANES, LANES), 0)
            ci = lax.broadcasted_iota(jnp.int32, (LANES, LANES), 1)
            upto = jnp.where(ri <= ci, 1.0, 0.0)
            seg_end = jnp.dot(padded, upto, preferred_element_type=F32, precision=lax.Precision.HIGHEST)
            segend_o[...] = seg_end
            seg_start[...] = seg_end - padded
            carry[...] = jnp.zeros_like(carry)

    @pl.when(sweep == 1)
    def _():
        ri = lax.broadcasted_iota(jnp.int32, (tm, tm), 0)
        ci = lax.broadcasted_iota(jnp.int32, (tm, tm), 1)
        earlier = jnp.where(ci < ri, 1.0, 0.0).astype(BF16)
        before = (jnp.dot(earlier, onehot.astype(BF16), preferred_element_type=F32)
                  + carry[0:1, :] + seg_start[0:1, :])
        slot = jnp.sum(jnp.where(hit, before, 0.0), axis=1, keepdims=True)
        slot_o[...] = jnp.broadcast_to(slot, (tm, LANES)).astype(jnp.int32)
        carry[...] = carry[...] + jnp.sum(onehot, axis=0, keepdims=True)


def _class_slots(cls, tm):
    t = cls.shape[0]
    return pl.pallas_call(
        _slots_body,
        grid=(2, t // tm),
        in_specs=[pl.BlockSpec((tm, LANES), lambda s, i: (i, 0))],
        out_specs=[pl.BlockSpec((tm, LANES), lambda s, i: (i * s, 0)),
                   pl.BlockSpec((SUB, LANES), lambda s, i: (0, 0))],
        out_shape=[jax.ShapeDtypeStruct((t, LANES), jnp.int32), jax.ShapeDtypeStruct((SUB, LANES), F32)],
        scratch_shapes=[pltpu.VMEM((SUB, LANES), F32), pltpu.VMEM((SUB, LANES), F32)],
        compiler_params=_cparams(("arbitrary", "arbitrary")),
        name="class_slots",
    )(cls)


INVERT_CHUNK = 4096


def _invert_body(dest_ref, src_o):
    i = pl.program_id(0)

    @pl.when(i == 0)
    def _():
        src_o[...] = jnp.zeros_like(src_o)

    def put(r, carry):
        src_o[pl.ds(dest_ref[0, 0, r], 1), :] = jnp.full((1, LANES), i * INVERT_CHUNK + r, jnp.int32)
        return carry

    lax.fori_loop(0, INVERT_CHUNK, put, 0, unroll=8)


def _invert(dest, n_sorted):
    t = dest.shape[0]
    return pl.pallas_call(
        _invert_body,
        grid=(t // INVERT_CHUNK,),
        in_specs=[pl.BlockSpec((1, 1, INVERT_CHUNK), lambda i: (i, 0, 0), memory_space=pltpu.SMEM)],
        out_specs=pl.BlockSpec((n_sorted, LANES), lambda i: (0, 0)),
        out_shape=jax.ShapeDtypeStruct((n_sorted, LANES), jnp.int32),
        compiler_params=_cparams(("arbitrary",)),
        name="invert_slots",
    )(dest.reshape(t // INVERT_CHUNK, 1, INVERT_CHUNK))


def _token_fetches(idx_ref, tiles_hbm, buf, slot, sem, n):
    return [pltpu.make_async_copy(tiles_hbm.at[pl.ds(pl.multiple_of(idx_ref[0, 0, r] * SUB, SUB), SUB)],
                                  buf.at[slot, pl.ds(r * SUB, SUB)], sem.at[slot])
            for r in range(n)]


def _start_all(copies):
    for r, cp in enumerate(copies):
        cp.start(priority=r % 2)


def _tile_fetches(src_ref, rows_hbm, buf, slot, sem):
    return _token_fetches(src_ref, rows_hbm, buf, slot, sem, ROW_TILE)


FETCH_AHEAD = 2


def _experts_body(ea_ref, eb_ref, used_ref, s0_ref, s1_ref, s2_ref, rows_hbm, wgu_a_ref, wgu_b_ref, wd_a_ref,
                  wd_b_ref, o_ref, buf, sem):
    j = pl.program_id(0)
    n_buf = FETCH_AHEAD + 1
    slot = j % n_buf
    n_used = used_ref[0]

    @pl.when(j == 0)
    def _():
        _start_all(_tile_fetches(s0_ref, rows_hbm, buf, 0, sem))

    @pl.when(jnp.logical_and(j == 0, 1 < n_used))
    def _():
        _start_all(_tile_fetches(s1_ref, rows_hbm, buf, 1, sem))

    @pl.when(j + FETCH_AHEAD < n_used)
    def _():
        _start_all(_tile_fetches(s2_ref, rows_hbm, buf, (j + FETCH_AHEAD) % n_buf, sem))

    @pl.when(jnp.logical_or(j < n_used, j == 0))
    def _():
        for cp in _tile_fetches(s0_ref, rows_hbm, buf, slot, sem):
            cp.wait()

    @pl.when(j < n_used)
    def _():
        tiles = buf.at[slot]
        x = _unpack_halves(_load_token_words(tiles, ROW_TILE)).astype(BF16)
        wts = lax.bitcast_convert_type(tiles[pl.ds(FEAT_SUB, ROW_TILE, stride=SUB), :], F32)

        def hidden(wgu_ref, wgt):
            gu = jnp.dot(x, wgu_ref[0], preferred_element_type=F32)
            a = gu[:, 0:D_EXPERT]
            return (a * jax.nn.sigmoid(a) * gu[:, D_EXPERT:2 * D_EXPERT] * wgt).astype(BF16)

        y = (jnp.dot(hidden(wgu_a_ref, wts[:, 0:1]), wd_a_ref[0], preferred_element_type=F32)
             + jnp.dot(hidden(wgu_b_ref, wts[:, 1:2]), wd_b_ref[0], preferred_element_type=F32))
        _store_token_tiles(o_ref, _pack_halves(y.astype(BF16).astype(F32)), None)

    @pl.when(j >= n_used)
    def _():
        o_ref[...] = jnp.zeros_like(o_ref)


def _experts(ea_t, eb_t, n_used, src3, rows, wgu, wd):
    n_tiles = src3.shape[0]
    w_a = lambda j, ea, eb, nu: (ea[j], 0, 0)
    w_b = lambda j, ea, eb, nu: (eb[j], 0, 0)
    smem_blk = lambda f: pl.BlockSpec((1, 1, ROW_TILE), f, memory_space=pltpu.SMEM)
    grid_spec = pltpu.PrefetchScalarGridSpec(
        num_scalar_prefetch=3,
        grid=(n_tiles,),
        in_specs=[smem_blk(lambda j, ea, eb, nu: (j, 0, 0)),
                  smem_blk(lambda j, ea, eb, nu: (jnp.minimum(j + 1, n_tiles - 1), 0, 0)),
                  smem_blk(lambda j, ea, eb, nu: (jnp.minimum(j + FETCH_AHEAD, n_tiles - 1), 0, 0)),
                  pl.BlockSpec(memory_space=pl.ANY),
                  pl.BlockSpec((1, D_MODEL, 2 * D_EXPERT), w_a), pl.BlockSpec((1, D_MODEL, 2 * D_EXPERT), w_b),
                  pl.BlockSpec((1, D_EXPERT, D_MODEL), w_a), pl.BlockSpec((1, D_EXPERT, D_MODEL), w_b)],
        out_specs=pl.BlockSpec((ROW_TILE * SUB, LANES), lambda j, ea, eb, nu: (j, 0)),
        scratch_shapes=[pltpu.VMEM((FETCH_AHEAD + 1, ROW_TILE * SUB, LANES), U32),
                        pltpu.SemaphoreType.DMA((FETCH_AHEAD + 1,))],
    )
    return pl.pallas_call(
        _experts_body,
        grid_spec=grid_spec,
        out_shape=jax.ShapeDtypeStruct((n_tiles * ROW_TILE * SUB, LANES), U32),
        compiler_params=_cparams(("arbitrary",)),
        name="experts",
    )(ea_t, eb_t, n_used, src3, src3, src3, rows, wgu, wgu, wd, wd)


def _row_fetches(dest_ref, ys_hbm, buf, slot, sem):
    return _token_fetches(dest_ref, ys_hbm, buf, slot, sem, MOVE_TILE)


def _combine_body(mod3d, dcur_ref, dnext_ref, ys_hbm, x1_ref, mod_ref, o_ref, buf, sem):
    i = pl.program_id(0)
    slot = i % 2

    @pl.when(i == 0)
    def _():
        _start_all(_row_fetches(dcur_ref, ys_hbm, buf, 0, sem))

    @pl.when(i + 1 < pl.num_programs(0))
    def _():
        _start_all(_row_fetches(dnext_ref, ys_hbm, buf, 1 - slot, sem))

    for cp in _row_fetches(dcur_ref, ys_hbm, buf, slot, sem):
        cp.wait()
    m = mod_ref[0] if mod3d else mod_ref[...]
    o_ref[...] = x1_ref[...] + m[:, 5 * D_MODEL:6 * D_MODEL] * _unpack_halves(_load_token_words(buf.at[slot], MOVE_TILE))


def _combine(dest3, ys, x1, mod, tiles_per_seq):
    steps = dest3.shape[0]
    t = x1.shape[0]
    mod3d = mod.ndim == 3
    row = lambda i: (i, 0)
    if mod3d:
        mod_spec = pl.BlockSpec((1, 1, 6 * D_MODEL), lambda i: (i // tiles_per_seq, 0, 0))
    else:
        mod_spec = pl.BlockSpec((MOVE_TILE, 6 * D_MODEL), row)
    smem_blk = lambda f: pl.BlockSpec((1, 1, MOVE_TILE), f, memory_space=pltpu.SMEM)
    return pl.pallas_call(
        functools.partial(_combine_body, mod3d),
        grid=(steps,),
        in_specs=[smem_blk(lambda i: (i, 0, 0)), smem_blk(lambda i: (jnp.minimum(i + 1, steps - 1), 0, 0)),
                  pl.BlockSpec(memory_space=pl.ANY),
                  pl.BlockSpec((MOVE_TILE, D_MODEL), row), mod_spec],
        out_specs=pl.BlockSpec((MOVE_TILE, D_MODEL), row),
        out_shape=jax.ShapeDtypeStruct((t, D_MODEL), F32),
        scratch_shapes=[pltpu.VMEM((2, MOVE_TILE * SUB, LANES), U32), pltpu.SemaphoreType.DMA((2,))],
        compiler_params=_cparams(("arbitrary",)),
        name="combine",
    )(dest3, dest3, ys, x1, mod)


def _class_expert_tables():
    ea, eb = [], []
    for g in range(N_EXPERT_GROUPS):
        for a in range(EXPERTS_PER_GROUP):
            for b in range(a + 1, EXPERTS_PER_GROUP):
                ea.append(g * EXPERTS_PER_GROUP + a)
                eb.append(g * EXPERTS_PER_GROUP + b)
    return np.asarray(ea, np.int32), np.asarray(eb, np.int32)


def _moe_dispatched(rows, cls, x1, mod, wgu, wd, tiles_per_seq):
    t = x1.shape[0]
    slots, seg_end8 = _class_slots(cls, 1024)
    seg_end = seg_end8[0, 0:N_CLASSES].astype(jnp.int32)
    dest = slots[:, 0]
    dest3 = dest.reshape(t // MOVE_TILE, 1, MOVE_TILE)
    n_sorted = t + N_CLASSES * ROW_TILE
    tile_row0 = jnp.arange(n_sorted // ROW_TILE, dtype=jnp.int32) * ROW_TILE
    tile_cls = jnp.minimum(jnp.sum((seg_end[None, :] <= tile_row0[:, None]).astype(jnp.int32), axis=1),
                           N_CLASSES - 1)
    ea_np, eb_np = _class_expert_tables()
    in_cls = (tile_cls[:, None] == jnp.arange(N_CLASSES, dtype=jnp.int32)[None, :]).astype(jnp.int32)
    ea_t = jnp.sum(in_cls * jnp.asarray(ea_np)[None, :], axis=1)
    eb_t = jnp.sum(in_cls * jnp.asarray(eb_np)[None, :], axis=1)
    n_used = (seg_end[N_CLASSES - 1] // ROW_TILE).astype(jnp.int32).reshape(1)
    src = _invert(dest, n_sorted)[:, 0]
    ys = _experts(ea_t, eb_t, n_used, src.reshape(n_sorted // ROW_TILE, 1, ROW_TILE), rows, wgu, wd)
    return _combine(dest3, ys, x1, mod, tiles_per_seq)


def _pad_lanes(v, fill):
    n = v.shape[-1]
    return jnp.concatenate([v, jnp.full((LANES - n,), fill, v.dtype)]).reshape(1, LANES)


def kernel(x_prompt, x_sample, c_prompt, c_sample, cache_k, cache_v, cache_kidx, page_table, w_ada, b_ada, norm_mix_g, norm_ffn_g, w_in, q_norm_g, k_norm_g, kidx_norm_g, gm_ln_g, gm_ln_b, gm_spatial_w, gm_spatial_b, w_proj_attn, w_proj_gmlp, w_out, w_router_group, b_router_group, w_router_expert, b_router_expert, w_expert_gate, w_expert_up, w_expert_down):
    depth = w_ada.shape[0]
    assert depth == 1
    l = 0
    bp, sp, _ = x_prompt.shape
    bs, ss, _ = x_sample.shape
    assert ss == 1
    n_pages = page_table.shape[1]
    past = n_pages * PAGE_SIZE
    tp = bp * sp

    w = w_in[l]
    zpad = jnp.zeros((D_MODEL, LANES - IDX_DIM - IDX_HEADS), F32)
    w_pad = jnp.concatenate([w[:, 0:1024], w[:, 1024:1088], w[:, 1088:1092], zpad, w[:, 1092:]], axis=1).astype(BF16)
    seg_np = (np.arange(LANES)[:, None] // HEAD_DIM) == (np.arange(LANES)[None, :] // HEAD_DIM)
    seg = jnp.asarray(seg_np, BF16)
    segki = jnp.asarray(seg_np & (np.arange(LANES)[:, None] < IDX_DIM) & (np.arange(LANES)[None, :] < IDX_DIM), BF16)
    consts = (norm_mix_g[l].reshape(1, D_MODEL),
              jnp.tile(q_norm_g[l], 2).reshape(1, LANES), jnp.tile(k_norm_g[l], 2).reshape(1, LANES),
              _pad_lanes(kidx_norm_g[l], 1.0),
              gm_ln_g[l].reshape(1, GM_WIDTH), gm_ln_b[l].reshape(1, GM_WIDTH), seg, segki)
    wpa = w_proj_attn[l].astype(BF16)
    wpg = w_proj_gmlp[l].astype(BF16)
    wo = w_out[l].astype(BF16)
    wr32 = jnp.concatenate([w_router_expert[l], w_router_group[l],
                            jnp.zeros((D_MODEL, LANES - N_EXPERTS - N_EXPERT_GROUPS), F32)], axis=1)
    wr_hi = wr32.astype(BF16)
    wr_lo = (wr32 - wr_hi.astype(F32)).astype(BF16)
    wr = jnp.concatenate([wr_hi, wr_lo], axis=1)
    br = _pad_lanes(jnp.concatenate([b_router_expert[l], b_router_group[l]]), 0.0)
    wgu = jnp.concatenate([w_expert_gate[l], w_expert_up[l]], axis=2).astype(BF16)
    wd = w_expert_down[l].astype(BF16)
    g2 = norm_ffn_g[l].reshape(1, D_MODEL)

    mod = _adaln(jnp.concatenate([c_prompt, c_sample], axis=0), w_ada[l], b_ada[l])
    mod_p = mod[0:bp].reshape(bp, 1, 6 * D_MODEL)
    mod_s = mod[bp:bp + bs]
    pos = jnp.concatenate([jnp.arange(sp, dtype=jnp.int32),
                           jnp.full((8,), past, jnp.int32)]).astype(F32).reshape(sp + 8, 1)
    tabs = _rope_tables(pos)
    tabs_p = tuple(t[0:sp] for t in tabs)
    tabs_s = tuple(t[sp:sp + 1] for t in tabs)

    tm = 512
    tps = sp // tm
    (q, kb, vb, qi, kib, kiwi, u, vg, ga, gb, kt_f, vt_f, kit_f, vt) = _project(
        x_prompt.reshape(tp, D_MODEL), mod_p, tabs_p, consts, w_pad, tm, tps, BF16)
    topk_p = min(TOPK_MAX, sp // 4)
    r3 = lambda a: a.reshape(bp, sp, a.shape[-1])
    attn_p = _prompt_attention(r3(qi), r3(kiwi), r3(kib), r3(q), r3(kb), vt, topk_p)
    bt = jnp.concatenate([gm_spatial_b[l].T, jnp.zeros((CHUNK, LANES - GM_GROUPS), F32)], axis=1)
    gm_p = _gmlp_prompt(u, vg, gm_spatial_w[l], bt, tm)
    x1_p, rows_p, cls_p = _merge(attn_p, gm_p, ga, gb, x_prompt.reshape(tp, D_MODEL), mod_p,
                                 wpa, wpg, wo, g2, wr, br, tm, tps, True)
    y_p = _moe_dispatched(rows_p, cls_p, x1_p, mod_p, wgu, wd, sp // MOVE_TILE)

    (q_s, kb_s, vb_s, qi_s, kib_s, kiwi_s, u_s, vg_s, ga_s, gb_s, kt_s, vt_s, kit_s, _) = _project(
        x_sample.reshape(bs, D_MODEL), mod_s, tabs_s, consts, w_pad, bs, 1, F32)
    qi8 = jnp.concatenate([qi_s.reshape(bs, IDX_HEADS, IDX_DIM),
                           jnp.zeros((bs, 8 - IDX_HEADS, IDX_DIM), BF16)], axis=1)
    wi8 = jnp.concatenate([kiwi_s[:, IDX_DIM:IDX_DIM + IDX_HEADS],
                           jnp.zeros((bs, 8 - IDX_HEADS), F32)], axis=1).reshape(bs, 8, 1)
    kidx_t = jnp.transpose(cache_kidx[l], (0, 2, 1))
    scores = _sample_scores(page_table, kidx_t, qi8, wi8, kib_s.reshape(bs, 1, IDX_DIM))
    topk_s = min(TOPK_MAX, (past + ss) // 4)
    bias = _sample_select(scores.reshape(bs, past + LANES), topk_s).reshape(bs, 1, past + LANES)
    kvw = N_KV_HEADS * HEAD_DIM
    k_t = jnp.transpose(cache_k[l], (0, 2, 3, 1)).reshape(-1, kvw, PAGE_SIZE)
    v_t = jnp.transpose(cache_v[l], (0, 2, 3, 1)).reshape(-1, kvw, PAGE_SIZE)
    q3 = q_s.reshape(bs, N_HEADS, HEAD_DIM)
    zq = jnp.zeros_like(q3)
    in_first = (jnp.arange(N_HEADS) < N_HEADS // N_KV_HEADS)[None, :, None]
    q2_s = jnp.where(in_first, jnp.concatenate([q3, zq], axis=2), jnp.concatenate([zq, q3], axis=2))
    attn_s = _sample_attention(page_table, k_t, v_t, q2_s, bias,
                               kb_s.reshape(bs, 1, kvw), vb_s.reshape(bs, 1, kvw))
    gd = GM_WIDTH // GM_GROUPS
    w0 = jnp.repeat(gm_spatial_w[l][:, 0, 0], gd).reshape(1, GM_WIDTH)
    b0 = jnp.repeat(gm_spatial_b[l][:, 0], gd).reshape(1, GM_WIDTH)
    gm_s = _gmlp_sample(u_s, vg_s, w0, b0)
    x1_s, h2_s, gate_s = _merge(attn_s.reshape(1, 1, bs, N_HEADS * HEAD_DIM), gm_s, ga_s, gb_s,
                                x_sample.reshape(bs, D_MODEL), mod_s, wpa, wpg, wo, g2, wr, br, bs, 1, False)
    y_s = _moe(h2_s, gate_s, x1_s, mod_s, wgu, wd, bs, 1)

    def rows_kv(a_t, n, s):
        return jnp.transpose(a_t.reshape(n, N_KV_HEADS, HEAD_DIM, s), (0, 3, 1, 2))[None]

    def rows_ki(a_t):
        return jnp.transpose(a_t, (0, 2, 1))[None]

    return (y_p.reshape(bp, sp, D_MODEL), y_s.reshape(bs, ss, D_MODEL),
            rows_kv(kt_f, bp, sp), rows_kv(vt_f, bp, sp), rows_ki(kit_f),
            rows_kv(kt_s, 1, bs).reshape(1, bs, ss, N_KV_HEADS, HEAD_DIM),
            rows_kv(vt_s, 1, bs).reshape(1, bs, ss, N_KV_HEADS, HEAD_DIM),
            rows_ki(kit_s).reshape(1, bs, ss, IDX_DIM), vg_s.reshape(1, bs, ss, GM_WIDTH))
```

```python
import functools

import numpy as np
import jax
import jax.numpy as jnp
from jax import lax
from jax.experimental import pallas as pl
from jax.experimental.pallas import tpu as pltpu

F32 = jnp.float32
BF16 = jnp.bfloat16
U32 = jnp.uint32

D_MODEL = 1024
N_HEADS = 8
HEAD_DIM = 64
N_KV_HEADS = 2
ROT_DIM = 16
ROPE_THETA = 500000.0
IDX_HEADS = 4
IDX_DIM = 64
TOPK_MAX = 256
Q_BLOCK = 128
GM_WIDTH = 512
GM_GROUPS = 8
CHUNK = 128
N_EXPERT_GROUPS = 4
EXPERTS_PER_GROUP = 8
N_EXPERTS = 32
D_EXPERT = 256
EPS = 1e-6
PAGE_SIZE = 128
LANES = 128
SUB = 8

C_Q, C_K, C_V, C_QI, C_KIWI, C_U, C_VG, C_GATE = 0, 512, 640, 768, 1024, 1152, 1664, 2176
D_IN_PAD = 4224
VMEM_LIMIT = 56 * 1024 * 1024

N_PAIRS = EXPERTS_PER_GROUP * (EXPERTS_PER_GROUP - 1) // 2
N_CLASSES = N_EXPERT_GROUPS * N_PAIRS
ROW_TILE = 128
HALF = D_MODEL // 2
FEAT_SUB = HALF // LANES
MOVE_TILE = 256
HI_MASK = 0xFFFF0000


def _cparams(sem):
    return pltpu.CompilerParams(dimension_semantics=sem, vmem_limit_bytes=VMEM_LIMIT)


def _pack_halves(x):
    n = x.shape[1] // 2
    hi = lax.bitcast_convert_type(x[:, 0:n], U32) & jnp.uint32(HI_MASK)
    lo = lax.bitcast_convert_type(x[:, n:2 * n], U32) >> 16
    return hi | lo


def _unpack_halves(u):
    hi = lax.bitcast_convert_type(u & jnp.uint32(HI_MASK), F32)
    lo = lax.bitcast_convert_type(u << 16, F32)
    return jnp.concatenate([hi, lo], axis=1)


def _store_token_tiles(ref, words, extra):
    n = words.shape[0]
    for s in range(FEAT_SUB):
        ref[pl.ds(s, n, stride=SUB), :] = words[:, s * LANES:(s + 1) * LANES]
    zero = jnp.zeros((n, LANES), U32)
    ref[pl.ds(FEAT_SUB, n, stride=SUB), :] = zero if extra is None else extra
    for s in range(FEAT_SUB + 1, SUB):
        ref[pl.ds(s, n, stride=SUB), :] = zero


def _load_token_words(ref, n):
    return jnp.concatenate([ref[pl.ds(s, n, stride=SUB), :] for s in range(FEAT_SUB)], axis=1)


def _adaln_body(c_ref, w_ref, b_ref, o_ref):
    c = c_ref[...]
    a = c * jax.nn.sigmoid(c)
    o_ref[...] = jnp.dot(a, w_ref[...], preferred_element_type=F32,
                         precision=lax.Precision.HIGHEST) + b_ref[...]


def _adaln(c, w, b):
    r = c.shape[0]
    n = w.shape[1]
    bn = 1536
    return pl.pallas_call(
        _adaln_body,
        grid=(n // bn,),
        in_specs=[pl.BlockSpec((r, D_MODEL), lambda j: (0, 0)),
                  pl.BlockSpec((D_MODEL, bn), lambda j: (0, j)),
                  pl.BlockSpec((1, bn), lambda j: (0, j))],
        out_specs=pl.BlockSpec((r, bn), lambda j: (0, j)),
        out_shape=jax.ShapeDtypeStruct((r, n), F32),
        compiler_params=_cparams(("arbitrary",)),
        name="adaln",
    )(c, w, b.reshape(1, n))


def _rope_table_body(pos_ref, invf_ref, sa_m_ref, sb_m_ref, c_ref, sa_ref, sb_ref):
    ang = pos_ref[...] * invf_ref[...]
    s = jnp.sin(ang)
    c_ref[...] = jnp.cos(ang)
    sa_ref[...] = s * sa_m_ref[...]
    sb_ref[...] = s * sb_m_ref[...]


def _rope_tables(pos):
    half = ROT_DIM // 2
    inv_freq = ROPE_THETA ** (-jnp.arange(half, dtype=F32) / half)
    d = np.arange(LANES) % HEAD_DIM
    invf = jnp.where(jnp.asarray(d < ROT_DIM), inv_freq[d % half], 0.0).reshape(1, LANES)
    sa_m = jnp.asarray(np.where(d < half, -1.0, 0.0), F32).reshape(1, LANES)
    sb_m = jnp.asarray(np.where((d >= half) & (d < ROT_DIM), 1.0, 0.0), F32).reshape(1, LANES)
    r = pos.shape[0]
    return pl.pallas_call(
        _rope_table_body,
        out_shape=[jax.ShapeDtypeStruct((r, LANES), F32)] * 3,
        name="rope_tables",
    )(pos, invf, sa_m, sb_m)


def _rope(y, c, sa, sb):
    return y * c + pltpu.roll(y, LANES - ROT_DIM // 2, 1) * sa + pltpu.roll(y, ROT_DIM // 2, 1) * sb


def _seg_rms(r, seg, gain):
    ss = r * r
    hi = ss.astype(BF16)
    lo = (ss - hi.astype(F32)).astype(BF16)
    tot = jnp.dot(hi, seg, preferred_element_type=F32) + jnp.dot(lo, seg, preferred_element_type=F32)
    return r * lax.rsqrt(tot * (1.0 / HEAD_DIM) + EPS) * gain


def _proj_body(mod3d, x_ref, mod_ref, g_ref, w_ref, c_ref, sa_ref, sb_ref, qg_ref, kg_ref, kig_ref,
               lng_ref, lnb_ref, seg_ref, segki_ref,
               q_o, kb_o, vb_o, qi_o, kib_o, kiwi_o, u_o, vg_o, ga_o, gb_o, kt_o, vtf_o, kit_o, vt_o):
    x = x_ref[...]
    m = mod_ref[0] if mod3d else mod_ref[...]
    shift1 = m[:, 0:D_MODEL]
    scale1 = m[:, D_MODEL:2 * D_MODEL]
    ms = jnp.mean(x * x, axis=-1, keepdims=True)
    y = x * lax.rsqrt(ms + EPS) * g_ref[...]
    h = (y * (1.0 + scale1) + shift1).astype(BF16)
    c, sa, sb = c_ref[...], sa_ref[...], sb_ref[...]
    seg = seg_ref[...]

    def proj(a, b):
        return jnp.dot(h, w_ref[:, a:b], preferred_element_type=F32)

    for g2 in range(2):
        r2 = proj(C_Q + g2 * 2 * LANES, C_Q + (g2 + 1) * 2 * LANES)
        for g in (2 * g2, 2 * g2 + 1):
            r = r2[:, (g % 2) * LANES:(g % 2 + 1) * LANES]
            yq = _rope(_seg_rms(r, seg, qg_ref[...]), c, sa, sb) * (HEAD_DIM ** -0.5)
            q_o[:, g * LANES:(g + 1) * LANES] = yq.astype(BF16)
    r_kv = proj(C_K, C_V + LANES)
    yk = _rope(_seg_rms(r_kv[:, 0:LANES], seg, kg_ref[...]), c, sa, sb)
    kt_o[0] = yk.T
    kb_o[...] = yk.astype(BF16)
    r = r_kv[:, LANES:2 * LANES]
    r_t = r.T
    vtf_o[0] = r_t
    vb_o[...] = r.astype(BF16)
    vt_o[0] = r_t.astype(BF16)
    r2 = proj(C_QI, C_QI + 2 * LANES)
    for g in range(2):
        yqi = _rope(r2[:, g * LANES:(g + 1) * LANES], c, sa, sb) * (IDX_DIM ** -0.5 * IDX_HEADS ** -0.5)
        qi_o[:, g * LANES:(g + 1) * LANES] = yqi.astype(BF16)
    r = proj(C_KIWI, C_KIWI + LANES)
    yki = _rope(_seg_rms(r, segki_ref[...], kig_ref[...]), c, sa, sb)
    lane = lax.broadcasted_iota(jnp.int32, r.shape, 1)
    kiwi = jnp.where(lane < IDX_DIM, yki, r)
    kiwi_o[...] = kiwi
    kit_o[0] = kiwi.T[0:IDX_DIM, :]
    kib_o[...] = kiwi[:, 0:IDX_DIM].astype(BF16)
    r = proj(C_U, C_U + GM_WIDTH)
    u_o[...] = jax.nn.gelu(r).astype(BF16)
    r = proj(C_VG, C_VG + GM_WIDTH)
    gl = jax.nn.gelu(r)
    mu = jnp.mean(gl, axis=-1, keepdims=True)
    dv = gl - mu
    var = jnp.mean(dv * dv, axis=-1, keepdims=True)
    vg_o[...] = (dv * lax.rsqrt(var + EPS) * lng_ref[...] + lnb_ref[...]).astype(vg_o.dtype)
    r = proj(C_GATE, C_GATE + D_MODEL)
    ga_o[...] = jax.nn.sigmoid(r).astype(BF16)
    r = proj(C_GATE + D_MODEL, C_GATE + 2 * D_MODEL)
    gb_o[...] = jax.nn.sigmoid(r).astype(BF16)


def _project(x, mod, tables, consts, w_pad, tm, tiles_per_seq, vg_dtype):
    t = x.shape[0]
    mod3d = mod.ndim == 3
    c_t, sa_t, sb_t = tables
    g_mix, qg, kg, kig, lng, lnb, seg, segki = consts
    row = lambda i: (i, 0)
    fixed = lambda i: (0, 0)
    if mod3d:
        mod_spec = pl.BlockSpec((1, 1, 6 * D_MODEL), lambda i: (i // tiles_per_seq, 0, 0))
        tab_spec = pl.BlockSpec((tm, LANES), lambda i: (i % tiles_per_seq, 0))
    else:
        mod_spec = pl.BlockSpec((tm, 6 * D_MODEL), row)
        tab_spec = pl.BlockSpec((1, LANES), fixed)
    widths = [(512, BF16), (128, BF16), (128, BF16), (256, BF16),
              (64, BF16), (128, F32), (512, BF16), (512, vg_dtype), (1024, BF16), (1024, BF16)]
    n_seq, seq = t // (tm * tiles_per_seq), tm * tiles_per_seq
    t_idx = lambda i: (i // tiles_per_seq, 0, i % tiles_per_seq)
    t_outs = [(LANES, F32), (LANES, F32), (IDX_DIM, F32), (LANES, BF16)]
    return pl.pallas_call(
        functools.partial(_proj_body, mod3d),
        grid=(t // tm,),
        in_specs=[pl.BlockSpec((tm, D_MODEL), row), mod_spec,
                  pl.BlockSpec((1, D_MODEL), fixed),
                  pl.BlockSpec((D_MODEL, D_IN_PAD), fixed),
                  tab_spec, tab_spec, tab_spec,
                  pl.BlockSpec((1, LANES), fixed), pl.BlockSpec((1, LANES), fixed), pl.BlockSpec((1, LANES), fixed),
                  pl.BlockSpec((1, GM_WIDTH), fixed), pl.BlockSpec((1, GM_WIDTH), fixed),
                  pl.BlockSpec((LANES, LANES), fixed), pl.BlockSpec((LANES, LANES), fixed)],
        out_specs=[pl.BlockSpec((tm, w), row) for w, _ in widths]
        + [pl.BlockSpec((1, f, tm), t_idx) for f, _ in t_outs],
        out_shape=[jax.ShapeDtypeStruct((t, w), dt) for w, dt in widths]
        + [jax.ShapeDtypeStruct((n_seq, f, seq), dt) for f, dt in t_outs],
        compiler_params=_cparams(("arbitrary",)),
        name="project",
    )(x, mod, g_mix, w_pad, c_t, sa_t, sb_t, qg, kg, kig, lng, lnb, seg, segki)


def _select_bias(s_ref, bias_ref, rows, width, topk, tie_check_start=16):
    nb = width // LANES
    kf = float(topk)
    neg, pos = -jnp.inf, jnp.inf

    def blk(j):
        return s_ref[:, j * LANES:(j + 1) * LANES]

    def count_above(t):
        tb = jnp.broadcast_to(t, (rows, LANES))
        acc = jnp.zeros((rows, LANES), F32)
        for j in range(nb):
            acc = acc + jnp.where(blk(j) > tb, 1.0, 0.0)
        return jnp.sum(acc, axis=1, keepdims=True)

    mx = jnp.full((rows, LANES), neg, F32)
    mn = jnp.full((rows, LANES), pos, F32)
    for j in range(nb):
        b = blk(j)
        mx = jnp.maximum(mx, b)
        mn = jnp.minimum(mn, jnp.where(b == neg, pos, b))
    hi0 = jnp.max(mx, axis=1, keepdims=True)
    smin = jnp.min(mn, axis=1, keepdims=True)
    lo0 = smin - jnp.abs(smin) - 1.0
    f_lo0 = count_above(lo0)
    zeros = jnp.zeros((rows, 1), F32)

    def active_of(f_lo, tie):
        return jnp.logical_and(f_lo > kf, tie == 0.0)

    def cond(st):
        _, _, _, f_lo, _, tie = st
        return jnp.max(jnp.where(active_of(f_lo, tie), 1.0, 0.0)) > 0.0

    def body(st):
        it, lo, hi, f_lo, f_hi, tie = st
        active = active_of(f_lo, tie)
        mid = lo + (hi - lo) * 0.5
        stuck = jnp.logical_or(mid <= lo, mid >= hi)
        cnt = count_above(mid)
        ge = cnt >= kf
        up_lo = jnp.logical_and(active, ge)
        up_hi = jnp.logical_and(active, jnp.logical_not(ge))
        lo = jnp.where(up_lo, mid, lo)
        f_lo = jnp.where(up_lo, cnt, f_lo)
        hi = jnp.where(up_hi, mid, hi)
        f_hi = jnp.where(up_hi, cnt, f_hi)
        tie = jnp.where(jnp.logical_and(active, stuck), 1.0, tie)

        def tie_check(_):
            lob = jnp.broadcast_to(lo, (rows, LANES))
            hib = jnp.broadcast_to(hi, (rows, LANES))
            vmx = jnp.full((rows, LANES), neg, F32)
            vmn = jnp.full((rows, LANES), pos, F32)
            for j in range(nb):
                b = blk(j)
                inn = jnp.logical_and(b > lob, b <= hib)
                vmx = jnp.maximum(vmx, jnp.where(inn, b, neg))
                vmn = jnp.minimum(vmn, jnp.where(inn, b, pos))
            one_value = jnp.max(vmx, axis=1, keepdims=True) == jnp.min(vmn, axis=1, keepdims=True)
            return jnp.where(one_value, 1.0, tie)

        run_check = jnp.logical_and(it >= tie_check_start, it % 4 == 0)
        tie = lax.cond(run_check, tie_check, lambda _: tie, 0)
        return it + 1, lo, hi, f_lo, f_hi, tie

    _, lo, hi, f_lo, f_hi, _ = lax.while_loop(cond, body, (jnp.int32(0), lo0, hi0, f_lo0, zeros, zeros))
    lob = jnp.broadcast_to(lo, (rows, LANES))
    need_prefix = jnp.max(jnp.where(f_lo > kf, 1.0, 0.0)) > 0.0

    @pl.when(jnp.logical_not(need_prefix))
    def _():
        for j in range(nb):
            bias_ref[:, j * LANES:(j + 1) * LANES] = jnp.where(blk(j) > lob, 0.0, neg)

    @pl.when(need_prefix)
    def _():
        hib = jnp.broadcast_to(hi, (rows, LANES))
        need = kf - f_hi
        ri = lax.broadcasted_iota(jnp.int32, (LANES, LANES), 0)
        ci = lax.broadcasted_iota(jnp.int32, (LANES, LANES), 1)
        upper = jnp.where(ri < ci, 1.0, 0.0).astype(BF16)
        off = jnp.zeros((rows, 1), F32)
        for j in range(nb):
            b = blk(j)
            inn = jnp.logical_and(b > lob, b <= hib)
            innf = jnp.where(inn, 1.0, 0.0)
            before = jnp.dot(innf.astype(BF16), upper, preferred_element_type=F32) + off
            sel = jnp.logical_or(b > hib, jnp.logical_and(inn, before < need))
            bias_ref[:, j * LANES:(j + 1) * LANES] = jnp.where(sel, 0.0, neg)
            off = off + jnp.sum(innf, axis=1, keepdims=True)


_NT = (((1,), (1,)), ((), ()))


def _sub_reduce(x, op):
    for sh in (4, 2, 1):
        x = op(x, pltpu.roll(x, sh, 0))
    return x


def _select_bias_t(s_ref, bias_ref, width, topk, n_adm, steps_per_check=4, tie_check_from=4):
    rb = 64
    nb = width // rb
    kf = float(topk)
    neg, pos = -jnp.inf, jnp.inf

    def blk(j):
        return s_ref[j * rb:(j + 1) * rb, :]

    def fold(x, op):
        y = x[0:SUB]
        for a in range(1, rb // SUB):
            y = op(y, x[a * SUB:(a + 1) * SUB])
        return _sub_reduce(y, op)

    def tile(v):
        return jnp.concatenate([v] * (rb // SUB), axis=0)

    def count_above(t):
        tb = tile(t)
        acc = jnp.zeros((rb, LANES), F32)
        for j in range(nb):
            acc = acc + jnp.where(blk(j) > tb, 1.0, 0.0)
        return fold(acc, jnp.add)

    mx = jnp.full((rb, LANES), neg, F32)
    mn = jnp.full((rb, LANES), pos, F32)
    for j in range(nb):
        b = blk(j)
        mx = jnp.maximum(mx, b)
        mn = jnp.minimum(mn, jnp.where(b == neg, pos, b))
    hi0 = fold(mx, jnp.maximum)
    smin = fold(mn, jnp.minimum)
    lo0 = smin - jnp.abs(smin) - 1.0
    zeros = jnp.zeros((SUB, LANES), F32)

    def active_of(f_lo, tie):
        return jnp.logical_and(f_lo > kf, tie == 0.0)

    def any_lane(cond):
        return jnp.max(jnp.where(cond, 1.0, 0.0)) > 0.0

    def step(lo, hi, f_lo, f_hi, tie):
        active = active_of(f_lo, tie)
        mid = lo + (hi - lo) * 0.5
        stuck = jnp.logical_or(mid <= lo, mid >= hi)
        cnt = count_above(mid)
        ge = cnt >= kf
        up_lo = jnp.logical_and(active, ge)
        up_hi = jnp.logical_and(active, jnp.logical_not(ge))
        return (jnp.where(up_lo, mid, lo), jnp.where(up_hi, mid, hi), jnp.where(up_lo, cnt, f_lo),
                jnp.where(up_hi, cnt, f_hi), jnp.where(jnp.logical_and(active, stuck), 1.0, tie))

    def cond(st):
        _, _, _, f_lo, _, tie = st
        return any_lane(active_of(f_lo, tie))

    def body(st):
        it, lo, hi, f_lo, f_hi, tie = st
        for _ in range(steps_per_check):
            lo, hi, f_lo, f_hi, tie = step(lo, hi, f_lo, f_hi, tie)

        def tie_check(_):
            lob, hib = tile(lo), tile(hi)
            vmx = jnp.full((rb, LANES), neg, F32)
            vmn = jnp.full((rb, LANES), pos, F32)
            for j in range(nb):
                b = blk(j)
                inn = jnp.logical_and(b > lob, b <= hib)
                vmx = jnp.maximum(vmx, jnp.where(inn, b, neg))
                vmn = jnp.minimum(vmn, jnp.where(inn, b, pos))
            return jnp.where(fold(vmx, jnp.maximum) == fold(vmn, jnp.minimum), 1.0, tie)

        run_check = jnp.logical_and(it + 1 >= tie_check_from, any_lane(active_of(f_lo, tie)))
        tie = lax.cond(run_check, tie_check, lambda _: tie, 0)
        return it + 1, lo, hi, f_lo, f_hi, tie

    _, lo, hi, f_lo, f_hi, _ = lax.while_loop(cond, body, (jnp.int32(0), lo0, hi0, n_adm, zeros, zeros))
    lob = tile(lo)
    need_prefix = any_lane(f_lo > kf)

    @pl.when(jnp.logical_not(need_prefix))
    def _():
        for j in range(nb):
            bias_ref[j * rb:(j + 1) * rb, :] = jnp.where(blk(j) > lob, 0.0, neg)

    @pl.when(need_prefix)
    def _():
        need = (kf - f_hi)[0:1]
        lo1, hi1 = lo[0:1], hi[0:1]
        ri = lax.broadcasted_iota(jnp.int32, (LANES, LANES), 0)
        ci = lax.broadcasted_iota(jnp.int32, (LANES, LANES), 1)
        lower = jnp.where(ci < ri, 1.0, 0.0).astype(BF16)
        off = jnp.zeros((1, LANES), F32)
        for j in range(width // LANES):
            b = s_ref[j * LANES:(j + 1) * LANES, :]
            inn = jnp.logical_and(b > lo1, b <= hi1)
            innf = jnp.where(inn, 1.0, 0.0)
            before = jnp.dot(lower, innf.astype(BF16), preferred_element_type=F32) + off
            sel = jnp.logical_or(b > hi1, jnp.logical_and(inn, before < need))
            bias_ref[j * LANES:(j + 1) * LANES, :] = jnp.where(sel, 0.0, neg)
            off = off + jnp.sum(innf, axis=0, keepdims=True)


def _prompt_attn_body(blk_i, topk, qi_ref, kiwi_ref, kib_ref, q_ref, k_ref, vt_ref, o_ref, s_ref, bias_ref):
    width = (blk_i + 1) * Q_BLOCK
    neg = -jnp.inf
    kiwi_t = kiwi_ref[0].T
    chunk = 512
    for c0 in range(0, width, chunk):
        c1 = min(width, c0 + chunk)
        kib = kib_ref[0, c0:c1, :]
        sc = None
        for h in range(IDX_HEADS):
            d = lax.dot_general(kib, qi_ref[0, :, h * IDX_DIM:(h + 1) * IDX_DIM], _NT,
                                preferred_element_type=F32)
            t = jnp.maximum(d, 0.0) * kiwi_t[IDX_DIM + h:IDX_DIM + h + 1, :]
            sc = t if sc is None else sc + t
        s_ref[c0:c1, :] = sc
    ki = lax.broadcasted_iota(jnp.int32, (Q_BLOCK, Q_BLOCK), 0)
    qj = lax.broadcasted_iota(jnp.int32, (Q_BLOCK, Q_BLOCK), 1)
    d0 = width - Q_BLOCK
    s_ref[d0:width, :] = jnp.where(ki <= qj, s_ref[d0:width, :], neg)
    if width > topk:
        n_adm = (lax.broadcasted_iota(jnp.int32, (SUB, LANES), 1) + (d0 + 1)).astype(F32)
        _select_bias_t(s_ref, bias_ref, width, topk, n_adm)
    else:
        bias_ref[...] = jnp.where(s_ref[...] == neg, neg, 0.0)
    lane = lax.broadcasted_iota(jnp.int32, (Q_BLOCK, LANES), 1)
    hpg = N_HEADS // N_KV_HEADS
    for h in range(N_HEADS):
        g = h // hpg
        qp = q_ref[0, :, (h // 2) * LANES:(h // 2 + 1) * LANES].astype(F32)
        if h % 2 != g:
            qp = pltpu.roll(qp, HEAD_DIM, 1)
        q2 = jnp.where((lane >= HEAD_DIM) == (g == 1), qp, 0.0).astype(BF16)
        s = lax.dot_general(k_ref[0], q2, _NT, preferred_element_type=F32) + bias_ref[...]
        m = jnp.max(s, axis=0, keepdims=True)
        p = jnp.exp(s - m)
        l = jnp.sum(p, axis=0, keepdims=True)
        o_t = jnp.dot(vt_ref[0, g * HEAD_DIM:(g + 1) * HEAD_DIM, :], p.astype(BF16), preferred_element_type=F32)
        o_ref[0, :, h * HEAD_DIM:(h + 1) * HEAD_DIM] = (o_t / l).T.astype(BF16)


def _prompt_attention(qi, kiwi, kib, q, kb, vt, topk):
    b, s, _ = q.shape
    outs = []
    for i in range(s // Q_BLOCK):
        width = (i + 1) * Q_BLOCK
        qblk = lambda bb, i=i: (bb, i, 0)
        kall = lambda bb: (bb, 0, 0)
        outs.append(pl.pallas_call(
            functools.partial(_prompt_attn_body, i, topk),
            grid=(b,),
            in_specs=[pl.BlockSpec((1, Q_BLOCK, IDX_HEADS * IDX_DIM), qblk),
                      pl.BlockSpec((1, Q_BLOCK, LANES), qblk),
                      pl.BlockSpec((1, width, IDX_DIM), kall),
                      pl.BlockSpec((1, Q_BLOCK, N_HEADS * HEAD_DIM), qblk),
                      pl.BlockSpec((1, width, N_KV_HEADS * HEAD_DIM), kall),
                      pl.BlockSpec((1, N_KV_HEADS * HEAD_DIM, width), kall)],
            out_specs=pl.BlockSpec((1, Q_BLOCK, N_HEADS * HEAD_DIM), lambda bb: (bb, 0, 0)),
            out_shape=jax.ShapeDtypeStruct((b, Q_BLOCK, N_HEADS * HEAD_DIM), BF16),
            scratch_shapes=[pltpu.VMEM((width, Q_BLOCK), F32), pltpu.VMEM((width, Q_BLOCK), F32)],
            compiler_params=_cparams(("arbitrary",)),
            name=f"prompt_attn_{i}",
        )(qi, kiwi, kib, q, kb, vt))
    return jnp.stack(outs)


def _page_copies(pt_ref, sample, src_hbm, buf, slot, sem, n_pages):
    return [pltpu.make_async_copy(src_hbm.at[pt_ref[sample, p]],
                                  buf.at[slot, :, pl.ds(p * PAGE_SIZE, PAGE_SIZE)], sem.at[slot])
            for p in range(n_pages)]


def _sample_scores_body(n_pages, pt_ref, kidx_hbm, qi_ref, wi_ref, kin_ref, o_ref, buf, sem):
    s = pl.program_id(0)
    slot = s % 2

    @pl.when(s == 0)
    def _():
        _start_all(_page_copies(pt_ref, 0, kidx_hbm, buf, 0, sem, n_pages))

    @pl.when(s + 1 < pl.num_programs(0))
    def _():
        _start_all(_page_copies(pt_ref, s + 1, kidx_hbm, buf, 1 - slot, sem, n_pages))

    for cp in _page_copies(pt_ref, s, kidx_hbm, buf, slot, sem, n_pages):
        cp.wait()
    qi = qi_ref[0]
    wi = wi_ref[0]
    ki_t = buf[slot].astype(BF16)
    d = jnp.dot(qi, ki_t, preferred_element_type=F32)
    past = jnp.sum(jnp.maximum(d, 0.0) * wi, axis=0, keepdims=True)
    dn = jnp.sum(qi.astype(F32) * kin_ref[0].astype(F32), axis=1, keepdims=True)
    new = jnp.sum(jnp.maximum(dn, 0.0) * wi, axis=0, keepdims=True)
    lane = lax.broadcasted_iota(jnp.int32, (1, LANES), 1)
    tail = jnp.where(lane == 0, jnp.broadcast_to(new, (1, LANES)), -jnp.inf)
    o_ref[0] = jnp.concatenate([past, tail], axis=1)


def _sample_scores(page_table, kidx_pool, qi8, wi8, ki_new):
    n, n_pages = page_table.shape
    past = n_pages * PAGE_SIZE
    grid_spec = pltpu.PrefetchScalarGridSpec(
        num_scalar_prefetch=1,
        grid=(n,),
        in_specs=[pl.BlockSpec(memory_space=pl.ANY),
                  pl.BlockSpec((1, 8, IDX_DIM), lambda s, pt: (s, 0, 0)),
                  pl.BlockSpec((1, 8, 1), lambda s, pt: (s, 0, 0)),
                  pl.BlockSpec((1, 1, IDX_DIM), lambda s, pt: (s, 0, 0))],
        out_specs=pl.BlockSpec((1, 1, past + LANES), lambda s, pt: (s, 0, 0)),
        scratch_shapes=[pltpu.VMEM((2, IDX_DIM, past), F32), pltpu.SemaphoreType.DMA((2,))],
    )
    return pl.pallas_call(
        functools.partial(_sample_scores_body, n_pages),
        grid_spec=grid_spec,
        out_shape=jax.ShapeDtypeStruct((n, 1, past + LANES), F32),
        compiler_params=_cparams(("arbitrary",)),
        name="sample_scores",
    )(page_table, kidx_pool, qi8, wi8, ki_new)


def _sample_select_body(topk, s_ref, bias_ref):
    rows, width = s_ref.shape
    _select_bias(s_ref, bias_ref, rows, width, topk)


def _sample_select(scores, topk):
    return pl.pallas_call(
        functools.partial(_sample_select_body, topk),
        out_shape=jax.ShapeDtypeStruct(scores.shape, F32),
        compiler_params=pltpu.CompilerParams(vmem_limit_bytes=VMEM_LIMIT),
        name="sample_select",
    )(scores)


def _sample_attn_body(n_pages, pt_ref, k_hbm, v_hbm, q_ref, bias_ref, kn_ref, vn_ref, o_ref, kbuf, vbuf, sem):
    s = pl.program_id(0)
    slot = s % 2
    past = n_pages * PAGE_SIZE

    def copies(sample, sl):
        return (_page_copies(pt_ref, sample, k_hbm, kbuf, sl, sem.at[0], n_pages)
                + _page_copies(pt_ref, sample, v_hbm, vbuf, sl, sem.at[1], n_pages))

    @pl.when(s == 0)
    def _():
        for cp in copies(0, 0):
            cp.start()

    @pl.when(s + 1 < pl.num_programs(0))
    def _():
        for cp in copies(s + 1, 1 - slot):
            cp.start()

    for cp in copies(s, slot):
        cp.wait()
    q2 = q_ref[0]
    k_t = kbuf[slot].astype(BF16)
    v_t = vbuf[slot].astype(BF16)
    row = lax.broadcasted_iota(jnp.int32, (N_HEADS, 1), 0)
    first = row < (N_HEADS // N_KV_HEADS)
    bias = bias_ref[0]
    sc = jnp.dot(q2, k_t, preferred_element_type=F32) + bias[:, 0:past]
    sn = jnp.sum(q2.astype(F32) * kn_ref[0].astype(F32), axis=1, keepdims=True) + bias[:, past:past + 1]
    m = jnp.maximum(jnp.max(sc, axis=1, keepdims=True), sn)
    p = jnp.exp(sc - m)
    pn = jnp.exp(sn - m)
    l = jnp.sum(p, axis=1, keepdims=True) + pn
    o2 = lax.dot_general(p.astype(BF16), v_t, _NT, preferred_element_type=F32)
    o2 = o2 + pn.astype(BF16).astype(F32) * vn_ref[0].astype(F32)
    o = jnp.where(first, o2[:, 0:HEAD_DIM], o2[:, HEAD_DIM:2 * HEAD_DIM])
    o_ref[0] = (o / l).astype(BF16)


def _sample_attention(page_table, k_pool, v_pool, q8, bias, k_new, v_new):
    n, n_pages = page_table.shape
    past = n_pages * PAGE_SIZE
    kvw = N_KV_HEADS * HEAD_DIM
    per = lambda s, pt: (s, 0, 0)
    grid_spec = pltpu.PrefetchScalarGridSpec(
        num_scalar_prefetch=1,
        grid=(n,),
        in_specs=[pl.BlockSpec(memory_space=pl.ANY), pl.BlockSpec(memory_space=pl.ANY),
                  pl.BlockSpec((1, N_HEADS, kvw), per),
                  pl.BlockSpec((1, 1, past + LANES), per),
                  pl.BlockSpec((1, 1, kvw), per),
                  pl.BlockSpec((1, 1, kvw), per)],
        out_specs=pl.BlockSpec((1, N_HEADS, HEAD_DIM), per),
        scratch_shapes=[pltpu.VMEM((2, kvw, past), F32), pltpu.VMEM((2, kvw, past), F32),
                        pltpu.SemaphoreType.DMA((2, 2))],
    )
    return pl.pallas_call(
        functools.partial(_sample_attn_body, n_pages),
        grid_spec=grid_spec,
        out_shape=jax.ShapeDtypeStruct((n, N_HEADS, HEAD_DIM), BF16),
        compiler_params=_cparams(("arbitrary",)),
        name="sample_attn",
    )(page_table, k_pool, v_pool, q8, bias, k_new, v_new)


def _gmlp_body(n_chunks, u_ref, vg_ref, w_ref, bt_ref, o_ref):
    ri = lax.broadcasted_iota(jnp.int32, (CHUNK, CHUNK), 0)
    ci = lax.broadcasted_iota(jnp.int32, (CHUNK, CHUNK), 1)
    gd = GM_WIDTH // GM_GROUPS
    for g in range(GM_GROUPS):
        wg = jnp.where(ci <= ri, w_ref[g], 0.0).astype(BF16)
        bg = bt_ref[:, g:g + 1]
        for c in range(n_chunks):
            rows = slice(c * CHUNK, (c + 1) * CHUNK)
            cols = slice(g * gd, (g + 1) * gd)
            mixed = jnp.dot(wg, vg_ref[rows, cols], preferred_element_type=F32) + bg
            o_ref[rows, cols] = (u_ref[rows, cols].astype(F32) * mixed).astype(BF16)


def _gmlp_prompt(u, vg, w, bt, tm):
    t = u.shape[0]
    row = lambda i: (i, 0)
    return pl.pallas_call(
        functools.partial(_gmlp_body, tm // CHUNK),
        grid=(t // tm,),
        in_specs=[pl.BlockSpec((tm, GM_WIDTH), row), pl.BlockSpec((tm, GM_WIDTH), row),
                  pl.BlockSpec((GM_GROUPS, CHUNK, CHUNK), lambda i: (0, 0, 0)),
                  pl.BlockSpec((CHUNK, LANES), lambda i: (0, 0))],
        out_specs=pl.BlockSpec((tm, GM_WIDTH), row),
        out_shape=jax.ShapeDtypeStruct((t, GM_WIDTH), BF16),
        compiler_params=_cparams(("arbitrary",)),
        name="gmlp",
    )(u, vg, w, bt)


def _gmlp_first_row_body(u_ref, vg_ref, w0_ref, b0_ref, o_ref):
    o_ref[...] = (u_ref[...].astype(F32) * (vg_ref[...] * w0_ref[...] + b0_ref[...])).astype(BF16)


def _gmlp_sample(u, vg, w0, b0):
    return pl.pallas_call(
        _gmlp_first_row_body,
        out_shape=jax.ShapeDtypeStruct(u.shape, BF16),
        name="gmlp_first_row",
    )(u, vg, w0, b0)


def _merge_body(mod3d, dispatch, attn_ref, gm_ref, ga_ref, gb_ref, x_ref, mod_ref, wpa_ref, wpg_ref, wo_ref,
                g2_ref, wr_ref, br_ref, x1_o, a_o, b_o):
    tm = x_ref.shape[0]
    attn = attn_ref[...].reshape(tm, N_HEADS * HEAD_DIM)
    a = jnp.dot(attn, wpa_ref[...], preferred_element_type=F32)
    g = jnp.dot(gm_ref[...], wpg_ref[...], preferred_element_type=F32)
    merged = ga_ref[...].astype(F32) * a + gb_ref[...].astype(F32) * g
    out = jnp.dot(merged.astype(BF16), wo_ref[...], preferred_element_type=F32)
    m = mod_ref[0] if mod3d else mod_ref[...]
    x1 = x_ref[...] + m[:, 2 * D_MODEL:3 * D_MODEL] * out
    x1_o[...] = x1
    ms = jnp.mean(x1 * x1, axis=-1, keepdims=True)
    y = x1 * lax.rsqrt(ms + EPS) * g2_ref[...]
    h2 = y * (1.0 + m[:, 4 * D_MODEL:5 * D_MODEL]) + m[:, 3 * D_MODEL:4 * D_MODEL]
    hi = h2.astype(BF16)
    lo = (h2 - hi.astype(F32)).astype(BF16)
    r = jnp.dot(hi, wr_ref[...], preferred_element_type=F32) + jnp.dot(lo, wr_ref[...], preferred_element_type=F32)
    logits = r[:, 0:LANES] + r[:, LANES:2 * LANES] + br_ref[...]
    neg = -jnp.inf
    big = jnp.int32(1 << 20)
    lane = lax.broadcasted_iota(jnp.int32, logits.shape, 1)
    is_g = jnp.logical_and(lane >= N_EXPERTS, lane < N_EXPERTS + N_EXPERT_GROUPS)
    gl = jnp.where(is_g, logits, neg)
    gmax = jnp.max(gl, axis=1, keepdims=True)
    g_lane = jnp.min(jnp.where(gl == gmax, lane, big), axis=1, keepdims=True)
    g_w = 1.0 / jnp.sum(jnp.exp(gl - gmax), axis=1, keepdims=True)
    g_sel = g_lane - N_EXPERTS
    in_grp = jnp.logical_and(lane < N_EXPERTS, (lane >> 3) == g_sel)
    el = jnp.where(in_grp, logits, neg)
    m1 = jnp.max(el, axis=1, keepdims=True)
    i1 = jnp.min(jnp.where(el == m1, lane, big), axis=1, keepdims=True)
    el2 = jnp.where(lane == i1, neg, el)
    m2 = jnp.max(el2, axis=1, keepdims=True)
    i2 = jnp.min(jnp.where(el2 == m2, lane, big), axis=1, keepdims=True)
    e2 = jnp.exp(m2 - m1)
    w1 = g_w / (1.0 + e2)
    w2 = g_w * e2 / (1.0 + e2)
    if not dispatch:
        a_o[...] = hi
        b_o[...] = jnp.where(lane == i1, w1, 0.0) + jnp.where(lane == i2, w2, 0.0)
        return
    low_first = i1 < i2
    ea = jnp.where(low_first, i1, i2) - g_sel * EXPERTS_PER_GROUP
    eb = jnp.where(low_first, i2, i1) - g_sel * EXPERTS_PER_GROUP
    cls = g_sel * N_PAIRS + ((ea * (2 * EXPERTS_PER_GROUP - 1 - ea)) >> 1) + (eb - ea - 1)
    b_o[...] = jnp.broadcast_to(cls, (tm, LANES))
    wa = jnp.where(low_first, w1, w2)
    wb = jnp.where(low_first, w2, w1)
    _store_token_tiles(a_o, _pack_halves(hi.astype(F32)), lax.bitcast_convert_type(
        jnp.where(lane == 0, wa, jnp.where(lane == 1, wb, 0.0)), U32))


def _merge(attn4, gm, ga, gb, x, mod, wpa, wpg, wo, g2, wr, br, tm, tiles_per_seq, dispatch):
    t = x.shape[0]
    mod3d = mod.ndim == 3
    nb = tm // Q_BLOCK
    row = lambda i: (i, 0)
    fixed = lambda i: (0, 0)
    if mod3d:
        mod_spec = pl.BlockSpec((1, 1, 6 * D_MODEL), lambda i: (i // tiles_per_seq, 0, 0))
    else:
        mod_spec = pl.BlockSpec((tm, 6 * D_MODEL), row)
    attn_spec = pl.BlockSpec((nb, 1, Q_BLOCK, N_HEADS * HEAD_DIM),
                             lambda i: (i % tiles_per_seq, i // tiles_per_seq, 0, 0))
    if dispatch:
        extra = [(SUB, LANES, U32), (1, LANES, jnp.int32)]
    else:
        extra = [(1, D_MODEL, BF16), (1, LANES, F32)]
    return pl.pallas_call(
        functools.partial(_merge_body, mod3d, dispatch),
        grid=(t // tm,),
        in_specs=[attn_spec, pl.BlockSpec((tm, GM_WIDTH), row),
                  pl.BlockSpec((tm, D_MODEL), row), pl.BlockSpec((tm, D_MODEL), row),
                  pl.BlockSpec((tm, D_MODEL), row), mod_spec,
                  pl.BlockSpec((N_HEADS * HEAD_DIM, D_MODEL), fixed), pl.BlockSpec((GM_WIDTH, D_MODEL), fixed),
                  pl.BlockSpec((D_MODEL, D_MODEL), fixed), pl.BlockSpec((1, D_MODEL), fixed),
                  pl.BlockSpec((D_MODEL, 2 * LANES), fixed), pl.BlockSpec((1, LANES), fixed)],
        out_specs=[pl.BlockSpec((tm, D_MODEL), row)] + [pl.BlockSpec((tm * r, w), row) for r, w, _ in extra],
        out_shape=[jax.ShapeDtypeStruct((t, D_MODEL), F32)]
        + [jax.ShapeDtypeStruct((t * r, w), dt) for r, w, dt in extra],
        compiler_params=_cparams(("arbitrary",)),
        name="merge",
    )(attn4, gm, ga, gb, x, mod, wpa, wpg, wo, g2, wr, br)


def _moe_body(mod3d, h_ref, gate_ref, x1_ref, mod_ref, wgu_ref, wd_ref, o_ref, acc_ref):
    e = pl.program_id(1)

    @pl.when(e == 0)
    def _():
        acc_ref[...] = jnp.zeros_like(acc_ref)

    gu = jnp.dot(h_ref[...], wgu_ref[0], preferred_element_type=F32)
    a = gu[:, 0:D_EXPERT]
    hid = a * jax.nn.sigmoid(a) * gu[:, D_EXPERT:2 * D_EXPERT]
    gate = gate_ref[...]
    lane = lax.broadcasted_iota(jnp.int32, gate.shape, 1)
    ge = jnp.sum(jnp.where(lane == e, gate, 0.0), axis=1, keepdims=True)
    acc_ref[...] += jnp.dot((hid * ge).astype(BF16), wd_ref[0], preferred_element_type=F32)

    @pl.when(e == pl.num_programs(1) - 1)
    def _():
        m = mod_ref[0] if mod3d else mod_ref[...]
        o_ref[...] = x1_ref[...] + m[:, 5 * D_MODEL:6 * D_MODEL] * acc_ref[...]


def _moe(h2, gate, x1, mod, wgu, wd, tm, tiles_per_seq):
    t = h2.shape[0]
    mod3d = mod.ndim == 3
    row = lambda i, e: (i, 0)
    if mod3d:
        mod_spec = pl.BlockSpec((1, 1, 6 * D_MODEL), lambda i, e: (i // tiles_per_seq, 0, 0))
    else:
        mod_spec = pl.BlockSpec((tm, 6 * D_MODEL), row)
    return pl.pallas_call(
        functools.partial(_moe_body, mod3d),
        grid=(t // tm, N_EXPERTS),
        in_specs=[pl.BlockSpec((tm, D_MODEL), row), pl.BlockSpec((tm, LANES), row),
                  pl.BlockSpec((tm, D_MODEL), row), mod_spec,
                  pl.BlockSpec((1, D_MODEL, 2 * D_EXPERT), lambda i, e: (e, 0, 0)),
                  pl.BlockSpec((1, D_EXPERT, D_MODEL), lambda i, e: (e, 0, 0))],
        out_specs=pl.BlockSpec((tm, D_MODEL), row),
        out_shape=jax.ShapeDtypeStruct((t, D_MODEL), F32),
        scratch_shapes=[pltpu.VMEM((tm, D_MODEL), F32)],
        compiler_params=_cparams(("arbitrary", "arbitrary")),
        name="moe",
    )(h2, gate, x1, mod, wgu, wd)


def _slots_body(cls_ref, slot_o, segend_o, carry, seg_start):
    sweep = pl.program_id(0)
    i = pl.program_id(1)
    tm = cls_ref.shape[0]
    lane = lax.broadcasted_iota(jnp.int32, (tm, LANES), 1)
    hit = lane == cls_ref[...]
    onehot = jnp.where(hit, 1.0, 0.0)

    @pl.when(jnp.logical_and(sweep == 0, i == 0))
    def _():
        carry[...] = jnp.zeros_like(carry)

    @pl.when(sweep == 0)
    def _():
        carry[...] = carry[...] + jnp.sum(onehot, axis=0, keepdims=True)
        slot_o[...] = jnp.zeros_like(slot_o)

        @pl.when(i == pl.num_programs(1) - 1)
        def _():
            padded = jnp.floor((carry[...] + (ROW_TILE - 1)) * (1.0 / ROW_TILE)) * ROW_TILE
            ri = lax.broadcasted_iota(jnp.int32, (LANES, LANES), 0)
            ci = lax.broadcasted_iota(jnp.int32, (LANES, LANES), 1)
            upto = jnp.where(ri <= ci, 1.0, 0.0)
            seg_end = jnp.dot(padded, upto, preferred_element_type=F32, precision=lax.Precision.HIGHEST)
            segend_o[...] = seg_end
            seg_start[...] = seg_end - padded
            carry[...] = jnp.zeros_like(carry)

    @pl.when(sweep == 1)
    def _():
        ri = lax.broadcasted_iota(jnp.int32, (tm, tm), 0)
        ci = lax.broadcasted_iota(jnp.int32, (tm, tm), 1)
        earlier = jnp.where(ci < ri, 1.0, 0.0).astype(BF16)
        before = (jnp.dot(earlier, onehot.astype(BF16), preferred_element_type=F32)
                  + carry[0:1, :] + seg_start[0:1, :])
        slot = jnp.sum(jnp.where(hit, before, 0.0), axis=1, keepdims=True)
        slot_o[...] = jnp.broadcast_to(slot, (tm, LANES)).astype(jnp.int32)
        carry[...] = carry[...] + jnp.sum(onehot, axis=0, keepdims=True)


def _class_slots(cls, tm):
    t = cls.shape[0]
    return pl.pallas_call(
        _slots_body,
        grid=(2, t // tm),
        in_specs=[pl.BlockSpec((tm, LANES), lambda s, i: (i, 0))],
        out_specs=[pl.BlockSpec((tm, LANES), lambda s, i: (i * s, 0)),
                   pl.BlockSpec((SUB, LANES), lambda s, i: (0, 0))],
        out_shape=[jax.ShapeDtypeStruct((t, LANES), jnp.int32), jax.ShapeDtypeStruct((SUB, LANES), F32)],
        scratch_shapes=[pltpu.VMEM((SUB, LANES), F32), pltpu.VMEM((SUB, LANES), F32)],
        compiler_params=_cparams(("arbitrary", "arbitrary")),
        name="class_slots",
    )(cls)


INVERT_CHUNK = 4096


def _invert_body(dest_ref, src_o):
    i = pl.program_id(0)

    @pl.when(i == 0)
    def _():
        src_o[...] = jnp.zeros_like(src_o)

    def put(r, carry):
        src_o[pl.ds(dest_ref[0, 0, r], 1), :] = jnp.full((1, LANES), i * INVERT_CHUNK + r, jnp.int32)
        return carry

    lax.fori_loop(0, INVERT_CHUNK, put, 0, unroll=8)


def _invert(dest, n_sorted):
    t = dest.shape[0]
    return pl.pallas_call(
        _invert_body,
        grid=(t // INVERT_CHUNK,),
        in_specs=[pl.BlockSpec((1, 1, INVERT_CHUNK), lambda i: (i, 0, 0), memory_space=pltpu.SMEM)],
        out_specs=pl.BlockSpec((n_sorted, LANES), lambda i: (0, 0)),
        out_shape=jax.ShapeDtypeStruct((n_sorted, LANES), jnp.int32),
        compiler_params=_cparams(("arbitrary",)),
        name="invert_slots",
    )(dest.reshape(t // INVERT_CHUNK, 1, INVERT_CHUNK))


def _token_fetches(idx_ref, tiles_hbm, buf, slot, sem, n):
    return [pltpu.make_async_copy(tiles_hbm.at[pl.ds(pl.multiple_of(idx_ref[0, 0, r] * SUB, SUB), SUB)],
                                  buf.at[slot, pl.ds(r * SUB, SUB)], sem.at[slot])
            for r in range(n)]


def _start_all(copies):
    for r, cp in enumerate(copies):
        cp.start(priority=r % 2)


def _tile_fetches(src_ref, rows_hbm, buf, slot, sem):
    return _token_fetches(src_ref, rows_hbm, buf, slot, sem, ROW_TILE)


FETCH_AHEAD = 2


def _experts_body(grp_ref, ea_ref, eb_ref, used_ref, s0_ref, s1_ref, s2_ref, rows_hbm, wgu_ref, wd_ref,
                  o_ref, buf, sem):
    j = pl.program_id(0)
    n_buf = FETCH_AHEAD + 1
    slot = j % n_buf
    n_used = used_ref[0]

    @pl.when(j == 0)
    def _():
        _start_all(_tile_fetches(s0_ref, rows_hbm, buf, 0, sem))

    @pl.when(jnp.logical_and(j == 0, 1 < n_used))
    def _():
        _start_all(_tile_fetches(s1_ref, rows_hbm, buf, 1, sem))

    @pl.when(j + FETCH_AHEAD < n_used)
    def _():
        _start_all(_tile_fetches(s2_ref, rows_hbm, buf, (j + FETCH_AHEAD) % n_buf, sem))

    @pl.when(jnp.logical_or(j < n_used, j == 0))
    def _():
        for cp in _tile_fetches(s0_ref, rows_hbm, buf, slot, sem):
            cp.wait()

    @pl.when(j < n_used)
    def _():
        tiles = buf.at[slot]
        x = _unpack_halves(_load_token_words(tiles, ROW_TILE)).astype(BF16)
        wts = lax.bitcast_convert_type(tiles[pl.ds(FEAT_SUB, ROW_TILE, stride=SUB), :], F32)

        def hidden(e, wgt):
            gu = jnp.dot(x, wgu_ref[e], preferred_element_type=F32)
            a = gu[:, 0:D_EXPERT]
            return (a * jax.nn.sigmoid(a) * gu[:, D_EXPERT:2 * D_EXPERT] * wgt).astype(BF16)

        ea, eb = ea_ref[j], eb_ref[j]
        y = (jnp.dot(hidden(ea, wts[:, 0:1]), wd_ref[ea], preferred_element_type=F32)
             + jnp.dot(hidden(eb, wts[:, 1:2]), wd_ref[eb], preferred_element_type=F32))
        _store_token_tiles(o_ref, _pack_halves(y.astype(BF16).astype(F32)), None)

    @pl.when(j >= n_used)
    def _():
        o_ref[...] = jnp.zeros_like(o_ref)


def _experts(grp_t, ea_t, eb_t, n_used, src3, rows, wgu, wd):
    n_tiles = src3.shape[0]
    w_grp = lambda j, grp, ea, eb, nu: (grp[j], 0, 0)
    smem_blk = lambda f: pl.BlockSpec((1, 1, ROW_TILE), f, memory_space=pltpu.SMEM)
    grid_spec = pltpu.PrefetchScalarGridSpec(
        num_scalar_prefetch=4,
        grid=(n_tiles,),
        in_specs=[smem_blk(lambda j, grp, ea, eb, nu: (j, 0, 0)),
                  smem_blk(lambda j, grp, ea, eb, nu: (jnp.minimum(j + 1, n_tiles - 1), 0, 0)),
                  smem_blk(lambda j, grp, ea, eb, nu: (jnp.minimum(j + FETCH_AHEAD, n_tiles - 1), 0, 0)),
                  pl.BlockSpec(memory_space=pl.ANY),
                  pl.BlockSpec((EXPERTS_PER_GROUP, D_MODEL, 2 * D_EXPERT), w_grp),
                  pl.BlockSpec((EXPERTS_PER_GROUP, D_EXPERT, D_MODEL), w_grp)],
        out_specs=pl.BlockSpec((ROW_TILE * SUB, LANES), lambda j, grp, ea, eb, nu: (j, 0)),
        scratch_shapes=[pltpu.VMEM((FETCH_AHEAD + 1, ROW_TILE * SUB, LANES), U32),
                        pltpu.SemaphoreType.DMA((FETCH_AHEAD + 1,))],
    )
    return pl.pallas_call(
        _experts_body,
        grid_spec=grid_spec,
        out_shape=jax.ShapeDtypeStruct((n_tiles * ROW_TILE * SUB, LANES), U32),
        compiler_params=_cparams(("arbitrary",)),
        name="experts",
    )(grp_t, ea_t, eb_t, n_used, src3, src3, src3, rows, wgu, wd)


def _row_fetches(dest_ref, ys_hbm, buf, slot, sem):
    return _token_fetches(dest_ref, ys_hbm, buf, slot, sem, MOVE_TILE)


def _combine_body(mod3d, dcur_ref, dnext_ref, ys_hbm, x1_ref, mod_ref, o_ref, buf, sem):
    i = pl.program_id(0)
    slot = i % 2

    @pl.when(i == 0)
    def _():
        _start_all(_row_fetches(dcur_ref, ys_hbm, buf, 0, sem))

    @pl.when(i + 1 < pl.num_programs(0))
    def _():
        _start_all(_row_fetches(dnext_ref, ys_hbm, buf, 1 - slot, sem))

    for cp in _row_fetches(dcur_ref, ys_hbm, buf, slot, sem):
        cp.wait()
    m = mod_ref[0] if mod3d else mod_ref[...]
    o_ref[...] = x1_ref[...] + m[:, 5 * D_MODEL:6 * D_MODEL] * _unpack_halves(_load_token_words(buf.at[slot], MOVE_TILE))


def _combine(dest3, ys, x1, mod, tiles_per_seq):
    steps = dest3.shape[0]
    t = x1.shape[0]
    mod3d = mod.ndim == 3
    row = lambda i: (i, 0)
    if mod3d:
        mod_spec = pl.BlockSpec((1, 1, 6 * D_MODEL), lambda i: (i // tiles_per_seq, 0, 0))
    else:
        mod_spec = pl.BlockSpec((MOVE_TILE, 6 * D_MODEL), row)
    smem_blk = lambda f: pl.BlockSpec((1, 1, MOVE_TILE), f, memory_space=pltpu.SMEM)
    return pl.pallas_call(
        functools.partial(_combine_body, mod3d),
        grid=(steps,),
        in_specs=[smem_blk(lambda i: (i, 0, 0)), smem_blk(lambda i: (jnp.minimum(i + 1, steps - 1), 0, 0)),
                  pl.BlockSpec(memory_space=pl.ANY),
                  pl.BlockSpec((MOVE_TILE, D_MODEL), row), mod_spec],
        out_specs=pl.BlockSpec((MOVE_TILE, D_MODEL), row),
        out_shape=jax.ShapeDtypeStruct((t, D_MODEL), F32),
        scratch_shapes=[pltpu.VMEM((2, MOVE_TILE * SUB, LANES), U32), pltpu.SemaphoreType.DMA((2,))],
        compiler_params=_cparams(("arbitrary",)),
        name="combine",
    )(dest3, dest3, ys, x1, mod)


def _class_expert_tables():
    ea, eb = [], []
    for g in range(N_EXPERT_GROUPS):
        for a in range(EXPERTS_PER_GROUP):
            for b in range(a + 1, EXPERTS_PER_GROUP):
                ea.append(g * EXPERTS_PER_GROUP + a)
                eb.append(g * EXPERTS_PER_GROUP + b)
    return np.asarray(ea, np.int32), np.asarray(eb, np.int32)


def _moe_dispatched(rows, cls, x1, mod, wgu, wd, tiles_per_seq):
    t = x1.shape[0]
    slots, seg_end8 = _class_slots(cls, 1024)
    seg_end = seg_end8[0, 0:N_CLASSES].astype(jnp.int32)
    dest = slots[:, 0]
    dest3 = dest.reshape(t // MOVE_TILE, 1, MOVE_TILE)
    n_sorted = t + N_CLASSES * ROW_TILE
    tile_row0 = jnp.arange(n_sorted // ROW_TILE, dtype=jnp.int32) * ROW_TILE
    tile_cls = jnp.minimum(jnp.sum((seg_end[None, :] <= tile_row0[:, None]).astype(jnp.int32), axis=1),
                           N_CLASSES - 1)
    ea_np, eb_np = _class_expert_tables()
    in_cls = (tile_cls[:, None] == jnp.arange(N_CLASSES, dtype=jnp.int32)[None, :]).astype(jnp.int32)
    ea_t = jnp.sum(in_cls * jnp.asarray(ea_np % EXPERTS_PER_GROUP)[None, :], axis=1)
    eb_t = jnp.sum(in_cls * jnp.asarray(eb_np % EXPERTS_PER_GROUP)[None, :], axis=1)
    grp_t = tile_cls // N_PAIRS
    n_used = (seg_end[N_CLASSES - 1] // ROW_TILE).astype(jnp.int32).reshape(1)
    src = _invert(dest, n_sorted)[:, 0]
    ys = _experts(grp_t, ea_t, eb_t, n_used, src.reshape(n_sorted // ROW_TILE, 1, ROW_TILE), rows, wgu, wd)
    return _combine(dest3, ys, x1, mod, tiles_per_seq)


def _pad_lanes(v, fill):
    n = v.shape[-1]
    return jnp.concatenate([v, jnp.full((LANES - n,), fill, v.dtype)]).reshape(1, LANES)


def kernel(x_prompt, x_sample, c_prompt, c_sample, cache_k, cache_v, cache_kidx, page_table, w_ada, b_ada, norm_mix_g, norm_ffn_g, w_in, q_norm_g, k_norm_g, kidx_norm_g, gm_ln_g, gm_ln_b, gm_spatial_w, gm_spatial_b, w_proj_attn, w_proj_gmlp, w_out, w_router_group, b_router_group, w_router_expert, b_router_expert, w_expert_gate, w_expert_up, w_expert_down):
    depth = w_ada.shape[0]
    assert depth == 1
    l = 0
    bp, sp, _ = x_prompt.shape
    bs, ss, _ = x_sample.shape
    assert ss == 1
    n_pages = page_table.shape[1]
    past = n_pages * PAGE_SIZE
    tp = bp * sp

    w = w_in[l]
    zpad = jnp.zeros((D_MODEL, LANES - IDX_DIM - IDX_HEADS), F32)
    w_pad = jnp.concatenate([w[:, 0:1024], w[:, 1024:1088], w[:, 1088:1092], zpad, w[:, 1092:]], axis=1).astype(BF16)
    seg_np = (np.arange(LANES)[:, None] // HEAD_DIM) == (np.arange(LANES)[None, :] // HEAD_DIM)
    seg = jnp.asarray(seg_np, BF16)
    segki = jnp.asarray(seg_np & (np.arange(LANES)[:, None] < IDX_DIM) & (np.arange(LANES)[None, :] < IDX_DIM), BF16)
    consts = (norm_mix_g[l].reshape(1, D_MODEL),
              jnp.tile(q_norm_g[l], 2).reshape(1, LANES), jnp.tile(k_norm_g[l], 2).reshape(1, LANES),
              _pad_lanes(kidx_norm_g[l], 1.0),
              gm_ln_g[l].reshape(1, GM_WIDTH), gm_ln_b[l].reshape(1, GM_WIDTH), seg, segki)
    wpa = w_proj_attn[l].astype(BF16)
    wpg = w_proj_gmlp[l].astype(BF16)
    wo = w_out[l].astype(BF16)
    wr32 = jnp.concatenate([w_router_expert[l], w_router_group[l],
                            jnp.zeros((D_MODEL, LANES - N_EXPERTS - N_EXPERT_GROUPS), F32)], axis=1)
    wr_hi = wr32.astype(BF16)
    wr_lo = (wr32 - wr_hi.astype(F32)).astype(BF16)
    wr = jnp.concatenate([wr_hi, wr_lo], axis=1)
    br = _pad_lanes(jnp.concatenate([b_router_expert[l], b_router_group[l]]), 0.0)
    wgu = jnp.concatenate([w_expert_gate[l], w_expert_up[l]], axis=2).astype(BF16)
    wd = w_expert_down[l].astype(BF16)
    g2 = norm_ffn_g[l].reshape(1, D_MODEL)

    mod = _adaln(jnp.concatenate([c_prompt, c_sample], axis=0), w_ada[l], b_ada[l])
    mod_p = mod[0:bp].reshape(bp, 1, 6 * D_MODEL)
    mod_s = mod[bp:bp + bs]
    pos = jnp.concatenate([jnp.arange(sp, dtype=jnp.int32),
                           jnp.full((8,), past, jnp.int32)]).astype(F32).reshape(sp + 8, 1)
    tabs = _rope_tables(pos)
    tabs_p = tuple(t[0:sp] for t in tabs)
    tabs_s = tuple(t[sp:sp + 1] for t in tabs)

    tm = 512
    tps = sp // tm
    (q, kb, vb, qi, kib, kiwi, u, vg, ga, gb, kt_f, vt_f, kit_f, vt) = _project(
        x_prompt.reshape(tp, D_MODEL), mod_p, tabs_p, consts, w_pad, tm, tps, BF16)
    topk_p = min(TOPK_MAX, sp // 4)
    r3 = lambda a: a.reshape(bp, sp, a.shape[-1])
    attn_p = _prompt_attention(r3(qi), r3(kiwi), r3(kib), r3(q), r3(kb), vt, topk_p)
    bt = jnp.concatenate([gm_spatial_b[l].T, jnp.zeros((CHUNK, LANES - GM_GROUPS), F32)], axis=1)
    gm_p = _gmlp_prompt(u, vg, gm_spatial_w[l], bt, tm)
    x1_p, rows_p, cls_p = _merge(attn_p, gm_p, ga, gb, x_prompt.reshape(tp, D_MODEL), mod_p,
                                 wpa, wpg, wo, g2, wr, br, tm, tps, True)
    y_p = _moe_dispatched(rows_p, cls_p, x1_p, mod_p, wgu, wd, sp // MOVE_TILE)

    (q_s, kb_s, vb_s, qi_s, kib_s, kiwi_s, u_s, vg_s, ga_s, gb_s, kt_s, vt_s, kit_s, _) = _project(
        x_sample.reshape(bs, D_MODEL), mod_s, tabs_s, consts, w_pad, bs, 1, F32)
    qi8 = jnp.concatenate([qi_s.reshape(bs, IDX_HEADS, IDX_DIM),
                           jnp.zeros((bs, 8 - IDX_HEADS, IDX_DIM), BF16)], axis=1)
    wi8 = jnp.concatenate([kiwi_s[:, IDX_DIM:IDX_DIM + IDX_HEADS],
                           jnp.zeros((bs, 8 - IDX_HEADS), F32)], axis=1).reshape(bs, 8, 1)
    kidx_t = jnp.transpose(cache_kidx[l], (0, 2, 1))
    scores = _sample_scores(page_table, kidx_t, qi8, wi8, kib_s.reshape(bs, 1, IDX_DIM))
    topk_s = min(TOPK_MAX, (past + ss) // 4)
    bias = _sample_select(scores.reshape(bs, past + LANES), topk_s).reshape(bs, 1, past + LANES)
    kvw = N_KV_HEADS * HEAD_DIM
    k_t = jnp.transpose(cache_k[l], (0, 2, 3, 1)).reshape(-1, kvw, PAGE_SIZE)
    v_t = jnp.transpose(cache_v[l], (0, 2, 3, 1)).reshape(-1, kvw, PAGE_SIZE)
    q3 = q_s.reshape(bs, N_HEADS, HEAD_DIM)
    zq = jnp.zeros_like(q3)
    in_first = (jnp.arange(N_HEADS) < N_HEADS // N_KV_HEADS)[None, :, None]
    q2_s = jnp.where(in_first, jnp.concatenate([q3, zq], axis=2), jnp.concatenate([zq, q3], axis=2))
    attn_s = _sample_attention(page_table, k_t, v_t, q2_s, bias,
                               kb_s.reshape(bs, 1, kvw), vb_s.reshape(bs, 1, kvw))
    gd = GM_WIDTH // GM_GROUPS
    w0 = jnp.repeat(gm_spatial_w[l][:, 0, 0], gd).reshape(1, GM_WIDTH)
    b0 = jnp.repeat(gm_spatial_b[l][:, 0], gd).reshape(1, GM_WIDTH)
    gm_s = _gmlp_sample(u_s, vg_s, w0, b0)
    x1_s, h2_s, gate_s = _merge(attn_s.reshape(1, 1, bs, N_HEADS * HEAD_DIM), gm_s, ga_s, gb_s,
                                x_sample.reshape(bs, D_MODEL), mod_s, wpa, wpg, wo, g2, wr, br, bs, 1, False)
    y_s = _moe(h2_s, gate_s, x1_s, mod_s, wgu, wd, bs, 1)

    def rows_kv(a_t, n, s):
        return jnp.transpose(a_t.reshape(n, N_KV_HEADS, HEAD_DIM, s), (0, 3, 1, 2))[None]

    def rows_ki(a_t):
        return jnp.transpose(a_t, (0, 2, 1))[None]

    return (y_p.reshape(bp, sp, D_MODEL), y_s.reshape(bs, ss, D_MODEL),
            rows_kv(kt_f, bp, sp), rows_kv(vt_f, bp, sp), rows_ki(kit_f),
            rows_kv(kt_s, 1, bs).reshape(1, bs, ss, N_KV_HEADS, HEAD_DIM),
            rows_kv(vt_s, 1, bs).reshape(1, bs, ss, N_KV_HEADS, HEAD_DIM),
            rows_ki(kit_s).reshape(1, bs, ss, IDX_DIM), vg_s.reshape(1, bs, ss, GM_WIDTH))
```

```python
import functools

import numpy as np
import jax
import jax.numpy as jnp
from jax import lax
from jax.experimental import pallas as pl
from jax.experimental.pallas import tpu as pltpu

F32 = jnp.float32
BF16 = jnp.bfloat16
U32 = jnp.uint32

D_MODEL = 1024
N_HEADS = 8
HEAD_DIM = 64
N_KV_HEADS = 2
ROT_DIM = 16
ROPE_THETA = 500000.0
IDX_HEADS = 4
IDX_DIM = 64
TOPK_MAX = 256
Q_BLOCK = 128
GM_WIDTH = 512
GM_GROUPS = 8
CHUNK = 128
N_EXPERT_GROUPS = 4
EXPERTS_PER_GROUP = 8
N_EXPERTS = 32
D_EXPERT = 256
EPS = 1e-6
PAGE_SIZE = 128
LANES = 128
SUB = 8

C_Q, C_K, C_V, C_QI, C_KIWI, C_U, C_VG, C_GATE = 0, 512, 640, 768, 1024, 1152, 1664, 2176
D_IN_PAD = 4224
VMEM_LIMIT = 56 * 1024 * 1024

N_PAIRS = EXPERTS_PER_GROUP * (EXPERTS_PER_GROUP - 1) // 2
N_CLASSES = N_EXPERT_GROUPS * N_PAIRS
ROW_TILE = 128
HALF = D_MODEL // 2
FEAT_SUB = HALF // LANES
MOVE_TILE = 256
HI_MASK = 0xFFFF0000


def _cparams(sem):
    return pltpu.CompilerParams(dimension_semantics=sem, vmem_limit_bytes=VMEM_LIMIT)


def _pack_halves(x):
    n = x.shape[1] // 2
    hi = lax.bitcast_convert_type(x[:, 0:n], U32) & jnp.uint32(HI_MASK)
    lo = lax.bitcast_convert_type(x[:, n:2 * n], U32) >> 16
    return hi | lo


def _unpack_halves(u):
    hi = lax.bitcast_convert_type(u & jnp.uint32(HI_MASK), F32)
    lo = lax.bitcast_convert_type(u << 16, F32)
    return jnp.concatenate([hi, lo], axis=1)


def _store_token_tiles(ref, words, extra):
    n = words.shape[0]
    for s in range(FEAT_SUB):
        ref[pl.ds(s, n, stride=SUB), :] = words[:, s * LANES:(s + 1) * LANES]
    zero = jnp.zeros((n, LANES), U32)
    ref[pl.ds(FEAT_SUB, n, stride=SUB), :] = zero if extra is None else extra
    for s in range(FEAT_SUB + 1, SUB):
        ref[pl.ds(s, n, stride=SUB), :] = zero


def _load_token_words(ref, n):
    return jnp.concatenate([ref[pl.ds(s, n, stride=SUB), :] for s in range(FEAT_SUB)], axis=1)


def _adaln_body(c_ref, w_ref, b_ref, o_ref):
    c = c_ref[...]
    a = c * jax.nn.sigmoid(c)
    o_ref[...] = jnp.dot(a, w_ref[...], preferred_element_type=F32,
                         precision=lax.Precision.HIGHEST) + b_ref[...]


def _adaln(c, w, b):
    r = c.shape[0]
    n = w.shape[1]
    bn = 1536
    return pl.pallas_call(
        _adaln_body,
        grid=(n // bn,),
        in_specs=[pl.BlockSpec((r, D_MODEL), lambda j: (0, 0)),
                  pl.BlockSpec((D_MODEL, bn), lambda j: (0, j)),
                  pl.BlockSpec((1, bn), lambda j: (0, j))],
        out_specs=pl.BlockSpec((r, bn), lambda j: (0, j)),
        out_shape=jax.ShapeDtypeStruct((r, n), F32),
        compiler_params=_cparams(("arbitrary",)),
        name="adaln",
    )(c, w, b.reshape(1, n))


def _rope_table_body(pos_ref, invf_ref, sa_m_ref, sb_m_ref, c_ref, sa_ref, sb_ref):
    ang = pos_ref[...] * invf_ref[...]
    s = jnp.sin(ang)
    c_ref[...] = jnp.cos(ang)
    sa_ref[...] = s * sa_m_ref[...]
    sb_ref[...] = s * sb_m_ref[...]


def _rope_tables(pos):
    half = ROT_DIM // 2
    inv_freq = ROPE_THETA ** (-jnp.arange(half, dtype=F32) / half)
    d = np.arange(LANES) % HEAD_DIM
    invf = jnp.where(jnp.asarray(d < ROT_DIM), inv_freq[d % half], 0.0).reshape(1, LANES)
    sa_m = jnp.asarray(np.where(d < half, -1.0, 0.0), F32).reshape(1, LANES)
    sb_m = jnp.asarray(np.where((d >= half) & (d < ROT_DIM), 1.0, 0.0), F32).reshape(1, LANES)
    r = pos.shape[0]
    return pl.pallas_call(
        _rope_table_body,
        out_shape=[jax.ShapeDtypeStruct((r, LANES), F32)] * 3,
        name="rope_tables",
    )(pos, invf, sa_m, sb_m)


def _rope(y, c, sa, sb):
    return y * c + pltpu.roll(y, LANES - ROT_DIM // 2, 1) * sa + pltpu.roll(y, ROT_DIM // 2, 1) * sb


def _seg_rms(r, seg, gain):
    ss = r * r
    hi = ss.astype(BF16)
    lo = (ss - hi.astype(F32)).astype(BF16)
    tot = jnp.dot(hi, seg, preferred_element_type=F32) + jnp.dot(lo, seg, preferred_element_type=F32)
    return r * lax.rsqrt(tot * (1.0 / HEAD_DIM) + EPS) * gain


def _proj_body(mod3d, x_ref, mod_ref, g_ref, w_ref, c_ref, sa_ref, sb_ref, qg_ref, kg_ref, kig_ref,
               lng_ref, lnb_ref, seg_ref, segki_ref,
               q_o, kb_o, vb_o, qi_o, kib_o, kiwi_o, u_o, vg_o, ga_o, gb_o, kt_o, vtf_o, kit_o, vt_o):
    x = x_ref[...]
    m = mod_ref[0] if mod3d else mod_ref[...]
    shift1 = m[:, 0:D_MODEL]
    scale1 = m[:, D_MODEL:2 * D_MODEL]
    ms = jnp.mean(x * x, axis=-1, keepdims=True)
    y = x * lax.rsqrt(ms + EPS) * g_ref[...]
    h = (y * (1.0 + scale1) + shift1).astype(BF16)
    c, sa, sb = c_ref[...], sa_ref[...], sb_ref[...]
    seg = seg_ref[...]

    def proj(a, b):
        return jnp.dot(h, w_ref[:, a:b], preferred_element_type=F32)

    for g2 in range(2):
        r2 = proj(C_Q + g2 * 2 * LANES, C_Q + (g2 + 1) * 2 * LANES)
        for g in (2 * g2, 2 * g2 + 1):
            r = r2[:, (g % 2) * LANES:(g % 2 + 1) * LANES]
            yq = _rope(_seg_rms(r, seg, qg_ref[...]), c, sa, sb) * (HEAD_DIM ** -0.5)
            q_o[:, g * LANES:(g + 1) * LANES] = yq.astype(BF16)
    r_kv = proj(C_K, C_V + LANES)
    yk = _rope(_seg_rms(r_kv[:, 0:LANES], seg, kg_ref[...]), c, sa, sb)
    kt_o[0] = yk.T
    kb_o[...] = yk.astype(BF16)
    r = r_kv[:, LANES:2 * LANES]
    r_t = r.T
    vtf_o[0] = r_t
    vb_o[...] = r.astype(BF16)
    vt_o[0] = r_t.astype(BF16)
    r2 = proj(C_QI, C_QI + 2 * LANES)
    for g in range(2):
        yqi = _rope(r2[:, g * LANES:(g + 1) * LANES], c, sa, sb) * (IDX_DIM ** -0.5 * IDX_HEADS ** -0.5)
        qi_o[:, g * LANES:(g + 1) * LANES] = yqi.astype(BF16)
    r = proj(C_KIWI, C_KIWI + LANES)
    yki = _rope(_seg_rms(r, segki_ref[...], kig_ref[...]), c, sa, sb)
    lane = lax.broadcasted_iota(jnp.int32, r.shape, 1)
    kiwi = jnp.where(lane < IDX_DIM, yki, r)
    kiwi_o[...] = kiwi
    kit_o[0] = kiwi.T[0:IDX_DIM, :]
    kib_o[...] = kiwi[:, 0:IDX_DIM].astype(BF16)
    r = proj(C_U, C_U + GM_WIDTH)
    u_o[...] = jax.nn.gelu(r).astype(BF16)
    r = proj(C_VG, C_VG + GM_WIDTH)
    gl = jax.nn.gelu(r)
    mu = jnp.mean(gl, axis=-1, keepdims=True)
    dv = gl - mu
    var = jnp.mean(dv * dv, axis=-1, keepdims=True)
    vg_o[...] = (dv * lax.rsqrt(var + EPS) * lng_ref[...] + lnb_ref[...]).astype(vg_o.dtype)
    r = proj(C_GATE, C_GATE + D_MODEL)
    ga_o[...] = jax.nn.sigmoid(r).astype(BF16)
    r = proj(C_GATE + D_MODEL, C_GATE + 2 * D_MODEL)
    gb_o[...] = jax.nn.sigmoid(r).astype(BF16)


def _project(x, mod, tables, consts, w_pad, tm, tiles_per_seq, vg_dtype):
    t = x.shape[0]
    mod3d = mod.ndim == 3
    c_t, sa_t, sb_t = tables
    g_mix, qg, kg, kig, lng, lnb, seg, segki = consts
    row = lambda i: (i, 0)
    fixed = lambda i: (0, 0)
    if mod3d:
        mod_spec = pl.BlockSpec((1, 1, 6 * D_MODEL), lambda i: (i // tiles_per_seq, 0, 0))
        tab_spec = pl.BlockSpec((tm, LANES), lambda i: (i % tiles_per_seq, 0))
    else:
        mod_spec = pl.BlockSpec((tm, 6 * D_MODEL), row)
        tab_spec = pl.BlockSpec((1, LANES), fixed)
    widths = [(512, BF16), (128, BF16), (128, BF16), (256, BF16),
              (64, BF16), (128, F32), (512, BF16), (512, vg_dtype), (1024, BF16), (1024, BF16)]
    n_seq, seq = t // (tm * tiles_per_seq), tm * tiles_per_seq
    t_idx = lambda i: (i // tiles_per_seq, 0, i % tiles_per_seq)
    t_outs = [(LANES, F32), (LANES, F32), (IDX_DIM, F32), (LANES, BF16)]
    return pl.pallas_call(
        functools.partial(_proj_body, mod3d),
        grid=(t // tm,),
        in_specs=[pl.BlockSpec((tm, D_MODEL), row), mod_spec,
                  pl.BlockSpec((1, D_MODEL), fixed),
                  pl.BlockSpec((D_MODEL, D_IN_PAD), fixed),
                  tab_spec, tab_spec, tab_spec,
                  pl.BlockSpec((1, LANES), fixed), pl.BlockSpec((1, LANES), fixed), pl.BlockSpec((1, LANES), fixed),
                  pl.BlockSpec((1, GM_WIDTH), fixed), pl.BlockSpec((1, GM_WIDTH), fixed),
                  pl.BlockSpec((LANES, LANES), fixed), pl.BlockSpec((LANES, LANES), fixed)],
        out_specs=[pl.BlockSpec((tm, w), row) for w, _ in widths]
        + [pl.BlockSpec((1, f, tm), t_idx) for f, _ in t_outs],
        out_shape=[jax.ShapeDtypeStruct((t, w), dt) for w, dt in widths]
        + [jax.ShapeDtypeStruct((n_seq, f, seq), dt) for f, dt in t_outs],
        compiler_params=_cparams(("arbitrary",)),
        name="project",
    )(x, mod, g_mix, w_pad, c_t, sa_t, sb_t, qg, kg, kig, lng, lnb, seg, segki)


def _select_bias(s_ref, bias_ref, rows, width, topk, tie_check_start=16):
    nb = width // LANES
    kf = float(topk)
    neg, pos = -jnp.inf, jnp.inf

    def blk(j):
        return s_ref[:, j * LANES:(j + 1) * LANES]

    def count_above(t):
        tb = jnp.broadcast_to(t, (rows, LANES))
        acc = jnp.zeros((rows, LANES), F32)
        for j in range(nb):
            acc = acc + jnp.where(blk(j) > tb, 1.0, 0.0)
        return jnp.sum(acc, axis=1, keepdims=True)

    mx = jnp.full((rows, LANES), neg, F32)
    mn = jnp.full((rows, LANES), pos, F32)
    for j in range(nb):
        b = blk(j)
        mx = jnp.maximum(mx, b)
        mn = jnp.minimum(mn, jnp.where(b == neg, pos, b))
    hi0 = jnp.max(mx, axis=1, keepdims=True)
    smin = jnp.min(mn, axis=1, keepdims=True)
    lo0 = smin - jnp.abs(smin) - 1.0
    f_lo0 = count_above(lo0)
    zeros = jnp.zeros((rows, 1), F32)

    def active_of(f_lo, tie):
        return jnp.logical_and(f_lo > kf, tie == 0.0)

    def cond(st):
        _, _, _, f_lo, _, tie = st
        return jnp.max(jnp.where(active_of(f_lo, tie), 1.0, 0.0)) > 0.0

    def body(st):
        it, lo, hi, f_lo, f_hi, tie = st
        active = active_of(f_lo, tie)
        mid = lo + (hi - lo) * 0.5
        stuck = jnp.logical_or(mid <= lo, mid >= hi)
        cnt = count_above(mid)
        ge = cnt >= kf
        up_lo = jnp.logical_and(active, ge)
        up_hi = jnp.logical_and(active, jnp.logical_not(ge))
        lo = jnp.where(up_lo, mid, lo)
        f_lo = jnp.where(up_lo, cnt, f_lo)
        hi = jnp.where(up_hi, mid, hi)
        f_hi = jnp.where(up_hi, cnt, f_hi)
        tie = jnp.where(jnp.logical_and(active, stuck), 1.0, tie)

        def tie_check(_):
            lob = jnp.broadcast_to(lo, (rows, LANES))
            hib = jnp.broadcast_to(hi, (rows, LANES))
            vmx = jnp.full((rows, LANES), neg, F32)
            vmn = jnp.full((rows, LANES), pos, F32)
            for j in range(nb):
                b = blk(j)
                inn = jnp.logical_and(b > lob, b <= hib)
                vmx = jnp.maximum(vmx, jnp.where(inn, b, neg))
                vmn = jnp.minimum(vmn, jnp.where(inn, b, pos))
            one_value = jnp.max(vmx, axis=1, keepdims=True) == jnp.min(vmn, axis=1, keepdims=True)
            return jnp.where(one_value, 1.0, tie)

        run_check = jnp.logical_and(it >= tie_check_start, it % 4 == 0)
        tie = lax.cond(run_check, tie_check, lambda _: tie, 0)
        return it + 1, lo, hi, f_lo, f_hi, tie

    _, lo, hi, f_lo, f_hi, _ = lax.while_loop(cond, body, (jnp.int32(0), lo0, hi0, f_lo0, zeros, zeros))
    lob = jnp.broadcast_to(lo, (rows, LANES))
    need_prefix = jnp.max(jnp.where(f_lo > kf, 1.0, 0.0)) > 0.0

    @pl.when(jnp.logical_not(need_prefix))
    def _():
        for j in range(nb):
            bias_ref[:, j * LANES:(j + 1) * LANES] = jnp.where(blk(j) > lob, 0.0, neg)

    @pl.when(need_prefix)
    def _():
        hib = jnp.broadcast_to(hi, (rows, LANES))
        need = kf - f_hi
        ri = lax.broadcasted_iota(jnp.int32, (LANES, LANES), 0)
        ci = lax.broadcasted_iota(jnp.int32, (LANES, LANES), 1)
        upper = jnp.where(ri < ci, 1.0, 0.0).astype(BF16)
        off = jnp.zeros((rows, 1), F32)
        for j in range(nb):
            b = blk(j)
            inn = jnp.logical_and(b > lob, b <= hib)
            innf = jnp.where(inn, 1.0, 0.0)
            before = jnp.dot(innf.astype(BF16), upper, preferred_element_type=F32) + off
            sel = jnp.logical_or(b > hib, jnp.logical_and(inn, before < need))
            bias_ref[:, j * LANES:(j + 1) * LANES] = jnp.where(sel, 0.0, neg)
            off = off + jnp.sum(innf, axis=1, keepdims=True)


_NT = (((1,), (1,)), ((), ()))


def _sub_reduce(x, op):
    for sh in (4, 2, 1):
        x = op(x, pltpu.roll(x, sh, 0))
    return x


def _select_bias_t(s_ref, bias_ref, width, topk, n_adm, steps_per_check=4, tie_check_from=4):
    rb = 64
    nb = width // rb
    kf = float(topk)
    neg, pos = -jnp.inf, jnp.inf

    def blk(j):
        return s_ref[j * rb:(j + 1) * rb, :]

    def fold(x, op):
        y = x[0:SUB]
        for a in range(1, rb // SUB):
            y = op(y, x[a * SUB:(a + 1) * SUB])
        return _sub_reduce(y, op)

    def tile(v):
        return jnp.concatenate([v] * (rb // SUB), axis=0)

    def count_above(t):
        tb = tile(t)
        acc = jnp.zeros((rb, LANES), F32)
        for j in range(nb):
            acc = acc + jnp.where(blk(j) > tb, 1.0, 0.0)
        return fold(acc, jnp.add)

    mx = jnp.full((rb, LANES), neg, F32)
    mn = jnp.full((rb, LANES), pos, F32)
    for j in range(nb):
        b = blk(j)
        mx = jnp.maximum(mx, b)
        mn = jnp.minimum(mn, jnp.where(b == neg, pos, b))
    hi0 = fold(mx, jnp.maximum)
    smin = fold(mn, jnp.minimum)
    lo0 = jnp.minimum(smin - jnp.abs(smin) * (2.0 ** -10) - 1e-30, smin - (hi0 - smin) * (2.0 ** -10))
    zeros = jnp.zeros((SUB, LANES), F32)

    def active_of(f_lo, tie):
        return jnp.logical_and(f_lo > kf, tie == 0.0)

    def any_lane(cond):
        return jnp.max(jnp.where(cond, 1.0, 0.0)) > 0.0

    def step(lo, hi, f_lo, f_hi, tie):
        active = active_of(f_lo, tie)
        mid = lo + (hi - lo) * 0.5
        stuck = jnp.logical_or(mid <= lo, mid >= hi)
        cnt = count_above(mid)
        ge = cnt >= kf
        up_lo = jnp.logical_and(active, ge)
        up_hi = jnp.logical_and(active, jnp.logical_not(ge))
        return (jnp.where(up_lo, mid, lo), jnp.where(up_hi, mid, hi), jnp.where(up_lo, cnt, f_lo),
                jnp.where(up_hi, cnt, f_hi), jnp.where(jnp.logical_and(active, stuck), 1.0, tie))

    def cond(st):
        _, _, _, f_lo, _, tie = st
        return any_lane(active_of(f_lo, tie))

    def body(st):
        it, lo, hi, f_lo, f_hi, tie = st
        for _ in range(steps_per_check):
            lo, hi, f_lo, f_hi, tie = step(lo, hi, f_lo, f_hi, tie)

        def tie_check(_):
            lob, hib = tile(lo), tile(hi)
            vmx = jnp.full((rb, LANES), neg, F32)
            vmn = jnp.full((rb, LANES), pos, F32)
            for j in range(nb):
                b = blk(j)
                inn = jnp.logical_and(b > lob, b <= hib)
                vmx = jnp.maximum(vmx, jnp.where(inn, b, neg))
                vmn = jnp.minimum(vmn, jnp.where(inn, b, pos))
            return jnp.where(fold(vmx, jnp.maximum) == fold(vmn, jnp.minimum), 1.0, tie)

        run_check = jnp.logical_and(it + 1 >= tie_check_from, any_lane(active_of(f_lo, tie)))
        tie = lax.cond(run_check, tie_check, lambda _: tie, 0)
        return it + 1, lo, hi, f_lo, f_hi, tie

    _, lo, hi, f_lo, f_hi, _ = lax.while_loop(cond, body, (jnp.int32(0), lo0, hi0, n_adm, zeros, zeros))
    lob = tile(lo)
    need_prefix = any_lane(f_lo > kf)

    @pl.when(jnp.logical_not(need_prefix))
    def _():
        for j in range(nb):
            bias_ref[j * rb:(j + 1) * rb, :] = jnp.where(blk(j) > lob, 0.0, neg)

    @pl.when(need_prefix)
    def _():
        need = (kf - f_hi)[0:1]
        lo1, hi1 = lo[0:1], hi[0:1]
        ri = lax.broadcasted_iota(jnp.int32, (LANES, LANES), 0)
        ci = lax.broadcasted_iota(jnp.int32, (LANES, LANES), 1)
        lower = jnp.where(ci < ri, 1.0, 0.0).astype(BF16)
        off = jnp.zeros((1, LANES), F32)
        for j in range(width // LANES):
            b = s_ref[j * LANES:(j + 1) * LANES, :]
            inn = jnp.logical_and(b > lo1, b <= hi1)
            innf = jnp.where(inn, 1.0, 0.0)
            before = jnp.dot(lower, innf.astype(BF16), preferred_element_type=F32) + off
            sel = jnp.logical_or(b > hi1, jnp.logical_and(inn, before < need))
            bias_ref[j * LANES:(j + 1) * LANES, :] = jnp.where(sel, 0.0, neg)
            off = off + jnp.sum(innf, axis=0, keepdims=True)


def _prompt_attn_body(blk_i, topk, qi_ref, kiwi_ref, kib_ref, q_ref, k_ref, vt_ref, o_ref, s_ref, bias_ref):
    width = (blk_i + 1) * Q_BLOCK
    neg = -jnp.inf
    kiwi_t = kiwi_ref[0].T
    chunk = 512
    for c0 in range(0, width, chunk):
        c1 = min(width, c0 + chunk)
        kib = kib_ref[0, c0:c1, :]
        sc = None
        for h in range(IDX_HEADS):
            d = lax.dot_general(kib, qi_ref[0, :, h * IDX_DIM:(h + 1) * IDX_DIM], _NT,
                                preferred_element_type=F32)
            t = jnp.maximum(d, 0.0) * kiwi_t[IDX_DIM + h:IDX_DIM + h + 1, :]
            sc = t if sc is None else sc + t
        s_ref[c0:c1, :] = sc
    ki = lax.broadcasted_iota(jnp.int32, (Q_BLOCK, Q_BLOCK), 0)
    qj = lax.broadcasted_iota(jnp.int32, (Q_BLOCK, Q_BLOCK), 1)
    d0 = width - Q_BLOCK
    s_ref[d0:width, :] = jnp.where(ki <= qj, s_ref[d0:width, :], neg)
    if width > topk:
        n_adm = (lax.broadcasted_iota(jnp.int32, (SUB, LANES), 1) + (d0 + 1)).astype(F32)
        _select_bias_t(s_ref, bias_ref, width, topk, n_adm)
    else:
        bias_ref[...] = jnp.where(s_ref[...] == neg, neg, 0.0)
    lane = lax.broadcasted_iota(jnp.int32, (Q_BLOCK, LANES), 1)
    hpg = N_HEADS // N_KV_HEADS
    for h in range(N_HEADS):
        g = h // hpg
        qp = q_ref[0, :, (h // 2) * LANES:(h // 2 + 1) * LANES].astype(F32)
        if h % 2 != g:
            qp = pltpu.roll(qp, HEAD_DIM, 1)
        q2 = jnp.where((lane >= HEAD_DIM) == (g == 1), qp, 0.0).astype(BF16)
        s = lax.dot_general(k_ref[0], q2, _NT, preferred_element_type=F32) + bias_ref[...]
        m = jnp.max(s, axis=0, keepdims=True)
        p = jnp.exp(s - m)
        l = jnp.sum(p, axis=0, keepdims=True)
        o_t = jnp.dot(vt_ref[0, g * HEAD_DIM:(g + 1) * HEAD_DIM, :], p.astype(BF16), preferred_element_type=F32)
        o_ref[0, :, h * HEAD_DIM:(h + 1) * HEAD_DIM] = (o_t / l).T.astype(BF16)


def _prompt_attention(qi, kiwi, kib, q, kb, vt, topk):
    b, s, _ = q.shape
    outs = []
    for i in range(s // Q_BLOCK):
        width = (i + 1) * Q_BLOCK
        qblk = lambda bb, i=i: (bb, i, 0)
        kall = lambda bb: (bb, 0, 0)
        outs.append(pl.pallas_call(
            functools.partial(_prompt_attn_body, i, topk),
            grid=(b,),
            in_specs=[pl.BlockSpec((1, Q_BLOCK, IDX_HEADS * IDX_DIM), qblk),
                      pl.BlockSpec((1, Q_BLOCK, LANES), qblk),
                      pl.BlockSpec((1, width, IDX_DIM), kall),
                      pl.BlockSpec((1, Q_BLOCK, N_HEADS * HEAD_DIM), qblk),
                      pl.BlockSpec((1, width, N_KV_HEADS * HEAD_DIM), kall),
                      pl.BlockSpec((1, N_KV_HEADS * HEAD_DIM, width), kall)],
            out_specs=pl.BlockSpec((1, Q_BLOCK, N_HEADS * HEAD_DIM), lambda bb: (bb, 0, 0)),
            out_shape=jax.ShapeDtypeStruct((b, Q_BLOCK, N_HEADS * HEAD_DIM), BF16),
            scratch_shapes=[pltpu.VMEM((width, Q_BLOCK), F32), pltpu.VMEM((width, Q_BLOCK), F32)],
            compiler_params=_cparams(("arbitrary",)),
            name=f"prompt_attn_{i}",
        )(qi, kiwi, kib, q, kb, vt))
    return jnp.stack(outs)


def _page_copies(pt_ref, sample, src_hbm, buf, slot, sem, n_pages):
    return [pltpu.make_async_copy(src_hbm.at[pt_ref[sample, p]],
                                  buf.at[slot, :, pl.ds(p * PAGE_SIZE, PAGE_SIZE)], sem.at[slot])
            for p in range(n_pages)]


def _sample_scores_body(n_pages, pt_ref, kidx_hbm, qi_ref, wi_ref, kin_ref, o_ref, buf, sem):
    s = pl.program_id(0)
    slot = s % 2

    @pl.when(s == 0)
    def _():
        _start_all(_page_copies(pt_ref, 0, kidx_hbm, buf, 0, sem, n_pages))

    @pl.when(s + 1 < pl.num_programs(0))
    def _():
        _start_all(_page_copies(pt_ref, s + 1, kidx_hbm, buf, 1 - slot, sem, n_pages))

    for cp in _page_copies(pt_ref, s, kidx_hbm, buf, slot, sem, n_pages):
        cp.wait()
    qi = qi_ref[0]
    wi = wi_ref[0]
    ki_t = buf[slot].astype(BF16)
    d = jnp.dot(qi, ki_t, preferred_element_type=F32)
    past = jnp.sum(jnp.maximum(d, 0.0) * wi, axis=0, keepdims=True)
    dn = jnp.sum(qi.astype(F32) * kin_ref[0].astype(F32), axis=1, keepdims=True)
    new = jnp.sum(jnp.maximum(dn, 0.0) * wi, axis=0, keepdims=True)
    lane = lax.broadcasted_iota(jnp.int32, (1, LANES), 1)
    tail = jnp.where(lane == 0, jnp.broadcast_to(new, (1, LANES)), -jnp.inf)
    o_ref[0] = jnp.concatenate([past, tail], axis=1)


def _sample_scores(page_table, kidx_pool, qi8, wi8, ki_new):
    n, n_pages = page_table.shape
    past = n_pages * PAGE_SIZE
    grid_spec = pltpu.PrefetchScalarGridSpec(
        num_scalar_prefetch=1,
        grid=(n,),
        in_specs=[pl.BlockSpec(memory_space=pl.ANY),
                  pl.BlockSpec((1, 8, IDX_DIM), lambda s, pt: (s, 0, 0)),
                  pl.BlockSpec((1, 8, 1), lambda s, pt: (s, 0, 0)),
                  pl.BlockSpec((1, 1, IDX_DIM), lambda s, pt: (s, 0, 0))],
        out_specs=pl.BlockSpec((1, 1, past + LANES), lambda s, pt: (s, 0, 0)),
        scratch_shapes=[pltpu.VMEM((2, IDX_DIM, past), F32), pltpu.SemaphoreType.DMA((2,))],
    )
    return pl.pallas_call(
        functools.partial(_sample_scores_body, n_pages),
        grid_spec=grid_spec,
        out_shape=jax.ShapeDtypeStruct((n, 1, past + LANES), F32),
        compiler_params=_cparams(("arbitrary",)),
        name="sample_scores",
    )(page_table, kidx_pool, qi8, wi8, ki_new)


def _sample_select_body(topk, s_ref, bias_ref):
    rows, width = s_ref.shape
    _select_bias(s_ref, bias_ref, rows, width, topk)


def _sample_select(scores, topk):
    return pl.pallas_call(
        functools.partial(_sample_select_body, topk),
        out_shape=jax.ShapeDtypeStruct(scores.shape, F32),
        compiler_params=pltpu.CompilerParams(vmem_limit_bytes=VMEM_LIMIT),
        name="sample_select",
    )(scores)


def _sample_attn_body(n_pages, pt_ref, k_hbm, v_hbm, q_ref, bias_ref, kn_ref, vn_ref, o_ref, kbuf, vbuf, sem):
    s = pl.program_id(0)
    slot = s % 2
    past = n_pages * PAGE_SIZE

    def copies(sample, sl):
        return (_page_copies(pt_ref, sample, k_hbm, kbuf, sl, sem.at[0], n_pages)
                + _page_copies(pt_ref, sample, v_hbm, vbuf, sl, sem.at[1], n_pages))

    @pl.when(s == 0)
    def _():
        _start_all(copies(0, 0))

    @pl.when(s + 1 < pl.num_programs(0))
    def _():
        _start_all(copies(s + 1, 1 - slot))

    for cp in copies(s, slot):
        cp.wait()
    q2 = q_ref[0]
    k_t = kbuf[slot].astype(BF16)
    v_t = vbuf[slot].astype(BF16)
    row = lax.broadcasted_iota(jnp.int32, (N_HEADS, 1), 0)
    first = row < (N_HEADS // N_KV_HEADS)
    bias = bias_ref[0]
    sc = jnp.dot(q2, k_t, preferred_element_type=F32) + bias[:, 0:past]
    sn = jnp.sum(q2.astype(F32) * kn_ref[0].astype(F32), axis=1, keepdims=True) + bias[:, past:past + 1]
    m = jnp.maximum(jnp.max(sc, axis=1, keepdims=True), sn)
    p = jnp.exp(sc - m)
    pn = jnp.exp(sn - m)
    l = jnp.sum(p, axis=1, keepdims=True) + pn
    o2 = lax.dot_general(p.astype(BF16), v_t, _NT, preferred_element_type=F32)
    o2 = o2 + pn.astype(BF16).astype(F32) * vn_ref[0].astype(F32)
    o = jnp.where(first, o2[:, 0:HEAD_DIM], o2[:, HEAD_DIM:2 * HEAD_DIM])
    o_ref[0] = (o / l).astype(BF16)


def _sample_attention(page_table, k_pool, v_pool, q8, bias, k_new, v_new):
    n, n_pages = page_table.shape
    past = n_pages * PAGE_SIZE
    kvw = N_KV_HEADS * HEAD_DIM
    per = lambda s, pt: (s, 0, 0)
    grid_spec = pltpu.PrefetchScalarGridSpec(
        num_scalar_prefetch=1,
        grid=(n,),
        in_specs=[pl.BlockSpec(memory_space=pl.ANY), pl.BlockSpec(memory_space=pl.ANY),
                  pl.BlockSpec((1, N_HEADS, kvw), per),
                  pl.BlockSpec((1, 1, past + LANES), per),
                  pl.BlockSpec((1, 1, kvw), per),
                  pl.BlockSpec((1, 1, kvw), per)],
        out_specs=pl.BlockSpec((1, N_HEADS, HEAD_DIM), per),
        scratch_shapes=[pltpu.VMEM((2, kvw, past), F32), pltpu.VMEM((2, kvw, past), F32),
                        pltpu.SemaphoreType.DMA((2, 2))],
    )
    return pl.pallas_call(
        functools.partial(_sample_attn_body, n_pages),
        grid_spec=grid_spec,
        out_shape=jax.ShapeDtypeStruct((n, N_HEADS, HEAD_DIM), BF16),
        compiler_params=_cparams(("arbitrary",)),
        name="sample_attn",
    )(page_table, k_pool, v_pool, q8, bias, k_new, v_new)


def _gmlp_body(n_chunks, u_ref, vg_ref, w_ref, bt_ref, o_ref):
    ri = lax.broadcasted_iota(jnp.int32, (CHUNK, CHUNK), 0)
    ci = lax.broadcasted_iota(jnp.int32, (CHUNK, CHUNK), 1)
    gd = GM_WIDTH // GM_GROUPS
    for g in range(GM_GROUPS):
        wg = jnp.where(ci <= ri, w_ref[g], 0.0).astype(BF16)
        bg = bt_ref[:, g:g + 1]
        for c in range(n_chunks):
            rows = slice(c * CHUNK, (c + 1) * CHUNK)
            cols = slice(g * gd, (g + 1) * gd)
            mixed = jnp.dot(wg, vg_ref[rows, cols], preferred_element_type=F32) + bg
            o_ref[rows, cols] = (u_ref[rows, cols].astype(F32) * mixed).astype(BF16)


def _gmlp_prompt(u, vg, w, bt, tm):
    t = u.shape[0]
    row = lambda i: (i, 0)
    return pl.pallas_call(
        functools.partial(_gmlp_body, tm // CHUNK),
        grid=(t // tm,),
        in_specs=[pl.BlockSpec((tm, GM_WIDTH), row), pl.BlockSpec((tm, GM_WIDTH), row),
                  pl.BlockSpec((GM_GROUPS, CHUNK, CHUNK), lambda i: (0, 0, 0)),
                  pl.BlockSpec((CHUNK, LANES), lambda i: (0, 0))],
        out_specs=pl.BlockSpec((tm, GM_WIDTH), row),
        out_shape=jax.ShapeDtypeStruct((t, GM_WIDTH), BF16),
        compiler_params=_cparams(("arbitrary",)),
        name="gmlp",
    )(u, vg, w, bt)


def _gmlp_first_row_body(u_ref, vg_ref, w0_ref, b0_ref, o_ref):
    o_ref[...] = (u_ref[...].astype(F32) * (vg_ref[...] * w0_ref[...] + b0_ref[...])).astype(BF16)


def _gmlp_sample(u, vg, w0, b0):
    return pl.pallas_call(
        _gmlp_first_row_body,
        out_shape=jax.ShapeDtypeStruct(u.shape, BF16),
        name="gmlp_first_row",
    )(u, vg, w0, b0)


def _merge_body(mod3d, dispatch, attn_ref, gm_ref, ga_ref, gb_ref, x_ref, mod_ref, wpa_ref, wpg_ref, wo_ref,
                g2_ref, wr_ref, br_ref, x1_o, a_o, b_o):
    tm = x_ref.shape[0]
    attn = attn_ref[...].reshape(tm, N_HEADS * HEAD_DIM)
    a = jnp.dot(attn, wpa_ref[...], preferred_element_type=F32)
    g = jnp.dot(gm_ref[...], wpg_ref[...], preferred_element_type=F32)
    merged = ga_ref[...].astype(F32) * a + gb_ref[...].astype(F32) * g
    out = jnp.dot(merged.astype(BF16), wo_ref[...], preferred_element_type=F32)
    m = mod_ref[0] if mod3d else mod_ref[...]
    x1 = x_ref[...] + m[:, 2 * D_MODEL:3 * D_MODEL] * out
    x1_o[...] = x1
    ms = jnp.mean(x1 * x1, axis=-1, keepdims=True)
    y = x1 * lax.rsqrt(ms + EPS) * g2_ref[...]
    h2 = y * (1.0 + m[:, 4 * D_MODEL:5 * D_MODEL]) + m[:, 3 * D_MODEL:4 * D_MODEL]
    hi = h2.astype(BF16)
    lo = (h2 - hi.astype(F32)).astype(BF16)
    r = jnp.dot(hi, wr_ref[...], preferred_element_type=F32) + jnp.dot(lo, wr_ref[...], preferred_element_type=F32)
    logits = r[:, 0:LANES] + r[:, LANES:2 * LANES] + br_ref[...]
    neg = -jnp.inf
    big = jnp.int32(1 << 20)
    lane = lax.broadcasted_iota(jnp.int32, logits.shape, 1)
    is_g = jnp.logical_and(lane >= N_EXPERTS, lane < N_EXPERTS + N_EXPERT_GROUPS)
    gl = jnp.where(is_g, logits, neg)
    gmax = jnp.max(gl, axis=1, keepdims=True)
    g_lane = jnp.min(jnp.where(gl == gmax, lane, big), axis=1, keepdims=True)
    g_w = 1.0 / jnp.sum(jnp.exp(gl - gmax), axis=1, keepdims=True)
    g_sel = g_lane - N_EXPERTS
    in_grp = jnp.logical_and(lane < N_EXPERTS, (lane >> 3) == g_sel)
    el = jnp.where(in_grp, logits, neg)
    m1 = jnp.max(el, axis=1, keepdims=True)
    i1 = jnp.min(jnp.where(el == m1, lane, big), axis=1, keepdims=True)
    el2 = jnp.where(lane == i1, neg, el)
    m2 = jnp.max(el2, axis=1, keepdims=True)
    i2 = jnp.min(jnp.where(el2 == m2, lane, big), axis=1, keepdims=True)
    e2 = jnp.exp(m2 - m1)
    w1 = g_w / (1.0 + e2)
    w2 = g_w * e2 / (1.0 + e2)
    if not dispatch:
        a_o[...] = hi
        b_o[...] = jnp.where(lane == i1, w1, 0.0) + jnp.where(lane == i2, w2, 0.0)
        return
    low_first = i1 < i2
    ea = jnp.where(low_first, i1, i2) - g_sel * EXPERTS_PER_GROUP
    eb = jnp.where(low_first, i2, i1) - g_sel * EXPERTS_PER_GROUP
    cls = g_sel * N_PAIRS + ((ea * (2 * EXPERTS_PER_GROUP - 1 - ea)) >> 1) + (eb - ea - 1)
    b_o[...] = jnp.broadcast_to(cls, (tm, LANES))
    wa = jnp.where(low_first, w1, w2)
    wb = jnp.where(low_first, w2, w1)
    _store_token_tiles(a_o, _pack_halves(hi.astype(F32)), lax.bitcast_convert_type(
        jnp.where(lane == 0, wa, jnp.where(lane == 1, wb, 0.0)), U32))


def _merge(attn4, gm, ga, gb, x, mod, wpa, wpg, wo, g2, wr, br, tm, tiles_per_seq, dispatch):
    t = x.shape[0]
    mod3d = mod.ndim == 3
    nb = tm // Q_BLOCK
    row = lambda i: (i, 0)
    fixed = lambda i: (0, 0)
    if mod3d:
        mod_spec = pl.BlockSpec((1, 1, 6 * D_MODEL), lambda i: (i // tiles_per_seq, 0, 0))
    else:
        mod_spec = pl.BlockSpec((tm, 6 * D_MODEL), row)
    attn_spec = pl.BlockSpec((nb, 1, Q_BLOCK, N_HEADS * HEAD_DIM),
                             lambda i: (i % tiles_per_seq, i // tiles_per_seq, 0, 0))
    if dispatch:
        extra = [(SUB, LANES, U32), (1, LANES, jnp.int32)]
    else:
        extra = [(1, D_MODEL, BF16), (1, LANES, F32)]
    return pl.pallas_call(
        functools.partial(_merge_body, mod3d, dispatch),
        grid=(t // tm,),
        in_specs=[attn_spec, pl.BlockSpec((tm, GM_WIDTH), row),
                  pl.BlockSpec((tm, D_MODEL), row), pl.BlockSpec((tm, D_MODEL), row),
                  pl.BlockSpec((tm, D_MODEL), row), mod_spec,
                  pl.BlockSpec((N_HEADS * HEAD_DIM, D_MODEL), fixed), pl.BlockSpec((GM_WIDTH, D_MODEL), fixed),
                  pl.BlockSpec((D_MODEL, D_MODEL), fixed), pl.BlockSpec((1, D_MODEL), fixed),
                  pl.BlockSpec((D_MODEL, 2 * LANES), fixed), pl.BlockSpec((1, LANES), fixed)],
        out_specs=[pl.BlockSpec((tm, D_MODEL), row)] + [pl.BlockSpec((tm * r, w), row) for r, w, _ in extra],
        out_shape=[jax.ShapeDtypeStruct((t, D_MODEL), F32)]
        + [jax.ShapeDtypeStruct((t * r, w), dt) for r, w, dt in extra],
        compiler_params=_cparams(("arbitrary",)),
        name="merge",
    )(attn4, gm, ga, gb, x, mod, wpa, wpg, wo, g2, wr, br)


def _moe_body(mod3d, h_ref, gate_ref, x1_ref, mod_ref, wgu_ref, wd_ref, o_ref, acc_ref):
    e = pl.program_id(1)

    @pl.when(e == 0)
    def _():
        acc_ref[...] = jnp.zeros_like(acc_ref)

    gu = jnp.dot(h_ref[...], wgu_ref[0], preferred_element_type=F32)
    a = gu[:, 0:D_EXPERT]
    hid = a * jax.nn.sigmoid(a) * gu[:, D_EXPERT:2 * D_EXPERT]
    gate = gate_ref[...]
    lane = lax.broadcasted_iota(jnp.int32, gate.shape, 1)
    ge = jnp.sum(jnp.where(lane == e, gate, 0.0), axis=1, keepdims=True)
    acc_ref[...] += jnp.dot((hid * ge).astype(BF16), wd_ref[0], preferred_element_type=F32)

    @pl.when(e == pl.num_programs(1) - 1)
    def _():
        m = mod_ref[0] if mod3d else mod_ref[...]
        o_ref[...] = x1_ref[...] + m[:, 5 * D_MODEL:6 * D_MODEL] * acc_ref[...]


def _moe(h2, gate, x1, mod, wgu, wd, tm, tiles_per_seq):
    t = h2.shape[0]
    mod3d = mod.ndim == 3
    row = lambda i, e: (i, 0)
    if mod3d:
        mod_spec = pl.BlockSpec((1, 1, 6 * D_MODEL), lambda i, e: (i // tiles_per_seq, 0, 0))
    else:
        mod_spec = pl.BlockSpec((tm, 6 * D_MODEL), row)
    return pl.pallas_call(
        functools.partial(_moe_body, mod3d),
        grid=(t // tm, N_EXPERTS),
        in_specs=[pl.BlockSpec((tm, D_MODEL), row), pl.BlockSpec((tm, LANES), row),
                  pl.BlockSpec((tm, D_MODEL), row), mod_spec,
                  pl.BlockSpec((1, D_MODEL, 2 * D_EXPERT), lambda i, e: (e, 0, 0)),
                  pl.BlockSpec((1, D_EXPERT, D_MODEL), lambda i, e: (e, 0, 0))],
        out_specs=pl.BlockSpec((tm, D_MODEL), row),
        out_shape=jax.ShapeDtypeStruct((t, D_MODEL), F32),
        scratch_shapes=[pltpu.VMEM((tm, D_MODEL), F32)],
        compiler_params=_cparams(("arbitrary", "arbitrary")),
        name="moe",
    )(h2, gate, x1, mod, wgu, wd)


def _slots_body(cls_ref, slot_o, segend_o, carry, seg_start):
    sweep = pl.program_id(0)
    i = pl.program_id(1)
    tm = cls_ref.shape[0]
    lane = lax.broadcasted_iota(jnp.int32, (tm, LANES), 1)
    hit = lane == cls_ref[...]
    onehot = jnp.where(hit, 1.0, 0.0)

    @pl.when(jnp.logical_and(sweep == 0, i == 0))
    def _():
        carry[...] = jnp.zeros_like(carry)

    @pl.when(sweep == 0)
    def _():
        carry[...] = carry[...] + jnp.sum(onehot, axis=0, keepdims=True)
        slot_o[...] = jnp.zeros_like(slot_o)

        @pl.when(i == pl.num_programs(1) - 1)
        def _():
            padded = jnp.floor((carry[...] + (ROW_TILE - 1)) * (1.0 / ROW_TILE)) * ROW_TILE
            ri = lax.broadcasted_iota(jnp.int32, (LANES, LANES), 0)
            ci = lax.broadcasted_iota(jnp.int32, (LANES, LANES), 1)
            upto = jnp.where(ri <= ci, 1.0, 0.0)
            seg_end = jnp.dot(padded, upto, preferred_element_type=F32, precision=lax.Precision.HIGHEST)
            segend_o[...] = seg_end
            seg_start[...] = seg_end - padded
            carry[...] = jnp.zeros_like(carry)

    @pl.when(sweep == 1)
    def _():
        ri = lax.broadcasted_iota(jnp.int32, (tm, tm), 0)
        ci = lax.broadcasted_iota(jnp.int32, (tm, tm), 1)
        earlier = jnp.where(ci < ri, 1.0, 0.0).astype(BF16)
        before = (jnp.dot(earlier, onehot.astype(BF16), preferred_element_type=F32)
                  + carry[0:1, :] + seg_start[0:1, :])
        slot = jnp.sum(jnp.where(hit, before, 0.0), axis=1, keepdims=True)
        slot_o[...] = jnp.broadcast_to(slot, (tm, LANES)).astype(jnp.int32)
        carry[...] = carry[...] + jnp.sum(onehot, axis=0, keepdims=True)


def _class_slots(cls, tm):
    t = cls.shape[0]
    return pl.pallas_call(
        _slots_body,
        grid=(2, t // tm),
        in_specs=[pl.BlockSpec((tm, LANES), lambda s, i: (i, 0))],
        out_specs=[pl.BlockSpec((tm, LANES), lambda s, i: (i * s, 0)),
                   pl.BlockSpec((SUB, LANES), lambda s, i: (0, 0))],
        out_shape=[jax.ShapeDtypeStruct((t, LANES), jnp.int32), jax.ShapeDtypeStruct((SUB, LANES), F32)],
        scratch_shapes=[pltpu.VMEM((SUB, LANES), F32), pltpu.VMEM((SUB, LANES), F32)],
        compiler_params=_cparams(("arbitrary", "arbitrary")),
        name="class_slots",
    )(cls)


INVERT_CHUNK = 4096


def _invert_body(dest_ref, src_o):
    i = pl.program_id(0)

    @pl.when(i == 0)
    def _():
        src_o[...] = jnp.zeros_like(src_o)

    def put(r, carry):
        src_o[pl.ds(dest_ref[0, 0, r], 1), :] = jnp.full((1, LANES), i * INVERT_CHUNK + r, jnp.int32)
        return carry

    lax.fori_loop(0, INVERT_CHUNK, put, 0, unroll=8)


def _invert(dest, n_sorted):
    t = dest.shape[0]
    return pl.pallas_call(
        _invert_body,
        grid=(t // INVERT_CHUNK,),
        in_specs=[pl.BlockSpec((1, 1, INVERT_CHUNK), lambda i: (i, 0, 0), memory_space=pltpu.SMEM)],
        out_specs=pl.BlockSpec((n_sorted, LANES), lambda i: (0, 0)),
        out_shape=jax.ShapeDtypeStruct((n_sorted, LANES), jnp.int32),
        compiler_params=_cparams(("arbitrary",)),
        name="invert_slots",
    )(dest.reshape(t // INVERT_CHUNK, 1, INVERT_CHUNK))


def _token_fetches(idx_ref, tiles_hbm, buf, slot, sem, n):
    return [pltpu.make_async_copy(tiles_hbm.at[pl.ds(pl.multiple_of(idx_ref[0, 0, r] * SUB, SUB), SUB)],
                                  buf.at[slot, pl.ds(r * SUB, SUB)], sem.at[slot])
            for r in range(n)]


def _start_all(copies):
    for cp in copies:
        cp.start(priority=1)


def _tile_fetches(src_ref, rows_hbm, buf, slot, sem):
    return _token_fetches(src_ref, rows_hbm, buf, slot, sem, ROW_TILE)


FETCH_AHEAD = 2


def _experts_body(grp_ref, ea_ref, eb_ref, used_ref, s0_ref, s1_ref, s2_ref, rows_hbm, wgu_ref, wd_ref,
                  o_ref, buf, sem):
    j = pl.program_id(0)
    n_buf = FETCH_AHEAD + 1
    slot = j % n_buf
    n_used = used_ref[0]

    @pl.when(j == 0)
    def _():
        _start_all(_tile_fetches(s0_ref, rows_hbm, buf, 0, sem))

    @pl.when(jnp.logical_and(j == 0, 1 < n_used))
    def _():
        _start_all(_tile_fetches(s1_ref, rows_hbm, buf, 1, sem))

    @pl.when(j + FETCH_AHEAD < n_used)
    def _():
        _start_all(_tile_fetches(s2_ref, rows_hbm, buf, (j + FETCH_AHEAD) % n_buf, sem))

    @pl.when(jnp.logical_or(j < n_used, j == 0))
    def _():
        for cp in _tile_fetches(s0_ref, rows_hbm, buf, slot, sem):
            cp.wait()

    @pl.when(j < n_used)
    def _():
        tiles = buf.at[slot]
        x = _unpack_halves(_load_token_words(tiles, ROW_TILE)).astype(BF16)
        wts = lax.bitcast_convert_type(tiles[pl.ds(FEAT_SUB, ROW_TILE, stride=SUB), :], F32)

        def hidden(e, wgt):
            gu = jnp.dot(x, wgu_ref[e], preferred_element_type=F32)
            a = gu[:, 0:D_EXPERT]
            return (a * jax.nn.sigmoid(a) * gu[:, D_EXPERT:2 * D_EXPERT] * wgt).astype(BF16)

        ea, eb = ea_ref[j], eb_ref[j]
        y = (jnp.dot(hidden(ea, wts[:, 0:1]), wd_ref[ea], preferred_element_type=F32)
             + jnp.dot(hidden(eb, wts[:, 1:2]), wd_ref[eb], preferred_element_type=F32))
        _store_token_tiles(o_ref, _pack_halves(y.astype(BF16).astype(F32)), None)

    @pl.when(j >= n_used)
    def _():
        o_ref[...] = jnp.zeros_like(o_ref)


def _experts(grp_t, ea_t, eb_t, n_used, src3, rows, wgu, wd):
    n_tiles = src3.shape[0]
    w_grp = lambda j, grp, ea, eb, nu: (grp[j], 0, 0)
    smem_blk = lambda f: pl.BlockSpec((1, 1, ROW_TILE), f, memory_space=pltpu.SMEM)
    grid_spec = pltpu.PrefetchScalarGridSpec(
        num_scalar_prefetch=4,
        grid=(n_tiles,),
        in_specs=[smem_blk(lambda j, grp, ea, eb, nu: (j, 0, 0)),
                  smem_blk(lambda j, grp, ea, eb, nu: (jnp.minimum(j + 1, n_tiles - 1), 0, 0)),
                  smem_blk(lambda j, grp, ea, eb, nu: (jnp.minimum(j + FETCH_AHEAD, n_tiles - 1), 0, 0)),
                  pl.BlockSpec(memory_space=pl.ANY),
                  pl.BlockSpec((EXPERTS_PER_GROUP, D_MODEL, 2 * D_EXPERT), w_grp),
                  pl.BlockSpec((EXPERTS_PER_GROUP, D_EXPERT, D_MODEL), w_grp)],
        out_specs=pl.BlockSpec((ROW_TILE * SUB, LANES), lambda j, grp, ea, eb, nu: (j, 0)),
        scratch_shapes=[pltpu.VMEM((FETCH_AHEAD + 1, ROW_TILE * SUB, LANES), U32),
                        pltpu.SemaphoreType.DMA((FETCH_AHEAD + 1,))],
    )
    return pl.pallas_call(
        _experts_body,
        grid_spec=grid_spec,
        out_shape=jax.ShapeDtypeStruct((n_tiles * ROW_TILE * SUB, LANES), U32),
        compiler_params=_cparams(("arbitrary",)),
        name="experts",
    )(grp_t, ea_t, eb_t, n_used, src3, src3, src3, rows, wgu, wd)


def _row_fetches(dest_ref, ys_hbm, buf, slot, sem):
    return _token_fetches(dest_ref, ys_hbm, buf, slot, sem, MOVE_TILE)


def _combine_body(mod3d, dcur_ref, dnext_ref, ys_hbm, x1_ref, mod_ref, o_ref, buf, sem):
    i = pl.program_id(0)
    slot = i % 2

    @pl.when(i == 0)
    def _():
        _start_all(_row_fetches(dcur_ref, ys_hbm, buf, 0, sem))

    @pl.when(i + 1 < pl.num_programs(0))
    def _():
        _start_all(_row_fetches(dnext_ref, ys_hbm, buf, 1 - slot, sem))

    for cp in _row_fetches(dcur_ref, ys_hbm, buf, slot, sem):
        cp.wait()
    m = mod_ref[0] if mod3d else mod_ref[...]
    o_ref[...] = x1_ref[...] + m[:, 5 * D_MODEL:6 * D_MODEL] * _unpack_halves(_load_token_words(buf.at[slot], MOVE_TILE))


def _combine(dest3, ys, x1, mod, tiles_per_seq):
    steps = dest3.shape[0]
    t = x1.shape[0]
    mod3d = mod.ndim == 3
    row = lambda i: (i, 0)
    if mod3d:
        mod_spec = pl.BlockSpec((1, 1, 6 * D_MODEL), lambda i: (i // tiles_per_seq, 0, 0))
    else:
        mod_spec = pl.BlockSpec((MOVE_TILE, 6 * D_MODEL), row)
    smem_blk = lambda f: pl.BlockSpec((1, 1, MOVE_TILE), f, memory_space=pltpu.SMEM)
    return pl.pallas_call(
        functools.partial(_combine_body, mod3d),
        grid=(steps,),
        in_specs=[smem_blk(lambda i: (i, 0, 0)), smem_blk(lambda i: (jnp.minimum(i + 1, steps - 1), 0, 0)),
                  pl.BlockSpec(memory_space=pl.ANY),
                  pl.BlockSpec((MOVE_TILE, D_MODEL), row), mod_spec],
        out_specs=pl.BlockSpec((MOVE_TILE, D_MODEL), row),
        out_shape=jax.ShapeDtypeStruct((t, D_MODEL), F32),
        scratch_shapes=[pltpu.VMEM((2, MOVE_TILE * SUB, LANES), U32), pltpu.SemaphoreType.DMA((2,))],
        compiler_params=_cparams(("arbitrary",)),
        name="combine",
    )(dest3, dest3, ys, x1, mod)


def _class_expert_tables():
    ea, eb = [], []
    for g in range(N_EXPERT_GROUPS):
        for a in range(EXPERTS_PER_GROUP):
            for b in range(a + 1, EXPERTS_PER_GROUP):
                ea.append(g * EXPERTS_PER_GROUP + a)
                eb.append(g * EXPERTS_PER_GROUP + b)
    return np.asarray(ea, np.int32), np.asarray(eb, np.int32)


def _moe_dispatched(rows, cls, x1, mod, wgu, wd, tiles_per_seq):
    t = x1.shape[0]
    slots, seg_end8 = _class_slots(cls, 1024)
    seg_end = seg_end8[0, 0:N_CLASSES].astype(jnp.int32)
    dest = slots[:, 0]
    dest3 = dest.reshape(t // MOVE_TILE, 1, MOVE_TILE)
    n_sorted = t + N_CLASSES * ROW_TILE
    tile_row0 = jnp.arange(n_sorted // ROW_TILE, dtype=jnp.int32) * ROW_TILE
    tile_cls = jnp.minimum(jnp.sum((seg_end[None, :] <= tile_row0[:, None]).astype(jnp.int32), axis=1),
                           N_CLASSES - 1)
    ea_np, eb_np = _class_expert_tables()
    in_cls = (tile_cls[:, None] == jnp.arange(N_CLASSES, dtype=jnp.int32)[None, :]).astype(jnp.int32)
    ea_t = jnp.sum(in_cls * jnp.asarray(ea_np % EXPERTS_PER_GROUP)[None, :], axis=1)
    eb_t = jnp.sum(in_cls * jnp.asarray(eb_np % EXPERTS_PER_GROUP)[None, :], axis=1)
    grp_t = tile_cls // N_PAIRS
    n_used = (seg_end[N_CLASSES - 1] // ROW_TILE).astype(jnp.int32).reshape(1)
    src = _invert(dest, n_sorted)[:, 0]
    ys = _experts(grp_t, ea_t, eb_t, n_used, src.reshape(n_sorted // ROW_TILE, 1, ROW_TILE), rows, wgu, wd)
    return _combine(dest3, ys, x1, mod, tiles_per_seq)


def _pad_lanes(v, fill):
    n = v.shape[-1]
    return jnp.concatenate([v, jnp.full((LANES - n,), fill, v.dtype)]).reshape(1, LANES)


def kernel(x_prompt, x_sample, c_prompt, c_sample, cache_k, cache_v, cache_kidx, page_table, w_ada, b_ada, norm_mix_g, norm_ffn_g, w_in, q_norm_g, k_norm_g, kidx_norm_g, gm_ln_g, gm_ln_b, gm_spatial_w, gm_spatial_b, w_proj_attn, w_proj_gmlp, w_out, w_router_group, b_router_group, w_router_expert, b_router_expert, w_expert_gate, w_expert_up, w_expert_down):
    depth = w_ada.shape[0]
    assert depth == 1
    l = 0
    bp, sp, _ = x_prompt.shape
    bs, ss, _ = x_sample.shape
    assert ss == 1
    n_pages = page_table.shape[1]
    past = n_pages * PAGE_SIZE
    tp = bp * sp

    w = w_in[l]
    zpad = jnp.zeros((D_MODEL, LANES - IDX_DIM - IDX_HEADS), F32)
    w_pad = jnp.concatenate([w[:, 0:1024], w[:, 1024:1088], w[:, 1088:1092], zpad, w[:, 1092:]], axis=1).astype(BF16)
    seg_np = (np.arange(LANES)[:, None] // HEAD_DIM) == (np.arange(LANES)[None, :] // HEAD_DIM)
    seg = jnp.asarray(seg_np, BF16)
    segki = jnp.asarray(seg_np & (np.arange(LANES)[:, None] < IDX_DIM) & (np.arange(LANES)[None, :] < IDX_DIM), BF16)
    consts = (norm_mix_g[l].reshape(1, D_MODEL),
              jnp.tile(q_norm_g[l], 2).reshape(1, LANES), jnp.tile(k_norm_g[l], 2).reshape(1, LANES),
              _pad_lanes(kidx_norm_g[l], 1.0),
              gm_ln_g[l].reshape(1, GM_WIDTH), gm_ln_b[l].reshape(1, GM_WIDTH), seg, segki)
    wpa = w_proj_attn[l].astype(BF16)
    wpg = w_proj_gmlp[l].astype(BF16)
    wo = w_out[l].astype(BF16)
    wr32 = jnp.concatenate([w_router_expert[l], w_router_group[l],
                            jnp.zeros((D_MODEL, LANES - N_EXPERTS - N_EXPERT_GROUPS), F32)], axis=1)
    wr_hi = wr32.astype(BF16)
    wr_lo = (wr32 - wr_hi.astype(F32)).astype(BF16)
    wr = jnp.concatenate([wr_hi, wr_lo], axis=1)
    br = _pad_lanes(jnp.concatenate([b_router_expert[l], b_router_group[l]]), 0.0)
    wgu = jnp.concatenate([w_expert_gate[l], w_expert_up[l]], axis=2).astype(BF16)
    wd = w_expert_down[l].astype(BF16)
    g2 = norm_ffn_g[l].reshape(1, D_MODEL)

    mod = _adaln(jnp.concatenate([c_prompt, c_sample], axis=0), w_ada[l], b_ada[l])
    mod_p = mod[0:bp].reshape(bp, 1, 6 * D_MODEL)
    mod_s = mod[bp:bp + bs]
    pos = jnp.concatenate([jnp.arange(sp, dtype=jnp.int32),
                           jnp.full((8,), past, jnp.int32)]).astype(F32).reshape(sp + 8, 1)
    tabs = _rope_tables(pos)
    tabs_p = tuple(t[0:sp] for t in tabs)
    tabs_s = tuple(t[sp:sp + 1] for t in tabs)

    tm = 512
    tps = sp // tm
    (q, kb, vb, qi, kib, kiwi, u, vg, ga, gb, kt_f, vt_f, kit_f, vt) = _project(
        x_prompt.reshape(tp, D_MODEL), mod_p, tabs_p, consts, w_pad, tm, tps, BF16)
    topk_p = min(TOPK_MAX, sp // 4)
    r3 = lambda a: a.reshape(bp, sp, a.shape[-1])
    attn_p = _prompt_attention(r3(qi), r3(kiwi), r3(kib), r3(q), r3(kb), vt, topk_p)
    bt = jnp.concatenate([gm_spatial_b[l].T, jnp.zeros((CHUNK, LANES - GM_GROUPS), F32)], axis=1)
    gm_p = _gmlp_prompt(u, vg, gm_spatial_w[l], bt, tm)
    x1_p, rows_p, cls_p = _merge(attn_p, gm_p, ga, gb, x_prompt.reshape(tp, D_MODEL), mod_p,
                                 wpa, wpg, wo, g2, wr, br, tm, tps, True)
    y_p = _moe_dispatched(rows_p, cls_p, x1_p, mod_p, wgu, wd, sp // MOVE_TILE)

    (q_s, kb_s, vb_s, qi_s, kib_s, kiwi_s, u_s, vg_s, ga_s, gb_s, kt_s, vt_s, kit_s, _) = _project(
        x_sample.reshape(bs, D_MODEL), mod_s, tabs_s, consts, w_pad, bs, 1, F32)
    qi8 = jnp.concatenate([qi_s.reshape(bs, IDX_HEADS, IDX_DIM),
                           jnp.zeros((bs, 8 - IDX_HEADS, IDX_DIM), BF16)], axis=1)
    wi8 = jnp.concatenate([kiwi_s[:, IDX_DIM:IDX_DIM + IDX_HEADS],
                           jnp.zeros((bs, 8 - IDX_HEADS), F32)], axis=1).reshape(bs, 8, 1)
    kidx_t = jnp.transpose(cache_kidx[l], (0, 2, 1))
    scores = _sample_scores(page_table, kidx_t, qi8, wi8, kib_s.reshape(bs, 1, IDX_DIM))
    topk_s = min(TOPK_MAX, (past + ss) // 4)
    bias = _sample_select(scores.reshape(bs, past + LANES), topk_s).reshape(bs, 1, past + LANES)
    kvw = N_KV_HEADS * HEAD_DIM
    k_t = jnp.transpose(cache_k[l], (0, 2, 3, 1)).reshape(-1, kvw, PAGE_SIZE)
    v_t = jnp.transpose(cache_v[l], (0, 2, 3, 1)).reshape(-1, kvw, PAGE_SIZE)
    q3 = q_s.reshape(bs, N_HEADS, HEAD_DIM)
    zq = jnp.zeros_like(q3)
    in_first = (jnp.arange(N_HEADS) < N_HEADS // N_KV_HEADS)[None, :, None]
    q2_s = jnp.where(in_first, jnp.concatenate([q3, zq], axis=2), jnp.concatenate([zq, q3], axis=2))
    attn_s = _sample_attention(page_table, k_t, v_t, q2_s, bias,
                               kb_s.reshape(bs, 1, kvw), vb_s.reshape(bs, 1, kvw))
    gd = GM_WIDTH // GM_GROUPS
    w0 = jnp.repeat(gm_spatial_w[l][:, 0, 0], gd).reshape(1, GM_WIDTH)
    b0 = jnp.repeat(gm_spatial_b[l][:, 0], gd).reshape(1, GM_WIDTH)
    gm_s = _gmlp_sample(u_s, vg_s, w0, b0)
    x1_s, h2_s, gate_s = _merge(attn_s.reshape(1, 1, bs, N_HEADS * HEAD_DIM), gm_s, ga_s, gb_s,
                                x_sample.reshape(bs, D_MODEL), mod_s, wpa, wpg, wo, g2, wr, br, bs, 1, False)
    y_s = _moe(h2_s, gate_s, x1_s, mod_s, wgu, wd, bs, 1)

    def rows_kv(a_t, n, s):
        return jnp.transpose(a_t.reshape(n, N_KV_HEADS, HEAD_DIM, s), (0, 3, 1, 2))[None]

    def rows_ki(a_t):
        return jnp.transpose(a_t, (0, 2, 1))[None]

    return (y_p.reshape(bp, sp, D_MODEL), y_s.reshape(bs, ss, D_MODEL),
            rows_kv(kt_f, bp, sp), rows_kv(vt_f, bp, sp), rows_ki(kit_f),
            rows_kv(kt_s, 1, bs).reshape(1, bs, ss, N_KV_HEADS, HEAD_DIM),
            rows_kv(vt_s, 1, bs).reshape(1, bs, ss, N_KV_HEADS, HEAD_DIM),
            rows_ki(kit_s).reshape(1, bs, ss, IDX_DIM), vg_s.reshape(1, bs, ss, GM_WIDTH))
```

```python
import functools

import numpy as np
import jax
import jax.numpy as jnp
from jax import lax
from jax.experimental import pallas as pl
from jax.experimental.pallas import tpu as pltpu

F32 = jnp.float32
BF16 = jnp.bfloat16
U32 = jnp.uint32

D_MODEL = 1024
N_HEADS = 8
HEAD_DIM = 64
N_KV_HEADS = 2
ROT_DIM = 16
ROPE_THETA = 500000.0
IDX_HEADS = 4
IDX_DIM = 64
TOPK_MAX = 256
Q_BLOCK = 128
GM_WIDTH = 512
GM_GROUPS = 8
CHUNK = 128
N_EXPERT_GROUPS = 4
EXPERTS_PER_GROUP = 8
N_EXPERTS = 32
D_EXPERT = 256
EPS = 1e-6
PAGE_SIZE = 128
LANES = 128
SUB = 8

C_Q, C_K, C_V, C_QI, C_KIWI, C_U, C_VG, C_GATE = 0, 512, 640, 768, 1024, 1152, 1664, 2176
D_IN_PAD = 4224
VMEM_LIMIT = 56 * 1024 * 1024

N_PAIRS = EXPERTS_PER_GROUP * (EXPERTS_PER_GROUP - 1) // 2
N_CLASSES = N_EXPERT_GROUPS * N_PAIRS
ROW_TILE = 128
HALF = D_MODEL // 2
FEAT_SUB = HALF // LANES
MOVE_TILE = 256
HI_MASK = 0xFFFF0000


def _cparams(sem):
    return pltpu.CompilerParams(dimension_semantics=sem, vmem_limit_bytes=VMEM_LIMIT)


def _pack_halves(x):
    n = x.shape[1] // 2
    hi = lax.bitcast_convert_type(x[:, 0:n], U32) & jnp.uint32(HI_MASK)
    lo = lax.bitcast_convert_type(x[:, n:2 * n], U32) >> 16
    return hi | lo


def _unpack_halves(u):
    hi = lax.bitcast_convert_type(u & jnp.uint32(HI_MASK), F32)
    lo = lax.bitcast_convert_type(u << 16, F32)
    return jnp.concatenate([hi, lo], axis=1)


def _store_token_tiles(ref, words, extra):
    n = words.shape[0]
    for s in range(FEAT_SUB):
        ref[pl.ds(s, n, stride=SUB), :] = words[:, s * LANES:(s + 1) * LANES]
    zero = jnp.zeros((n, LANES), U32)
    ref[pl.ds(FEAT_SUB, n, stride=SUB), :] = zero if extra is None else extra
    for s in range(FEAT_SUB + 1, SUB):
        ref[pl.ds(s, n, stride=SUB), :] = zero


def _load_token_words(ref, n):
    return jnp.concatenate([ref[pl.ds(s, n, stride=SUB), :] for s in range(FEAT_SUB)], axis=1)


def _adaln_body(c_ref, w_ref, b_ref, o_ref):
    c = c_ref[...]
    a = c * jax.nn.sigmoid(c)
    o_ref[...] = jnp.dot(a, w_ref[...], preferred_element_type=F32,
                         precision=lax.Precision.HIGHEST) + b_ref[...]


def _adaln(c, w, b):
    r = c.shape[0]
    n = w.shape[1]
    bn = 1536
    return pl.pallas_call(
        _adaln_body,
        grid=(n // bn,),
        in_specs=[pl.BlockSpec((r, D_MODEL), lambda j: (0, 0)),
                  pl.BlockSpec((D_MODEL, bn), lambda j: (0, j)),
                  pl.BlockSpec((1, bn), lambda j: (0, j))],
        out_specs=pl.BlockSpec((r, bn), lambda j: (0, j)),
        out_shape=jax.ShapeDtypeStruct((r, n), F32),
        compiler_params=_cparams(("arbitrary",)),
        name="adaln",
    )(c, w, b.reshape(1, n))


def _rope_table_body(pos_ref, invf_ref, sa_m_ref, sb_m_ref, c_ref, sa_ref, sb_ref):
    ang = pos_ref[...] * invf_ref[...]
    s = jnp.sin(ang)
    c_ref[...] = jnp.cos(ang)
    sa_ref[...] = s * sa_m_ref[...]
    sb_ref[...] = s * sb_m_ref[...]


def _rope_tables(pos):
    half = ROT_DIM // 2
    inv_freq = ROPE_THETA ** (-jnp.arange(half, dtype=F32) / half)
    d = np.arange(LANES) % HEAD_DIM
    invf = jnp.where(jnp.asarray(d < ROT_DIM), inv_freq[d % half], 0.0).reshape(1, LANES)
    sa_m = jnp.asarray(np.where(d < half, -1.0, 0.0), F32).reshape(1, LANES)
    sb_m = jnp.asarray(np.where((d >= half) & (d < ROT_DIM), 1.0, 0.0), F32).reshape(1, LANES)
    r = pos.shape[0]
    return pl.pallas_call(
        _rope_table_body,
        out_shape=[jax.ShapeDtypeStruct((r, LANES), F32)] * 3,
        name="rope_tables",
    )(pos, invf, sa_m, sb_m)


def _rope(y, c, sa, sb):
    return y * c + pltpu.roll(y, LANES - ROT_DIM // 2, 1) * sa + pltpu.roll(y, ROT_DIM // 2, 1) * sb


def _seg_rms(r, seg, gain):
    ss = r * r
    hi = ss.astype(BF16)
    lo = (ss - hi.astype(F32)).astype(BF16)
    tot = jnp.dot(hi, seg, preferred_element_type=F32) + jnp.dot(lo, seg, preferred_element_type=F32)
    return r * lax.rsqrt(tot * (1.0 / HEAD_DIM) + EPS) * gain


def _proj_body(mod3d, x_ref, mod_ref, g_ref, w_ref, c_ref, sa_ref, sb_ref, qg_ref, kg_ref, kig_ref,
               lng_ref, lnb_ref, seg_ref, segki_ref,
               q_o, kb_o, vb_o, qi_o, kib_o, kiwi_o, u_o, vg_o, ga_o, gb_o, kt_o, vtf_o, kit_o, vt_o):
    x = x_ref[...]
    m = mod_ref[0] if mod3d else mod_ref[...]
    shift1 = m[:, 0:D_MODEL]
    scale1 = m[:, D_MODEL:2 * D_MODEL]
    ms = jnp.mean(x * x, axis=-1, keepdims=True)
    y = x * lax.rsqrt(ms + EPS) * g_ref[...]
    h = (y * (1.0 + scale1) + shift1).astype(BF16)
    c, sa, sb = c_ref[...], sa_ref[...], sb_ref[...]
    seg = seg_ref[...]

    def proj(a, b):
        return jnp.dot(h, w_ref[:, a:b], preferred_element_type=F32)

    for g2 in range(2):
        r2 = proj(C_Q + g2 * 2 * LANES, C_Q + (g2 + 1) * 2 * LANES)
        for g in (2 * g2, 2 * g2 + 1):
            r = r2[:, (g % 2) * LANES:(g % 2 + 1) * LANES]
            yq = _rope(_seg_rms(r, seg, qg_ref[...]), c, sa, sb) * (HEAD_DIM ** -0.5)
            q_o[:, g * LANES:(g + 1) * LANES] = yq.astype(BF16)
    r_kv = proj(C_K, C_V + LANES)
    yk = _rope(_seg_rms(r_kv[:, 0:LANES], seg, kg_ref[...]), c, sa, sb)
    kt_o[0] = yk.T
    kb_o[...] = yk.astype(BF16)
    r = r_kv[:, LANES:2 * LANES]
    r_t = r.T
    vtf_o[0] = r_t
    vb_o[...] = r.astype(BF16)
    vt_o[0] = r_t.astype(BF16)
    r2 = proj(C_QI, C_QI + 2 * LANES)
    for g in range(2):
        yqi = _rope(r2[:, g * LANES:(g + 1) * LANES], c, sa, sb) * (IDX_DIM ** -0.5 * IDX_HEADS ** -0.5)
        qi_o[:, g * LANES:(g + 1) * LANES] = yqi.astype(BF16)
    r = proj(C_KIWI, C_KIWI + LANES)
    yki = _rope(_seg_rms(r, segki_ref[...], kig_ref[...]), c, sa, sb)
    lane = lax.broadcasted_iota(jnp.int32, r.shape, 1)
    kiwi = jnp.where(lane < IDX_DIM, yki, r)
    kiwi_o[...] = kiwi
    kit_o[0] = kiwi.T[0:IDX_DIM, :]
    kib_o[...] = kiwi[:, 0:IDX_DIM].astype(BF16)
    r = proj(C_U, C_U + GM_WIDTH)
    u_o[...] = jax.nn.gelu(r).astype(BF16)
    r = proj(C_VG, C_VG + GM_WIDTH)
    gl = jax.nn.gelu(r)
    mu = jnp.mean(gl, axis=-1, keepdims=True)
    dv = gl - mu
    var = jnp.mean(dv * dv, axis=-1, keepdims=True)
    vg_o[...] = (dv * lax.rsqrt(var + EPS) * lng_ref[...] + lnb_ref[...]).astype(vg_o.dtype)
    r = proj(C_GATE, C_GATE + D_MODEL)
    ga_o[...] = jax.nn.sigmoid(r).astype(BF16)
    r = proj(C_GATE + D_MODEL, C_GATE + 2 * D_MODEL)
    gb_o[...] = jax.nn.sigmoid(r).astype(BF16)


def _project(x, mod, tables, consts, w_pad, tm, tiles_per_seq, vg_dtype):
    t = x.shape[0]
    mod3d = mod.ndim == 3
    c_t, sa_t, sb_t = tables
    g_mix, qg, kg, kig, lng, lnb, seg, segki = consts
    row = lambda i: (i, 0)
    fixed = lambda i: (0, 0)
    if mod3d:
        mod_spec = pl.BlockSpec((1, 1, 6 * D_MODEL), lambda i: (i // tiles_per_seq, 0, 0))
        tab_spec = pl.BlockSpec((tm, LANES), lambda i: (i % tiles_per_seq, 0))
    else:
        mod_spec = pl.BlockSpec((tm, 6 * D_MODEL), row)
        tab_spec = pl.BlockSpec((1, LANES), fixed)
    widths = [(512, BF16), (128, BF16), (128, BF16), (256, BF16),
              (64, BF16), (128, F32), (512, BF16), (512, vg_dtype), (1024, BF16), (1024, BF16)]
    n_seq, seq = t // (tm * tiles_per_seq), tm * tiles_per_seq
    t_idx = lambda i: (i // tiles_per_seq, 0, i % tiles_per_seq)
    t_outs = [(LANES, F32), (LANES, F32), (IDX_DIM, F32), (LANES, BF16)]
    return pl.pallas_call(
        functools.partial(_proj_body, mod3d),
        grid=(t // tm,),
        in_specs=[pl.BlockSpec((tm, D_MODEL), row), mod_spec,
                  pl.BlockSpec((1, D_MODEL), fixed),
                  pl.BlockSpec((D_MODEL, D_IN_PAD), fixed),
                  tab_spec, tab_spec, tab_spec,
                  pl.BlockSpec((1, LANES), fixed), pl.BlockSpec((1, LANES), fixed), pl.BlockSpec((1, LANES), fixed),
                  pl.BlockSpec((1, GM_WIDTH), fixed), pl.BlockSpec((1, GM_WIDTH), fixed),
                  pl.BlockSpec((LANES, LANES), fixed), pl.BlockSpec((LANES, LANES), fixed)],
        out_specs=[pl.BlockSpec((tm, w), row) for w, _ in widths]
        + [pl.BlockSpec((1, f, tm), t_idx) for f, _ in t_outs],
        out_shape=[jax.ShapeDtypeStruct((t, w), dt) for w, dt in widths]
        + [jax.ShapeDtypeStruct((n_seq, f, seq), dt) for f, dt in t_outs],
        compiler_params=_cparams(("arbitrary",)),
        name="project",
    )(x, mod, g_mix, w_pad, c_t, sa_t, sb_t, qg, kg, kig, lng, lnb, seg, segki)


def _select_bias(s_ref, bias_ref, rows, width, topk, tie_check_start=16):
    nb = width // LANES
    kf = float(topk)
    neg, pos = -jnp.inf, jnp.inf

    def blk(j):
        return s_ref[:, j * LANES:(j + 1) * LANES]

    def count_above(t):
        tb = jnp.broadcast_to(t, (rows, LANES))
        acc = jnp.zeros((rows, LANES), F32)
        for j in range(nb):
            acc = acc + jnp.where(blk(j) > tb, 1.0, 0.0)
        return jnp.sum(acc, axis=1, keepdims=True)

    mx = jnp.full((rows, LANES), neg, F32)
    mn = jnp.full((rows, LANES), pos, F32)
    for j in range(nb):
        b = blk(j)
        mx = jnp.maximum(mx, b)
        mn = jnp.minimum(mn, jnp.where(b == neg, pos, b))
    hi0 = jnp.max(mx, axis=1, keepdims=True)
    smin = jnp.min(mn, axis=1, keepdims=True)
    lo0 = smin - jnp.abs(smin) - 1.0
    f_lo0 = count_above(lo0)
    zeros = jnp.zeros((rows, 1), F32)

    def active_of(f_lo, tie):
        return jnp.logical_and(f_lo > kf, tie == 0.0)

    def cond(st):
        _, _, _, f_lo, _, tie = st
        return jnp.max(jnp.where(active_of(f_lo, tie), 1.0, 0.0)) > 0.0

    def body(st):
        it, lo, hi, f_lo, f_hi, tie = st
        active = active_of(f_lo, tie)
        mid = lo + (hi - lo) * 0.5
        stuck = jnp.logical_or(mid <= lo, mid >= hi)
        cnt = count_above(mid)
        ge = cnt >= kf
        up_lo = jnp.logical_and(active, ge)
        up_hi = jnp.logical_and(active, jnp.logical_not(ge))
        lo = jnp.where(up_lo, mid, lo)
        f_lo = jnp.where(up_lo, cnt, f_lo)
        hi = jnp.where(up_hi, mid, hi)
        f_hi = jnp.where(up_hi, cnt, f_hi)
        tie = jnp.where(jnp.logical_and(active, stuck), 1.0, tie)

        def tie_check(_):
            lob = jnp.broadcast_to(lo, (rows, LANES))
            hib = jnp.broadcast_to(hi, (rows, LANES))
            vmx = jnp.full((rows, LANES), neg, F32)
            vmn = jnp.full((rows, LANES), pos, F32)
            for j in range(nb):
                b = blk(j)
                inn = jnp.logical_and(b > lob, b <= hib)
                vmx = jnp.maximum(vmx, jnp.where(inn, b, neg))
                vmn = jnp.minimum(vmn, jnp.where(inn, b, pos))
            one_value = jnp.max(vmx, axis=1, keepdims=True) == jnp.min(vmn, axis=1, keepdims=True)
            return jnp.where(one_value, 1.0, tie)

        run_check = jnp.logical_and(it >= tie_check_start, it % 4 == 0)
        tie = lax.cond(run_check, tie_check, lambda _: tie, 0)
        return it + 1, lo, hi, f_lo, f_hi, tie

    _, lo, hi, f_lo, f_hi, _ = lax.while_loop(cond, body, (jnp.int32(0), lo0, hi0, f_lo0, zeros, zeros))
    lob = jnp.broadcast_to(lo, (rows, LANES))
    need_prefix = jnp.max(jnp.where(f_lo > kf, 1.0, 0.0)) > 0.0

    @pl.when(jnp.logical_not(need_prefix))
    def _():
        for j in range(nb):
            bias_ref[:, j * LANES:(j + 1) * LANES] = jnp.where(blk(j) > lob, 0.0, neg)

    @pl.when(need_prefix)
    def _():
        hib = jnp.broadcast_to(hi, (rows, LANES))
        need = kf - f_hi
        ri = lax.broadcasted_iota(jnp.int32, (LANES, LANES), 0)
        ci = lax.broadcasted_iota(jnp.int32, (LANES, LANES), 1)
        upper = jnp.where(ri < ci, 1.0, 0.0).astype(BF16)
        off = jnp.zeros((rows, 1), F32)
        for j in range(nb):
            b = blk(j)
            inn = jnp.logical_and(b > lob, b <= hib)
            innf = jnp.where(inn, 1.0, 0.0)
            before = jnp.dot(innf.astype(BF16), upper, preferred_element_type=F32) + off
            sel = jnp.logical_or(b > hib, jnp.logical_and(inn, before < need))
            bias_ref[:, j * LANES:(j + 1) * LANES] = jnp.where(sel, 0.0, neg)
            off = off + jnp.sum(innf, axis=1, keepdims=True)


_NT = (((1,), (1,)), ((), ()))


def _sub_reduce(x, op):
    for sh in (4, 2, 1):
        x = op(x, pltpu.roll(x, sh, 0))
    return x


def _select_bias_t(s_ref, bias_ref, width, topk, n_adm, steps_per_check=4, tie_check_from=4):
    rb = 64
    nb = width // rb
    kf = float(topk)
    neg, pos = -jnp.inf, jnp.inf

    def blk(j):
        return s_ref[j * rb:(j + 1) * rb, :]

    def fold(x, op):
        y = x[0:SUB]
        for a in range(1, rb // SUB):
            y = op(y, x[a * SUB:(a + 1) * SUB])
        return _sub_reduce(y, op)

    def tile(v):
        return jnp.concatenate([v] * (rb // SUB), axis=0)

    def count_above(t):
        tb = tile(t)
        acc = jnp.zeros((rb, LANES), F32)
        for j in range(nb):
            acc = acc + jnp.where(blk(j) > tb, 1.0, 0.0)
        return fold(acc, jnp.add)

    mx = jnp.full((rb, LANES), neg, F32)
    mn = jnp.full((rb, LANES), pos, F32)
    for j in range(nb):
        b = blk(j)
        mx = jnp.maximum(mx, b)
        mn = jnp.minimum(mn, jnp.where(b == neg, pos, b))
    hi0 = fold(mx, jnp.maximum)
    smin = fold(mn, jnp.minimum)
    lo0 = jnp.minimum(smin - jnp.abs(smin) * (2.0 ** -10) - 1e-30, smin - (hi0 - smin) * (2.0 ** -10))
    zeros = jnp.zeros((SUB, LANES), F32)

    def active_of(f_lo, tie):
        return jnp.logical_and(f_lo > kf, tie == 0.0)

    def any_lane(cond):
        return jnp.max(jnp.where(cond, 1.0, 0.0)) > 0.0

    def step(lo, hi, f_lo, f_hi, tie):
        active = active_of(f_lo, tie)
        mid = lo + (hi - lo) * 0.5
        stuck = jnp.logical_or(mid <= lo, mid >= hi)
        cnt = count_above(mid)
        ge = cnt >= kf
        up_lo = jnp.logical_and(active, ge)
        up_hi = jnp.logical_and(active, jnp.logical_not(ge))
        return (jnp.where(up_lo, mid, lo), jnp.where(up_hi, mid, hi), jnp.where(up_lo, cnt, f_lo),
                jnp.where(up_hi, cnt, f_hi), jnp.where(jnp.logical_and(active, stuck), 1.0, tie))

    def cond(st):
        _, _, _, f_lo, _, tie = st
        return any_lane(active_of(f_lo, tie))

    def body(st):
        it, lo, hi, f_lo, f_hi, tie = st
        for _ in range(steps_per_check):
            lo, hi, f_lo, f_hi, tie = step(lo, hi, f_lo, f_hi, tie)

        def tie_check(_):
            lob, hib = tile(lo), tile(hi)
            vmx = jnp.full((rb, LANES), neg, F32)
            vmn = jnp.full((rb, LANES), pos, F32)
            for j in range(nb):
                b = blk(j)
                inn = jnp.logical_and(b > lob, b <= hib)
                vmx = jnp.maximum(vmx, jnp.where(inn, b, neg))
                vmn = jnp.minimum(vmn, jnp.where(inn, b, pos))
            return jnp.where(fold(vmx, jnp.maximum) == fold(vmn, jnp.minimum), 1.0, tie)

        run_check = jnp.logical_and(it + 1 >= tie_check_from, any_lane(active_of(f_lo, tie)))
        tie = lax.cond(run_check, tie_check, lambda _: tie, 0)
        return it + 1, lo, hi, f_lo, f_hi, tie

    _, lo, hi, f_lo, f_hi, _ = lax.while_loop(cond, body, (jnp.int32(0), lo0, hi0, n_adm, zeros, zeros))
    lob = tile(lo)
    need_prefix = any_lane(f_lo > kf)

    @pl.when(jnp.logical_not(need_prefix))
    def _():
        for j in range(nb):
            bias_ref[j * rb:(j + 1) * rb, :] = jnp.where(blk(j) > lob, 0.0, neg)

    @pl.when(need_prefix)
    def _():
        need = (kf - f_hi)[0:1]
        lo1, hi1 = lo[0:1], hi[0:1]
        ri = lax.broadcasted_iota(jnp.int32, (LANES, LANES), 0)
        ci = lax.broadcasted_iota(jnp.int32, (LANES, LANES), 1)
        lower = jnp.where(ci < ri, 1.0, 0.0).astype(BF16)
        off = jnp.zeros((1, LANES), F32)
        for j in range(width // LANES):
            b = s_ref[j * LANES:(j + 1) * LANES, :]
            inn = jnp.logical_and(b > lo1, b <= hi1)
            innf = jnp.where(inn, 1.0, 0.0)
            before = jnp.dot(lower, innf.astype(BF16), preferred_element_type=F32) + off
            sel = jnp.logical_or(b > hi1, jnp.logical_and(inn, before < need))
            bias_ref[j * LANES:(j + 1) * LANES, :] = jnp.where(sel, 0.0, neg)
            off = off + jnp.sum(innf, axis=0, keepdims=True)


def _prompt_attn_body(blk_i, topk, qi_ref, kiwi_ref, kib_ref, q_ref, k_ref, vt_ref, o_ref, s_ref, bias_ref):
    width = (blk_i + 1) * Q_BLOCK
    neg = -jnp.inf
    kiwi_t = kiwi_ref[0].T
    chunk = 512
    for c0 in range(0, width, chunk):
        c1 = min(width, c0 + chunk)
        kib = kib_ref[0, c0:c1, :]
        sc = None
        for h in range(IDX_HEADS):
            d = lax.dot_general(kib, qi_ref[0, :, h * IDX_DIM:(h + 1) * IDX_DIM], _NT,
                                preferred_element_type=F32)
            t = jnp.maximum(d, 0.0) * kiwi_t[IDX_DIM + h:IDX_DIM + h + 1, :]
            sc = t if sc is None else sc + t
        s_ref[c0:c1, :] = sc
    ki = lax.broadcasted_iota(jnp.int32, (Q_BLOCK, Q_BLOCK), 0)
    qj = lax.broadcasted_iota(jnp.int32, (Q_BLOCK, Q_BLOCK), 1)
    d0 = width - Q_BLOCK
    s_ref[d0:width, :] = jnp.where(ki <= qj, s_ref[d0:width, :], neg)
    if width > topk:
        n_adm = (lax.broadcasted_iota(jnp.int32, (SUB, LANES), 1) + (d0 + 1)).astype(F32)
        _select_bias_t(s_ref, bias_ref, width, topk, n_adm)
    else:
        bias_ref[...] = jnp.where(s_ref[...] == neg, neg, 0.0)
    lane = lax.broadcasted_iota(jnp.int32, (Q_BLOCK, LANES), 1)
    hpg = N_HEADS // N_KV_HEADS
    for h in range(N_HEADS):
        g = h // hpg
        qp = q_ref[0, :, (h // 2) * LANES:(h // 2 + 1) * LANES].astype(F32)
        if h % 2 != g:
            qp = pltpu.roll(qp, HEAD_DIM, 1)
        q2 = jnp.where((lane >= HEAD_DIM) == (g == 1), qp, 0.0).astype(BF16)
        s = lax.dot_general(k_ref[0], q2, _NT, preferred_element_type=F32) + bias_ref[...]
        m = jnp.max(s, axis=0, keepdims=True)
        p = jnp.exp(s - m)
        l = jnp.sum(p, axis=0, keepdims=True)
        o_t = jnp.dot(vt_ref[0, g * HEAD_DIM:(g + 1) * HEAD_DIM, :], p.astype(BF16), preferred_element_type=F32)
        o_ref[0, :, h * HEAD_DIM:(h + 1) * HEAD_DIM] = (o_t / l).T.astype(BF16)


def _prompt_attention(qi, kiwi, kib, q, kb, vt, topk):
    b, s, _ = q.shape
    outs = []
    for i in range(s // Q_BLOCK):
        width = (i + 1) * Q_BLOCK
        qblk = lambda bb, i=i: (bb, i, 0)
        kall = lambda bb: (bb, 0, 0)
        outs.append(pl.pallas_call(
            functools.partial(_prompt_attn_body, i, topk),
            grid=(b,),
            in_specs=[pl.BlockSpec((1, Q_BLOCK, IDX_HEADS * IDX_DIM), qblk),
                      pl.BlockSpec((1, Q_BLOCK, LANES), qblk),
                      pl.BlockSpec((1, width, IDX_DIM), kall),
                      pl.BlockSpec((1, Q_BLOCK, N_HEADS * HEAD_DIM), qblk),
                      pl.BlockSpec((1, width, N_KV_HEADS * HEAD_DIM), kall),
                      pl.BlockSpec((1, N_KV_HEADS * HEAD_DIM, width), kall)],
            out_specs=pl.BlockSpec((1, Q_BLOCK, N_HEADS * HEAD_DIM), lambda bb: (bb, 0, 0)),
            out_shape=jax.ShapeDtypeStruct((b, Q_BLOCK, N_HEADS * HEAD_DIM), BF16),
            scratch_shapes=[pltpu.VMEM((width, Q_BLOCK), F32), pltpu.VMEM((width, Q_BLOCK), F32)],
            compiler_params=_cparams(("arbitrary",)),
            name=f"prompt_attn_{i}",
        )(qi, kiwi, kib, q, kb, vt))
    return jnp.stack(outs)


def _page_copies(pt_ref, sample, src_hbm, buf, slot, sem, n_pages):
    return [pltpu.make_async_copy(src_hbm.at[pt_ref[sample, p]],
                                  buf.at[slot, :, pl.ds(p * PAGE_SIZE, PAGE_SIZE)], sem.at[slot])
            for p in range(n_pages)]


def _sample_scores_body(n_pages, pt_ref, kidx_hbm, qi_ref, wi_ref, kin_ref, o_ref, buf, sem):
    s = pl.program_id(0)
    slot = s % 2

    @pl.when(s == 0)
    def _():
        _start_all(_page_copies(pt_ref, 0, kidx_hbm, buf, 0, sem, n_pages))

    @pl.when(s + 1 < pl.num_programs(0))
    def _():
        _start_all(_page_copies(pt_ref, s + 1, kidx_hbm, buf, 1 - slot, sem, n_pages))

    for cp in _page_copies(pt_ref, s, kidx_hbm, buf, slot, sem, n_pages):
        cp.wait()
    qi = qi_ref[0]
    wi = wi_ref[0]
    ki_t = buf[slot].astype(BF16)
    d = jnp.dot(qi, ki_t, preferred_element_type=F32)
    past = jnp.sum(jnp.maximum(d, 0.0) * wi, axis=0, keepdims=True)
    dn = jnp.sum(qi.astype(F32) * kin_ref[0].astype(F32), axis=1, keepdims=True)
    new = jnp.sum(jnp.maximum(dn, 0.0) * wi, axis=0, keepdims=True)
    lane = lax.broadcasted_iota(jnp.int32, (1, LANES), 1)
    tail = jnp.where(lane == 0, jnp.broadcast_to(new, (1, LANES)), -jnp.inf)
    o_ref[0] = jnp.concatenate([past, tail], axis=1)


def _sample_scores(page_table, kidx_pool, qi8, wi8, ki_new):
    n, n_pages = page_table.shape
    past = n_pages * PAGE_SIZE
    grid_spec = pltpu.PrefetchScalarGridSpec(
        num_scalar_prefetch=1,
        grid=(n,),
        in_specs=[pl.BlockSpec(memory_space=pl.ANY),
                  pl.BlockSpec((1, 8, IDX_DIM), lambda s, pt: (s, 0, 0)),
                  pl.BlockSpec((1, 8, 1), lambda s, pt: (s, 0, 0)),
                  pl.BlockSpec((1, 1, IDX_DIM), lambda s, pt: (s, 0, 0))],
        out_specs=pl.BlockSpec((1, 1, past + LANES), lambda s, pt: (s, 0, 0)),
        scratch_shapes=[pltpu.VMEM((2, IDX_DIM, past), F32), pltpu.SemaphoreType.DMA((2,))],
    )
    return pl.pallas_call(
        functools.partial(_sample_scores_body, n_pages),
        grid_spec=grid_spec,
        out_shape=jax.ShapeDtypeStruct((n, 1, past + LANES), F32),
        compiler_params=_cparams(("arbitrary",)),
        name="sample_scores",
    )(page_table, kidx_pool, qi8, wi8, ki_new)


def _sample_select_body(topk, s_ref, bias_ref):
    rows, width = s_ref.shape
    _select_bias(s_ref, bias_ref, rows, width, topk)


def _sample_select(scores, topk):
    return pl.pallas_call(
        functools.partial(_sample_select_body, topk),
        out_shape=jax.ShapeDtypeStruct(scores.shape, F32),
        compiler_params=pltpu.CompilerParams(vmem_limit_bytes=VMEM_LIMIT),
        name="sample_select",
    )(scores)


def _sample_attn_body(n_pages, pt_ref, k_hbm, v_hbm, q_ref, bias_ref, kn_ref, vn_ref, o_ref, kbuf, vbuf, sem):
    s = pl.program_id(0)
    slot = s % 2
    past = n_pages * PAGE_SIZE

    def copies(sample, sl):
        return (_page_copies(pt_ref, sample, k_hbm, kbuf, sl, sem.at[0], n_pages)
                + _page_copies(pt_ref, sample, v_hbm, vbuf, sl, sem.at[1], n_pages))

    @pl.when(s == 0)
    def _():
        _start_all(copies(0, 0))

    @pl.when(s + 1 < pl.num_programs(0))
    def _():
        _start_all(copies(s + 1, 1 - slot))

    for cp in copies(s, slot):
        cp.wait()
    q2 = q_ref[0]
    k_t = kbuf[slot].astype(BF16)
    v_t = vbuf[slot].astype(BF16)
    row = lax.broadcasted_iota(jnp.int32, (N_HEADS, 1), 0)
    first = row < (N_HEADS // N_KV_HEADS)
    bias = bias_ref[0]
    sc = jnp.dot(q2, k_t, preferred_element_type=F32) + bias[:, 0:past]
    sn = jnp.sum(q2.astype(F32) * kn_ref[0].astype(F32), axis=1, keepdims=True) + bias[:, past:past + 1]
    m = jnp.maximum(jnp.max(sc, axis=1, keepdims=True), sn)
    p = jnp.exp(sc - m)
    pn = jnp.exp(sn - m)
    l = jnp.sum(p, axis=1, keepdims=True) + pn
    o2 = lax.dot_general(p.astype(BF16), v_t, _NT, preferred_element_type=F32)
    o2 = o2 + pn.astype(BF16).astype(F32) * vn_ref[0].astype(F32)
    o = jnp.where(first, o2[:, 0:HEAD_DIM], o2[:, HEAD_DIM:2 * HEAD_DIM])
    o_ref[0] = (o / l).astype(BF16)


def _sample_attention(page_table, k_pool, v_pool, q8, bias, k_new, v_new):
    n, n_pages = page_table.shape
    past = n_pages * PAGE_SIZE
    kvw = N_KV_HEADS * HEAD_DIM
    per = lambda s, pt: (s, 0, 0)
    grid_spec = pltpu.PrefetchScalarGridSpec(
        num_scalar_prefetch=1,
        grid=(n,),
        in_specs=[pl.BlockSpec(memory_space=pl.ANY), pl.BlockSpec(memory_space=pl.ANY),
                  pl.BlockSpec((1, N_HEADS, kvw), per),
                  pl.BlockSpec((1, 1, past + LANES), per),
                  pl.BlockSpec((1, 1, kvw), per),
                  pl.BlockSpec((1, 1, kvw), per)],
        out_specs=pl.BlockSpec((1, N_HEADS, HEAD_DIM), per),
        scratch_shapes=[pltpu.VMEM((2, kvw, past), F32), pltpu.VMEM((2, kvw, past), F32),
                        pltpu.SemaphoreType.DMA((2, 2))],
    )
    return pl.pallas_call(
        functools.partial(_sample_attn_body, n_pages),
        grid_spec=grid_spec,
        out_shape=jax.ShapeDtypeStruct((n, N_HEADS, HEAD_DIM), BF16),
        compiler_params=_cparams(("arbitrary",)),
        name="sample_attn",
    )(page_table, k_pool, v_pool, q8, bias, k_new, v_new)


def _gmlp_body(n_chunks, u_ref, vg_ref, w_ref, bt_ref, o_ref):
    ri = lax.broadcasted_iota(jnp.int32, (CHUNK, CHUNK), 0)
    ci = lax.broadcasted_iota(jnp.int32, (CHUNK, CHUNK), 1)
    gd = GM_WIDTH // GM_GROUPS
    for g in range(GM_GROUPS):
        wg = jnp.where(ci <= ri, w_ref[g], 0.0).astype(BF16)
        bg = bt_ref[:, g:g + 1]
        for c in range(n_chunks):
            rows = slice(c * CHUNK, (c + 1) * CHUNK)
            cols = slice(g * gd, (g + 1) * gd)
            mixed = jnp.dot(wg, vg_ref[rows, cols], preferred_element_type=F32) + bg
            o_ref[rows, cols] = (u_ref[rows, cols].astype(F32) * mixed).astype(BF16)


def _gmlp_prompt(u, vg, w, bt, tm):
    t = u.shape[0]
    row = lambda i: (i, 0)
    return pl.pallas_call(
        functools.partial(_gmlp_body, tm // CHUNK),
        grid=(t // tm,),
        in_specs=[pl.BlockSpec((tm, GM_WIDTH), row), pl.BlockSpec((tm, GM_WIDTH), row),
                  pl.BlockSpec((GM_GROUPS, CHUNK, CHUNK), lambda i: (0, 0, 0)),
                  pl.BlockSpec((CHUNK, LANES), lambda i: (0, 0))],
        out_specs=pl.BlockSpec((tm, GM_WIDTH), row),
        out_shape=jax.ShapeDtypeStruct((t, GM_WIDTH), BF16),
        compiler_params=_cparams(("arbitrary",)),
        name="gmlp",
    )(u, vg, w, bt)


def _gmlp_first_row_body(u_ref, vg_ref, w0_ref, b0_ref, o_ref):
    o_ref[...] = (u_ref[...].astype(F32) * (vg_ref[...] * w0_ref[...] + b0_ref[...])).astype(BF16)


def _gmlp_sample(u, vg, w0, b0):
    return pl.pallas_call(
        _gmlp_first_row_body,
        out_shape=jax.ShapeDtypeStruct(u.shape, BF16),
        name="gmlp_first_row",
    )(u, vg, w0, b0)


def _merge_body(mod3d, dispatch, attn_ref, gm_ref, ga_ref, gb_ref, x_ref, mod_ref, wpa_ref, wpg_ref, wo_ref,
                g2_ref, wr_ref, br_ref, x1_o, a_o, b_o):
    tm = x_ref.shape[0]
    attn = attn_ref[...].reshape(tm, N_HEADS * HEAD_DIM)
    a = jnp.dot(attn, wpa_ref[...], preferred_element_type=F32)
    g = jnp.dot(gm_ref[...], wpg_ref[...], preferred_element_type=F32)
    merged = ga_ref[...].astype(F32) * a + gb_ref[...].astype(F32) * g
    out = jnp.dot(merged.astype(BF16), wo_ref[...], preferred_element_type=F32)
    m = mod_ref[0] if mod3d else mod_ref[...]
    x1 = x_ref[...] + m[:, 2 * D_MODEL:3 * D_MODEL] * out
    x1_o[...] = x1
    ms = jnp.mean(x1 * x1, axis=-1, keepdims=True)
    y = x1 * lax.rsqrt(ms + EPS) * g2_ref[...]
    h2 = y * (1.0 + m[:, 4 * D_MODEL:5 * D_MODEL]) + m[:, 3 * D_MODEL:4 * D_MODEL]
    hi = h2.astype(BF16)
    lo = (h2 - hi.astype(F32)).astype(BF16)
    r = jnp.dot(hi, wr_ref[...], preferred_element_type=F32) + jnp.dot(lo, wr_ref[...], preferred_element_type=F32)
    logits = r[:, 0:LANES] + r[:, LANES:2 * LANES] + br_ref[...]
    neg = -jnp.inf
    big = jnp.int32(1 << 20)
    lane = lax.broadcasted_iota(jnp.int32, logits.shape, 1)
    is_g = jnp.logical_and(lane >= N_EXPERTS, lane < N_EXPERTS + N_EXPERT_GROUPS)
    gl = jnp.where(is_g, logits, neg)
    gmax = jnp.max(gl, axis=1, keepdims=True)
    g_lane = jnp.min(jnp.where(gl == gmax, lane, big), axis=1, keepdims=True)
    g_w = 1.0 / jnp.sum(jnp.exp(gl - gmax), axis=1, keepdims=True)
    g_sel = g_lane - N_EXPERTS
    in_grp = jnp.logical_and(lane < N_EXPERTS, (lane >> 3) == g_sel)
    el = jnp.where(in_grp, logits, neg)
    m1 = jnp.max(el, axis=1, keepdims=True)
    i1 = jnp.min(jnp.where(el == m1, lane, big), axis=1, keepdims=True)
    el2 = jnp.where(lane == i1, neg, el)
    m2 = jnp.max(el2, axis=1, keepdims=True)
    i2 = jnp.min(jnp.where(el2 == m2, lane, big), axis=1, keepdims=True)
    e2 = jnp.exp(m2 - m1)
    w1 = g_w / (1.0 + e2)
    w2 = g_w * e2 / (1.0 + e2)
    if not dispatch:
        a_o[...] = hi
        b_o[...] = jnp.where(lane == i1, w1, 0.0) + jnp.where(lane == i2, w2, 0.0)
        return
    low_first = i1 < i2
    ea = jnp.where(low_first, i1, i2) - g_sel * EXPERTS_PER_GROUP
    eb = jnp.where(low_first, i2, i1) - g_sel * EXPERTS_PER_GROUP
    cls = g_sel * N_PAIRS + ((ea * (2 * EXPERTS_PER_GROUP - 1 - ea)) >> 1) + (eb - ea - 1)
    b_o[...] = jnp.broadcast_to(cls, (tm, LANES))
    wa = jnp.where(low_first, w1, w2)
    wb = jnp.where(low_first, w2, w1)
    _store_token_tiles(a_o, _pack_halves(hi.astype(F32)), lax.bitcast_convert_type(
        jnp.where(lane == 0, wa, jnp.where(lane == 1, wb, 0.0)), U32))


def _merge(attn4, gm, ga, gb, x, mod, wpa, wpg, wo, g2, wr, br, tm, tiles_per_seq, dispatch):
    t = x.shape[0]
    mod3d = mod.ndim == 3
    nb = tm // Q_BLOCK
    row = lambda i: (i, 0)
    fixed = lambda i: (0, 0)
    if mod3d:
        mod_spec = pl.BlockSpec((1, 1, 6 * D_MODEL), lambda i: (i // tiles_per_seq, 0, 0))
    else:
        mod_spec = pl.BlockSpec((tm, 6 * D_MODEL), row)
    attn_spec = pl.BlockSpec((nb, 1, Q_BLOCK, N_HEADS * HEAD_DIM),
                             lambda i: (i % tiles_per_seq, i // tiles_per_seq, 0, 0))
    if dispatch:
        extra = [(SUB, LANES, U32), (1, LANES, jnp.int32)]
    else:
        extra = [(1, D_MODEL, BF16), (1, LANES, F32)]
    return pl.pallas_call(
        functools.partial(_merge_body, mod3d, dispatch),
        grid=(t // tm,),
        in_specs=[attn_spec, pl.BlockSpec((tm, GM_WIDTH), row),
                  pl.BlockSpec((tm, D_MODEL), row), pl.BlockSpec((tm, D_MODEL), row),
                  pl.BlockSpec((tm, D_MODEL), row), mod_spec,
                  pl.BlockSpec((N_HEADS * HEAD_DIM, D_MODEL), fixed), pl.BlockSpec((GM_WIDTH, D_MODEL), fixed),
                  pl.BlockSpec((D_MODEL, D_MODEL), fixed), pl.BlockSpec((1, D_MODEL), fixed),
                  pl.BlockSpec((D_MODEL, 2 * LANES), fixed), pl.BlockSpec((1, LANES), fixed)],
        out_specs=[pl.BlockSpec((tm, D_MODEL), row)] + [pl.BlockSpec((tm * r, w), row) for r, w, _ in extra],
        out_shape=[jax.ShapeDtypeStruct((t, D_MODEL), F32)]
        + [jax.ShapeDtypeStruct((t * r, w), dt) for r, w, dt in extra],
        compiler_params=_cparams(("arbitrary",)),
        name="merge",
    )(attn4, gm, ga, gb, x, mod, wpa, wpg, wo, g2, wr, br)


def _moe_body(mod3d, h_ref, gate_ref, x1_ref, mod_ref, wgu_ref, wd_ref, o_ref, acc_ref):
    e = pl.program_id(1)

    @pl.when(e == 0)
    def _():
        acc_ref[...] = jnp.zeros_like(acc_ref)

    gu = jnp.dot(h_ref[...], wgu_ref[0], preferred_element_type=F32)
    a = gu[:, 0:D_EXPERT]
    hid = a * jax.nn.sigmoid(a) * gu[:, D_EXPERT:2 * D_EXPERT]
    gate = gate_ref[...]
    lane = lax.broadcasted_iota(jnp.int32, gate.shape, 1)
    ge = jnp.sum(jnp.where(lane == e, gate, 0.0), axis=1, keepdims=True)
    acc_ref[...] += jnp.dot((hid * ge).astype(BF16), wd_ref[0], preferred_element_type=F32)

    @pl.when(e == pl.num_programs(1) - 1)
    def _():
        m = mod_ref[0] if mod3d else mod_ref[...]
        o_ref[...] = x1_ref[...] + m[:, 5 * D_MODEL:6 * D_MODEL] * acc_ref[...]


def _moe(h2, gate, x1, mod, wgu, wd, tm, tiles_per_seq):
    t = h2.shape[0]
    mod3d = mod.ndim == 3
    row = lambda i, e: (i, 0)
    if mod3d:
        mod_spec = pl.BlockSpec((1, 1, 6 * D_MODEL), lambda i, e: (i // tiles_per_seq, 0, 0))
    else:
        mod_spec = pl.BlockSpec((tm, 6 * D_MODEL), row)
    return pl.pallas_call(
        functools.partial(_moe_body, mod3d),
        grid=(t // tm, N_EXPERTS),
        in_specs=[pl.BlockSpec((tm, D_MODEL), row), pl.BlockSpec((tm, LANES), row),
                  pl.BlockSpec((tm, D_MODEL), row), mod_spec,
                  pl.BlockSpec((1, D_MODEL, 2 * D_EXPERT), lambda i, e: (e, 0, 0)),
                  pl.BlockSpec((1, D_EXPERT, D_MODEL), lambda i, e: (e, 0, 0))],
        out_specs=pl.BlockSpec((tm, D_MODEL), row),
        out_shape=jax.ShapeDtypeStruct((t, D_MODEL), F32),
        scratch_shapes=[pltpu.VMEM((tm, D_MODEL), F32)],
        compiler_params=_cparams(("arbitrary", "arbitrary")),
        name="moe",
    )(h2, gate, x1, mod, wgu, wd)


def _slots_body(cls_ref, slot_o, segend_o, carry, seg_start):
    sweep = pl.program_id(0)
    i = pl.program_id(1)
    tm = cls_ref.shape[0]
    lane = lax.broadcasted_iota(jnp.int32, (tm, LANES), 1)
    hit = lane == cls_ref[...]
    onehot = jnp.where(hit, 1.0, 0.0)

    @pl.when(jnp.logical_and(sweep == 0, i == 0))
    def _():
        carry[...] = jnp.zeros_like(carry)

    @pl.when(sweep == 0)
    def _():
        carry[...] = carry[...] + jnp.sum(onehot, axis=0, keepdims=True)
        slot_o[...] = jnp.zeros_like(slot_o)

        @pl.when(i == pl.num_programs(1) - 1)
        def _():
            padded = jnp.floor((carry[...] + (ROW_TILE - 1)) * (1.0 / ROW_TILE)) * ROW_TILE
            ri = lax.broadcasted_iota(jnp.int32, (LANES, LANES), 0)
            ci = lax.broadcasted_iota(jnp.int32, (LANES, LANES), 1)
            upto = jnp.where(ri <= ci, 1.0, 0.0)
            seg_end = jnp.dot(padded, upto, preferred_element_type=F32, precision=lax.Precision.HIGHEST)
            segend_o[...] = seg_end
            seg_start[...] = seg_end - padded
            carry[...] = jnp.zeros_like(carry)

    @pl.when(sweep == 1)
    def _():
        ri = lax.broadcasted_iota(jnp.int32, (tm, tm), 0)
        ci = lax.broadcasted_iota(jnp.int32, (tm, tm), 1)
        earlier = jnp.where(ci < ri, 1.0, 0.0).astype(BF16)
        before = (jnp.dot(earlier, onehot.astype(BF16), preferred_element_type=F32)
                  + carry[0:1, :] + seg_start[0:1, :])
        slot = jnp.sum(jnp.where(hit, before, 0.0), axis=1, keepdims=True)
        slot_o[...] = jnp.broadcast_to(slot, (tm, LANES)).astype(jnp.int32)
        carry[...] = carry[...] + jnp.sum(onehot, axis=0, keepdims=True)


def _class_slots(cls, tm):
    t = cls.shape[0]
    return pl.pallas_call(
        _slots_body,
        grid=(2, t // tm),
        in_specs=[pl.BlockSpec((tm, LANES), lambda s, i: (i, 0))],
        out_specs=[pl.BlockSpec((tm, LANES), lambda s, i: (i * s, 0)),
                   pl.BlockSpec((SUB, LANES), lambda s, i: (0, 0))],
        out_shape=[jax.ShapeDtypeStruct((t, LANES), jnp.int32), jax.ShapeDtypeStruct((SUB, LANES), F32)],
        scratch_shapes=[pltpu.VMEM((SUB, LANES), F32), pltpu.VMEM((SUB, LANES), F32)],
        compiler_params=_cparams(("arbitrary", "arbitrary")),
        name="class_slots",
    )(cls)


INVERT_CHUNK = 4096


def _invert_body(dest_ref, src_o):
    i = pl.program_id(0)

    @pl.when(i == 0)
    def _():
        src_o[...] = jnp.zeros_like(src_o)

    def put(r, carry):
        src_o[pl.ds(dest_ref[0, 0, r], 1), :] = jnp.full((1, LANES), i * INVERT_CHUNK + r, jnp.int32)
        return carry

    lax.fori_loop(0, INVERT_CHUNK, put, 0, unroll=8)


def _invert(dest, n_sorted):
    t = dest.shape[0]
    return pl.pallas_call(
        _invert_body,
        grid=(t // INVERT_CHUNK,),
        in_specs=[pl.BlockSpec((1, 1, INVERT_CHUNK), lambda i: (i, 0, 0), memory_space=pltpu.SMEM)],
        out_specs=pl.BlockSpec((n_sorted, LANES), lambda i: (0, 0)),
        out_shape=jax.ShapeDtypeStruct((n_sorted, LANES), jnp.int32),
        compiler_params=_cparams(("arbitrary",)),
        name="invert_slots",
    )(dest.reshape(t // INVERT_CHUNK, 1, INVERT_CHUNK))


def _token_fetches(idx_ref, tiles_hbm, buf, slot, sem, n):
    return [pltpu.make_async_copy(tiles_hbm.at[pl.ds(pl.multiple_of(idx_ref[0, 0, r] * SUB, SUB), SUB)],
                                  buf.at[slot, pl.ds(r * SUB, SUB)], sem.at[slot])
            for r in range(n)]


def _start_all(copies):
    for r, cp in enumerate(copies):
        cp.start(priority=r % 2)


def _tile_fetches(src_ref, rows_hbm, buf, slot, sem):
    return _token_fetches(src_ref, rows_hbm, buf, slot, sem, ROW_TILE)


FETCH_AHEAD = 2


def _experts_body(grp_ref, ea_ref, eb_ref, used_ref, s0_ref, s1_ref, s2_ref, rows_hbm, wgu_ref, wd_ref,
                  o_ref, buf, sem):
    j = pl.program_id(0)
    n_buf = FETCH_AHEAD + 1
    slot = j % n_buf
    n_used = used_ref[0]

    @pl.when(j == 0)
    def _():
        _start_all(_tile_fetches(s0_ref, rows_hbm, buf, 0, sem))

    @pl.when(jnp.logical_and(j == 0, 1 < n_used))
    def _():
        _start_all(_tile_fetches(s1_ref, rows_hbm, buf, 1, sem))

    @pl.when(jnp.logical_or(j < n_used, j == 0))
    def _():
        for cp in _tile_fetches(s0_ref, rows_hbm, buf, slot, sem):
            cp.wait()

    @pl.when(j < n_used)
    def _():
        ahead = _tile_fetches(s2_ref, rows_hbm, buf, (j + FETCH_AHEAD) % n_buf, sem)
        quarter = ROW_TILE // 4

        def issue(part):
            @pl.when(j + FETCH_AHEAD < n_used)
            def _():
                _start_all(ahead[part * quarter:(part + 1) * quarter])

        tiles = buf.at[slot]
        x = _unpack_halves(_load_token_words(tiles, ROW_TILE)).astype(BF16)
        wts = lax.bitcast_convert_type(tiles[pl.ds(FEAT_SUB, ROW_TILE, stride=SUB), :], F32)

        def hidden(e, wgt):
            gu = jnp.dot(x, wgu_ref[e], preferred_element_type=F32)
            a = gu[:, 0:D_EXPERT]
            return (a * jax.nn.sigmoid(a) * gu[:, D_EXPERT:2 * D_EXPERT] * wgt).astype(BF16)

        ea, eb = ea_ref[j], eb_ref[j]
        issue(0)
        hid_a = hidden(ea, wts[:, 0:1])
        issue(1)
        hid_b = hidden(eb, wts[:, 1:2])
        issue(2)
        y = (jnp.dot(hid_a, wd_ref[ea], preferred_element_type=F32)
             + jnp.dot(hid_b, wd_ref[eb], preferred_element_type=F32))
        issue(3)
        _store_token_tiles(o_ref, _pack_halves(y.astype(BF16).astype(F32)), None)

    @pl.when(j >= n_used)
    def _():
        o_ref[...] = jnp.zeros_like(o_ref)


def _experts(grp_t, ea_t, eb_t, n_used, src3, rows, wgu, wd):
    n_tiles = src3.shape[0]
    w_grp = lambda j, grp, ea, eb, nu: (grp[j], 0, 0)
    smem_blk = lambda f: pl.BlockSpec((1, 1, ROW_TILE), f, memory_space=pltpu.SMEM)
    grid_spec = pltpu.PrefetchScalarGridSpec(
        num_scalar_prefetch=4,
        grid=(n_tiles,),
        in_specs=[smem_blk(lambda j, grp, ea, eb, nu: (j, 0, 0)),
                  smem_blk(lambda j, grp, ea, eb, nu: (jnp.minimum(j + 1, n_tiles - 1), 0, 0)),
                  smem_blk(lambda j, grp, ea, eb, nu: (jnp.minimum(j + FETCH_AHEAD, n_tiles - 1), 0, 0)),
                  pl.BlockSpec(memory_space=pl.ANY),
                  pl.BlockSpec((EXPERTS_PER_GROUP, D_MODEL, 2 * D_EXPERT), w_grp),
                  pl.BlockSpec((EXPERTS_PER_GROUP, D_EXPERT, D_MODEL), w_grp)],
        out_specs=pl.BlockSpec((ROW_TILE * SUB, LANES), lambda j, grp, ea, eb, nu: (j, 0)),
        scratch_shapes=[pltpu.VMEM((FETCH_AHEAD + 1, ROW_TILE * SUB, LANES), U32),
                        pltpu.SemaphoreType.DMA((FETCH_AHEAD + 1,))],
    )
    return pl.pallas_call(
        _experts_body,
        grid_spec=grid_spec,
        out_shape=jax.ShapeDtypeStruct((n_tiles * ROW_TILE * SUB, LANES), U32),
        compiler_params=_cparams(("arbitrary",)),
        name="experts",
    )(grp_t, ea_t, eb_t, n_used, src3, src3, src3, rows, wgu, wd)


def _row_fetches(dest_ref, ys_hbm, buf, slot, sem):
    return _token_fetches(dest_ref, ys_hbm, buf, slot, sem, MOVE_TILE)


def _combine_body(mod3d, dcur_ref, dnext_ref, ys_hbm, x1_ref, mod_ref, o_ref, buf, sem):
    i = pl.program_id(0)
    slot = i % 2

    @pl.when(i == 0)
    def _():
        _start_all(_row_fetches(dcur_ref, ys_hbm, buf, 0, sem))

    @pl.when(i + 1 < pl.num_programs(0))
    def _():
        _start_all(_row_fetches(dnext_ref, ys_hbm, buf, 1 - slot, sem))

    for cp in _row_fetches(dcur_ref, ys_hbm, buf, slot, sem):
        cp.wait()
    m = mod_ref[0] if mod3d else mod_ref[...]
    o_ref[...] = x1_ref[...] + m[:, 5 * D_MODEL:6 * D_MODEL] * _unpack_halves(_load_token_words(buf.at[slot], MOVE_TILE))


def _combine(dest3, ys, x1, mod, tiles_per_seq):
    steps = dest3.shape[0]
    t = x1.shape[0]
    mod3d = mod.ndim == 3
    row = lambda i: (i, 0)
    if mod3d:
        mod_spec = pl.BlockSpec((1, 1, 6 * D_MODEL), lambda i: (i // tiles_per_seq, 0, 0))
    else:
        mod_spec = pl.BlockSpec((MOVE_TILE, 6 * D_MODEL), row)
    smem_blk = lambda f: pl.BlockSpec((1, 1, MOVE_TILE), f, memory_space=pltpu.SMEM)
    return pl.pallas_call(
        functools.partial(_combine_body, mod3d),
        grid=(steps,),
        in_specs=[smem_blk(lambda i: (i, 0, 0)), smem_blk(lambda i: (jnp.minimum(i + 1, steps - 1), 0, 0)),
                  pl.BlockSpec(memory_space=pl.ANY),
                  pl.BlockSpec((MOVE_TILE, D_MODEL), row), mod_spec],
        out_specs=pl.BlockSpec((MOVE_TILE, D_MODEL), row),
        out_shape=jax.ShapeDtypeStruct((t, D_MODEL), F32),
        scratch_shapes=[pltpu.VMEM((2, MOVE_TILE * SUB, LANES), U32), pltpu.SemaphoreType.DMA((2,))],
        compiler_params=_cparams(("arbitrary",)),
        name="combine",
    )(dest3, dest3, ys, x1, mod)


def _class_expert_tables():
    ea, eb = [], []
    for g in range(N_EXPERT_GROUPS):
        for a in range(EXPERTS_PER_GROUP):
            for b in range(a + 1, EXPERTS_PER_GROUP):
                ea.append(g * EXPERTS_PER_GROUP + a)
                eb.append(g * EXPERTS_PER_GROUP + b)
    return np.asarray(ea, np.int32), np.asarray(eb, np.int32)


def _moe_dispatched(rows, cls, x1, mod, wgu, wd, tiles_per_seq):
    t = x1.shape[0]
    slots, seg_end8 = _class_slots(cls, 1024)
    seg_end = seg_end8[0, 0:N_CLASSES].astype(jnp.int32)
    dest = slots[:, 0]
    dest3 = dest.reshape(t // MOVE_TILE, 1, MOVE_TILE)
    n_sorted = t + N_CLASSES * ROW_TILE
    tile_row0 = jnp.arange(n_sorted // ROW_TILE, dtype=jnp.int32) * ROW_TILE
    tile_cls = jnp.minimum(jnp.sum((seg_end[None, :] <= tile_row0[:, None]).astype(jnp.int32), axis=1),
                           N_CLASSES - 1)
    ea_np, eb_np = _class_expert_tables()
    in_cls = (tile_cls[:, None] == jnp.arange(N_CLASSES, dtype=jnp.int32)[None, :]).astype(jnp.int32)
    ea_t = jnp.sum(in_cls * jnp.asarray(ea_np % EXPERTS_PER_GROUP)[None, :], axis=1)
    eb_t = jnp.sum(in_cls * jnp.asarray(eb_np % EXPERTS_PER_GROUP)[None, :], axis=1)
    grp_t = tile_cls // N_PAIRS
    n_used = (seg_end[N_CLASSES - 1] // ROW_TILE).astype(jnp.int32).reshape(1)
    src = _invert(dest, n_sorted)[:, 0]
    ys = _experts(grp_t, ea_t, eb_t, n_used, src.reshape(n_sorted // ROW_TILE, 1, ROW_TILE), rows, wgu, wd)
    return _combine(dest3, ys, x1, mod, tiles_per_seq)


def _pad_lanes(v, fill):
    n = v.shape[-1]
    return jnp.concatenate([v, jnp.full((LANES - n,), fill, v.dtype)]).reshape(1, LANES)


def kernel(x_prompt, x_sample, c_prompt, c_sample, cache_k, cache_v, cache_kidx, page_table, w_ada, b_ada, norm_mix_g, norm_ffn_g, w_in, q_norm_g, k_norm_g, kidx_norm_g, gm_ln_g, gm_ln_b, gm_spatial_w, gm_spatial_b, w_proj_attn, w_proj_gmlp, w_out, w_router_group, b_router_group, w_router_expert, b_router_expert, w_expert_gate, w_expert_up, w_expert_down):
    depth = w_ada.shape[0]
    assert depth == 1
    l = 0
    bp, sp, _ = x_prompt.shape
    bs, ss, _ = x_sample.shape
    assert ss == 1
    n_pages = page_table.shape[1]
    past = n_pages * PAGE_SIZE
    tp = bp * sp

    w = w_in[l]
    zpad = jnp.zeros((D_MODEL, LANES - IDX_DIM - IDX_HEADS), F32)
    w_pad = jnp.concatenate([w[:, 0:1024], w[:, 1024:1088], w[:, 1088:1092], zpad, w[:, 1092:]], axis=1).astype(BF16)
    seg_np = (np.arange(LANES)[:, None] // HEAD_DIM) == (np.arange(LANES)[None, :] // HEAD_DIM)
    seg = jnp.asarray(seg_np, BF16)
    segki = jnp.asarray(seg_np & (np.arange(LANES)[:, None] < IDX_DIM) & (np.arange(LANES)[None, :] < IDX_DIM), BF16)
    consts = (norm_mix_g[l].reshape(1, D_MODEL),
              jnp.tile(q_norm_g[l], 2).reshape(1, LANES), jnp.tile(k_norm_g[l], 2).reshape(1, LANES),
              _pad_lanes(kidx_norm_g[l], 1.0),
              gm_ln_g[l].reshape(1, GM_WIDTH), gm_ln_b[l].reshape(1, GM_WIDTH), seg, segki)
    wpa = w_proj_attn[l].astype(BF16)
    wpg = w_proj_gmlp[l].astype(BF16)
    wo = w_out[l].astype(BF16)
    wr32 = jnp.concatenate([w_router_expert[l], w_router_group[l],
                            jnp.zeros((D_MODEL, LANES - N_EXPERTS - N_EXPERT_GROUPS), F32)], axis=1)
    wr_hi = wr32.astype(BF16)
    wr_lo = (wr32 - wr_hi.astype(F32)).astype(BF16)
    wr = jnp.concatenate([wr_hi, wr_lo], axis=1)
    br = _pad_lanes(jnp.concatenate([b_router_expert[l], b_router_group[l]]), 0.0)
    wgu = jnp.concatenate([w_expert_gate[l], w_expert_up[l]], axis=2).astype(BF16)
    wd = w_expert_down[l].astype(BF16)
    g2 = norm_ffn_g[l].reshape(1, D_MODEL)

    mod = _adaln(jnp.concatenate([c_prompt, c_sample], axis=0), w_ada[l], b_ada[l])
    mod_p = mod[0:bp].reshape(bp, 1, 6 * D_MODEL)
    mod_s = mod[bp:bp + bs]
    pos = jnp.concatenate([jnp.arange(sp, dtype=jnp.int32),
                           jnp.full((8,), past, jnp.int32)]).astype(F32).reshape(sp + 8, 1)
    tabs = _rope_tables(pos)
    tabs_p = tuple(t[0:sp] for t in tabs)
    tabs_s = tuple(t[sp:sp + 1] for t in tabs)

    tm = 512
    tps = sp // tm
    (q, kb, vb, qi, kib, kiwi, u, vg, ga, gb, kt_f, vt_f, kit_f, vt) = _project(
        x_prompt.reshape(tp, D_MODEL), mod_p, tabs_p, consts, w_pad, tm, tps, BF16)
    topk_p = min(TOPK_MAX, sp // 4)
    r3 = lambda a: a.reshape(bp, sp, a.shape[-1])
    attn_p = _prompt_attention(r3(qi), r3(kiwi), r3(kib), r3(q), r3(kb), vt, topk_p)
    bt = jnp.concatenate([gm_spatial_b[l].T, jnp.zeros((CHUNK, LANES - GM_GROUPS), F32)], axis=1)
    gm_p = _gmlp_prompt(u, vg, gm_spatial_w[l], bt, tm)
    x1_p, rows_p, cls_p = _merge(attn_p, gm_p, ga, gb, x_prompt.reshape(tp, D_MODEL), mod_p,
                                 wpa, wpg, wo, g2, wr, br, tm, tps, True)
    y_p = _moe_dispatched(rows_p, cls_p, x1_p, mod_p, wgu, wd, sp // MOVE_TILE)

    (q_s, kb_s, vb_s, qi_s, kib_s, kiwi_s, u_s, vg_s, ga_s, gb_s, kt_s, vt_s, kit_s, _) = _project(
        x_sample.reshape(bs, D_MODEL), mod_s, tabs_s, consts, w_pad, bs, 1, F32)
    qi8 = jnp.concatenate([qi_s.reshape(bs, IDX_HEADS, IDX_DIM),
                           jnp.zeros((bs, 8 - IDX_HEADS, IDX_DIM), BF16)], axis=1)
    wi8 = jnp.concatenate([kiwi_s[:, IDX_DIM:IDX_DIM + IDX_HEADS],
                           jnp.zeros((bs, 8 - IDX_HEADS), F32)], axis=1).reshape(bs, 8, 1)
    kidx_t = jnp.transpose(cache_kidx[l], (0, 2, 1))
    scores = _sample_scores(page_table, kidx_t, qi8, wi8, kib_s.reshape(bs, 1, IDX_DIM))
    topk_s = min(TOPK_MAX, (past + ss) // 4)
    bias = _sample_select(scores.reshape(bs, past + LANES), topk_s).reshape(bs, 1, past + LANES)
    kvw = N_KV_HEADS * HEAD_DIM
    k_t = jnp.transpose(cache_k[l], (0, 2, 3, 1)).reshape(-1, kvw, PAGE_SIZE)
    v_t = jnp.transpose(cache_v[l], (0, 2, 3, 1)).reshape(-1, kvw, PAGE_SIZE)
    q3 = q_s.reshape(bs, N_HEADS, HEAD_DIM)
    zq = jnp.zeros_like(q3)
    in_first = (jnp.arange(N_HEADS) < N_HEADS // N_KV_HEADS)[None, :, None]
    q2_s = jnp.where(in_first, jnp.concatenate([q3, zq], axis=2), jnp.concatenate([zq, q3], axis=2))
    attn_s = _sample_attention(page_table, k_t, v_t, q2_s, bias,
                               kb_s.reshape(bs, 1, kvw), vb_s.reshape(bs, 1, kvw))
    gd = GM_WIDTH // GM_GROUPS
    w0 = jnp.repeat(gm_spatial_w[l][:, 0, 0], gd).reshape(1, GM_WIDTH)
    b0 = jnp.repeat(gm_spatial_b[l][:, 0], gd).reshape(1, GM_WIDTH)
    gm_s = _gmlp_sample(u_s, vg_s, w0, b0)
    x1_s, h2_s, gate_s = _merge(attn_s.reshape(1, 1, bs, N_HEADS * HEAD_DIM), gm_s, ga_s, gb_s,
                                x_sample.reshape(bs, D_MODEL), mod_s, wpa, wpg, wo, g2, wr, br, bs, 1, False)
    y_s = _moe(h2_s, gate_s, x1_s, mod_s, wgu, wd, bs, 1)

    def rows_kv(a_t, n, s):
        return jnp.transpose(a_t.reshape(n, N_KV_HEADS, HEAD_DIM, s), (0, 3, 1, 2))[None]

    def rows_ki(a_t):
        return jnp.transpose(a_t, (0, 2, 1))[None]

    return (y_p.reshape(bp, sp, D_MODEL), y_s.reshape(bs, ss, D_MODEL),
            rows_kv(kt_f, bp, sp), rows_kv(vt_f, bp, sp), rows_ki(kit_f),
            rows_kv(kt_s, 1, bs).reshape(1, bs, ss, N_KV_HEADS, HEAD_DIM),
            rows_kv(vt_s, 1, bs).reshape(1, bs, ss, N_KV_HEADS, HEAD_DIM),
            rows_ki(kit_s).reshape(1, bs, ss, IDX_DIM), vg_s.reshape(1, bs, ss, GM_WIDTH))
```

```python
import functools

import numpy as np
import jax
import jax.numpy as jnp
from jax import lax
from jax.experimental import pallas as pl
from jax.experimental.pallas import tpu as pltpu

F32 = jnp.float32
BF16 = jnp.bfloat16
U32 = jnp.uint32

D_MODEL = 1024
N_HEADS = 8
HEAD_DIM = 64
N_KV_HEADS = 2
ROT_DIM = 16
ROPE_THETA = 500000.0
IDX_HEADS = 4
IDX_DIM = 64
TOPK_MAX = 256
Q_BLOCK = 128
GM_WIDTH = 512
GM_GROUPS = 8
CHUNK = 128
N_EXPERT_GROUPS = 4
EXPERTS_PER_GROUP = 8
N_EXPERTS = 32
D_EXPERT = 256
EPS = 1e-6
PAGE_SIZE = 128
LANES = 128
SUB = 8

C_Q, C_K, C_V, C_QI, C_KIWI, C_U, C_VG, C_GATE = 0, 512, 640, 768, 1024, 1152, 1664, 2176
D_IN_PAD = 4224
VMEM_LIMIT = 56 * 1024 * 1024

N_PAIRS = EXPERTS_PER_GROUP * (EXPERTS_PER_GROUP - 1) // 2
N_CLASSES = N_EXPERT_GROUPS * N_PAIRS
ROW_TILE = 128
HALF = D_MODEL // 2
FEAT_SUB = HALF // LANES
MOVE_TILE = 256
HI_MASK = 0xFFFF0000


def _cparams(sem):
    return pltpu.CompilerParams(dimension_semantics=sem, vmem_limit_bytes=VMEM_LIMIT)


def _pack_halves(x):
    n = x.shape[1] // 2
    hi = lax.bitcast_convert_type(x[:, 0:n], U32) & jnp.uint32(HI_MASK)
    lo = lax.bitcast_convert_type(x[:, n:2 * n], U32) >> 16
    return hi | lo


def _unpack_halves(u):
    hi = lax.bitcast_convert_type(u & jnp.uint32(HI_MASK), F32)
    lo = lax.bitcast_convert_type(u << 16, F32)
    return jnp.concatenate([hi, lo], axis=1)


def _store_token_tiles(ref, words, extra):
    n = words.shape[0]
    for s in range(FEAT_SUB):
        ref[pl.ds(s, n, stride=SUB), :] = words[:, s * LANES:(s + 1) * LANES]
    zero = jnp.zeros((n, LANES), U32)
    ref[pl.ds(FEAT_SUB, n, stride=SUB), :] = zero if extra is None else extra
    for s in range(FEAT_SUB + 1, SUB):
        ref[pl.ds(s, n, stride=SUB), :] = zero


def _load_token_words(ref, n):
    return jnp.concatenate([ref[pl.ds(s, n, stride=SUB), :] for s in range(FEAT_SUB)], axis=1)


def _adaln_body(c_ref, w_ref, b_ref, o_ref):
    c = c_ref[...]
    a = c * jax.nn.sigmoid(c)
    o_ref[...] = jnp.dot(a, w_ref[...], preferred_element_type=F32,
                         precision=lax.Precision.HIGHEST) + b_ref[...]


def _adaln(c, w, b):
    r = c.shape[0]
    n = w.shape[1]
    bn = 1536
    return pl.pallas_call(
        _adaln_body,
        grid=(n // bn,),
        in_specs=[pl.BlockSpec((r, D_MODEL), lambda j: (0, 0)),
                  pl.BlockSpec((D_MODEL, bn), lambda j: (0, j)),
                  pl.BlockSpec((1, bn), lambda j: (0, j))],
        out_specs=pl.BlockSpec((r, bn), lambda j: (0, j)),
        out_shape=jax.ShapeDtypeStruct((r, n), F32),
        compiler_params=_cparams(("arbitrary",)),
        name="adaln",
    )(c, w, b.reshape(1, n))


def _rope_table_body(pos_ref, invf_ref, sa_m_ref, sb_m_ref, c_ref, sa_ref, sb_ref):
    ang = pos_ref[...] * invf_ref[...]
    s = jnp.sin(ang)
    c_ref[...] = jnp.cos(ang)
    sa_ref[...] = s * sa_m_ref[...]
    sb_ref[...] = s * sb_m_ref[...]


def _rope_tables(pos):
    half = ROT_DIM // 2
    inv_freq = ROPE_THETA ** (-jnp.arange(half, dtype=F32) / half)
    d = np.arange(LANES) % HEAD_DIM
    invf = jnp.where(jnp.asarray(d < ROT_DIM), inv_freq[d % half], 0.0).reshape(1, LANES)
    sa_m = jnp.asarray(np.where(d < half, -1.0, 0.0), F32).reshape(1, LANES)
    sb_m = jnp.asarray(np.where((d >= half) & (d < ROT_DIM), 1.0, 0.0), F32).reshape(1, LANES)
    r = pos.shape[0]
    return pl.pallas_call(
        _rope_table_body,
        out_shape=[jax.ShapeDtypeStruct((r, LANES), F32)] * 3,
        name="rope_tables",
    )(pos, invf, sa_m, sb_m)


def _rope(y, c, sa, sb):
    return y * c + pltpu.roll(y, LANES - ROT_DIM // 2, 1) * sa + pltpu.roll(y, ROT_DIM // 2, 1) * sb


def _seg_rms(r, seg, gain):
    ss = r * r
    hi = ss.astype(BF16)
    lo = (ss - hi.astype(F32)).astype(BF16)
    tot = jnp.dot(hi, seg, preferred_element_type=F32) + jnp.dot(lo, seg, preferred_element_type=F32)
    return r * lax.rsqrt(tot * (1.0 / HEAD_DIM) + EPS) * gain


def _proj_body(mod3d, x_ref, mod_ref, g_ref, w_ref, c_ref, sa_ref, sb_ref, qg_ref, kg_ref, kig_ref,
               lng_ref, lnb_ref, seg_ref, segki_ref,
               q_o, kb_o, vb_o, qi_o, kib_o, kiwi_o, u_o, vg_o, ga_o, gb_o, kt_o, vtf_o, kit_o, vt_o):
    x = x_ref[...]
    m = mod_ref[0] if mod3d else mod_ref[...]
    shift1 = m[:, 0:D_MODEL]
    scale1 = m[:, D_MODEL:2 * D_MODEL]
    ms = jnp.mean(x * x, axis=-1, keepdims=True)
    y = x * lax.rsqrt(ms + EPS) * g_ref[...]
    h = (y * (1.0 + scale1) + shift1).astype(BF16)
    c, sa, sb = c_ref[...], sa_ref[...], sb_ref[...]
    seg = seg_ref[...]

    def proj(a, b):
        return jnp.dot(h, w_ref[:, a:b], preferred_element_type=F32)

    for g2 in range(2):
        r2 = proj(C_Q + g2 * 2 * LANES, C_Q + (g2 + 1) * 2 * LANES)
        for g in (2 * g2, 2 * g2 + 1):
            r = r2[:, (g % 2) * LANES:(g % 2 + 1) * LANES]
            yq = _rope(_seg_rms(r, seg, qg_ref[...]), c, sa, sb) * (HEAD_DIM ** -0.5)
            q_o[:, g * LANES:(g + 1) * LANES] = yq.astype(BF16)
    r_kv = proj(C_K, C_V + LANES)
    yk = _rope(_seg_rms(r_kv[:, 0:LANES], seg, kg_ref[...]), c, sa, sb)
    kt_o[0] = yk.T
    kb_o[...] = yk.astype(BF16)
    r = r_kv[:, LANES:2 * LANES]
    r_t = r.T
    vtf_o[0] = r_t
    vb_o[...] = r.astype(BF16)
    vt_o[0] = r_t.astype(BF16)
    r2 = proj(C_QI, C_QI + 2 * LANES)
    for g in range(2):
        yqi = _rope(r2[:, g * LANES:(g + 1) * LANES], c, sa, sb) * (IDX_DIM ** -0.5 * IDX_HEADS ** -0.5)
        qi_o[:, g * LANES:(g + 1) * LANES] = yqi.astype(BF16)
    r = proj(C_KIWI, C_KIWI + LANES)
    yki = _rope(_seg_rms(r, segki_ref[...], kig_ref[...]), c, sa, sb)
    lane = lax.broadcasted_iota(jnp.int32, r.shape, 1)
    kiwi = jnp.where(lane < IDX_DIM, yki, r)
    kiwi_o[...] = kiwi
    kit_o[0] = kiwi.T[0:IDX_DIM, :]
    kib_o[...] = kiwi[:, 0:IDX_DIM].astype(BF16)
    r = proj(C_U, C_U + GM_WIDTH)
    u_o[...] = jax.nn.gelu(r).astype(BF16)
    r = proj(C_VG, C_VG + GM_WIDTH)
    gl = jax.nn.gelu(r)
    mu = jnp.mean(gl, axis=-1, keepdims=True)
    dv = gl - mu
    var = jnp.mean(dv * dv, axis=-1, keepdims=True)
    vg_o[...] = (dv * lax.rsqrt(var + EPS) * lng_ref[...] + lnb_ref[...]).astype(vg_o.dtype)
    r = proj(C_GATE, C_GATE + D_MODEL)
    ga_o[...] = jax.nn.sigmoid(r).astype(BF16)
    r = proj(C_GATE + D_MODEL, C_GATE + 2 * D_MODEL)
    gb_o[...] = jax.nn.sigmoid(r).astype(BF16)


def _project(x, mod, tables, consts, w_pad, tm, tiles_per_seq, vg_dtype):
    t = x.shape[0]
    mod3d = mod.ndim == 3
    c_t, sa_t, sb_t = tables
    g_mix, qg, kg, kig, lng, lnb, seg, segki = consts
    row = lambda i: (i, 0)
    fixed = lambda i: (0, 0)
    if mod3d:
        mod_spec = pl.BlockSpec((1, 1, 6 * D_MODEL), lambda i: (i // tiles_per_seq, 0, 0))
        tab_spec = pl.BlockSpec((tm, LANES), lambda i: (i % tiles_per_seq, 0))
    else:
        mod_spec = pl.BlockSpec((tm, 6 * D_MODEL), row)
        tab_spec = pl.BlockSpec((1, LANES), fixed)
    widths = [(512, BF16), (128, BF16), (128, BF16), (256, BF16),
              (64, BF16), (128, F32), (512, BF16), (512, vg_dtype), (1024, BF16), (1024, BF16)]
    n_seq, seq = t // (tm * tiles_per_seq), tm * tiles_per_seq
    t_idx = lambda i: (i // tiles_per_seq, 0, i % tiles_per_seq)
    t_outs = [(LANES, F32), (LANES, F32), (IDX_DIM, F32), (LANES, BF16)]
    return pl.pallas_call(
        functools.partial(_proj_body, mod3d),
        grid=(t // tm,),
        in_specs=[pl.BlockSpec((tm, D_MODEL), row), mod_spec,
                  pl.BlockSpec((1, D_MODEL), fixed),
                  pl.BlockSpec((D_MODEL, D_IN_PAD), fixed),
                  tab_spec, tab_spec, tab_spec,
                  pl.BlockSpec((1, LANES), fixed), pl.BlockSpec((1, LANES), fixed), pl.BlockSpec((1, LANES), fixed),
                  pl.BlockSpec((1, GM_WIDTH), fixed), pl.BlockSpec((1, GM_WIDTH), fixed),
                  pl.BlockSpec((LANES, LANES), fixed), pl.BlockSpec((LANES, LANES), fixed)],
        out_specs=[pl.BlockSpec((tm, w), row) for w, _ in widths]
        + [pl.BlockSpec((1, f, tm), t_idx) for f, _ in t_outs],
        out_shape=[jax.ShapeDtypeStruct((t, w), dt) for w, dt in widths]
        + [jax.ShapeDtypeStruct((n_seq, f, seq), dt) for f, dt in t_outs],
        compiler_params=_cparams(("arbitrary",)),
        name="project",
    )(x, mod, g_mix, w_pad, c_t, sa_t, sb_t, qg, kg, kig, lng, lnb, seg, segki)


def _select_bias(s_ref, bias_ref, rows, width, topk, tie_check_start=16):
    nb = width // LANES
    kf = float(topk)
    neg, pos = -jnp.inf, jnp.inf

    def blk(j):
        return s_ref[:, j * LANES:(j + 1) * LANES]

    def count_above(t):
        tb = jnp.broadcast_to(t, (rows, LANES))
        acc = jnp.zeros((rows, LANES), F32)
        for j in range(nb):
            acc = acc + jnp.where(blk(j) > tb, 1.0, 0.0)
        return jnp.sum(acc, axis=1, keepdims=True)

    mx = jnp.full((rows, LANES), neg, F32)
    mn = jnp.full((rows, LANES), pos, F32)
    for j in range(nb):
        b = blk(j)
        mx = jnp.maximum(mx, b)
        mn = jnp.minimum(mn, jnp.where(b == neg, pos, b))
    hi0 = jnp.max(mx, axis=1, keepdims=True)
    smin = jnp.min(mn, axis=1, keepdims=True)
    lo0 = smin - jnp.abs(smin) - 1.0
    f_lo0 = count_above(lo0)
    zeros = jnp.zeros((rows, 1), F32)

    def active_of(f_lo, tie):
        return jnp.logical_and(f_lo > kf, tie == 0.0)

    def cond(st):
        _, _, _, f_lo, _, tie = st
        return jnp.max(jnp.where(active_of(f_lo, tie), 1.0, 0.0)) > 0.0

    def body(st):
        it, lo, hi, f_lo, f_hi, tie = st
        active = active_of(f_lo, tie)
        mid = lo + (hi - lo) * 0.5
        stuck = jnp.logical_or(mid <= lo, mid >= hi)
        cnt = count_above(mid)
        ge = cnt >= kf
        up_lo = jnp.logical_and(active, ge)
        up_hi = jnp.logical_and(active, jnp.logical_not(ge))
        lo = jnp.where(up_lo, mid, lo)
        f_lo = jnp.where(up_lo, cnt, f_lo)
        hi = jnp.where(up_hi, mid, hi)
        f_hi = jnp.where(up_hi, cnt, f_hi)
        tie = jnp.where(jnp.logical_and(active, stuck), 1.0, tie)

        def tie_check(_):
            lob = jnp.broadcast_to(lo, (rows, LANES))
            hib = jnp.broadcast_to(hi, (rows, LANES))
            vmx = jnp.full((rows, LANES), neg, F32)
            vmn = jnp.full((rows, LANES), pos, F32)
            for j in range(nb):
                b = blk(j)
                inn = jnp.logical_and(b > lob, b <= hib)
                vmx = jnp.maximum(vmx, jnp.where(inn, b, neg))
                vmn = jnp.minimum(vmn, jnp.where(inn, b, pos))
            one_value = jnp.max(vmx, axis=1, keepdims=True) == jnp.min(vmn, axis=1, keepdims=True)
            return jnp.where(one_value, 1.0, tie)

        run_check = jnp.logical_and(it >= tie_check_start, it % 4 == 0)
        tie = lax.cond(run_check, tie_check, lambda _: tie, 0)
        return it + 1, lo, hi, f_lo, f_hi, tie

    _, lo, hi, f_lo, f_hi, _ = lax.while_loop(cond, body, (jnp.int32(0), lo0, hi0, f_lo0, zeros, zeros))
    lob = jnp.broadcast_to(lo, (rows, LANES))
    need_prefix = jnp.max(jnp.where(f_lo > kf, 1.0, 0.0)) > 0.0

    @pl.when(jnp.logical_not(need_prefix))
    def _():
        for j in range(nb):
            bias_ref[:, j * LANES:(j + 1) * LANES] = jnp.where(blk(j) > lob, 0.0, neg)

    @pl.when(need_prefix)
    def _():
        hib = jnp.broadcast_to(hi, (rows, LANES))
        need = kf - f_hi
        ri = lax.broadcasted_iota(jnp.int32, (LANES, LANES), 0)
        ci = lax.broadcasted_iota(jnp.int32, (LANES, LANES), 1)
        upper = jnp.where(ri < ci, 1.0, 0.0).astype(BF16)
        off = jnp.zeros((rows, 1), F32)
        for j in range(nb):
            b = blk(j)
            inn = jnp.logical_and(b > lob, b <= hib)
            innf = jnp.where(inn, 1.0, 0.0)
            before = jnp.dot(innf.astype(BF16), upper, preferred_element_type=F32) + off
            sel = jnp.logical_or(b > hib, jnp.logical_and(inn, before < need))
            bias_ref[:, j * LANES:(j + 1) * LANES] = jnp.where(sel, 0.0, neg)
            off = off + jnp.sum(innf, axis=1, keepdims=True)


_NT = (((1,), (1,)), ((), ()))


def _sub_reduce(x, op):
    for sh in (4, 2, 1):
        x = op(x, pltpu.roll(x, sh, 0))
    return x


def _select_bias_t(s_ref, bias_ref, width, topk, n_adm, steps_per_check=4, tie_check_from=4):
    rb = 64
    nb = width // rb
    kf = float(topk)
    neg, pos = -jnp.inf, jnp.inf

    def blk(j):
        return s_ref[j * rb:(j + 1) * rb, :]

    def fold(x, op):
        y = x[0:SUB]
        for a in range(1, rb // SUB):
            y = op(y, x[a * SUB:(a + 1) * SUB])
        return _sub_reduce(y, op)

    def tile(v):
        return jnp.concatenate([v] * (rb // SUB), axis=0)

    def count_above(t):
        tb = tile(t)
        acc = jnp.zeros((rb, LANES), F32)
        for j in range(nb):
            acc = acc + jnp.where(blk(j) > tb, 1.0, 0.0)
        return fold(acc, jnp.add)

    mx = jnp.full((rb, LANES), neg, F32)
    mn = jnp.full((rb, LANES), pos, F32)
    for j in range(nb):
        b = blk(j)
        mx = jnp.maximum(mx, b)
        mn = jnp.minimum(mn, jnp.where(b == neg, pos, b))
    hi0 = fold(mx, jnp.maximum)
    smin = fold(mn, jnp.minimum)
    lo0 = jnp.minimum(smin - jnp.abs(smin) * (2.0 ** -10) - 1e-30, smin - (hi0 - smin) * (2.0 ** -10))
    zeros = jnp.zeros((SUB, LANES), F32)

    def active_of(f_lo, tie):
        return jnp.logical_and(f_lo > kf, tie == 0.0)

    def any_lane(cond):
        return jnp.max(jnp.where(cond, 1.0, 0.0)) > 0.0

    def step(lo, hi, f_lo, f_hi, tie):
        active = active_of(f_lo, tie)
        mid = lo + (hi - lo) * 0.5
        stuck = jnp.logical_or(mid <= lo, mid >= hi)
        cnt = count_above(mid)
        ge = cnt >= kf
        up_lo = jnp.logical_and(active, ge)
        up_hi = jnp.logical_and(active, jnp.logical_not(ge))
        return (jnp.where(up_lo, mid, lo), jnp.where(up_hi, mid, hi), jnp.where(up_lo, cnt, f_lo),
                jnp.where(up_hi, cnt, f_hi), jnp.where(jnp.logical_and(active, stuck), 1.0, tie))

    def cond(st):
        _, _, _, f_lo, _, tie = st
        return any_lane(active_of(f_lo, tie))

    def body(st):
        it, lo, hi, f_lo, f_hi, tie = st
        for _ in range(steps_per_check):
            lo, hi, f_lo, f_hi, tie = step(lo, hi, f_lo, f_hi, tie)

        def tie_check(_):
            lob, hib = tile(lo), tile(hi)
            vmx = jnp.full((rb, LANES), neg, F32)
            vmn = jnp.full((rb, LANES), pos, F32)
            for j in range(nb):
                b = blk(j)
                inn = jnp.logical_and(b > lob, b <= hib)
                vmx = jnp.maximum(vmx, jnp.where(inn, b, neg))
                vmn = jnp.minimum(vmn, jnp.where(inn, b, pos))
            return jnp.where(fold(vmx, jnp.maximum) == fold(vmn, jnp.minimum), 1.0, tie)

        run_check = jnp.logical_and(it + 1 >= tie_check_from, any_lane(active_of(f_lo, tie)))
        tie = lax.cond(run_check, tie_check, lambda _: tie, 0)
        return it + 1, lo, hi, f_lo, f_hi, tie

    _, lo, hi, f_lo, f_hi, _ = lax.while_loop(cond, body, (jnp.int32(0), lo0, hi0, n_adm, zeros, zeros))
    lob = tile(lo)
    need_prefix = any_lane(f_lo > kf)

    @pl.when(jnp.logical_not(need_prefix))
    def _():
        for j in range(nb):
            bias_ref[j * rb:(j + 1) * rb, :] = jnp.where(blk(j) > lob, 0.0, neg)

    @pl.when(need_prefix)
    def _():
        need = (kf - f_hi)[0:1]
        lo1, hi1 = lo[0:1], hi[0:1]
        ri = lax.broadcasted_iota(jnp.int32, (LANES, LANES), 0)
        ci = lax.broadcasted_iota(jnp.int32, (LANES, LANES), 1)
        lower = jnp.where(ci < ri, 1.0, 0.0).astype(BF16)
        off = jnp.zeros((1, LANES), F32)
        for j in range(width // LANES):
            b = s_ref[j * LANES:(j + 1) * LANES, :]
            inn = jnp.logical_and(b > lo1, b <= hi1)
            innf = jnp.where(inn, 1.0, 0.0)
            before = jnp.dot(lower, innf.astype(BF16), preferred_element_type=F32) + off
            sel = jnp.logical_or(b > hi1, jnp.logical_and(inn, before < need))
            bias_ref[j * LANES:(j + 1) * LANES, :] = jnp.where(sel, 0.0, neg)
            off = off + jnp.sum(innf, axis=0, keepdims=True)


def _prompt_attn_body(blk_i, topk, qi_ref, kiwi_ref, kib_ref, q_ref, k_ref, vt_ref, o_ref, s_ref, bias_ref):
    width = (blk_i + 1) * Q_BLOCK
    neg = -jnp.inf
    kiwi_t = kiwi_ref[0].T
    chunk = 512
    for c0 in range(0, width, chunk):
        c1 = min(width, c0 + chunk)
        kib = kib_ref[0, c0:c1, :]
        sc = None
        for h in range(IDX_HEADS):
            d = lax.dot_general(kib, qi_ref[0, :, h * IDX_DIM:(h + 1) * IDX_DIM], _NT,
                                preferred_element_type=F32)
            t = jnp.maximum(d, 0.0) * kiwi_t[IDX_DIM + h:IDX_DIM + h + 1, :]
            sc = t if sc is None else sc + t
        s_ref[c0:c1, :] = sc
    ki = lax.broadcasted_iota(jnp.int32, (Q_BLOCK, Q_BLOCK), 0)
    qj = lax.broadcasted_iota(jnp.int32, (Q_BLOCK, Q_BLOCK), 1)
    d0 = width - Q_BLOCK
    s_ref[d0:width, :] = jnp.where(ki <= qj, s_ref[d0:width, :], neg)
    if width > topk:
        n_adm = (lax.broadcasted_iota(jnp.int32, (SUB, LANES), 1) + (d0 + 1)).astype(F32)
        _select_bias_t(s_ref, bias_ref, width, topk, n_adm)
    else:
        bias_ref[...] = jnp.where(s_ref[...] == neg, neg, 0.0)
    lane = lax.broadcasted_iota(jnp.int32, (Q_BLOCK, LANES), 1)
    hpg = N_HEADS // N_KV_HEADS
    for h in range(N_HEADS):
        g = h // hpg
        qp = q_ref[0, :, (h // 2) * LANES:(h // 2 + 1) * LANES].astype(F32)
        if h % 2 != g:
            qp = pltpu.roll(qp, HEAD_DIM, 1)
        q2 = jnp.where((lane >= HEAD_DIM) == (g == 1), qp, 0.0).astype(BF16)
        s = lax.dot_general(k_ref[0], q2, _NT, preferred_element_type=F32) + bias_ref[...]
        m = jnp.max(s, axis=0, keepdims=True)
        p = jnp.exp(s - m)
        l = jnp.sum(p, axis=0, keepdims=True)
        o_t = jnp.dot(vt_ref[0, g * HEAD_DIM:(g + 1) * HEAD_DIM, :], p.astype(BF16), preferred_element_type=F32)
        o_ref[0, :, h * HEAD_DIM:(h + 1) * HEAD_DIM] = (o_t / l).T.astype(BF16)


def _prompt_attention(qi, kiwi, kib, q, kb, vt, topk):
    b, s, _ = q.shape
    outs = []
    for i in range(s // Q_BLOCK):
        width = (i + 1) * Q_BLOCK
        qblk = lambda bb, i=i: (bb, i, 0)
        kall = lambda bb: (bb, 0, 0)
        outs.append(pl.pallas_call(
            functools.partial(_prompt_attn_body, i, topk),
            grid=(b,),
            in_specs=[pl.BlockSpec((1, Q_BLOCK, IDX_HEADS * IDX_DIM), qblk),
                      pl.BlockSpec((1, Q_BLOCK, LANES), qblk),
                      pl.BlockSpec((1, width, IDX_DIM), kall),
                      pl.BlockSpec((1, Q_BLOCK, N_HEADS * HEAD_DIM), qblk),
                      pl.BlockSpec((1, width, N_KV_HEADS * HEAD_DIM), kall),
                      pl.BlockSpec((1, N_KV_HEADS * HEAD_DIM, width), kall)],
            out_specs=pl.BlockSpec((1, Q_BLOCK, N_HEADS * HEAD_DIM), lambda bb: (bb, 0, 0)),
            out_shape=jax.ShapeDtypeStruct((b, Q_BLOCK, N_HEADS * HEAD_DIM), BF16),
            scratch_shapes=[pltpu.VMEM((width, Q_BLOCK), F32), pltpu.VMEM((width, Q_BLOCK), F32)],
            compiler_params=_cparams(("arbitrary",)),
            name=f"prompt_attn_{i}",
        )(qi, kiwi, kib, q, kb, vt))
    return jnp.stack(outs)


def _page_copies(pt_ref, sample, src_hbm, buf, slot, sem, n_pages):
    return [pltpu.make_async_copy(src_hbm.at[pt_ref[sample, p]],
                                  buf.at[slot, :, pl.ds(p * PAGE_SIZE, PAGE_SIZE)], sem.at[slot])
            for p in range(n_pages)]


def _sample_scores_body(n_pages, pt_ref, kidx_hbm, qi_ref, wi_ref, kin_ref, o_ref, buf, sem):
    s = pl.program_id(0)
    slot = s % 2

    @pl.when(s == 0)
    def _():
        for cp in _page_copies(pt_ref, 0, kidx_hbm, buf, 0, sem, n_pages):
            cp.start()

    @pl.when(s + 1 < pl.num_programs(0))
    def _():
        for cp in _page_copies(pt_ref, s + 1, kidx_hbm, buf, 1 - slot, sem, n_pages):
            cp.start()

    for cp in _page_copies(pt_ref, s, kidx_hbm, buf, slot, sem, n_pages):
        cp.wait()
    qi = qi_ref[0]
    wi = wi_ref[0]
    ki_t = buf[slot].astype(BF16)
    d = jnp.dot(qi, ki_t, preferred_element_type=F32)
    past = jnp.sum(jnp.maximum(d, 0.0) * wi, axis=0, keepdims=True)
    dn = jnp.sum(qi.astype(F32) * kin_ref[0].astype(F32), axis=1, keepdims=True)
    new = jnp.sum(jnp.maximum(dn, 0.0) * wi, axis=0, keepdims=True)
    lane = lax.broadcasted_iota(jnp.int32, (1, LANES), 1)
    tail = jnp.where(lane == 0, jnp.broadcast_to(new, (1, LANES)), -jnp.inf)
    o_ref[0] = jnp.concatenate([past, tail], axis=1)


def _sample_scores(page_table, kidx_pool, qi8, wi8, ki_new):
    n, n_pages = page_table.shape
    past = n_pages * PAGE_SIZE
    grid_spec = pltpu.PrefetchScalarGridSpec(
        num_scalar_prefetch=1,
        grid=(n,),
        in_specs=[pl.BlockSpec(memory_space=pl.ANY),
                  pl.BlockSpec((1, 8, IDX_DIM), lambda s, pt: (s, 0, 0)),
                  pl.BlockSpec((1, 8, 1), lambda s, pt: (s, 0, 0)),
                  pl.BlockSpec((1, 1, IDX_DIM), lambda s, pt: (s, 0, 0))],
        out_specs=pl.BlockSpec((1, 1, past + LANES), lambda s, pt: (s, 0, 0)),
        scratch_shapes=[pltpu.VMEM((2, IDX_DIM, past), F32), pltpu.SemaphoreType.DMA((2,))],
    )
    return pl.pallas_call(
        functools.partial(_sample_scores_body, n_pages),
        grid_spec=grid_spec,
        out_shape=jax.ShapeDtypeStruct((n, 1, past + LANES), F32),
        compiler_params=_cparams(("arbitrary",)),
        name="sample_scores",
    )(page_table, kidx_pool, qi8, wi8, ki_new)


def _sample_select_body(topk, s_ref, bias_ref):
    rows, width = s_ref.shape
    _select_bias(s_ref, bias_ref, rows, width, topk)


def _sample_select(scores, topk):
    return pl.pallas_call(
        functools.partial(_sample_select_body, topk),
        out_shape=jax.ShapeDtypeStruct(scores.shape, F32),
        compiler_params=pltpu.CompilerParams(vmem_limit_bytes=VMEM_LIMIT),
        name="sample_select",
    )(scores)


def _sample_attn_body(n_pages, pt_ref, k_hbm, v_hbm, q_ref, bias_ref, kn_ref, vn_ref, o_ref, kbuf, vbuf, sem):
    s = pl.program_id(0)
    slot = s % 2
    past = n_pages * PAGE_SIZE

    def copies(sample, sl):
        return (_page_copies(pt_ref, sample, k_hbm, kbuf, sl, sem.at[0], n_pages)
                + _page_copies(pt_ref, sample, v_hbm, vbuf, sl, sem.at[1], n_pages))

    @pl.when(s == 0)
    def _():
        for cp in copies(0, 0):
            cp.start()

    @pl.when(s + 1 < pl.num_programs(0))
    def _():
        for cp in copies(s + 1, 1 - slot):
            cp.start()

    for cp in copies(s, slot):
        cp.wait()
    q2 = q_ref[0]
    k_t = kbuf[slot].astype(BF16)
    v_t = vbuf[slot].astype(BF16)
    row = lax.broadcasted_iota(jnp.int32, (N_HEADS, 1), 0)
    first = row < (N_HEADS // N_KV_HEADS)
    bias = bias_ref[0]
    sc = jnp.dot(q2, k_t, preferred_element_type=F32) + bias[:, 0:past]
    sn = jnp.sum(q2.astype(F32) * kn_ref[0].astype(F32), axis=1, keepdims=True) + bias[:, past:past + 1]
    m = jnp.maximum(jnp.max(sc, axis=1, keepdims=True), sn)
    p = jnp.exp(sc - m)
    pn = jnp.exp(sn - m)
    l = jnp.sum(p, axis=1, keepdims=True) + pn
    o2 = lax.dot_general(p.astype(BF16), v_t, _NT, preferred_element_type=F32)
    o2 = o2 + pn.astype(BF16).astype(F32) * vn_ref[0].astype(F32)
    o = jnp.where(first, o2[:, 0:HEAD_DIM], o2[:, HEAD_DIM:2 * HEAD_DIM])
    o_ref[0] = (o / l).astype(BF16)


def _sample_attention(page_table, k_pool, v_pool, q8, bias, k_new, v_new):
    n, n_pages = page_table.shape
    past = n_pages * PAGE_SIZE
    kvw = N_KV_HEADS * HEAD_DIM
    per = lambda s, pt: (s, 0, 0)
    grid_spec = pltpu.PrefetchScalarGridSpec(
        num_scalar_prefetch=1,
        grid=(n,),
        in_specs=[pl.BlockSpec(memory_space=pl.ANY), pl.BlockSpec(memory_space=pl.ANY),
                  pl.BlockSpec((1, N_HEADS, kvw), per),
                  pl.BlockSpec((1, 1, past + LANES), per),
                  pl.BlockSpec((1, 1, kvw), per),
                  pl.BlockSpec((1, 1, kvw), per)],
        out_specs=pl.BlockSpec((1, N_HEADS, HEAD_DIM), per),
        scratch_shapes=[pltpu.VMEM((2, kvw, past), F32), pltpu.VMEM((2, kvw, past), F32),
                        pltpu.SemaphoreType.DMA((2, 2))],
    )
    return pl.pallas_call(
        functools.partial(_sample_attn_body, n_pages),
        grid_spec=grid_spec,
        out_shape=jax.ShapeDtypeStruct((n, N_HEADS, HEAD_DIM), BF16),
        compiler_params=_cparams(("arbitrary",)),
        name="sample_attn",
    )(page_table, k_pool, v_pool, q8, bias, k_new, v_new)


def _gmlp_body(n_chunks, u_ref, vg_ref, w_ref, bt_ref, o_ref):
    ri = lax.broadcasted_iota(jnp.int32, (CHUNK, CHUNK), 0)
    ci = lax.broadcasted_iota(jnp.int32, (CHUNK, CHUNK), 1)
    gd = GM_WIDTH // GM_GROUPS
    for g in range(GM_GROUPS):
        wg = jnp.where(ci <= ri, w_ref[g], 0.0).astype(BF16)
        bg = bt_ref[:, g:g + 1]
        for c in range(n_chunks):
            rows = slice(c * CHUNK, (c + 1) * CHUNK)
            cols = slice(g * gd, (g + 1) * gd)
            mixed = jnp.dot(wg, vg_ref[rows, cols], preferred_element_type=F32) + bg
            o_ref[rows, cols] = (u_ref[rows, cols].astype(F32) * mixed).astype(BF16)


def _gmlp_prompt(u, vg, w, bt, tm):
    t = u.shape[0]
    row = lambda i: (i, 0)
    return pl.pallas_call(
        functools.partial(_gmlp_body, tm // CHUNK),
        grid=(t // tm,),
        in_specs=[pl.BlockSpec((tm, GM_WIDTH), row), pl.BlockSpec((tm, GM_WIDTH), row),
                  pl.BlockSpec((GM_GROUPS, CHUNK, CHUNK), lambda i: (0, 0, 0)),
                  pl.BlockSpec((CHUNK, LANES), lambda i: (0, 0))],
        out_specs=pl.BlockSpec((tm, GM_WIDTH), row),
        out_shape=jax.ShapeDtypeStruct((t, GM_WIDTH), BF16),
        compiler_params=_cparams(("arbitrary",)),
        name="gmlp",
    )(u, vg, w, bt)


def _gmlp_first_row_body(u_ref, vg_ref, w0_ref, b0_ref, o_ref):
    o_ref[...] = (u_ref[...].astype(F32) * (vg_ref[...] * w0_ref[...] + b0_ref[...])).astype(BF16)


def _gmlp_sample(u, vg, w0, b0):
    return pl.pallas_call(
        _gmlp_first_row_body,
        out_shape=jax.ShapeDtypeStruct(u.shape, BF16),
        name="gmlp_first_row",
    )(u, vg, w0, b0)


def _merge_body(mod3d, dispatch, attn_ref, gm_ref, ga_ref, gb_ref, x_ref, mod_ref, wpa_ref, wpg_ref, wo_ref,
                g2_ref, wr_ref, br_ref, x1_o, a_o, b_o):
    tm = x_ref.shape[0]
    attn = attn_ref[...].reshape(tm, N_HEADS * HEAD_DIM)
    a = jnp.dot(attn, wpa_ref[...], preferred_element_type=F32)
    g = jnp.dot(gm_ref[...], wpg_ref[...], preferred_element_type=F32)
    merged = ga_ref[...].astype(F32) * a + gb_ref[...].astype(F32) * g
    out = jnp.dot(merged.astype(BF16), wo_ref[...], preferred_element_type=F32)
    m = mod_ref[0] if mod3d else mod_ref[...]
    x1 = x_ref[...] + m[:, 2 * D_MODEL:3 * D_MODEL] * out
    x1_o[...] = x1
    ms = jnp.mean(x1 * x1, axis=-1, keepdims=True)
    y = x1 * lax.rsqrt(ms + EPS) * g2_ref[...]
    h2 = y * (1.0 + m[:, 4 * D_MODEL:5 * D_MODEL]) + m[:, 3 * D_MODEL:4 * D_MODEL]
    hi = h2.astype(BF16)
    lo = (h2 - hi.astype(F32)).astype(BF16)
    r = jnp.dot(hi, wr_ref[...], preferred_element_type=F32) + jnp.dot(lo, wr_ref[...], preferred_element_type=F32)
    logits = r[:, 0:LANES] + r[:, LANES:2 * LANES] + br_ref[...]
    neg = -jnp.inf
    big = jnp.int32(1 << 20)
    lane = lax.broadcasted_iota(jnp.int32, logits.shape, 1)
    is_g = jnp.logical_and(lane >= N_EXPERTS, lane < N_EXPERTS + N_EXPERT_GROUPS)
    gl = jnp.where(is_g, logits, neg)
    gmax = jnp.max(gl, axis=1, keepdims=True)
    g_lane = jnp.min(jnp.where(gl == gmax, lane, big), axis=1, keepdims=True)
    g_w = 1.0 / jnp.sum(jnp.exp(gl - gmax), axis=1, keepdims=True)
    g_sel = g_lane - N_EXPERTS
    in_grp = jnp.logical_and(lane < N_EXPERTS, (lane >> 3) == g_sel)
    el = jnp.where(in_grp, logits, neg)
    m1 = jnp.max(el, axis=1, keepdims=True)
    i1 = jnp.min(jnp.where(el == m1, lane, big), axis=1, keepdims=True)
    el2 = jnp.where(lane == i1, neg, el)
    m2 = jnp.max(el2, axis=1, keepdims=True)
    i2 = jnp.min(jnp.where(el2 == m2, lane, big), axis=1, keepdims=True)
    e2 = jnp.exp(m2 - m1)
    w1 = g_w / (1.0 + e2)
    w2 = g_w * e2 / (1.0 + e2)
    if not dispatch:
        a_o[...] = hi
        b_o[...] = jnp.where(lane == i1, w1, 0.0) + jnp.where(lane == i2, w2, 0.0)
        return
    low_first = i1 < i2
    ea = jnp.where(low_first, i1, i2) - g_sel * EXPERTS_PER_GROUP
    eb = jnp.where(low_first, i2, i1) - g_sel * EXPERTS_PER_GROUP
    cls = g_sel * N_PAIRS + ((ea * (2 * EXPERTS_PER_GROUP - 1 - ea)) >> 1) + (eb - ea - 1)
    b_o[...] = jnp.broadcast_to(cls, (tm, LANES))
    wa = jnp.where(low_first, w1, w2)
    wb = jnp.where(low_first, w2, w1)
    _store_token_tiles(a_o, _pack_halves(hi.astype(F32)), lax.bitcast_convert_type(
        jnp.where(lane == 0, wa, jnp.where(lane == 1, wb, 0.0)), U32))


def _merge(attn4, gm, ga, gb, x, mod, wpa, wpg, wo, g2, wr, br, tm, tiles_per_seq, dispatch):
    t = x.shape[0]
    mod3d = mod.ndim == 3
    nb = tm // Q_BLOCK
    row = lambda i: (i, 0)
    fixed = lambda i: (0, 0)
    if mod3d:
        mod_spec = pl.BlockSpec((1, 1, 6 * D_MODEL), lambda i: (i // tiles_per_seq, 0, 0))
    else:
        mod_spec = pl.BlockSpec((tm, 6 * D_MODEL), row)
    attn_spec = pl.BlockSpec((nb, 1, Q_BLOCK, N_HEADS * HEAD_DIM),
                             lambda i: (i % tiles_per_seq, i // tiles_per_seq, 0, 0))
    if dispatch:
        extra = [(SUB, LANES, U32), (1, LANES, jnp.int32)]
    else:
        extra = [(1, D_MODEL, BF16), (1, LANES, F32)]
    return pl.pallas_call(
        functools.partial(_merge_body, mod3d, dispatch),
        grid=(t // tm,),
        in_specs=[attn_spec, pl.BlockSpec((tm, GM_WIDTH), row),
                  pl.BlockSpec((tm, D_MODEL), row), pl.BlockSpec((tm, D_MODEL), row),
                  pl.BlockSpec((tm, D_MODEL), row), mod_spec,
                  pl.BlockSpec((N_HEADS * HEAD_DIM, D_MODEL), fixed), pl.BlockSpec((GM_WIDTH, D_MODEL), fixed),
                  pl.BlockSpec((D_MODEL, D_MODEL), fixed), pl.BlockSpec((1, D_MODEL), fixed),
                  pl.BlockSpec((D_MODEL, 2 * LANES), fixed), pl.BlockSpec((1, LANES), fixed)],
        out_specs=[pl.BlockSpec((tm, D_MODEL), row)] + [pl.BlockSpec((tm * r, w), row) for r, w, _ in extra],
        out_shape=[jax.ShapeDtypeStruct((t, D_MODEL), F32)]
        + [jax.ShapeDtypeStruct((t * r, w), dt) for r, w, dt in extra],
        compiler_params=_cparams(("arbitrary",)),
        name="merge",
    )(attn4, gm, ga, gb, x, mod, wpa, wpg, wo, g2, wr, br)


def _moe_body(mod3d, h_ref, gate_ref, x1_ref, mod_ref, wgu_ref, wd_ref, o_ref, acc_ref):
    e = pl.program_id(1)

    @pl.when(e == 0)
    def _():
        acc_ref[...] = jnp.zeros_like(acc_ref)

    gu = jnp.dot(h_ref[...], wgu_ref[0], preferred_element_type=F32)
    a = gu[:, 0:D_EXPERT]
    hid = a * jax.nn.sigmoid(a) * gu[:, D_EXPERT:2 * D_EXPERT]
    gate = gate_ref[...]
    lane = lax.broadcasted_iota(jnp.int32, gate.shape, 1)
    ge = jnp.sum(jnp.where(lane == e, gate, 0.0), axis=1, keepdims=True)
    acc_ref[...] += jnp.dot((hid * ge).astype(BF16), wd_ref[0], preferred_element_type=F32)

    @pl.when(e == pl.num_programs(1) - 1)
    def _():
        m = mod_ref[0] if mod3d else mod_ref[...]
        o_ref[...] = x1_ref[...] + m[:, 5 * D_MODEL:6 * D_MODEL] * acc_ref[...]


def _moe(h2, gate, x1, mod, wgu, wd, tm, tiles_per_seq):
    t = h2.shape[0]
    mod3d = mod.ndim == 3
    row = lambda i, e: (i, 0)
    if mod3d:
        mod_spec = pl.BlockSpec((1, 1, 6 * D_MODEL), lambda i, e: (i // tiles_per_seq, 0, 0))
    else:
        mod_spec = pl.BlockSpec((tm, 6 * D_MODEL), row)
    return pl.pallas_call(
        functools.partial(_moe_body, mod3d),
        grid=(t // tm, N_EXPERTS),
        in_specs=[pl.BlockSpec((tm, D_MODEL), row), pl.BlockSpec((tm, LANES), row),
                  pl.BlockSpec((tm, D_MODEL), row), mod_spec,
                  pl.BlockSpec((1, D_MODEL, 2 * D_EXPERT), lambda i, e: (e, 0, 0)),
                  pl.BlockSpec((1, D_EXPERT, D_MODEL), lambda i, e: (e, 0, 0))],
        out_specs=pl.BlockSpec((tm, D_MODEL), row),
        out_shape=jax.ShapeDtypeStruct((t, D_MODEL), F32),
        scratch_shapes=[pltpu.VMEM((tm, D_MODEL), F32)],
        compiler_params=_cparams(("arbitrary", "arbitrary")),
        name="moe",
    )(h2, gate, x1, mod, wgu, wd)


def _slots_body(cls_ref, slot_o, segend_o, carry, seg_start):
    sweep = pl.program_id(0)
    i = pl.program_id(1)
    tm = cls_ref.shape[0]
    lane = lax.broadcasted_iota(jnp.int32, (tm, LANES), 1)
    hit = lane == cls_ref[...]
    onehot = jnp.where(hit, 1.0, 0.0)

    @pl.when(jnp.logical_and(sweep == 0, i == 0))
    def _():
        carry[...] = jnp.zeros_like(carry)

    @pl.when(sweep == 0)
    def _():
        carry[...] = carry[...] + jnp.sum(onehot, axis=0, keepdims=True)
        slot_o[...] = jnp.zeros_like(slot_o)

        @pl.when(i == pl.num_programs(1) - 1)
        def _():
            padded = jnp.floor((carry[...] + (ROW_TILE - 1)) * (1.0 / ROW_TILE)) * ROW_TILE
            ri = lax.broadcasted_iota(jnp.int32, (LANES, LANES), 0)
            ci = lax.broadcasted_iota(jnp.int32, (LANES, LANES), 1)
            upto = jnp.where(ri <= ci, 1.0, 0.0)
            seg_end = jnp.dot(padded, upto, preferred_element_type=F32, precision=lax.Precision.HIGHEST)
            segend_o[...] = seg_end
            seg_start[...] = seg_end - padded
            carry[...] = jnp.zeros_like(carry)

    @pl.when(sweep == 1)
    def _():
        ri = lax.broadcasted_iota(jnp.int32, (tm, tm), 0)
        ci = lax.broadcasted_iota(jnp.int32, (tm, tm), 1)
        earlier = jnp.where(ci < ri, 1.0, 0.0).astype(BF16)
        before = (jnp.dot(earlier, onehot.astype(BF16), preferred_element_type=F32)
                  + carry[0:1, :] + seg_start[0:1, :])
        slot = jnp.sum(jnp.where(hit, before, 0.0), axis=1, keepdims=True)
        slot_o[...] = jnp.broadcast_to(slot, (tm, LANES)).astype(jnp.int32)
        carry[...] = carry[...] + jnp.sum(onehot, axis=0, keepdims=True)


def _class_slots(cls, tm):
    t = cls.shape[0]
    return pl.pallas_call(
        _slots_body,
        grid=(2, t // tm),
        in_specs=[pl.BlockSpec((tm, LANES), lambda s, i: (i, 0))],
        out_specs=[pl.BlockSpec((tm, LANES), lambda s, i: (i * s, 0)),
                   pl.BlockSpec((SUB, LANES), lambda s, i: (0, 0))],
        out_shape=[jax.ShapeDtypeStruct((t, LANES), jnp.int32), jax.ShapeDtypeStruct((SUB, LANES), F32)],
        scratch_shapes=[pltpu.VMEM((SUB, LANES), F32), pltpu.VMEM((SUB, LANES), F32)],
        compiler_params=_cparams(("arbitrary", "arbitrary")),
        name="class_slots",
    )(cls)


INVERT_CHUNK = 4096


def _invert_body(dest_ref, src_o):
    i = pl.program_id(0)

    @pl.when(i == 0)
    def _():
        src_o[...] = jnp.zeros_like(src_o)

    def put(r, carry):
        src_o[pl.ds(dest_ref[0, 0, r], 1), :] = jnp.full((1, LANES), i * INVERT_CHUNK + r, jnp.int32)
        return carry

    lax.fori_loop(0, INVERT_CHUNK, put, 0, unroll=8)


def _invert(dest, n_sorted):
    t = dest.shape[0]
    return pl.pallas_call(
        _invert_body,
        grid=(t // INVERT_CHUNK,),
        in_specs=[pl.BlockSpec((1, 1, INVERT_CHUNK), lambda i: (i, 0, 0), memory_space=pltpu.SMEM)],
        out_specs=pl.BlockSpec((n_sorted, LANES), lambda i: (0, 0)),
        out_shape=jax.ShapeDtypeStruct((n_sorted, LANES), jnp.int32),
        compiler_params=_cparams(("arbitrary",)),
        name="invert_slots",
    )(dest.reshape(t // INVERT_CHUNK, 1, INVERT_CHUNK))


def _token_fetches(idx_ref, tiles_hbm, buf, slot, sem, n):
    return [pltpu.make_async_copy(tiles_hbm.at[pl.ds(pl.multiple_of(idx_ref[0, 0, r] * SUB, SUB), SUB)],
                                  buf.at[slot, pl.ds(r * SUB, SUB)], sem.at[slot])
            for r in range(n)]


def _start_all(copies):
    for r, cp in enumerate(copies):
        cp.start(priority=r % 2)


def _tile_fetches(src_ref, rows_hbm, buf, slot, sem):
    return _token_fetches(src_ref, rows_hbm, buf, slot, sem, ROW_TILE)


FETCH_AHEAD = 2


def _experts_body(grp_ref, ea_ref, eb_ref, used_ref, s0_ref, s1_ref, s2_ref, rows_hbm, wgu_ref, wd_ref,
                  o_ref, buf, sem):
    j = pl.program_id(0)
    n_buf = FETCH_AHEAD + 1
    slot = j % n_buf
    n_used = used_ref[0]

    @pl.when(j == 0)
    def _():
        _start_all(_tile_fetches(s0_ref, rows_hbm, buf, 0, sem))

    @pl.when(jnp.logical_and(j == 0, 1 < n_used))
    def _():
        _start_all(_tile_fetches(s1_ref, rows_hbm, buf, 1, sem))

    @pl.when(j + FETCH_AHEAD < n_used)
    def _():
        _start_all(_tile_fetches(s2_ref, rows_hbm, buf, (j + FETCH_AHEAD) % n_buf, sem))

    @pl.when(jnp.logical_or(j < n_used, j == 0))
    def _():
        for cp in _tile_fetches(s0_ref, rows_hbm, buf, slot, sem):
            cp.wait()

    @pl.when(j < n_used)
    def _():
        tiles = buf.at[slot]
        x = _unpack_halves(_load_token_words(tiles, ROW_TILE)).astype(BF16)
        wts = lax.bitcast_convert_type(tiles[pl.ds(FEAT_SUB, ROW_TILE, stride=SUB), :], F32)

        def hidden(e, wgt):
            gu = jnp.dot(x, wgu_ref[e], preferred_element_type=F32)
            a = gu[:, 0:D_EXPERT]
            return (a * jax.nn.sigmoid(a) * gu[:, D_EXPERT:2 * D_EXPERT] * wgt).astype(BF16)

        ea, eb = ea_ref[j], eb_ref[j]
        y = (jnp.dot(hidden(ea, wts[:, 0:1]), wd_ref[ea], preferred_element_type=F32)
             + jnp.dot(hidden(eb, wts[:, 1:2]), wd_ref[eb], preferred_element_type=F32))
        _store_token_tiles(o_ref, _pack_halves(y.astype(BF16).astype(F32)), None)

    @pl.when(j >= n_used)
    def _():
        o_ref[...] = jnp.zeros_like(o_ref)


def _experts(grp_t, ea_t, eb_t, n_used, src3, rows, wgu, wd):
    n_tiles = src3.shape[0]
    w_grp = lambda j, grp, ea, eb, nu: (grp[j], 0, 0)
    smem_blk = lambda f: pl.BlockSpec((1, 1, ROW_TILE), f, memory_space=pltpu.SMEM)
    grid_spec = pltpu.PrefetchScalarGridSpec(
        num_scalar_prefetch=4,
        grid=(n_tiles,),
        in_specs=[smem_blk(lambda j, grp, ea, eb, nu: (j, 0, 0)),
                  smem_blk(lambda j, grp, ea, eb, nu: (jnp.minimum(j + 1, n_tiles - 1), 0, 0)),
                  smem_blk(lambda j, grp, ea, eb, nu: (jnp.minimum(j + FETCH_AHEAD, n_tiles - 1), 0, 0)),
                  pl.BlockSpec(memory_space=pl.ANY),
                  pl.BlockSpec((EXPERTS_PER_GROUP, D_MODEL, 2 * D_EXPERT), w_grp),
                  pl.BlockSpec((EXPERTS_PER_GROUP, D_EXPERT, D_MODEL), w_grp)],
        out_specs=pl.BlockSpec((ROW_TILE * SUB, LANES), lambda j, grp, ea, eb, nu: (j, 0)),
        scratch_shapes=[pltpu.VMEM((FETCH_AHEAD + 1, ROW_TILE * SUB, LANES), U32),
                        pltpu.SemaphoreType.DMA((FETCH_AHEAD + 1,))],
    )
    return pl.pallas_call(
        _experts_body,
        grid_spec=grid_spec,
        out_shape=jax.ShapeDtypeStruct((n_tiles * ROW_TILE * SUB, LANES), U32),
        compiler_params=_cparams(("arbitrary",)),
        name="experts",
    )(grp_t, ea_t, eb_t, n_used, src3, src3, src3, rows, wgu, wd)


def _row_fetches(dest_ref, ys_hbm, buf, slot, sem):
    return _token_fetches(dest_ref, ys_hbm, buf, slot, sem, MOVE_TILE)


def _combine_body(mod3d, dcur_ref, dnext_ref, ys_hbm, x1_ref, mod_ref, o_ref, buf, sem):
    i = pl.program_id(0)
    slot = i % 2

    @pl.when(i == 0)
    def _():
        _start_all(_row_fetches(dcur_ref, ys_hbm, buf, 0, sem))

    @pl.when(i + 1 < pl.num_programs(0))
    def _():
        _start_all(_row_fetches(dnext_ref, ys_hbm, buf, 1 - slot, sem))

    for cp in _row_fetches(dcur_ref, ys_hbm, buf, slot, sem):
        cp.wait()
    m = mod_ref[0] if mod3d else mod_ref[...]
    o_ref[...] = x1_ref[...] + m[:, 5 * D_MODEL:6 * D_MODEL] * _unpack_halves(_load_token_words(buf.at[slot], MOVE_TILE))


def _combine(dest3, ys, x1, mod, tiles_per_seq):
    steps = dest3.shape[0]
    t = x1.shape[0]
    mod3d = mod.ndim == 3
    row = lambda i: (i, 0)
    if mod3d:
        mod_spec = pl.BlockSpec((1, 1, 6 * D_MODEL), lambda i: (i // tiles_per_seq, 0, 0))
    else:
        mod_spec = pl.BlockSpec((MOVE_TILE, 6 * D_MODEL), row)
    smem_blk = lambda f: pl.BlockSpec((1, 1, MOVE_TILE), f, memory_space=pltpu.SMEM)
    return pl.pallas_call(
        functools.partial(_combine_body, mod3d),
        grid=(steps,),
        in_specs=[smem_blk(lambda i: (i, 0, 0)), smem_blk(lambda i: (jnp.minimum(i + 1, steps - 1), 0, 0)),
                  pl.BlockSpec(memory_space=pl.ANY),
                  pl.BlockSpec((MOVE_TILE, D_MODEL), row), mod_spec],
        out_specs=pl.BlockSpec((MOVE_TILE, D_MODEL), row),
        out_shape=jax.ShapeDtypeStruct((t, D_MODEL), F32),
        scratch_shapes=[pltpu.VMEM((2, MOVE_TILE * SUB, LANES), U32), pltpu.SemaphoreType.DMA((2,))],
        compiler_params=_cparams(("arbitrary",)),
        name="combine",
    )(dest3, dest3, ys, x1, mod)


def _class_expert_tables():
    ea, eb = [], []
    for g in range(N_EXPERT_GROUPS):
        for a in range(EXPERTS_PER_GROUP):
            for b in range(a + 1, EXPERTS_PER_GROUP):
                ea.append(g * EXPERTS_PER_GROUP + a)
                eb.append(g * EXPERTS_PER_GROUP + b)
    return np.asarray(ea, np.int32), np.asarray(eb, np.int32)


def _moe_dispatched(rows, cls, x1, mod, wgu, wd, tiles_per_seq):
    t = x1.shape[0]
    slots, seg_end8 = _class_slots(cls, 1024)
    seg_end = seg_end8[0, 0:N_CLASSES].astype(jnp.int32)
    dest = slots[:, 0]
    dest3 = dest.reshape(t // MOVE_TILE, 1, MOVE_TILE)
    n_sorted = t + N_CLASSES * ROW_TILE
    tile_row0 = jnp.arange(n_sorted // ROW_TILE, dtype=jnp.int32) * ROW_TILE
    tile_cls = jnp.minimum(jnp.sum((seg_end[None, :] <= tile_row0[:, None]).astype(jnp.int32), axis=1),
                           N_CLASSES - 1)
    ea_np, eb_np = _class_expert_tables()
    in_cls = (tile_cls[:, None] == jnp.arange(N_CLASSES, dtype=jnp.int32)[None, :]).astype(jnp.int32)
    ea_t = jnp.sum(in_cls * jnp.asarray(ea_np % EXPERTS_PER_GROUP)[None, :], axis=1)
    eb_t = jnp.sum(in_cls * jnp.asarray(eb_np % EXPERTS_PER_GROUP)[None, :], axis=1)
    grp_t = tile_cls // N_PAIRS
    n_used = (seg_end[N_CLASSES - 1] // ROW_TILE).astype(jnp.int32).reshape(1)
    src = _invert(dest, n_sorted)[:, 0]
    ys = _experts(grp_t, ea_t, eb_t, n_used, src.reshape(n_sorted // ROW_TILE, 1, ROW_TILE), rows, wgu, wd)
    return _combine(dest3, ys, x1, mod, tiles_per_seq)


def _pad_lanes(v, fill):
    n = v.shape[-1]
    return jnp.concatenate([v, jnp.full((LANES - n,), fill, v.dtype)]).reshape(1, LANES)


def kernel(x_prompt, x_sample, c_prompt, c_sample, cache_k, cache_v, cache_kidx, page_table, w_ada, b_ada, norm_mix_g, norm_ffn_g, w_in, q_norm_g, k_norm_g, kidx_norm_g, gm_ln_g, gm_ln_b, gm_spatial_w, gm_spatial_b, w_proj_attn, w_proj_gmlp, w_out, w_router_group, b_router_group, w_router_expert, b_router_expert, w_expert_gate, w_expert_up, w_expert_down):
    depth = w_ada.shape[0]
    assert depth == 1
    l = 0
    bp, sp, _ = x_prompt.shape
    bs, ss, _ = x_sample.shape
    assert ss == 1
    n_pages = page_table.shape[1]
    past = n_pages * PAGE_SIZE
    tp = bp * sp

    w = w_in[l]
    zpad = jnp.zeros((D_MODEL, LANES - IDX_DIM - IDX_HEADS), F32)
    w_pad = jnp.concatenate([w[:, 0:1024], w[:, 1024:1088], w[:, 1088:1092], zpad, w[:, 1092:]], axis=1).astype(BF16)
    seg_np = (np.arange(LANES)[:, None] // HEAD_DIM) == (np.arange(LANES)[None, :] // HEAD_DIM)
    seg = jnp.asarray(seg_np, BF16)
    segki = jnp.asarray(seg_np & (np.arange(LANES)[:, None] < IDX_DIM) & (np.arange(LANES)[None, :] < IDX_DIM), BF16)
    consts = (norm_mix_g[l].reshape(1, D_MODEL),
              jnp.tile(q_norm_g[l], 2).reshape(1, LANES), jnp.tile(k_norm_g[l], 2).reshape(1, LANES),
              _pad_lanes(kidx_norm_g[l], 1.0),
              gm_ln_g[l].reshape(1, GM_WIDTH), gm_ln_b[l].reshape(1, GM_WIDTH), seg, segki)
    wpa = w_proj_attn[l].astype(BF16)
    wpg = w_proj_gmlp[l].astype(BF16)
    wo = w_out[l].astype(BF16)
    wr32 = jnp.concatenate([w_router_expert[l], w_router_group[l],
                            jnp.zeros((D_MODEL, LANES - N_EXPERTS - N_EXPERT_GROUPS), F32)], axis=1)
    wr_hi = wr32.astype(BF16)
    wr_lo = (wr32 - wr_hi.astype(F32)).astype(BF16)
    wr = jnp.concatenate([wr_hi, wr_lo], axis=1)
    br = _pad_lanes(jnp.concatenate([b_router_expert[l], b_router_group[l]]), 0.0)
    wgu = jnp.concatenate([w_expert_gate[l], w_expert_up[l]], axis=2).astype(BF16)
    wd = w_expert_down[l].astype(BF16)
    g2 = norm_ffn_g[l].reshape(1, D_MODEL)

    mod = _adaln(jnp.concatenate([c_prompt, c_sample], axis=0), w_ada[l], b_ada[l])
    mod_p = mod[0:bp].reshape(bp, 1, 6 * D_MODEL)
    mod_s = mod[bp:bp + bs]
    pos = jnp.concatenate([jnp.arange(sp, dtype=jnp.int32),
                           jnp.full((8,), past, jnp.int32)]).astype(F32).reshape(sp + 8, 1)
    tabs = _rope_tables(pos)
    tabs_p = tuple(t[0:sp] for t in tabs)
    tabs_s = tuple(t[sp:sp + 1] for t in tabs)

    tm = 512
    tps = sp // tm
    (q, kb, vb, qi, kib, kiwi, u, vg, ga, gb, kt_f, vt_f, kit_f, vt) = _project(
        x_prompt.reshape(tp, D_MODEL), mod_p, tabs_p, consts, w_pad, tm, tps, BF16)
    topk_p = min(TOPK_MAX, sp // 4)
    r3 = lambda a: a.reshape(bp, sp, a.shape[-1])
    attn_p = _prompt_attention(r3(qi), r3(kiwi), r3(kib), r3(q), r3(kb), vt, topk_p)
    bt = jnp.concatenate([gm_spatial_b[l].T, jnp.zeros((CHUNK, LANES - GM_GROUPS), F32)], axis=1)
    gm_p = _gmlp_prompt(u, vg, gm_spatial_w[l], bt, tm)
    x1_p, rows_p, cls_p = _merge(attn_p, gm_p, ga, gb, x_prompt.reshape(tp, D_MODEL), mod_p,
                                 wpa, wpg, wo, g2, wr, br, tm, tps, True)
    y_p = _moe_dispatched(rows_p, cls_p, x1_p, mod_p, wgu, wd, sp // MOVE_TILE)

    (q_s, kb_s, vb_s, qi_s, kib_s, kiwi_s, u_s, vg_s, ga_s, gb_s, kt_s, vt_s, kit_s, _) = _project(
        x_sample.reshape(bs, D_MODEL), mod_s, tabs_s, consts, w_pad, bs, 1, F32)
    qi8 = jnp.concatenate([qi_s.reshape(bs, IDX_HEADS, IDX_DIM),
                           jnp.zeros((bs, 8 - IDX_HEADS, IDX_DIM), BF16)], axis=1)
    wi8 = jnp.concatenate([kiwi_s[:, IDX_DIM:IDX_DIM + IDX_HEADS],
                           jnp.zeros((bs, 8 - IDX_HEADS), F32)], axis=1).reshape(bs, 8, 1)
    kidx_t = jnp.transpose(cache_kidx[l], (0, 2, 1))
    scores = _sample_scores(page_table, kidx_t, qi8, wi8, kib_s.reshape(bs, 1, IDX_DIM))
    topk_s = min(TOPK_MAX, (past + ss) // 4)
    bias = _sample_select(scores.reshape(bs, past + LANES), topk_s).reshape(bs, 1, past + LANES)
    kvw = N_KV_HEADS * HEAD_DIM
    k_t = jnp.transpose(cache_k[l], (0, 2, 3, 1)).reshape(-1, kvw, PAGE_SIZE)
    v_t = jnp.transpose(cache_v[l], (0, 2, 3, 1)).reshape(-1, kvw, PAGE_SIZE)
    q3 = q_s.reshape(bs, N_HEADS, HEAD_DIM)
    zq = jnp.zeros_like(q3)
    in_first = (jnp.arange(N_HEADS) < N_HEADS // N_KV_HEADS)[None, :, None]
    q2_s = jnp.where(in_first, jnp.concatenate([q3, zq], axis=2), jnp.concatenate([zq, q3], axis=2))
    attn_s = _sample_attention(page_table, k_t, v_t, q2_s, bias,
                               kb_s.reshape(bs, 1, kvw), vb_s.reshape(bs, 1, kvw))
    gd = GM_WIDTH // GM_GROUPS
    w0 = jnp.repeat(gm_spatial_w[l][:, 0, 0], gd).reshape(1, GM_WIDTH)
    b0 = jnp.repeat(gm_spatial_b[l][:, 0], gd).reshape(1, GM_WIDTH)
    gm_s = _gmlp_sample(u_s, vg_s, w0, b0)
    x1_s, h2_s, gate_s = _merge(attn_s.reshape(1, 1, bs, N_HEADS * HEAD_DIM), gm_s, ga_s, gb_s,
                                x_sample.reshape(bs, D_MODEL), mod_s, wpa, wpg, wo, g2, wr, br, bs, 1, False)
    y_s = _moe(h2_s, gate_s, x1_s, mod_s, wgu, wd, bs, 1)

    def rows_kv(a_t, n, s):
        return jnp.transpose(a_t.reshape(n, N_KV_HEADS, HEAD_DIM, s), (0, 3, 1, 2))[None]

    def rows_ki(a_t):
        return jnp.transpose(a_t, (0, 2, 1))[None]

    return (y_p.reshape(bp, sp, D_MODEL), y_s.reshape(bs, ss, D_MODEL),
            rows_kv(kt_f, bp, sp), rows_kv(vt_f, bp, sp), rows_ki(kit_f),
            rows_kv(kt_s, 1, bs).reshape(1, bs, ss, N_KV_HEADS, HEAD_DIM),
            rows_kv(vt_s, 1, bs).reshape(1, bs, ss, N_KV_HEADS, HEAD_DIM),
            rows_ki(kit_s).reshape(1, bs, ss, IDX_DIM), vg_s.reshape(1, bs, ss, GM_WIDTH))
```

```python
import functools

import numpy as np
import jax
import jax.numpy as jnp
from jax import lax
from jax.experimental import pallas as pl
from jax.experimental.pallas import tpu as pltpu

F32 = jnp.float32
BF16 = jnp.bfloat16
U32 = jnp.uint32

D_MODEL = 1024
N_HEADS = 8
HEAD_DIM = 64
N_KV_HEADS = 2
ROT_DIM = 16
ROPE_THETA = 500000.0
IDX_HEADS = 4
IDX_DIM = 64
TOPK_MAX = 256
Q_BLOCK = 128
GM_WIDTH = 512
GM_GROUPS = 8
CHUNK = 128
N_EXPERT_GROUPS = 4
EXPERTS_PER_GROUP = 8
N_EXPERTS = 32
D_EXPERT = 256
EPS = 1e-6
PAGE_SIZE = 128
LANES = 128
SUB = 8

C_Q, C_K, C_V, C_QI, C_KIWI, C_U, C_VG, C_GATE = 0, 512, 640, 768, 1024, 1152, 1664, 2176
D_IN_PAD = 4224
VMEM_LIMIT = 56 * 1024 * 1024

N_PAIRS = EXPERTS_PER_GROUP * (EXPERTS_PER_GROUP - 1) // 2
N_CLASSES = N_EXPERT_GROUPS * N_PAIRS
ROW_TILE = 128
HALF = D_MODEL // 2
FEAT_SUB = HALF // LANES
MOVE_TILE = 256
HI_MASK = 0xFFFF0000


def _cparams(sem):
    return pltpu.CompilerParams(dimension_semantics=sem, vmem_limit_bytes=VMEM_LIMIT)


def _pack_halves(x):
    n = x.shape[1] // 2
    hi = lax.bitcast_convert_type(x[:, 0:n], U32) & jnp.uint32(HI_MASK)
    lo = lax.bitcast_convert_type(x[:, n:2 * n], U32) >> 16
    return hi | lo


def _unpack_halves(u):
    hi = lax.bitcast_convert_type(u & jnp.uint32(HI_MASK), F32)
    lo = lax.bitcast_convert_type(u << 16, F32)
    return jnp.concatenate([hi, lo], axis=1)


def _store_token_tiles(ref, words, extra):
    n = words.shape[0]
    for s in range(FEAT_SUB):
        ref[pl.ds(s, n, stride=SUB), :] = words[:, s * LANES:(s + 1) * LANES]
    zero = jnp.zeros((n, LANES), U32)
    ref[pl.ds(FEAT_SUB, n, stride=SUB), :] = zero if extra is None else extra
    for s in range(FEAT_SUB + 1, SUB):
        ref[pl.ds(s, n, stride=SUB), :] = zero


def _load_token_words(ref, n):
    return jnp.concatenate([ref[pl.ds(s, n, stride=SUB), :] for s in range(FEAT_SUB)], axis=1)


def _adaln_body(c_ref, w_ref, b_ref, o_ref):
    c = c_ref[...]
    a = c * jax.nn.sigmoid(c)
    o_ref[...] = jnp.dot(a, w_ref[...], preferred_element_type=F32,
                         precision=lax.Precision.HIGHEST) + b_ref[...]


def _adaln(c, w, b):
    r = c.shape[0]
    n = w.shape[1]
    bn = 1536
    return pl.pallas_call(
        _adaln_body,
        grid=(n // bn,),
        in_specs=[pl.BlockSpec((r, D_MODEL), lambda j: (0, 0)),
                  pl.BlockSpec((D_MODEL, bn), lambda j: (0, j)),
                  pl.BlockSpec((1, bn), lambda j: (0, j))],
        out_specs=pl.BlockSpec((r, bn), lambda j: (0, j)),
        out_shape=jax.ShapeDtypeStruct((r, n), F32),
        compiler_params=_cparams(("arbitrary",)),
        name="adaln",
    )(c, w, b.reshape(1, n))


def _rope_table_body(pos_ref, invf_ref, sa_m_ref, sb_m_ref, c_ref, sa_ref, sb_ref):
    ang = pos_ref[...] * invf_ref[...]
    s = jnp.sin(ang)
    c_ref[...] = jnp.cos(ang)
    sa_ref[...] = s * sa_m_ref[...]
    sb_ref[...] = s * sb_m_ref[...]


def _rope_tables(pos):
    half = ROT_DIM // 2
    inv_freq = ROPE_THETA ** (-jnp.arange(half, dtype=F32) / half)
    d = np.arange(LANES) % HEAD_DIM
    invf = jnp.where(jnp.asarray(d < ROT_DIM), inv_freq[d % half], 0.0).reshape(1, LANES)
    sa_m = jnp.asarray(np.where(d < half, -1.0, 0.0), F32).reshape(1, LANES)
    sb_m = jnp.asarray(np.where((d >= half) & (d < ROT_DIM), 1.0, 0.0), F32).reshape(1, LANES)
    r = pos.shape[0]
    return pl.pallas_call(
        _rope_table_body,
        out_shape=[jax.ShapeDtypeStruct((r, LANES), F32)] * 3,
        name="rope_tables",
    )(pos, invf, sa_m, sb_m)


def _rope(y, c, sa, sb):
    return y * c + pltpu.roll(y, LANES - ROT_DIM // 2, 1) * sa + pltpu.roll(y, ROT_DIM // 2, 1) * sb


def _seg_rms(r, seg, gain):
    ss = r * r
    hi = ss.astype(BF16)
    lo = (ss - hi.astype(F32)).astype(BF16)
    tot = jnp.dot(hi, seg, preferred_element_type=F32) + jnp.dot(lo, seg, preferred_element_type=F32)
    return r * lax.rsqrt(tot * (1.0 / HEAD_DIM) + EPS) * gain


def _proj_body(mod3d, x_ref, mod_ref, g_ref, w_ref, c_ref, sa_ref, sb_ref, qg_ref, kg_ref, kig_ref,
               lng_ref, lnb_ref, seg_ref, segki_ref,
               q_o, kb_o, vb_o, qi_o, kib_o, kiwi_o, u_o, vg_o, ga_o, gb_o, kt_o, vtf_o, kit_o, vt_o):
    x = x_ref[...]
    m = mod_ref[0] if mod3d else mod_ref[...]
    shift1 = m[:, 0:D_MODEL]
    scale1 = m[:, D_MODEL:2 * D_MODEL]
    ms = jnp.mean(x * x, axis=-1, keepdims=True)
    y = x * lax.rsqrt(ms + EPS) * g_ref[...]
    h = (y * (1.0 + scale1) + shift1).astype(BF16)
    c, sa, sb = c_ref[...], sa_ref[...], sb_ref[...]
    seg = seg_ref[...]

    def proj(a, b):
        return jnp.dot(h, w_ref[:, a:b], preferred_element_type=F32)

    for g2 in range(2):
        r2 = proj(C_Q + g2 * 2 * LANES, C_Q + (g2 + 1) * 2 * LANES)
        for g in (2 * g2, 2 * g2 + 1):
            r = r2[:, (g % 2) * LANES:(g % 2 + 1) * LANES]
            yq = _rope(_seg_rms(r, seg, qg_ref[...]), c, sa, sb) * (HEAD_DIM ** -0.5)
            q_o[:, g * LANES:(g + 1) * LANES] = yq.astype(BF16)
    r_kv = proj(C_K, C_V + LANES)
    yk = _rope(_seg_rms(r_kv[:, 0:LANES], seg, kg_ref[...]), c, sa, sb)
    kt_o[0] = yk.T
    kb_o[...] = yk.astype(BF16)
    r = r_kv[:, LANES:2 * LANES]
    r_t = r.T
    vtf_o[0] = r_t
    vb_o[...] = r.astype(BF16)
    vt_o[0] = r_t.astype(BF16)
    r2 = proj(C_QI, C_QI + 2 * LANES)
    for g in range(2):
        yqi = _rope(r2[:, g * LANES:(g + 1) * LANES], c, sa, sb) * (IDX_DIM ** -0.5 * IDX_HEADS ** -0.5)
        qi_o[:, g * LANES:(g + 1) * LANES] = yqi.astype(BF16)
    r = proj(C_KIWI, C_KIWI + LANES)
    yki = _rope(_seg_rms(r, segki_ref[...], kig_ref[...]), c, sa, sb)
    lane = lax.broadcasted_iota(jnp.int32, r.shape, 1)
    kiwi = jnp.where(lane < IDX_DIM, yki, r)
    kiwi_o[...] = kiwi
    kit_o[0] = kiwi.T[0:IDX_DIM, :]
    kib_o[...] = kiwi[:, 0:IDX_DIM].astype(BF16)
    r = proj(C_U, C_U + GM_WIDTH)
    u_o[...] = jax.nn.gelu(r).astype(BF16)
    r = proj(C_VG, C_VG + GM_WIDTH)
    gl = jax.nn.gelu(r)
    mu = jnp.mean(gl, axis=-1, keepdims=True)
    dv = gl - mu
    var = jnp.mean(dv * dv, axis=-1, keepdims=True)
    vg_o[...] = (dv * lax.rsqrt(var + EPS) * lng_ref[...] + lnb_ref[...]).astype(vg_o.dtype)
    r = proj(C_GATE, C_GATE + D_MODEL)
    ga_o[...] = jax.nn.sigmoid(r).astype(BF16)
    r = proj(C_GATE + D_MODEL, C_GATE + 2 * D_MODEL)
    gb_o[...] = jax.nn.sigmoid(r).astype(BF16)


def _project(x, mod, tables, consts, w_pad, tm, tiles_per_seq, vg_dtype):
    t = x.shape[0]
    mod3d = mod.ndim == 3
    c_t, sa_t, sb_t = tables
    g_mix, qg, kg, kig, lng, lnb, seg, segki = consts
    row = lambda i: (i, 0)
    fixed = lambda i: (0, 0)
    if mod3d:
        mod_spec = pl.BlockSpec((1, 1, 6 * D_MODEL), lambda i: (i // tiles_per_seq, 0, 0))
        tab_spec = pl.BlockSpec((tm, LANES), lambda i: (i % tiles_per_seq, 0))
    else:
        mod_spec = pl.BlockSpec((tm, 6 * D_MODEL), row)
        tab_spec = pl.BlockSpec((1, LANES), fixed)
    widths = [(512, BF16), (128, BF16), (128, BF16), (256, BF16),
              (64, BF16), (128, F32), (512, BF16), (512, vg_dtype), (1024, BF16), (1024, BF16)]
    n_seq, seq = t // (tm * tiles_per_seq), tm * tiles_per_seq
    t_idx = lambda i: (i // tiles_per_seq, 0, i % tiles_per_seq)
    t_outs = [(LANES, F32), (LANES, F32), (IDX_DIM, F32), (LANES, BF16)]
    return pl.pallas_call(
        functools.partial(_proj_body, mod3d),
        grid=(t // tm,),
        in_specs=[pl.BlockSpec((tm, D_MODEL), row), mod_spec,
                  pl.BlockSpec((1, D_MODEL), fixed),
                  pl.BlockSpec((D_MODEL, D_IN_PAD), fixed),
                  tab_spec, tab_spec, tab_spec,
                  pl.BlockSpec((1, LANES), fixed), pl.BlockSpec((1, LANES), fixed), pl.BlockSpec((1, LANES), fixed),
                  pl.BlockSpec((1, GM_WIDTH), fixed), pl.BlockSpec((1, GM_WIDTH), fixed),
                  pl.BlockSpec((LANES, LANES), fixed), pl.BlockSpec((LANES, LANES), fixed)],
        out_specs=[pl.BlockSpec((tm, w), row) for w, _ in widths]
        + [pl.BlockSpec((1, f, tm), t_idx) for f, _ in t_outs],
        out_shape=[jax.ShapeDtypeStruct((t, w), dt) for w, dt in widths]
        + [jax.ShapeDtypeStruct((n_seq, f, seq), dt) for f, dt in t_outs],
        compiler_params=_cparams(("arbitrary",)),
        name="project",
    )(x, mod, g_mix, w_pad, c_t, sa_t, sb_t, qg, kg, kig, lng, lnb, seg, segki)


def _select_bias(s_ref, bias_ref, rows, width, topk, tie_check_start=16):
    nb = width // LANES
    kf = float(topk)
    neg, pos = -jnp.inf, jnp.inf

    def blk(j):
        return s_ref[:, j * LANES:(j + 1) * LANES]

    def count_above(t):
        tb = jnp.broadcast_to(t, (rows, LANES))
        acc = jnp.zeros((rows, LANES), F32)
        for j in range(nb):
            acc = acc + jnp.where(blk(j) > tb, 1.0, 0.0)
        return jnp.sum(acc, axis=1, keepdims=True)

    mx = jnp.full((rows, LANES), neg, F32)
    mn = jnp.full((rows, LANES), pos, F32)
    for j in range(nb):
        b = blk(j)
        mx = jnp.maximum(mx, b)
        mn = jnp.minimum(mn, jnp.where(b == neg, pos, b))
    hi0 = jnp.max(mx, axis=1, keepdims=True)
    smin = jnp.min(mn, axis=1, keepdims=True)
    lo0 = smin - jnp.abs(smin) - 1.0
    f_lo0 = count_above(lo0)
    zeros = jnp.zeros((rows, 1), F32)

    def active_of(f_lo, tie):
        return jnp.logical_and(f_lo > kf, tie == 0.0)

    def cond(st):
        _, _, _, f_lo, _, tie = st
        return jnp.max(jnp.where(active_of(f_lo, tie), 1.0, 0.0)) > 0.0

    def body(st):
        it, lo, hi, f_lo, f_hi, tie = st
        active = active_of(f_lo, tie)
        mid = lo + (hi - lo) * 0.5
        stuck = jnp.logical_or(mid <= lo, mid >= hi)
        cnt = count_above(mid)
        ge = cnt >= kf
        up_lo = jnp.logical_and(active, ge)
        up_hi = jnp.logical_and(active, jnp.logical_not(ge))
        lo = jnp.where(up_lo, mid, lo)
        f_lo = jnp.where(up_lo, cnt, f_lo)
        hi = jnp.where(up_hi, mid, hi)
        f_hi = jnp.where(up_hi, cnt, f_hi)
        tie = jnp.where(jnp.logical_and(active, stuck), 1.0, tie)

        def tie_check(_):
            lob = jnp.broadcast_to(lo, (rows, LANES))
            hib = jnp.broadcast_to(hi, (rows, LANES))
            vmx = jnp.full((rows, LANES), neg, F32)
            vmn = jnp.full((rows, LANES), pos, F32)
            for j in range(nb):
                b = blk(j)
                inn = jnp.logical_and(b > lob, b <= hib)
                vmx = jnp.maximum(vmx, jnp.where(inn, b, neg))
                vmn = jnp.minimum(vmn, jnp.where(inn, b, pos))
            one_value = jnp.max(vmx, axis=1, keepdims=True) == jnp.min(vmn, axis=1, keepdims=True)
            return jnp.where(one_value, 1.0, tie)

        run_check = jnp.logical_and(it >= tie_check_start, it % 4 == 0)
        tie = lax.cond(run_check, tie_check, lambda _: tie, 0)
        return it + 1, lo, hi, f_lo, f_hi, tie

    _, lo, hi, f_lo, f_hi, _ = lax.while_loop(cond, body, (jnp.int32(0), lo0, hi0, f_lo0, zeros, zeros))
    lob = jnp.broadcast_to(lo, (rows, LANES))
    need_prefix = jnp.max(jnp.where(f_lo > kf, 1.0, 0.0)) > 0.0

    @pl.when(jnp.logical_not(need_prefix))
    def _():
        for j in range(nb):
            bias_ref[:, j * LANES:(j + 1) * LANES] = jnp.where(blk(j) > lob, 0.0, neg)

    @pl.when(need_prefix)
    def _():
        hib = jnp.broadcast_to(hi, (rows, LANES))
        need = kf - f_hi
        ri = lax.broadcasted_iota(jnp.int32, (LANES, LANES), 0)
        ci = lax.broadcasted_iota(jnp.int32, (LANES, LANES), 1)
        upper = jnp.where(ri < ci, 1.0, 0.0).astype(BF16)
        off = jnp.zeros((rows, 1), F32)
        for j in range(nb):
            b = blk(j)
            inn = jnp.logical_and(b > lob, b <= hib)
            innf = jnp.where(inn, 1.0, 0.0)
            before = jnp.dot(innf.astype(BF16), upper, preferred_element_type=F32) + off
            sel = jnp.logical_or(b > hib, jnp.logical_and(inn, before < need))
            bias_ref[:, j * LANES:(j + 1) * LANES] = jnp.where(sel, 0.0, neg)
            off = off + jnp.sum(innf, axis=1, keepdims=True)


_NT = (((1,), (1,)), ((), ()))


def _sub_reduce(x, op):
    for sh in (4, 2, 1):
        x = op(x, pltpu.roll(x, sh, 0))
    return x


def _select_bias_t(s_ref, bias_ref, width, topk, n_adm, steps_per_check=4, tie_check_from=4):
    rb = 64
    nb = width // rb
    kf = float(topk)
    neg, pos = -jnp.inf, jnp.inf

    def blk(j):
        return s_ref[j * rb:(j + 1) * rb, :]

    def fold(x, op):
        y = x[0:SUB]
        for a in range(1, rb // SUB):
            y = op(y, x[a * SUB:(a + 1) * SUB])
        return _sub_reduce(y, op)

    def tile(v):
        return jnp.concatenate([v] * (rb // SUB), axis=0)

    def count_above(t):
        tb = tile(t)
        acc = jnp.zeros((rb, LANES), F32)
        for j in range(nb):
            acc = acc + jnp.where(blk(j) > tb, 1.0, 0.0)
        return fold(acc, jnp.add)

    mx = jnp.full((rb, LANES), neg, F32)
    mn = jnp.full((rb, LANES), pos, F32)
    for j in range(nb):
        b = blk(j)
        mx = jnp.maximum(mx, b)
        mn = jnp.minimum(mn, jnp.where(b == neg, pos, b))
    hi0 = fold(mx, jnp.maximum)
    smin = fold(mn, jnp.minimum)
    lo0 = jnp.minimum(smin - jnp.abs(smin) * (2.0 ** -10) - 1e-30, smin - (hi0 - smin) * (2.0 ** -10))
    zeros = jnp.zeros((SUB, LANES), F32)

    def active_of(f_lo, tie):
        return jnp.logical_and(f_lo > kf, tie == 0.0)

    def any_lane(cond):
        return jnp.max(jnp.where(cond, 1.0, 0.0)) > 0.0

    def step(lo, hi, f_lo, f_hi, tie):
        active = active_of(f_lo, tie)
        mid = lo + (hi - lo) * 0.5
        stuck = jnp.logical_or(mid <= lo, mid >= hi)
        cnt = count_above(mid)
        ge = cnt >= kf
        up_lo = jnp.logical_and(active, ge)
        up_hi = jnp.logical_and(active, jnp.logical_not(ge))
        return (jnp.where(up_lo, mid, lo), jnp.where(up_hi, mid, hi), jnp.where(up_lo, cnt, f_lo),
                jnp.where(up_hi, cnt, f_hi), jnp.where(jnp.logical_and(active, stuck), 1.0, tie))

    def cond(st):
        _, _, _, f_lo, _, tie = st
        return any_lane(active_of(f_lo, tie))

    def body(st):
        it, lo, hi, f_lo, f_hi, tie = st
        for _ in range(steps_per_check):
            lo, hi, f_lo, f_hi, tie = step(lo, hi, f_lo, f_hi, tie)

        def tie_check(_):
            lob, hib = tile(lo), tile(hi)
            vmx = jnp.full((rb, LANES), neg, F32)
            vmn = jnp.full((rb, LANES), pos, F32)
            for j in range(nb):
                b = blk(j)
                inn = jnp.logical_and(b > lob, b <= hib)
                vmx = jnp.maximum(vmx, jnp.where(inn, b, neg))
                vmn = jnp.minimum(vmn, jnp.where(inn, b, pos))
            return jnp.where(fold(vmx, jnp.maximum) == fold(vmn, jnp.minimum), 1.0, tie)

        run_check = jnp.logical_and(it + 1 >= tie_check_from, any_lane(active_of(f_lo, tie)))
        tie = lax.cond(run_check, tie_check, lambda _: tie, 0)
        return it + 1, lo, hi, f_lo, f_hi, tie

    _, lo, hi, f_lo, f_hi, _ = lax.while_loop(cond, body, (jnp.int32(0), lo0, hi0, n_adm, zeros, zeros))
    lob = tile(lo)
    need_prefix = any_lane(f_lo > kf)

    @pl.when(jnp.logical_not(need_prefix))
    def _():
        for j in range(nb):
            bias_ref[j * rb:(j + 1) * rb, :] = jnp.where(blk(j) > lob, 0.0, neg)

    @pl.when(need_prefix)
    def _():
        need = (kf - f_hi)[0:1]
        lo1, hi1 = lo[0:1], hi[0:1]
        ri = lax.broadcasted_iota(jnp.int32, (LANES, LANES), 0)
        ci = lax.broadcasted_iota(jnp.int32, (LANES, LANES), 1)
        lower = jnp.where(ci < ri, 1.0, 0.0).astype(BF16)
        off = jnp.zeros((1, LANES), F32)
        for j in range(width // LANES):
            b = s_ref[j * LANES:(j + 1) * LANES, :]
            inn = jnp.logical_and(b > lo1, b <= hi1)
            innf = jnp.where(inn, 1.0, 0.0)
            before = jnp.dot(lower, innf.astype(BF16), preferred_element_type=F32) + off
            sel = jnp.logical_or(b > hi1, jnp.logical_and(inn, before < need))
            bias_ref[j * LANES:(j + 1) * LANES, :] = jnp.where(sel, 0.0, neg)
            off = off + jnp.sum(innf, axis=0, keepdims=True)


def _prompt_attn_body(blk_i, topk, qi_ref, kiwi_ref, kib_ref, q_ref, k_ref, vt_ref, o_ref, s_ref, bias_ref):
    width = (blk_i + 1) * Q_BLOCK
    neg = -jnp.inf
    kiwi_t = kiwi_ref[0].T
    chunk = 512
    for c0 in range(0, width, chunk):
        c1 = min(width, c0 + chunk)
        kib = kib_ref[0, c0:c1, :]
        sc = None
        for h in range(IDX_HEADS):
            d = lax.dot_general(kib, qi_ref[0, :, h * IDX_DIM:(h + 1) * IDX_DIM], _NT,
                                preferred_element_type=F32)
            t = jnp.maximum(d, 0.0) * kiwi_t[IDX_DIM + h:IDX_DIM + h + 1, :]
            sc = t if sc is None else sc + t
        s_ref[c0:c1, :] = sc
    ki = lax.broadcasted_iota(jnp.int32, (Q_BLOCK, Q_BLOCK), 0)
    qj = lax.broadcasted_iota(jnp.int32, (Q_BLOCK, Q_BLOCK), 1)
    d0 = width - Q_BLOCK
    s_ref[d0:width, :] = jnp.where(ki <= qj, s_ref[d0:width, :], neg)
    if width > topk:
        n_adm = (lax.broadcasted_iota(jnp.int32, (SUB, LANES), 1) + (d0 + 1)).astype(F32)
        _select_bias_t(s_ref, bias_ref, width, topk, n_adm)
    else:
        bias_ref[...] = jnp.where(s_ref[...] == neg, neg, 0.0)
    lane = lax.broadcasted_iota(jnp.int32, (Q_BLOCK, LANES), 1)
    hpg = N_HEADS // N_KV_HEADS
    for h in range(N_HEADS):
        g = h // hpg
        qp = q_ref[0, :, (h // 2) * LANES:(h // 2 + 1) * LANES].astype(F32)
        if h % 2 != g:
            qp = pltpu.roll(qp, HEAD_DIM, 1)
        q2 = jnp.where((lane >= HEAD_DIM) == (g == 1), qp, 0.0).astype(BF16)
        s = lax.dot_general(k_ref[0], q2, _NT, preferred_element_type=F32) + bias_ref[...]
        m = jnp.max(s, axis=0, keepdims=True)
        p = jnp.exp(s - m)
        l = jnp.sum(p, axis=0, keepdims=True)
        o_t = jnp.dot(vt_ref[0, g * HEAD_DIM:(g + 1) * HEAD_DIM, :], p.astype(BF16), preferred_element_type=F32)
        o_ref[0, :, h * HEAD_DIM:(h + 1) * HEAD_DIM] = (o_t / l).T.astype(BF16)


def _prompt_attention(qi, kiwi, kib, q, kb, vt, topk):
    b, s, _ = q.shape
    outs = []
    for i in range(s // Q_BLOCK):
        width = (i + 1) * Q_BLOCK
        qblk = lambda bb, i=i: (bb, i, 0)
        kall = lambda bb: (bb, 0, 0)
        outs.append(pl.pallas_call(
            functools.partial(_prompt_attn_body, i, topk),
            grid=(b,),
            in_specs=[pl.BlockSpec((1, Q_BLOCK, IDX_HEADS * IDX_DIM), qblk),
                      pl.BlockSpec((1, Q_BLOCK, LANES), qblk),
                      pl.BlockSpec((1, width, IDX_DIM), kall),
                      pl.BlockSpec((1, Q_BLOCK, N_HEADS * HEAD_DIM), qblk),
                      pl.BlockSpec((1, width, N_KV_HEADS * HEAD_DIM), kall),
                      pl.BlockSpec((1, N_KV_HEADS * HEAD_DIM, width), kall)],
            out_specs=pl.BlockSpec((1, Q_BLOCK, N_HEADS * HEAD_DIM), lambda bb: (bb, 0, 0)),
            out_shape=jax.ShapeDtypeStruct((b, Q_BLOCK, N_HEADS * HEAD_DIM), BF16),
            scratch_shapes=[pltpu.VMEM((width, Q_BLOCK), F32), pltpu.VMEM((width, Q_BLOCK), F32)],
            compiler_params=_cparams(("arbitrary",)),
            name=f"prompt_attn_{i}",
        )(qi, kiwi, kib, q, kb, vt))
    return jnp.stack(outs)


def _page_copies(pt_ref, sample, src_hbm, buf, slot, sem, n_pages):
    return [pltpu.make_async_copy(src_hbm.at[pt_ref[sample, p]],
                                  buf.at[slot, :, pl.ds(p * PAGE_SIZE, PAGE_SIZE)], sem.at[slot])
            for p in range(n_pages)]


def _sample_scores_body(n_pages, pt_ref, kidx_hbm, qi_ref, wi_ref, kin_ref, o_ref, buf, sem):
    s = pl.program_id(0)
    slot = s % 2

    @pl.when(s == 0)
    def _():
        for cp in _page_copies(pt_ref, 0, kidx_hbm, buf, 0, sem, n_pages):
            cp.start()

    @pl.when(s + 1 < pl.num_programs(0))
    def _():
        for cp in _page_copies(pt_ref, s + 1, kidx_hbm, buf, 1 - slot, sem, n_pages):
            cp.start()

    for cp in _page_copies(pt_ref, s, kidx_hbm, buf, slot, sem, n_pages):
        cp.wait()
    qi = qi_ref[0]
    wi = wi_ref[0]
    ki_t = buf[slot].astype(BF16)
    d = jnp.dot(qi, ki_t, preferred_element_type=F32)
    past = jnp.sum(jnp.maximum(d, 0.0) * wi, axis=0, keepdims=True)
    dn = jnp.sum(qi.astype(F32) * kin_ref[0].astype(F32), axis=1, keepdims=True)
    new = jnp.sum(jnp.maximum(dn, 0.0) * wi, axis=0, keepdims=True)
    lane = lax.broadcasted_iota(jnp.int32, (1, LANES), 1)
    tail = jnp.where(lane == 0, jnp.broadcast_to(new, (1, LANES)), -jnp.inf)
    o_ref[0] = jnp.concatenate([past, tail], axis=1)


def _sample_scores(page_table, kidx_pool, qi8, wi8, ki_new):
    n, n_pages = page_table.shape
    past = n_pages * PAGE_SIZE
    grid_spec = pltpu.PrefetchScalarGridSpec(
        num_scalar_prefetch=1,
        grid=(n,),
        in_specs=[pl.BlockSpec(memory_space=pl.ANY),
                  pl.BlockSpec((1, 8, IDX_DIM), lambda s, pt: (s, 0, 0)),
                  pl.BlockSpec((1, 8, 1), lambda s, pt: (s, 0, 0)),
                  pl.BlockSpec((1, 1, IDX_DIM), lambda s, pt: (s, 0, 0))],
        out_specs=pl.BlockSpec((1, 1, past + LANES), lambda s, pt: (s, 0, 0)),
        scratch_shapes=[pltpu.VMEM((2, IDX_DIM, past), F32), pltpu.SemaphoreType.DMA((2,))],
    )
    return pl.pallas_call(
        functools.partial(_sample_scores_body, n_pages),
        grid_spec=grid_spec,
        out_shape=jax.ShapeDtypeStruct((n, 1, past + LANES), F32),
        compiler_params=_cparams(("arbitrary",)),
        name="sample_scores",
    )(page_table, kidx_pool, qi8, wi8, ki_new)


def _sample_select_body(topk, s_ref, bias_ref):
    rows, width = s_ref.shape
    _select_bias(s_ref, bias_ref, rows, width, topk)


def _sample_select(scores, topk):
    return pl.pallas_call(
        functools.partial(_sample_select_body, topk),
        out_shape=jax.ShapeDtypeStruct(scores.shape, F32),
        compiler_params=pltpu.CompilerParams(vmem_limit_bytes=VMEM_LIMIT),
        name="sample_select",
    )(scores)


def _sample_attn_body(n_pages, pt_ref, k_hbm, v_hbm, q_ref, bias_ref, kn_ref, vn_ref, o_ref, kbuf, vbuf, sem):
    s = pl.program_id(0)
    slot = s % 2
    past = n_pages * PAGE_SIZE

    def copies(sample, sl):
        return (_page_copies(pt_ref, sample, k_hbm, kbuf, sl, sem.at[0], n_pages)
                + _page_copies(pt_ref, sample, v_hbm, vbuf, sl, sem.at[1], n_pages))

    @pl.when(s == 0)
    def _():
        for cp in copies(0, 0):
            cp.start()

    @pl.when(s + 1 < pl.num_programs(0))
    def _():
        for cp in copies(s + 1, 1 - slot):
            cp.start()

    for cp in copies(s, slot):
        cp.wait()
    q2 = q_ref[0]
    k_t = kbuf[slot].astype(BF16)
    v_t = vbuf[slot].astype(BF16)
    row = lax.broadcasted_iota(jnp.int32, (N_HEADS, 1), 0)
    first = row < (N_HEADS // N_KV_HEADS)
    bias = bias_ref[0]
    sc = jnp.dot(q2, k_t, preferred_element_type=F32) + bias[:, 0:past]
    sn = jnp.sum(q2.astype(F32) * kn_ref[0].astype(F32), axis=1, keepdims=True) + bias[:, past:past + 1]
    m = jnp.maximum(jnp.max(sc, axis=1, keepdims=True), sn)
    p = jnp.exp(sc - m)
    pn = jnp.exp(sn - m)
    l = jnp.sum(p, axis=1, keepdims=True) + pn
    o2 = lax.dot_general(p.astype(BF16), v_t, _NT, preferred_element_type=F32)
    o2 = o2 + pn.astype(BF16).astype(F32) * vn_ref[0].astype(F32)
    o = jnp.where(first, o2[:, 0:HEAD_DIM], o2[:, HEAD_DIM:2 * HEAD_DIM])
    o_ref[0] = (o / l).astype(BF16)


def _sample_attention(page_table, k_pool, v_pool, q8, bias, k_new, v_new):
    n, n_pages = page_table.shape
    past = n_pages * PAGE_SIZE
    kvw = N_KV_HEADS * HEAD_DIM
    per = lambda s, pt: (s, 0, 0)
    grid_spec = pltpu.PrefetchScalarGridSpec(
        num_scalar_prefetch=1,
        grid=(n,),
        in_specs=[pl.BlockSpec(memory_space=pl.ANY), pl.BlockSpec(memory_space=pl.ANY),
                  pl.BlockSpec((1, N_HEADS, kvw), per),
                  pl.BlockSpec((1, 1, past + LANES), per),
                  pl.BlockSpec((1, 1, kvw), per),
                  pl.BlockSpec((1, 1, kvw), per)],
        out_specs=pl.BlockSpec((1, N_HEADS, HEAD_DIM), per),
        scratch_shapes=[pltpu.VMEM((2, kvw, past), F32), pltpu.VMEM((2, kvw, past), F32),
                        pltpu.SemaphoreType.DMA((2, 2))],
    )
    return pl.pallas_call(
        functools.partial(_sample_attn_body, n_pages),
        grid_spec=grid_spec,
        out_shape=jax.ShapeDtypeStruct((n, N_HEADS, HEAD_DIM), BF16),
        compiler_params=_cparams(("arbitrary",)),
        name="sample_attn",
    )(page_table, k_pool, v_pool, q8, bias, k_new, v_new)


def _gmlp_body(n_chunks, u_ref, vg_ref, w_ref, bt_ref, o_ref):
    ri = lax.broadcasted_iota(jnp.int32, (CHUNK, CHUNK), 0)
    ci = lax.broadcasted_iota(jnp.int32, (CHUNK, CHUNK), 1)
    gd = GM_WIDTH // GM_GROUPS
    for g in range(GM_GROUPS):
        wg = jnp.where(ci <= ri, w_ref[g], 0.0).astype(BF16)
        bg = bt_ref[:, g:g + 1]
        for c in range(n_chunks):
            rows = slice(c * CHUNK, (c + 1) * CHUNK)
            cols = slice(g * gd, (g + 1) * gd)
            mixed = jnp.dot(wg, vg_ref[rows, cols], preferred_element_type=F32) + bg
            o_ref[rows, cols] = (u_ref[rows, cols].astype(F32) * mixed).astype(BF16)


def _gmlp_prompt(u, vg, w, bt, tm):
    t = u.shape[0]
    row = lambda i: (i, 0)
    return pl.pallas_call(
        functools.partial(_gmlp_body, tm // CHUNK),
        grid=(t // tm,),
        in_specs=[pl.BlockSpec((tm, GM_WIDTH), row), pl.BlockSpec((tm, GM_WIDTH), row),
                  pl.BlockSpec((GM_GROUPS, CHUNK, CHUNK), lambda i: (0, 0, 0)),
                  pl.BlockSpec((CHUNK, LANES), lambda i: (0, 0))],
        out_specs=pl.BlockSpec((tm, GM_WIDTH), row),
        out_shape=jax.ShapeDtypeStruct((t, GM_WIDTH), BF16),
        compiler_params=_cparams(("arbitrary",)),
        name="gmlp",
    )(u, vg, w, bt)


def _gmlp_first_row_body(u_ref, vg_ref, w0_ref, b0_ref, o_ref):
    o_ref[...] = (u_ref[...].astype(F32) * (vg_ref[...] * w0_ref[...] + b0_ref[...])).astype(BF16)


def _gmlp_sample(u, vg, w0, b0):
    return pl.pallas_call(
        _gmlp_first_row_body,
        out_shape=jax.ShapeDtypeStruct(u.shape, BF16),
        name="gmlp_first_row",
    )(u, vg, w0, b0)


def _merge_body(mod3d, dispatch, attn_ref, gm_ref, ga_ref, gb_ref, x_ref, mod_ref, wpa_ref, wpg_ref, wo_ref,
                g2_ref, wr_ref, br_ref, x1_o, a_o, b_o):
    tm = x_ref.shape[0]
    attn = attn_ref[...].reshape(tm, N_HEADS * HEAD_DIM)
    a = jnp.dot(attn, wpa_ref[...], preferred_element_type=F32)
    g = jnp.dot(gm_ref[...], wpg_ref[...], preferred_element_type=F32)
    merged = ga_ref[...].astype(F32) * a + gb_ref[...].astype(F32) * g
    out = jnp.dot(merged.astype(BF16), wo_ref[...], preferred_element_type=F32)
    m = mod_ref[0] if mod3d else mod_ref[...]
    x1 = x_ref[...] + m[:, 2 * D_MODEL:3 * D_MODEL] * out
    x1_o[...] = x1
    ms = jnp.mean(x1 * x1, axis=-1, keepdims=True)
    y = x1 * lax.rsqrt(ms + EPS) * g2_ref[...]
    h2 = y * (1.0 + m[:, 4 * D_MODEL:5 * D_MODEL]) + m[:, 3 * D_MODEL:4 * D_MODEL]
    hi = h2.astype(BF16)
    lo = (h2 - hi.astype(F32)).astype(BF16)
    r = jnp.dot(hi, wr_ref[...], preferred_element_type=F32) + jnp.dot(lo, wr_ref[...], preferred_element_type=F32)
    logits = r[:, 0:LANES] + r[:, LANES:2 * LANES] + br_ref[...]
    neg = -jnp.inf
    big = jnp.int32(1 << 20)
    lane = lax.broadcasted_iota(jnp.int32, logits.shape, 1)
    is_g = jnp.logical_and(lane >= N_EXPERTS, lane < N_EXPERTS + N_EXPERT_GROUPS)
    gl = jnp.where(is_g, logits, neg)
    gmax = jnp.max(gl, axis=1, keepdims=True)
    g_lane = jnp.min(jnp.where(gl == gmax, lane, big), axis=1, keepdims=True)
    g_w = 1.0 / jnp.sum(jnp.exp(gl - gmax), axis=1, keepdims=True)
    g_sel = g_lane - N_EXPERTS
    in_grp = jnp.logical_and(lane < N_EXPERTS, (lane >> 3) == g_sel)
    el = jnp.where(in_grp, logits, neg)
    m1 = jnp.max(el, axis=1, keepdims=True)
    i1 = jnp.min(jnp.where(el == m1, lane, big), axis=1, keepdims=True)
    el2 = jnp.where(lane == i1, neg, el)
    m2 = jnp.max(el2, axis=1, keepdims=True)
    i2 = jnp.min(jnp.where(el2 == m2, lane, big), axis=1, keepdims=True)
    e2 = jnp.exp(m2 - m1)
    w1 = g_w / (1.0 + e2)
    w2 = g_w * e2 / (1.0 + e2)
    if not dispatch:
        a_o[...] = hi
        b_o[...] = jnp.where(lane == i1, w1, 0.0) + jnp.where(lane == i2, w2, 0.0)
        return
    low_first = i1 < i2
    ea = jnp.where(low_first, i1, i2) - g_sel * EXPERTS_PER_GROUP
    eb = jnp.where(low_first, i2, i1) - g_sel * EXPERTS_PER_GROUP
    cls = g_sel * N_PAIRS + ((ea * (2 * EXPERTS_PER_GROUP - 1 - ea)) >> 1) + (eb - ea - 1)
    b_o[...] = jnp.broadcast_to(cls, (tm, LANES))
    wa = jnp.where(low_first, w1, w2)
    wb = jnp.where(low_first, w2, w1)
    _store_token_tiles(a_o, _pack_halves(hi.astype(F32)), lax.bitcast_convert_type(
        jnp.where(lane == 0, wa, jnp.where(lane == 1, wb, 0.0)), U32))


def _merge(attn4, gm, ga, gb, x, mod, wpa, wpg, wo, g2, wr, br, tm, tiles_per_seq, dispatch):
    t = x.shape[0]
    mod3d = mod.ndim == 3
    nb = tm // Q_BLOCK
    row = lambda i: (i, 0)
    fixed = lambda i: (0, 0)
    if mod3d:
        mod_spec = pl.BlockSpec((1, 1, 6 * D_MODEL), lambda i: (i // tiles_per_seq, 0, 0))
    else:
        mod_spec = pl.BlockSpec((tm, 6 * D_MODEL), row)
    attn_spec = pl.BlockSpec((nb, 1, Q_BLOCK, N_HEADS * HEAD_DIM),
                             lambda i: (i % tiles_per_seq, i // tiles_per_seq, 0, 0))
    if dispatch:
        extra = [(SUB, LANES, U32), (1, LANES, jnp.int32)]
    else:
        extra = [(1, D_MODEL, BF16), (1, LANES, F32)]
    return pl.pallas_call(
        functools.partial(_merge_body, mod3d, dispatch),
        grid=(t // tm,),
        in_specs=[attn_spec, pl.BlockSpec((tm, GM_WIDTH), row),
                  pl.BlockSpec((tm, D_MODEL), row), pl.BlockSpec((tm, D_MODEL), row),
                  pl.BlockSpec((tm, D_MODEL), row), mod_spec,
                  pl.BlockSpec((N_HEADS * HEAD_DIM, D_MODEL), fixed), pl.BlockSpec((GM_WIDTH, D_MODEL), fixed),
                  pl.BlockSpec((D_MODEL, D_MODEL), fixed), pl.BlockSpec((1, D_MODEL), fixed),
                  pl.BlockSpec((D_MODEL, 2 * LANES), fixed), pl.BlockSpec((1, LANES), fixed)],
        out_specs=[pl.BlockSpec((tm, D_MODEL), row)] + [pl.BlockSpec((tm * r, w), row) for r, w, _ in extra],
        out_shape=[jax.ShapeDtypeStruct((t, D_MODEL), F32)]
        + [jax.ShapeDtypeStruct((t * r, w), dt) for r, w, dt in extra],
        compiler_params=_cparams(("arbitrary",)),
        name="merge",
    )(attn4, gm, ga, gb, x, mod, wpa, wpg, wo, g2, wr, br)


def _moe_body(mod3d, h_ref, gate_ref, x1_ref, mod_ref, wgu_ref, wd_ref, o_ref, acc_ref):
    e = pl.program_id(1)

    @pl.when(e == 0)
    def _():
        acc_ref[...] = jnp.zeros_like(acc_ref)

    gu = jnp.dot(h_ref[...], wgu_ref[0], preferred_element_type=F32)
    a = gu[:, 0:D_EXPERT]
    hid = a * jax.nn.sigmoid(a) * gu[:, D_EXPERT:2 * D_EXPERT]
    gate = gate_ref[...]
    lane = lax.broadcasted_iota(jnp.int32, gate.shape, 1)
    ge = jnp.sum(jnp.where(lane == e, gate, 0.0), axis=1, keepdims=True)
    acc_ref[...] += jnp.dot((hid * ge).astype(BF16), wd_ref[0], preferred_element_type=F32)

    @pl.when(e == pl.num_programs(1) - 1)
    def _():
        m = mod_ref[0] if mod3d else mod_ref[...]
        o_ref[...] = x1_ref[...] + m[:, 5 * D_MODEL:6 * D_MODEL] * acc_ref[...]


def _moe(h2, gate, x1, mod, wgu, wd, tm, tiles_per_seq):
    t = h2.shape[0]
    mod3d = mod.ndim == 3
    row = lambda i, e: (i, 0)
    if mod3d:
        mod_spec = pl.BlockSpec((1, 1, 6 * D_MODEL), lambda i, e: (i // tiles_per_seq, 0, 0))
    else:
        mod_spec = pl.BlockSpec((tm, 6 * D_MODEL), row)
    return pl.pallas_call(
        functools.partial(_moe_body, mod3d),
        grid=(t // tm, N_EXPERTS),
        in_specs=[pl.BlockSpec((tm, D_MODEL), row), pl.BlockSpec((tm, LANES), row),
                  pl.BlockSpec((tm, D_MODEL), row), mod_spec,
                  pl.BlockSpec((1, D_MODEL, 2 * D_EXPERT), lambda i, e: (e, 0, 0)),
                  pl.BlockSpec((1, D_EXPERT, D_MODEL), lambda i, e: (e, 0, 0))],
        out_specs=pl.BlockSpec((tm, D_MODEL), row),
        out_shape=jax.ShapeDtypeStruct((t, D_MODEL), F32),
        scratch_shapes=[pltpu.VMEM((tm, D_MODEL), F32)],
        compiler_params=_cparams(("arbitrary", "arbitrary")),
        name="moe",
    )(h2, gate, x1, mod, wgu, wd)


def _slots_body(cls_ref, slot_o, segend_o, carry, seg_start):
    sweep = pl.program_id(0)
    i = pl.program_id(1)
    tm = cls_ref.shape[0]
    lane = lax.broadcasted_iota(jnp.int32, (tm, LANES), 1)
    hit = lane == cls_ref[...]
    onehot = jnp.where(hit, 1.0, 0.0)

    @pl.when(jnp.logical_and(sweep == 0, i == 0))
    def _():
        carry[...] = jnp.zeros_like(carry)

    @pl.when(sweep == 0)
    def _():
        carry[...] = carry[...] + jnp.sum(onehot, axis=0, keepdims=True)
        slot_o[...] = jnp.zeros_like(slot_o)

        @pl.when(i == pl.num_programs(1) - 1)
        def _():
            padded = jnp.floor((carry[...] + (ROW_TILE - 1)) * (1.0 / ROW_TILE)) * ROW_TILE
            ri = lax.broadcasted_iota(jnp.int32, (LANES, LANES), 0)
            ci = lax.broadcasted_iota(jnp.int32, (LANES, LANES), 1)
            upto = jnp.where(ri <= ci, 1.0, 0.0)
            seg_end = jnp.dot(padded, upto, preferred_element_type=F32, precision=lax.Precision.HIGHEST)
            segend_o[...] = seg_end
            seg_start[...] = seg_end - padded
            carry[...] = jnp.zeros_like(carry)

    @pl.when(sweep == 1)
    def _():
        ri = lax.broadcasted_iota(jnp.int32, (tm, tm), 0)
        ci = lax.broadcasted_iota(jnp.int32, (tm, tm), 1)
        earlier = jnp.where(ci < ri, 1.0, 0.0).astype(BF16)
        before = (jnp.dot(earlier, onehot.astype(BF16), preferred_element_type=F32)
                  + carry[0:1, :] + seg_start[0:1, :])
        slot = jnp.sum(jnp.where(hit, before, 0.0), axis=1, keepdims=True)
        slot_o[...] = jnp.broadcast_to(slot, (tm, LANES)).astype(jnp.int32)
        carry[...] = carry[...] + jnp.sum(onehot, axis=0, keepdims=True)


def _class_slots(cls, tm):
    t = cls.shape[0]
    return pl.pallas_call(
        _slots_body,
        grid=(2, t // tm),
        in_specs=[pl.BlockSpec((tm, LANES), lambda s, i: (i, 0))],
        out_specs=[pl.BlockSpec((tm, LANES), lambda s, i: (i * s, 0)),
                   pl.BlockSpec((SUB, LANES), lambda s, i: (0, 0))],
        out_shape=[jax.ShapeDtypeStruct((t, LANES), jnp.int32), jax.ShapeDtypeStruct((SUB, LANES), F32)],
        scratch_shapes=[pltpu.VMEM((SUB, LANES), F32), pltpu.VMEM((SUB, LANES), F32)],
        compiler_params=_cparams(("arbitrary", "arbitrary")),
        name="class_slots",
    )(cls)


INVERT_CHUNK = 4096


def _invert_body(dest_ref, src_o):
    i = pl.program_id(0)

    @pl.when(i == 0)
    def _():
        src_o[...] = jnp.zeros_like(src_o)

    def put(r, carry):
        src_o[pl.ds(dest_ref[0, 0, r], 1), :] = jnp.full((1, LANES), i * INVERT_CHUNK + r, jnp.int32)
        return carry

    lax.fori_loop(0, INVERT_CHUNK, put, 0, unroll=8)


def _invert(dest, n_sorted):
    t = dest.shape[0]
    return pl.pallas_call(
        _invert_body,
        grid=(t // INVERT_CHUNK,),
        in_specs=[pl.BlockSpec((1, 1, INVERT_CHUNK), lambda i: (i, 0, 0), memory_space=pltpu.SMEM)],
        out_specs=pl.BlockSpec((n_sorted, LANES), lambda i: (0, 0)),
        out_shape=jax.ShapeDtypeStruct((n_sorted, LANES), jnp.int32),
        compiler_params=_cparams(("arbitrary",)),
        name="invert_slots",
    )(dest.reshape(t // INVERT_CHUNK, 1, INVERT_CHUNK))


def _token_fetches(idx_ref, tiles_hbm, buf, slot, sem, n):
    return [pltpu.make_async_copy(tiles_hbm.at[pl.ds(pl.multiple_of(idx_ref[0, 0, r] * SUB, SUB), SUB)],
                                  buf.at[slot, pl.ds(r * SUB, SUB)], sem.at[slot])
            for r in range(n)]


def _start_all(copies):
    for r, cp in enumerate(copies):
        cp.start(priority=r % 2)


def _tile_fetches(src_ref, rows_hbm, buf, slot, sem):
    return _token_fetches(src_ref, rows_hbm, buf, slot, sem, ROW_TILE)


FETCH_AHEAD = 3


def _experts_body(grp_ref, ea_ref, eb_ref, used_ref, *refs):
    src_refs = refs[0:FETCH_AHEAD + 1]
    rows_hbm, wgu_ref, wd_ref, o_ref, buf, sem = refs[FETCH_AHEAD + 1:]
    j = pl.program_id(0)
    n_buf = FETCH_AHEAD + 1
    slot = j % n_buf
    n_used = used_ref[0]

    for k in range(FETCH_AHEAD):
        @pl.when(jnp.logical_and(j == 0, jnp.logical_or(k == 0, k < n_used)))
        def _(k=k):
            _start_all(_tile_fetches(src_refs[k], rows_hbm, buf, k, sem))

    @pl.when(j + FETCH_AHEAD < n_used)
    def _():
        _start_all(_tile_fetches(src_refs[FETCH_AHEAD], rows_hbm, buf, (j + FETCH_AHEAD) % n_buf, sem))

    @pl.when(jnp.logical_or(j < n_used, j == 0))
    def _():
        for cp in _tile_fetches(src_refs[0], rows_hbm, buf, slot, sem):
            cp.wait()

    @pl.when(j < n_used)
    def _():
        tiles = buf.at[slot]
        x = _unpack_halves(_load_token_words(tiles, ROW_TILE)).astype(BF16)
        wts = lax.bitcast_convert_type(tiles[pl.ds(FEAT_SUB, ROW_TILE, stride=SUB), :], F32)

        def hidden(e, wgt):
            gu = jnp.dot(x, wgu_ref[e], preferred_element_type=F32)
            a = gu[:, 0:D_EXPERT]
            return (a * jax.nn.sigmoid(a) * gu[:, D_EXPERT:2 * D_EXPERT] * wgt).astype(BF16)

        ea, eb = ea_ref[j], eb_ref[j]
        y = (jnp.dot(hidden(ea, wts[:, 0:1]), wd_ref[ea], preferred_element_type=F32)
             + jnp.dot(hidden(eb, wts[:, 1:2]), wd_ref[eb], preferred_element_type=F32))
        _store_token_tiles(o_ref, _pack_halves(y.astype(BF16).astype(F32)), None)

    @pl.when(j >= n_used)
    def _():
        o_ref[...] = jnp.zeros_like(o_ref)


def _experts(grp_t, ea_t, eb_t, n_used, src3, rows, wgu, wd):
    n_tiles = src3.shape[0]
    w_grp = lambda j, grp, ea, eb, nu: (grp[j], 0, 0)
    smem_blk = lambda f: pl.BlockSpec((1, 1, ROW_TILE), f, memory_space=pltpu.SMEM)
    grid_spec = pltpu.PrefetchScalarGridSpec(
        num_scalar_prefetch=4,
        grid=(n_tiles,),
        in_specs=[smem_blk(lambda j, grp, ea, eb, nu, k=k: (jnp.minimum(j + k, n_tiles - 1), 0, 0))
                  for k in range(FETCH_AHEAD + 1)]
        + [pl.BlockSpec(memory_space=pl.ANY),
                  pl.BlockSpec((EXPERTS_PER_GROUP, D_MODEL, 2 * D_EXPERT), w_grp),
                  pl.BlockSpec((EXPERTS_PER_GROUP, D_EXPERT, D_MODEL), w_grp)],
        out_specs=pl.BlockSpec((ROW_TILE * SUB, LANES), lambda j, grp, ea, eb, nu: (j, 0)),
        scratch_shapes=[pltpu.VMEM((FETCH_AHEAD + 1, ROW_TILE * SUB, LANES), U32),
                        pltpu.SemaphoreType.DMA((FETCH_AHEAD + 1,))],
    )
    return pl.pallas_call(
        _experts_body,
        grid_spec=grid_spec,
        out_shape=jax.ShapeDtypeStruct((n_tiles * ROW_TILE * SUB, LANES), U32),
        compiler_params=_cparams(("arbitrary",)),
        name="experts",
    )(grp_t, ea_t, eb_t, n_used, *([src3] * (FETCH_AHEAD + 1)), rows, wgu, wd)


def _row_fetches(dest_ref, ys_hbm, buf, slot, sem):
    return _token_fetches(dest_ref, ys_hbm, buf, slot, sem, MOVE_TILE)


def _combine_body(mod3d, dcur_ref, dnext_ref, ys_hbm, x1_ref, mod_ref, o_ref, buf, sem):
    i = pl.program_id(0)
    slot = i % 2

    @pl.when(i == 0)
    def _():
        _start_all(_row_fetches(dcur_ref, ys_hbm, buf, 0, sem))

    @pl.when(i + 1 < pl.num_programs(0))
    def _():
        _start_all(_row_fetches(dnext_ref, ys_hbm, buf, 1 - slot, sem))

    for cp in _row_fetches(dcur_ref, ys_hbm, buf, slot, sem):
        cp.wait()
    m = mod_ref[0] if mod3d else mod_ref[...]
    o_ref[...] = x1_ref[...] + m[:, 5 * D_MODEL:6 * D_MODEL] * _unpack_halves(_load_token_words(buf.at[slot], MOVE_TILE))


def _combine(dest3, ys, x1, mod, tiles_per_seq):
    steps = dest3.shape[0]
    t = x1.shape[0]
    mod3d = mod.ndim == 3
    row = lambda i: (i, 0)
    if mod3d:
        mod_spec = pl.BlockSpec((1, 1, 6 * D_MODEL), lambda i: (i // tiles_per_seq, 0, 0))
    else:
        mod_spec = pl.BlockSpec((MOVE_TILE, 6 * D_MODEL), row)
    smem_blk = lambda f: pl.BlockSpec((1, 1, MOVE_TILE), f, memory_space=pltpu.SMEM)
    return pl.pallas_call(
        functools.partial(_combine_body, mod3d),
        grid=(steps,),
        in_specs=[smem_blk(lambda i: (i, 0, 0)), smem_blk(lambda i: (jnp.minimum(i + 1, steps - 1), 0, 0)),
                  pl.BlockSpec(memory_space=pl.ANY),
                  pl.BlockSpec((MOVE_TILE, D_MODEL), row), mod_spec],
        out_specs=pl.BlockSpec((MOVE_TILE, D_MODEL), row),
        out_shape=jax.ShapeDtypeStruct((t, D_MODEL), F32),
        scratch_shapes=[pltpu.VMEM((2, MOVE_TILE * SUB, LANES), U32), pltpu.SemaphoreType.DMA((2,))],
        compiler_params=_cparams(("arbitrary",)),
        name="combine",
    )(dest3, dest3, ys, x1, mod)


def _class_expert_tables():
    ea, eb = [], []
    for g in range(N_EXPERT_GROUPS):
        for a in range(EXPERTS_PER_GROUP):
            for b in range(a + 1, EXPERTS_PER_GROUP):
                ea.append(g * EXPERTS_PER_GROUP + a)
                eb.append(g * EXPERTS_PER_GROUP + b)
    return np.asarray(ea, np.int32), np.asarray(eb, np.int32)


def _moe_dispatched(rows, cls, x1, mod, wgu, wd, tiles_per_seq):
    t = x1.shape[0]
    slots, seg_end8 = _class_slots(cls, 1024)
    seg_end = seg_end8[0, 0:N_CLASSES].astype(jnp.int32)
    dest = slots[:, 0]
    dest3 = dest.reshape(t // MOVE_TILE, 1, MOVE_TILE)
    n_sorted = t + N_CLASSES * ROW_TILE
    tile_row0 = jnp.arange(n_sorted // ROW_TILE, dtype=jnp.int32) * ROW_TILE
    tile_cls = jnp.minimum(jnp.sum((seg_end[None, :] <= tile_row0[:, None]).astype(jnp.int32), axis=1),
                           N_CLASSES - 1)
    ea_np, eb_np = _class_expert_tables()
    in_cls = (tile_cls[:, None] == jnp.arange(N_CLASSES, dtype=jnp.int32)[None, :]).astype(jnp.int32)
    ea_t = jnp.sum(in_cls * jnp.asarray(ea_np % EXPERTS_PER_GROUP)[None, :], axis=1)
    eb_t = jnp.sum(in_cls * jnp.asarray(eb_np % EXPERTS_PER_GROUP)[None, :], axis=1)
    grp_t = tile_cls // N_PAIRS
    n_used = (seg_end[N_CLASSES - 1] // ROW_TILE).astype(jnp.int32).reshape(1)
    src = _invert(dest, n_sorted)[:, 0]
    ys = _experts(grp_t, ea_t, eb_t, n_used, src.reshape(n_sorted // ROW_TILE, 1, ROW_TILE), rows, wgu, wd)
    return _combine(dest3, ys, x1, mod, tiles_per_seq)


def _pad_lanes(v, fill):
    n = v.shape[-1]
    return jnp.concatenate([v, jnp.full((LANES - n,), fill, v.dtype)]).reshape(1, LANES)


def kernel(x_prompt, x_sample, c_prompt, c_sample, cache_k, cache_v, cache_kidx, page_table, w_ada, b_ada, norm_mix_g, norm_ffn_g, w_in, q_norm_g, k_norm_g, kidx_norm_g, gm_ln_g, gm_ln_b, gm_spatial_w, gm_spatial_b, w_proj_attn, w_proj_gmlp, w_out, w_router_group, b_router_group, w_router_expert, b_router_expert, w_expert_gate, w_expert_up, w_expert_down):
    depth = w_ada.shape[0]
    assert depth == 1
    l = 0
    bp, sp, _ = x_prompt.shape
    bs, ss, _ = x_sample.shape
    assert ss == 1
    n_pages = page_table.shape[1]
    past = n_pages * PAGE_SIZE
    tp = bp * sp

    w = w_in[l]
    zpad = jnp.zeros((D_MODEL, LANES - IDX_DIM - IDX_HEADS), F32)
    w_pad = jnp.concatenate([w[:, 0:1024], w[:, 1024:1088], w[:, 1088:1092], zpad, w[:, 1092:]], axis=1).astype(BF16)
    seg_np = (np.arange(LANES)[:, None] // HEAD_DIM) == (np.arange(LANES)[None, :] // HEAD_DIM)
    seg = jnp.asarray(seg_np, BF16)
    segki = jnp.asarray(seg_np & (np.arange(LANES)[:, None] < IDX_DIM) & (np.arange(LANES)[None, :] < IDX_DIM), BF16)
    consts = (norm_mix_g[l].reshape(1, D_MODEL),
              jnp.tile(q_norm_g[l], 2).reshape(1, LANES), jnp.tile(k_norm_g[l], 2).reshape(1, LANES),
              _pad_lanes(kidx_norm_g[l], 1.0),
              gm_ln_g[l].reshape(1, GM_WIDTH), gm_ln_b[l].reshape(1, GM_WIDTH), seg, segki)
    wpa = w_proj_attn[l].astype(BF16)
    wpg = w_proj_gmlp[l].astype(BF16)
    wo = w_out[l].astype(BF16)
    wr32 = jnp.concatenate([w_router_expert[l], w_router_group[l],
                            jnp.zeros((D_MODEL, LANES - N_EXPERTS - N_EXPERT_GROUPS), F32)], axis=1)
    wr_hi = wr32.astype(BF16)
    wr_lo = (wr32 - wr_hi.astype(F32)).astype(BF16)
    wr = jnp.concatenate([wr_hi, wr_lo], axis=1)
    br = _pad_lanes(jnp.concatenate([b_router_expert[l], b_router_group[l]]), 0.0)
    wgu = jnp.concatenate([w_expert_gate[l], w_expert_up[l]], axis=2).astype(BF16)
    wd = w_expert_down[l].astype(BF16)
    g2 = norm_ffn_g[l].reshape(1, D_MODEL)

    mod = _adaln(jnp.concatenate([c_prompt, c_sample], axis=0), w_ada[l], b_ada[l])
    mod_p = mod[0:bp].reshape(bp, 1, 6 * D_MODEL)
    mod_s = mod[bp:bp + bs]
    pos = jnp.concatenate([jnp.arange(sp, dtype=jnp.int32),
                           jnp.full((8,), past, jnp.int32)]).astype(F32).reshape(sp + 8, 1)
    tabs = _rope_tables(pos)
    tabs_p = tuple(t[0:sp] for t in tabs)
    tabs_s = tuple(t[sp:sp + 1] for t in tabs)

    tm = 512
    tps = sp // tm
    (q, kb, vb, qi, kib, kiwi, u, vg, ga, gb, kt_f, vt_f, kit_f, vt) = _project(
        x_prompt.reshape(tp, D_MODEL), mod_p, tabs_p, consts, w_pad, tm, tps, BF16)
    topk_p = min(TOPK_MAX, sp // 4)
    r3 = lambda a: a.reshape(bp, sp, a.shape[-1])
    attn_p = _prompt_attention(r3(qi), r3(kiwi), r3(kib), r3(q), r3(kb), vt, topk_p)
    bt = jnp.concatenate([gm_spatial_b[l].T, jnp.zeros((CHUNK, LANES - GM_GROUPS), F32)], axis=1)
    gm_p = _gmlp_prompt(u, vg, gm_spatial_w[l], bt, tm)
    x1_p, rows_p, cls_p = _merge(attn_p, gm_p, ga, gb, x_prompt.reshape(tp, D_MODEL), mod_p,
                                 wpa, wpg, wo, g2, wr, br, tm, tps, True)
    y_p = _moe_dispatched(rows_p, cls_p, x1_p, mod_p, wgu, wd, sp // MOVE_TILE)

    (q_s, kb_s, vb_s, qi_s, kib_s, kiwi_s, u_s, vg_s, ga_s, gb_s, kt_s, vt_s, kit_s, _) = _project(
        x_sample.reshape(bs, D_MODEL), mod_s, tabs_s, consts, w_pad, bs, 1, F32)
    qi8 = jnp.concatenate([qi_s.reshape(bs, IDX_HEADS, IDX_DIM),
                           jnp.zeros((bs, 8 - IDX_HEADS, IDX_DIM), BF16)], axis=1)
    wi8 = jnp.concatenate([kiwi_s[:, IDX_DIM:IDX_DIM + IDX_HEADS],
                           jnp.zeros((bs, 8 - IDX_HEADS), F32)], axis=1).reshape(bs, 8, 1)
    kidx_t = jnp.transpose(cache_kidx[l], (0, 2, 1))
    scores = _sample_scores(page_table, kidx_t, qi8, wi8, kib_s.reshape(bs, 1, IDX_DIM))
    topk_s = min(TOPK_MAX, (past + ss) // 4)
    bias = _sample_select(scores.reshape(bs, past + LANES), topk_s).reshape(bs, 1, past + LANES)
    kvw = N_KV_HEADS * HEAD_DIM
    k_t = jnp.transpose(cache_k[l], (0, 2, 3, 1)).reshape(-1, kvw, PAGE_SIZE)
    v_t = jnp.transpose(cache_v[l], (0, 2, 3, 1)).reshape(-1, kvw, PAGE_SIZE)
    q3 = q_s.reshape(bs, N_HEADS, HEAD_DIM)
    zq = jnp.zeros_like(q3)
    in_first = (jnp.arange(N_HEADS) < N_HEADS // N_KV_HEADS)[None, :, None]
    q2_s = jnp.where(in_first, jnp.concatenate([q3, zq], axis=2), jnp.concatenate([zq, q3], axis=2))
    attn_s = _sample_attention(page_table, k_t, v_t, q2_s, bias,
                               kb_s.reshape(bs, 1, kvw), vb_s.reshape(bs, 1, kvw))
    gd = GM_WIDTH // GM_GROUPS
    w0 = jnp.repeat(gm_spatial_w[l][:, 0, 0], gd).reshape(1, GM_WIDTH)
    b0 = jnp.repeat(gm_spatial_b[l][:, 0], gd).reshape(1, GM_WIDTH)
    gm_s = _gmlp_sample(u_s, vg_s, w0, b0)
    x1_s, h2_s, gate_s = _merge(attn_s.reshape(1, 1, bs, N_HEADS * HEAD_DIM), gm_s, ga_s, gb_s,
                                x_sample.reshape(bs, D_MODEL), mod_s, wpa, wpg, wo, g2, wr, br, bs, 1, False)
    y_s = _moe(h2_s, gate_s, x1_s, mod_s, wgu, wd, bs, 1)

    def rows_kv(a_t, n, s):
        return jnp.transpose(a_t.reshape(n, N_KV_HEADS, HEAD_DIM, s), (0, 3, 1, 2))[None]

    def rows_ki(a_t):
        return jnp.transpose(a_t, (0, 2, 1))[None]

    return (y_p.reshape(bp, sp, D_MODEL), y_s.reshape(bs, ss, D_MODEL),
            rows_kv(kt_f, bp, sp), rows_kv(vt_f, bp, sp), rows_ki(kit_f),
            rows_kv(kt_s, 1, bs).reshape(1, bs, ss, N_KV_HEADS, HEAD_DIM),
            rows_kv(vt_s, 1, bs).reshape(1, bs, ss, N_KV_HEADS, HEAD_DIM),
            rows_ki(kit_s).reshape(1, bs, ss, IDX_DIM), vg_s.reshape(1, bs, ss, GM_WIDTH))
```

```python
import functools

import numpy as np
import jax
import jax.numpy as jnp
from jax import lax
from jax.experimental import pallas as pl
from jax.experimental.pallas import tpu as pltpu

F32 = jnp.float32
BF16 = jnp.bfloat16
U32 = jnp.uint32

D_MODEL = 1024
N_HEADS = 8
HEAD_DIM = 64
N_KV_HEADS = 2
ROT_DIM = 16
ROPE_THETA = 500000.0
IDX_HEADS = 4
IDX_DIM = 64
TOPK_MAX = 256
Q_BLOCK = 128
GM_WIDTH = 512
GM_GROUPS = 8
CHUNK = 128
N_EXPERT_GROUPS = 4
EXPERTS_PER_GROUP = 8
N_EXPERTS = 32
D_EXPERT = 256
EPS = 1e-6
PAGE_SIZE = 128
LANES = 128
SUB = 8

C_Q, C_K, C_V, C_QI, C_KIWI, C_U, C_VG, C_GATE = 0, 512, 640, 768, 1024, 1152, 1664, 2176
D_IN_PAD = 4224
VMEM_LIMIT = 56 * 1024 * 1024

N_PAIRS = EXPERTS_PER_GROUP * (EXPERTS_PER_GROUP - 1) // 2
N_CLASSES = N_EXPERT_GROUPS * N_PAIRS
ROW_TILE = 128
HALF = D_MODEL // 2
FEAT_SUB = HALF // LANES
MOVE_TILE = 256
HI_MASK = 0xFFFF0000


def _cparams(sem):
    return pltpu.CompilerParams(dimension_semantics=sem, vmem_limit_bytes=VMEM_LIMIT)


def _pack_halves(x):
    n = x.shape[1] // 2
    hi = lax.bitcast_convert_type(x[:, 0:n], U32) & jnp.uint32(HI_MASK)
    lo = lax.bitcast_convert_type(x[:, n:2 * n], U32) >> 16
    return hi | lo


def _unpack_halves(u):
    hi = lax.bitcast_convert_type(u & jnp.uint32(HI_MASK), F32)
    lo = lax.bitcast_convert_type(u << 16, F32)
    return jnp.concatenate([hi, lo], axis=1)


def _store_token_tiles(ref, words, extra):
    n = words.shape[0]
    for s in range(FEAT_SUB):
        ref[pl.ds(s, n, stride=SUB), :] = words[:, s * LANES:(s + 1) * LANES]
    zero = jnp.zeros((n, LANES), U32)
    ref[pl.ds(FEAT_SUB, n, stride=SUB), :] = zero if extra is None else extra
    for s in range(FEAT_SUB + 1, SUB):
        ref[pl.ds(s, n, stride=SUB), :] = zero


def _load_token_words(ref, n):
    return jnp.concatenate([ref[pl.ds(s, n, stride=SUB), :] for s in range(FEAT_SUB)], axis=1)


def _adaln_body(c_ref, w_ref, b_ref, o_ref):
    c = c_ref[...]
    a = c * jax.nn.sigmoid(c)
    o_ref[...] = jnp.dot(a, w_ref[...], preferred_element_type=F32,
                         precision=lax.Precision.HIGHEST) + b_ref[...]


def _adaln(c, w, b):
    r = c.shape[0]
    n = w.shape[1]
    bn = 1536
    return pl.pallas_call(
        _adaln_body,
        grid=(n // bn,),
        in_specs=[pl.BlockSpec((r, D_MODEL), lambda j: (0, 0)),
                  pl.BlockSpec((D_MODEL, bn), lambda j: (0, j)),
                  pl.BlockSpec((1, bn), lambda j: (0, j))],
        out_specs=pl.BlockSpec((r, bn), lambda j: (0, j)),
        out_shape=jax.ShapeDtypeStruct((r, n), F32),
        compiler_params=_cparams(("arbitrary",)),
        name="adaln",
    )(c, w, b.reshape(1, n))


def _rope_table_body(pos_ref, invf_ref, sa_m_ref, sb_m_ref, c_ref, sa_ref, sb_ref):
    ang = pos_ref[...] * invf_ref[...]
    s = jnp.sin(ang)
    c_ref[...] = jnp.cos(ang)
    sa_ref[...] = s * sa_m_ref[...]
    sb_ref[...] = s * sb_m_ref[...]


def _rope_tables(pos):
    half = ROT_DIM // 2
    inv_freq = ROPE_THETA ** (-jnp.arange(half, dtype=F32) / half)
    d = np.arange(LANES) % HEAD_DIM
    invf = jnp.where(jnp.asarray(d < ROT_DIM), inv_freq[d % half], 0.0).reshape(1, LANES)
    sa_m = jnp.asarray(np.where(d < half, -1.0, 0.0), F32).reshape(1, LANES)
    sb_m = jnp.asarray(np.where((d >= half) & (d < ROT_DIM), 1.0, 0.0), F32).reshape(1, LANES)
    r = pos.shape[0]
    return pl.pallas_call(
        _rope_table_body,
        out_shape=[jax.ShapeDtypeStruct((r, LANES), F32)] * 3,
        name="rope_tables",
    )(pos, invf, sa_m, sb_m)


def _rope(y, c, sa, sb):
    return y * c + pltpu.roll(y, LANES - ROT_DIM // 2, 1) * sa + pltpu.roll(y, ROT_DIM // 2, 1) * sb


def _seg_rms(r, seg, gain):
    ss = r * r
    hi = ss.astype(BF16)
    lo = (ss - hi.astype(F32)).astype(BF16)
    tot = jnp.dot(hi, seg, preferred_element_type=F32) + jnp.dot(lo, seg, preferred_element_type=F32)
    return r * lax.rsqrt(tot * (1.0 / HEAD_DIM) + EPS) * gain


def _proj_body(mod3d, x_ref, mod_ref, g_ref, w_ref, c_ref, sa_ref, sb_ref, qg_ref, kg_ref, kig_ref,
               lng_ref, lnb_ref, seg_ref, segki_ref,
               q_o, kb_o, vb_o, qi_o, kib_o, kiwi_o, u_o, vg_o, ga_o, gb_o, kt_o, vtf_o, kit_o, vt_o):
    x = x_ref[...]
    m = mod_ref[0] if mod3d else mod_ref[...]
    shift1 = m[:, 0:D_MODEL]
    scale1 = m[:, D_MODEL:2 * D_MODEL]
    ms = jnp.mean(x * x, axis=-1, keepdims=True)
    y = x * lax.rsqrt(ms + EPS) * g_ref[...]
    h = (y * (1.0 + scale1) + shift1).astype(BF16)
    c, sa, sb = c_ref[...], sa_ref[...], sb_ref[...]
    seg = seg_ref[...]

    def proj(a, b):
        return jnp.dot(h, w_ref[:, a:b], preferred_element_type=F32)

    for g2 in range(2):
        r2 = proj(C_Q + g2 * 2 * LANES, C_Q + (g2 + 1) * 2 * LANES)
        for g in (2 * g2, 2 * g2 + 1):
            r = r2[:, (g % 2) * LANES:(g % 2 + 1) * LANES]
            yq = _rope(_seg_rms(r, seg, qg_ref[...]), c, sa, sb) * (HEAD_DIM ** -0.5)
            q_o[:, g * LANES:(g + 1) * LANES] = yq.astype(BF16)
    r_kv = proj(C_K, C_V + LANES)
    yk = _rope(_seg_rms(r_kv[:, 0:LANES], seg, kg_ref[...]), c, sa, sb)
    kt_o[0] = yk.T
    kb_o[...] = yk.astype(BF16)
    r = r_kv[:, LANES:2 * LANES]
    r_t = r.T
    vtf_o[0] = r_t
    vb_o[...] = r.astype(BF16)
    vt_o[0] = r_t.astype(BF16)
    r2 = proj(C_QI, C_QI + 2 * LANES)
    for g in range(2):
        yqi = _rope(r2[:, g * LANES:(g + 1) * LANES], c, sa, sb) * (IDX_DIM ** -0.5 * IDX_HEADS ** -0.5)
        qi_o[:, g * LANES:(g + 1) * LANES] = yqi.astype(BF16)
    r = proj(C_KIWI, C_KIWI + LANES)
    yki = _rope(_seg_rms(r, segki_ref[...], kig_ref[...]), c, sa, sb)
    lane = lax.broadcasted_iota(jnp.int32, r.shape, 1)
    kiwi = jnp.where(lane < IDX_DIM, yki, r)
    kiwi_o[...] = kiwi
    kit_o[0] = kiwi.T[0:IDX_DIM, :]
    kib_o[...] = kiwi[:, 0:IDX_DIM].astype(BF16)
    r = proj(C_U, C_U + GM_WIDTH)
    u_o[...] = jax.nn.gelu(r).astype(BF16)
    r = proj(C_VG, C_VG + GM_WIDTH)
    gl = jax.nn.gelu(r)
    mu = jnp.mean(gl, axis=-1, keepdims=True)
    dv = gl - mu
    var = jnp.mean(dv * dv, axis=-1, keepdims=True)
    vg_o[...] = (dv * lax.rsqrt(var + EPS) * lng_ref[...] + lnb_ref[...]).astype(vg_o.dtype)
    r = proj(C_GATE, C_GATE + D_MODEL)
    ga_o[...] = jax.nn.sigmoid(r).astype(BF16)
    r = proj(C_GATE + D_MODEL, C_GATE + 2 * D_MODEL)
    gb_o[...] = jax.nn.sigmoid(r).astype(BF16)


def _project(x, mod, tables, consts, w_pad, tm, tiles_per_seq, vg_dtype):
    t = x.shape[0]
    mod3d = mod.ndim == 3
    c_t, sa_t, sb_t = tables
    g_mix, qg, kg, kig, lng, lnb, seg, segki = consts
    row = lambda i: (i, 0)
    fixed = lambda i: (0, 0)
    if mod3d:
        mod_spec = pl.BlockSpec((1, 1, 6 * D_MODEL), lambda i: (i // tiles_per_seq, 0, 0))
        tab_spec = pl.BlockSpec((tm, LANES), lambda i: (i % tiles_per_seq, 0))
    else:
        mod_spec = pl.BlockSpec((tm, 6 * D_MODEL), row)
        tab_spec = pl.BlockSpec((1, LANES), fixed)
    widths = [(512, BF16), (128, BF16), (128, BF16), (256, BF16),
              (64, BF16), (128, F32), (512, BF16), (512, vg_dtype), (1024, BF16), (1024, BF16)]
    n_seq, seq = t // (tm * tiles_per_seq), tm * tiles_per_seq
    t_idx = lambda i: (i // tiles_per_seq, 0, i % tiles_per_seq)
    t_outs = [(LANES, F32), (LANES, F32), (IDX_DIM, F32), (LANES, BF16)]
    return pl.pallas_call(
        functools.partial(_proj_body, mod3d),
        grid=(t // tm,),
        in_specs=[pl.BlockSpec((tm, D_MODEL), row), mod_spec,
                  pl.BlockSpec((1, D_MODEL), fixed),
                  pl.BlockSpec((D_MODEL, D_IN_PAD), fixed),
                  tab_spec, tab_spec, tab_spec,
                  pl.BlockSpec((1, LANES), fixed), pl.BlockSpec((1, LANES), fixed), pl.BlockSpec((1, LANES), fixed),
                  pl.BlockSpec((1, GM_WIDTH), fixed), pl.BlockSpec((1, GM_WIDTH), fixed),
                  pl.BlockSpec((LANES, LANES), fixed), pl.BlockSpec((LANES, LANES), fixed)],
        out_specs=[pl.BlockSpec((tm, w), row) for w, _ in widths]
        + [pl.BlockSpec((1, f, tm), t_idx) for f, _ in t_outs],
        out_shape=[jax.ShapeDtypeStruct((t, w), dt) for w, dt in widths]
        + [jax.ShapeDtypeStruct((n_seq, f, seq), dt) for f, dt in t_outs],
        compiler_params=_cparams(("arbitrary",)),
        name="project",
    )(x, mod, g_mix, w_pad, c_t, sa_t, sb_t, qg, kg, kig, lng, lnb, seg, segki)


def _select_bias(s_ref, bias_ref, rows, width, topk, tie_check_start=16):
    nb = width // LANES
    kf = float(topk)
    neg, pos = -jnp.inf, jnp.inf

    def blk(j):
        return s_ref[:, j * LANES:(j + 1) * LANES]

    def count_above(t):
        tb = jnp.broadcast_to(t, (rows, LANES))
        acc = jnp.zeros((rows, LANES), F32)
        for j in range(nb):
            acc = acc + jnp.where(blk(j) > tb, 1.0, 0.0)
        return jnp.sum(acc, axis=1, keepdims=True)

    mx = jnp.full((rows, LANES), neg, F32)
    mn = jnp.full((rows, LANES), pos, F32)
    for j in range(nb):
        b = blk(j)
        mx = jnp.maximum(mx, b)
        mn = jnp.minimum(mn, jnp.where(b == neg, pos, b))
    hi0 = jnp.max(mx, axis=1, keepdims=True)
    smin = jnp.min(mn, axis=1, keepdims=True)
    lo0 = smin - jnp.abs(smin) - 1.0
    f_lo0 = count_above(lo0)
    zeros = jnp.zeros((rows, 1), F32)

    def active_of(f_lo, tie):
        return jnp.logical_and(f_lo > kf, tie == 0.0)

    def cond(st):
        _, _, _, f_lo, _, tie = st
        return jnp.max(jnp.where(active_of(f_lo, tie), 1.0, 0.0)) > 0.0

    def body(st):
        it, lo, hi, f_lo, f_hi, tie = st
        active = active_of(f_lo, tie)
        mid = lo + (hi - lo) * 0.5
        stuck = jnp.logical_or(mid <= lo, mid >= hi)
        cnt = count_above(mid)
        ge = cnt >= kf
        up_lo = jnp.logical_and(active, ge)
        up_hi = jnp.logical_and(active, jnp.logical_not(ge))
        lo = jnp.where(up_lo, mid, lo)
        f_lo = jnp.where(up_lo, cnt, f_lo)
        hi = jnp.where(up_hi, mid, hi)
        f_hi = jnp.where(up_hi, cnt, f_hi)
        tie = jnp.where(jnp.logical_and(active, stuck), 1.0, tie)

        def tie_check(_):
            lob = jnp.broadcast_to(lo, (rows, LANES))
            hib = jnp.broadcast_to(hi, (rows, LANES))
            vmx = jnp.full((rows, LANES), neg, F32)
            vmn = jnp.full((rows, LANES), pos, F32)
            for j in range(nb):
                b = blk(j)
                inn = jnp.logical_and(b > lob, b <= hib)
                vmx = jnp.maximum(vmx, jnp.where(inn, b, neg))
                vmn = jnp.minimum(vmn, jnp.where(inn, b, pos))
            one_value = jnp.max(vmx, axis=1, keepdims=True) == jnp.min(vmn, axis=1, keepdims=True)
            return jnp.where(one_value, 1.0, tie)

        run_check = jnp.logical_and(it >= tie_check_start, it % 4 == 0)
        tie = lax.cond(run_check, tie_check, lambda _: tie, 0)
        return it + 1, lo, hi, f_lo, f_hi, tie

    _, lo, hi, f_lo, f_hi, _ = lax.while_loop(cond, body, (jnp.int32(0), lo0, hi0, f_lo0, zeros, zeros))
    lob = jnp.broadcast_to(lo, (rows, LANES))
    need_prefix = jnp.max(jnp.where(f_lo > kf, 1.0, 0.0)) > 0.0

    @pl.when(jnp.logical_not(need_prefix))
    def _():
        for j in range(nb):
            bias_ref[:, j * LANES:(j + 1) * LANES] = jnp.where(blk(j) > lob, 0.0, neg)

    @pl.when(need_prefix)
    def _():
        hib = jnp.broadcast_to(hi, (rows, LANES))
        need = kf - f_hi
        ri = lax.broadcasted_iota(jnp.int32, (LANES, LANES), 0)
        ci = lax.broadcasted_iota(jnp.int32, (LANES, LANES), 1)
        upper = jnp.where(ri < ci, 1.0, 0.0).astype(BF16)
        off = jnp.zeros((rows, 1), F32)
        for j in range(nb):
            b = blk(j)
            inn = jnp.logical_and(b > lob, b <= hib)
            innf = jnp.where(inn, 1.0, 0.0)
            before = jnp.dot(innf.astype(BF16), upper, preferred_element_type=F32) + off
            sel = jnp.logical_or(b > hib, jnp.logical_and(inn, before < need))
            bias_ref[:, j * LANES:(j + 1) * LANES] = jnp.where(sel, 0.0, neg)
            off = off + jnp.sum(innf, axis=1, keepdims=True)


_NT = (((1,), (1,)), ((), ()))


def _sub_reduce(x, op):
    for sh in (4, 2, 1):
        x = op(x, pltpu.roll(x, sh, 0))
    return x


def _select_bias_t(s_ref, bias_ref, width, topk, n_adm, steps_per_check=4, tie_check_from=4):
    rb = 64
    nb = width // rb
    kf = float(topk)
    neg, pos = -jnp.inf, jnp.inf

    def blk(j):
        return s_ref[j * rb:(j + 1) * rb, :]

    def fold(x, op):
        y = x[0:SUB]
        for a in range(1, rb // SUB):
            y = op(y, x[a * SUB:(a + 1) * SUB])
        return _sub_reduce(y, op)

    def tile(v):
        return jnp.concatenate([v] * (rb // SUB), axis=0)

    def count_above(t):
        tb = tile(t)
        acc = jnp.zeros((rb, LANES), F32)
        for j in range(nb):
            acc = acc + jnp.where(blk(j) > tb, 1.0, 0.0)
        return fold(acc, jnp.add)

    mx = jnp.full((rb, LANES), neg, F32)
    mn = jnp.full((rb, LANES), pos, F32)
    for j in range(nb):
        b = blk(j)
        mx = jnp.maximum(mx, b)
        mn = jnp.minimum(mn, jnp.where(b == neg, pos, b))
    hi0 = fold(mx, jnp.maximum)
    smin = fold(mn, jnp.minimum)
    lo0 = jnp.minimum(smin - jnp.abs(smin) * (2.0 ** -10) - 1e-30, smin - (hi0 - smin) * (2.0 ** -10))
    zeros = jnp.zeros((SUB, LANES), F32)

    def active_of(f_lo, tie):
        return jnp.logical_and(f_lo > kf, tie == 0.0)

    def any_lane(cond):
        return jnp.max(jnp.where(cond, 1.0, 0.0)) > 0.0

    def step(lo, hi, f_lo, f_hi, tie):
        active = active_of(f_lo, tie)
        mid = lo + (hi - lo) * 0.5
        stuck = jnp.logical_or(mid <= lo, mid >= hi)
        cnt = count_above(mid)
        ge = cnt >= kf
        up_lo = jnp.logical_and(active, ge)
        up_hi = jnp.logical_and(active, jnp.logical_not(ge))
        return (jnp.where(up_lo, mid, lo), jnp.where(up_hi, mid, hi), jnp.where(up_lo, cnt, f_lo),
                jnp.where(up_hi, cnt, f_hi), jnp.where(jnp.logical_and(active, stuck), 1.0, tie))

    def cond(st):
        _, _, _, f_lo, _, tie = st
        return any_lane(active_of(f_lo, tie))

    def body(st):
        it, lo, hi, f_lo, f_hi, tie = st
        for _ in range(steps_per_check):
            lo, hi, f_lo, f_hi, tie = step(lo, hi, f_lo, f_hi, tie)

        def tie_check(_):
            lob, hib = tile(lo), tile(hi)
            vmx = jnp.full((rb, LANES), neg, F32)
            vmn = jnp.full((rb, LANES), pos, F32)
            for j in range(nb):
                b = blk(j)
                inn = jnp.logical_and(b > lob, b <= hib)
                vmx = jnp.maximum(vmx, jnp.where(inn, b, neg))
                vmn = jnp.minimum(vmn, jnp.where(inn, b, pos))
            return jnp.where(fold(vmx, jnp.maximum) == fold(vmn, jnp.minimum), 1.0, tie)

        run_check = jnp.logical_and(it + 1 >= tie_check_from, any_lane(active_of(f_lo, tie)))
        tie = lax.cond(run_check, tie_check, lambda _: tie, 0)
        return it + 1, lo, hi, f_lo, f_hi, tie

    _, lo, hi, f_lo, f_hi, _ = lax.while_loop(cond, body, (jnp.int32(0), lo0, hi0, n_adm, zeros, zeros))
    lob = tile(lo)
    need_prefix = any_lane(f_lo > kf)

    @pl.when(jnp.logical_not(need_prefix))
    def _():
        for j in range(nb):
            bias_ref[j * rb:(j + 1) * rb, :] = jnp.where(blk(j) > lob, 0.0, neg)

    @pl.when(need_prefix)
    def _():
        need = (kf - f_hi)[0:1]
        lo1, hi1 = lo[0:1], hi[0:1]
        ri = lax.broadcasted_iota(jnp.int32, (LANES, LANES), 0)
        ci = lax.broadcasted_iota(jnp.int32, (LANES, LANES), 1)
        lower = jnp.where(ci < ri, 1.0, 0.0).astype(BF16)
        off = jnp.zeros((1, LANES), F32)
        for j in range(width // LANES):
            b = s_ref[j * LANES:(j + 1) * LANES, :]
            inn = jnp.logical_and(b > lo1, b <= hi1)
            innf = jnp.where(inn, 1.0, 0.0)
            before = jnp.dot(lower, innf.astype(BF16), preferred_element_type=F32) + off
            sel = jnp.logical_or(b > hi1, jnp.logical_and(inn, before < need))
            bias_ref[j * LANES:(j + 1) * LANES, :] = jnp.where(sel, 0.0, neg)
            off = off + jnp.sum(innf, axis=0, keepdims=True)


def _prompt_attn_body(blk_i, topk, qi_ref, kiwi_ref, kib_ref, q_ref, k_ref, vt_ref, o_ref, s_ref, bias_ref):
    width = (blk_i + 1) * Q_BLOCK
    neg = -jnp.inf
    kiwi_t = kiwi_ref[0].T
    chunk = 512
    for c0 in range(0, width, chunk):
        c1 = min(width, c0 + chunk)
        kib = kib_ref[0, c0:c1, :]
        sc = None
        for h in range(IDX_HEADS):
            d = lax.dot_general(kib, qi_ref[0, :, h * IDX_DIM:(h + 1) * IDX_DIM], _NT,
                                preferred_element_type=F32)
            t = jnp.maximum(d, 0.0) * kiwi_t[IDX_DIM + h:IDX_DIM + h + 1, :]
            sc = t if sc is None else sc + t
        s_ref[c0:c1, :] = sc
    ki = lax.broadcasted_iota(jnp.int32, (Q_BLOCK, Q_BLOCK), 0)
    qj = lax.broadcasted_iota(jnp.int32, (Q_BLOCK, Q_BLOCK), 1)
    d0 = width - Q_BLOCK
    s_ref[d0:width, :] = jnp.where(ki <= qj, s_ref[d0:width, :], neg)
    if width > topk:
        n_adm = (lax.broadcasted_iota(jnp.int32, (SUB, LANES), 1) + (d0 + 1)).astype(F32)
        _select_bias_t(s_ref, bias_ref, width, topk, n_adm)
    else:
        bias_ref[...] = jnp.where(s_ref[...] == neg, neg, 0.0)
    lane = lax.broadcasted_iota(jnp.int32, (Q_BLOCK, LANES), 1)
    hpg = N_HEADS // N_KV_HEADS
    for h in range(N_HEADS):
        g = h // hpg
        qp = q_ref[0, :, (h // 2) * LANES:(h // 2 + 1) * LANES].astype(F32)
        if h % 2 != g:
            qp = pltpu.roll(qp, HEAD_DIM, 1)
        q2 = jnp.where((lane >= HEAD_DIM) == (g == 1), qp, 0.0).astype(BF16)
        s = lax.dot_general(k_ref[0], q2, _NT, preferred_element_type=F32) + bias_ref[...]
        m = jnp.max(s, axis=0, keepdims=True)
        p = jnp.exp(s - m)
        l = jnp.sum(p, axis=0, keepdims=True)
        o_t = jnp.dot(vt_ref[0, g * HEAD_DIM:(g + 1) * HEAD_DIM, :], p.astype(BF16), preferred_element_type=F32)
        o_ref[0, :, h * HEAD_DIM:(h + 1) * HEAD_DIM] = (o_t / l).T.astype(BF16)


def _prompt_attention(qi, kiwi, kib, q, kb, vt, topk):
    b, s, _ = q.shape
    outs = []
    for i in range(s // Q_BLOCK):
        width = (i + 1) * Q_BLOCK
        qblk = lambda bb, i=i: (bb, i, 0)
        kall = lambda bb: (bb, 0, 0)
        outs.append(pl.pallas_call(
            functools.partial(_prompt_attn_body, i, topk),
            grid=(b,),
            in_specs=[pl.BlockSpec((1, Q_BLOCK, IDX_HEADS * IDX_DIM), qblk),
                      pl.BlockSpec((1, Q_BLOCK, LANES), qblk),
                      pl.BlockSpec((1, width, IDX_DIM), kall),
                      pl.BlockSpec((1, Q_BLOCK, N_HEADS * HEAD_DIM), qblk),
                      pl.BlockSpec((1, width, N_KV_HEADS * HEAD_DIM), kall),
                      pl.BlockSpec((1, N_KV_HEADS * HEAD_DIM, width), kall)],
            out_specs=pl.BlockSpec((1, Q_BLOCK, N_HEADS * HEAD_DIM), lambda bb: (bb, 0, 0)),
            out_shape=jax.ShapeDtypeStruct((b, Q_BLOCK, N_HEADS * HEAD_DIM), BF16),
            scratch_shapes=[pltpu.VMEM((width, Q_BLOCK), F32), pltpu.VMEM((width, Q_BLOCK), F32)],
            compiler_params=_cparams(("arbitrary",)),
            name=f"prompt_attn_{i}",
        )(qi, kiwi, kib, q, kb, vt))
    return jnp.stack(outs)


def _page_copies(pt_ref, sample, src_hbm, buf, slot, sem, n_pages):
    return [pltpu.make_async_copy(src_hbm.at[pt_ref[sample, p]],
                                  buf.at[slot, :, pl.ds(p * PAGE_SIZE, PAGE_SIZE)], sem.at[slot])
            for p in range(n_pages)]


def _sample_scores_body(n_pages, pt_ref, kidx_hbm, qi_ref, wi_ref, kin_ref, o_ref, buf, sem):
    s = pl.program_id(0)
    slot = s % 2

    @pl.when(s == 0)
    def _():
        for cp in _page_copies(pt_ref, 0, kidx_hbm, buf, 0, sem, n_pages):
            cp.start()

    @pl.when(s + 1 < pl.num_programs(0))
    def _():
        for cp in _page_copies(pt_ref, s + 1, kidx_hbm, buf, 1 - slot, sem, n_pages):
            cp.start()

    for cp in _page_copies(pt_ref, s, kidx_hbm, buf, slot, sem, n_pages):
        cp.wait()
    qi = qi_ref[0]
    wi = wi_ref[0]
    ki_t = buf[slot].astype(BF16)
    d = jnp.dot(qi, ki_t, preferred_element_type=F32)
    past = jnp.sum(jnp.maximum(d, 0.0) * wi, axis=0, keepdims=True)
    dn = jnp.sum(qi.astype(F32) * kin_ref[0].astype(F32), axis=1, keepdims=True)
    new = jnp.sum(jnp.maximum(dn, 0.0) * wi, axis=0, keepdims=True)
    lane = lax.broadcasted_iota(jnp.int32, (1, LANES), 1)
    tail = jnp.where(lane == 0, jnp.broadcast_to(new, (1, LANES)), -jnp.inf)
    o_ref[0] = jnp.concatenate([past, tail], axis=1)


def _sample_scores(page_table, kidx_pool, qi8, wi8, ki_new):
    n, n_pages = page_table.shape
    past = n_pages * PAGE_SIZE
    grid_spec = pltpu.PrefetchScalarGridSpec(
        num_scalar_prefetch=1,
        grid=(n,),
        in_specs=[pl.BlockSpec(memory_space=pl.ANY),
                  pl.BlockSpec((1, 8, IDX_DIM), lambda s, pt: (s, 0, 0)),
                  pl.BlockSpec((1, 8, 1), lambda s, pt: (s, 0, 0)),
                  pl.BlockSpec((1, 1, IDX_DIM), lambda s, pt: (s, 0, 0))],
        out_specs=pl.BlockSpec((1, 1, past + LANES), lambda s, pt: (s, 0, 0)),
        scratch_shapes=[pltpu.VMEM((2, IDX_DIM, past), F32), pltpu.SemaphoreType.DMA((2,))],
    )
    return pl.pallas_call(
        functools.partial(_sample_scores_body, n_pages),
        grid_spec=grid_spec,
        out_shape=jax.ShapeDtypeStruct((n, 1, past + LANES), F32),
        compiler_params=_cparams(("arbitrary",)),
        name="sample_scores",
    )(page_table, kidx_pool, qi8, wi8, ki_new)


def _sample_select_body(topk, s_ref, bias_ref):
    rows, width = s_ref.shape
    _select_bias(s_ref, bias_ref, rows, width, topk)


def _sample_select(scores, topk):
    return pl.pallas_call(
        functools.partial(_sample_select_body, topk),
        out_shape=jax.ShapeDtypeStruct(scores.shape, F32),
        compiler_params=pltpu.CompilerParams(vmem_limit_bytes=VMEM_LIMIT),
        name="sample_select",
    )(scores)


def _sample_attn_body(n_pages, pt_ref, k_hbm, v_hbm, q_ref, bias_ref, kn_ref, vn_ref, o_ref, kbuf, vbuf, sem):
    s = pl.program_id(0)
    slot = s % 2
    past = n_pages * PAGE_SIZE

    def copies(sample, sl):
        return (_page_copies(pt_ref, sample, k_hbm, kbuf, sl, sem.at[0], n_pages)
                + _page_copies(pt_ref, sample, v_hbm, vbuf, sl, sem.at[1], n_pages))

    @pl.when(s == 0)
    def _():
        for cp in copies(0, 0):
            cp.start()

    @pl.when(s + 1 < pl.num_programs(0))
    def _():
        for cp in copies(s + 1, 1 - slot):
            cp.start()

    for cp in copies(s, slot):
        cp.wait()
    q2 = q_ref[0]
    k_t = kbuf[slot].astype(BF16)
    v_t = vbuf[slot].astype(BF16)
    row = lax.broadcasted_iota(jnp.int32, (N_HEADS, 1), 0)
    first = row < (N_HEADS // N_KV_HEADS)
    bias = bias_ref[0]
    sc = jnp.dot(q2, k_t, preferred_element_type=F32) + bias[:, 0:past]
    sn = jnp.sum(q2.astype(F32) * kn_ref[0].astype(F32), axis=1, keepdims=True) + bias[:, past:past + 1]
    m = jnp.maximum(jnp.max(sc, axis=1, keepdims=True), sn)
    p = jnp.exp(sc - m)
    pn = jnp.exp(sn - m)
    l = jnp.sum(p, axis=1, keepdims=True) + pn
    o2 = lax.dot_general(p.astype(BF16), v_t, _NT, preferred_element_type=F32)
    o2 = o2 + pn.astype(BF16).astype(F32) * vn_ref[0].astype(F32)
    o = jnp.where(first, o2[:, 0:HEAD_DIM], o2[:, HEAD_DIM:2 * HEAD_DIM])
    o_ref[0] = (o / l).astype(BF16)


def _sample_attention(page_table, k_pool, v_pool, q8, bias, k_new, v_new):
    n, n_pages = page_table.shape
    past = n_pages * PAGE_SIZE
    kvw = N_KV_HEADS * HEAD_DIM
    per = lambda s, pt: (s, 0, 0)
    grid_spec = pltpu.PrefetchScalarGridSpec(
        num_scalar_prefetch=1,
        grid=(n,),
        in_specs=[pl.BlockSpec(memory_space=pl.ANY), pl.BlockSpec(memory_space=pl.ANY),
                  pl.BlockSpec((1, N_HEADS, kvw), per),
                  pl.BlockSpec((1, 1, past + LANES), per),
                  pl.BlockSpec((1, 1, kvw), per),
                  pl.BlockSpec((1, 1, kvw), per)],
        out_specs=pl.BlockSpec((1, N_HEADS, HEAD_DIM), per),
        scratch_shapes=[pltpu.VMEM((2, kvw, past), F32), pltpu.VMEM((2, kvw, past), F32),
                        pltpu.SemaphoreType.DMA((2, 2))],
    )
    return pl.pallas_call(
        functools.partial(_sample_attn_body, n_pages),
        grid_spec=grid_spec,
        out_shape=jax.ShapeDtypeStruct((n, N_HEADS, HEAD_DIM), BF16),
        compiler_params=_cparams(("arbitrary",)),
        name="sample_attn",
    )(page_table, k_pool, v_pool, q8, bias, k_new, v_new)


def _gmlp_body(n_chunks, u_ref, vg_ref, w_ref, bt_ref, o_ref):
    ri = lax.broadcasted_iota(jnp.int32, (CHUNK, CHUNK), 0)
    ci = lax.broadcasted_iota(jnp.int32, (CHUNK, CHUNK), 1)
    gd = GM_WIDTH // GM_GROUPS
    for g in range(GM_GROUPS):
        wg = jnp.where(ci <= ri, w_ref[g], 0.0).astype(BF16)
        bg = bt_ref[:, g:g + 1]
        for c in range(n_chunks):
            rows = slice(c * CHUNK, (c + 1) * CHUNK)
            cols = slice(g * gd, (g + 1) * gd)
            mixed = jnp.dot(wg, vg_ref[rows, cols], preferred_element_type=F32) + bg
            o_ref[rows, cols] = (u_ref[rows, cols].astype(F32) * mixed).astype(BF16)


def _gmlp_prompt(u, vg, w, bt, tm):
    t = u.shape[0]
    row = lambda i: (i, 0)
    return pl.pallas_call(
        functools.partial(_gmlp_body, tm // CHUNK),
        grid=(t // tm,),
        in_specs=[pl.BlockSpec((tm, GM_WIDTH), row), pl.BlockSpec((tm, GM_WIDTH), row),
                  pl.BlockSpec((GM_GROUPS, CHUNK, CHUNK), lambda i: (0, 0, 0)),
                  pl.BlockSpec((CHUNK, LANES), lambda i: (0, 0))],
        out_specs=pl.BlockSpec((tm, GM_WIDTH), row),
        out_shape=jax.ShapeDtypeStruct((t, GM_WIDTH), BF16),
        compiler_params=_cparams(("arbitrary",)),
        name="gmlp",
    )(u, vg, w, bt)


def _gmlp_first_row_body(u_ref, vg_ref, w0_ref, b0_ref, o_ref):
    o_ref[...] = (u_ref[...].astype(F32) * (vg_ref[...] * w0_ref[...] + b0_ref[...])).astype(BF16)


def _gmlp_sample(u, vg, w0, b0):
    return pl.pallas_call(
        _gmlp_first_row_body,
        out_shape=jax.ShapeDtypeStruct(u.shape, BF16),
        name="gmlp_first_row",
    )(u, vg, w0, b0)


def _merge_body(mod3d, dispatch, attn_ref, gm_ref, ga_ref, gb_ref, x_ref, mod_ref, wpa_ref, wpg_ref, wo_ref,
                g2_ref, wr_ref, br_ref, x1_o, a_o, b_o):
    tm = x_ref.shape[0]
    attn = attn_ref[...].reshape(tm, N_HEADS * HEAD_DIM)
    a = jnp.dot(attn, wpa_ref[...], preferred_element_type=F32)
    g = jnp.dot(gm_ref[...], wpg_ref[...], preferred_element_type=F32)
    merged = ga_ref[...].astype(F32) * a + gb_ref[...].astype(F32) * g
    out = jnp.dot(merged.astype(BF16), wo_ref[...], preferred_element_type=F32)
    m = mod_ref[0] if mod3d else mod_ref[...]
    x1 = x_ref[...] + m[:, 2 * D_MODEL:3 * D_MODEL] * out
    x1_o[...] = x1
    ms = jnp.mean(x1 * x1, axis=-1, keepdims=True)
    y = x1 * lax.rsqrt(ms + EPS) * g2_ref[...]
    h2 = y * (1.0 + m[:, 4 * D_MODEL:5 * D_MODEL]) + m[:, 3 * D_MODEL:4 * D_MODEL]
    hi = h2.astype(BF16)
    lo = (h2 - hi.astype(F32)).astype(BF16)
    r = jnp.dot(hi, wr_ref[...], preferred_element_type=F32) + jnp.dot(lo, wr_ref[...], preferred_element_type=F32)
    logits = r[:, 0:LANES] + r[:, LANES:2 * LANES] + br_ref[...]
    neg = -jnp.inf
    big = jnp.int32(1 << 20)
    lane = lax.broadcasted_iota(jnp.int32, logits.shape, 1)
    is_g = jnp.logical_and(lane >= N_EXPERTS, lane < N_EXPERTS + N_EXPERT_GROUPS)
    gl = jnp.where(is_g, logits, neg)
    gmax = jnp.max(gl, axis=1, keepdims=True)
    g_lane = jnp.min(jnp.where(gl == gmax, lane, big), axis=1, keepdims=True)
    g_w = 1.0 / jnp.sum(jnp.exp(gl - gmax), axis=1, keepdims=True)
    g_sel = g_lane - N_EXPERTS
    in_grp = jnp.logical_and(lane < N_EXPERTS, (lane >> 3) == g_sel)
    el = jnp.where(in_grp, logits, neg)
    m1 = jnp.max(el, axis=1, keepdims=True)
    i1 = jnp.min(jnp.where(el == m1, lane, big), axis=1, keepdims=True)
    el2 = jnp.where(lane == i1, neg, el)
    m2 = jnp.max(el2, axis=1, keepdims=True)
    i2 = jnp.min(jnp.where(el2 == m2, lane, big), axis=1, keepdims=True)
    e2 = jnp.exp(m2 - m1)
    w1 = g_w / (1.0 + e2)
    w2 = g_w * e2 / (1.0 + e2)
    if not dispatch:
        a_o[...] = hi
        b_o[...] = jnp.where(lane == i1, w1, 0.0) + jnp.where(lane == i2, w2, 0.0)
        return
    low_first = i1 < i2
    ea = jnp.where(low_first, i1, i2) - g_sel * EXPERTS_PER_GROUP
    eb = jnp.where(low_first, i2, i1) - g_sel * EXPERTS_PER_GROUP
    cls = g_sel * N_PAIRS + ((ea * (2 * EXPERTS_PER_GROUP - 1 - ea)) >> 1) + (eb - ea - 1)
    b_o[...] = jnp.broadcast_to(cls, (tm, LANES))
    wa = jnp.where(low_first, w1, w2)
    wb = jnp.where(low_first, w2, w1)
    _store_token_tiles(a_o, _pack_halves(hi.astype(F32)), lax.bitcast_convert_type(
        jnp.where(lane == 0, wa, jnp.where(lane == 1, wb, 0.0)), U32))


def _merge(attn4, gm, ga, gb, x, mod, wpa, wpg, wo, g2, wr, br, tm, tiles_per_seq, dispatch):
    t = x.shape[0]
    mod3d = mod.ndim == 3
    nb = tm // Q_BLOCK
    row = lambda i: (i, 0)
    fixed = lambda i: (0, 0)
    if mod3d:
        mod_spec = pl.BlockSpec((1, 1, 6 * D_MODEL), lambda i: (i // tiles_per_seq, 0, 0))
    else:
        mod_spec = pl.BlockSpec((tm, 6 * D_MODEL), row)
    attn_spec = pl.BlockSpec((nb, 1, Q_BLOCK, N_HEADS * HEAD_DIM),
                             lambda i: (i % tiles_per_seq, i // tiles_per_seq, 0, 0))
    if dispatch:
        extra = [(SUB, LANES, U32), (1, LANES, jnp.int32)]
    else:
        extra = [(1, D_MODEL, BF16), (1, LANES, F32)]
    return pl.pallas_call(
        functools.partial(_merge_body, mod3d, dispatch),
        grid=(t // tm,),
        in_specs=[attn_spec, pl.BlockSpec((tm, GM_WIDTH), row),
                  pl.BlockSpec((tm, D_MODEL), row), pl.BlockSpec((tm, D_MODEL), row),
                  pl.BlockSpec((tm, D_MODEL), row), mod_spec,
                  pl.BlockSpec((N_HEADS * HEAD_DIM, D_MODEL), fixed), pl.BlockSpec((GM_WIDTH, D_MODEL), fixed),
                  pl.BlockSpec((D_MODEL, D_MODEL), fixed), pl.BlockSpec((1, D_MODEL), fixed),
                  pl.BlockSpec((D_MODEL, 2 * LANES), fixed), pl.BlockSpec((1, LANES), fixed)],
        out_specs=[pl.BlockSpec((tm, D_MODEL), row)] + [pl.BlockSpec((tm * r, w), row) for r, w, _ in extra],
        out_shape=[jax.ShapeDtypeStruct((t, D_MODEL), F32)]
        + [jax.ShapeDtypeStruct((t * r, w), dt) for r, w, dt in extra],
        compiler_params=_cparams(("arbitrary",)),
        name="merge",
    )(attn4, gm, ga, gb, x, mod, wpa, wpg, wo, g2, wr, br)


def _moe_body(mod3d, h_ref, gate_ref, x1_ref, mod_ref, wgu_ref, wd_ref, o_ref, acc_ref):
    e = pl.program_id(1)

    @pl.when(e == 0)
    def _():
        acc_ref[...] = jnp.zeros_like(acc_ref)

    gu = jnp.dot(h_ref[...], wgu_ref[0], preferred_element_type=F32)
    a = gu[:, 0:D_EXPERT]
    hid = a * jax.nn.sigmoid(a) * gu[:, D_EXPERT:2 * D_EXPERT]
    gate = gate_ref[...]
    lane = lax.broadcasted_iota(jnp.int32, gate.shape, 1)
    ge = jnp.sum(jnp.where(lane == e, gate, 0.0), axis=1, keepdims=True)
    acc_ref[...] += jnp.dot((hid * ge).astype(BF16), wd_ref[0], preferred_element_type=F32)

    @pl.when(e == pl.num_programs(1) - 1)
    def _():
        m = mod_ref[0] if mod3d else mod_ref[...]
        o_ref[...] = x1_ref[...] + m[:, 5 * D_MODEL:6 * D_MODEL] * acc_ref[...]


def _moe(h2, gate, x1, mod, wgu, wd, tm, tiles_per_seq):
    t = h2.shape[0]
    mod3d = mod.ndim == 3
    row = lambda i, e: (i, 0)
    if mod3d:
        mod_spec = pl.BlockSpec((1, 1, 6 * D_MODEL), lambda i, e: (i // tiles_per_seq, 0, 0))
    else:
        mod_spec = pl.BlockSpec((tm, 6 * D_MODEL), row)
    return pl.pallas_call(
        functools.partial(_moe_body, mod3d),
        grid=(t // tm, N_EXPERTS),
        in_specs=[pl.BlockSpec((tm, D_MODEL), row), pl.BlockSpec((tm, LANES), row),
                  pl.BlockSpec((tm, D_MODEL), row), mod_spec,
                  pl.BlockSpec((1, D_MODEL, 2 * D_EXPERT), lambda i, e: (e, 0, 0)),
                  pl.BlockSpec((1, D_EXPERT, D_MODEL), lambda i, e: (e, 0, 0))],
        out_specs=pl.BlockSpec((tm, D_MODEL), row),
        out_shape=jax.ShapeDtypeStruct((t, D_MODEL), F32),
        scratch_shapes=[pltpu.VMEM((tm, D_MODEL), F32)],
        compiler_params=_cparams(("arbitrary", "arbitrary")),
        name="moe",
    )(h2, gate, x1, mod, wgu, wd)


def _slots_body(cls_ref, slot_o, segend_o, carry, seg_start):
    sweep = pl.program_id(0)
    i = pl.program_id(1)
    tm = cls_ref.shape[0]
    lane = lax.broadcasted_iota(jnp.int32, (tm, LANES), 1)
    hit = lane == cls_ref[...]
    onehot = jnp.where(hit, 1.0, 0.0)

    @pl.when(jnp.logical_and(sweep == 0, i == 0))
    def _():
        carry[...] = jnp.zeros_like(carry)

    @pl.when(sweep == 0)
    def _():
        carry[...] = carry[...] + jnp.sum(onehot, axis=0, keepdims=True)
        slot_o[...] = jnp.zeros_like(slot_o)

        @pl.when(i == pl.num_programs(1) - 1)
        def _():
            padded = jnp.floor((carry[...] + (ROW_TILE - 1)) * (1.0 / ROW_TILE)) * ROW_TILE
            ri = lax.broadcasted_iota(jnp.int32, (LANES, LANES), 0)
            ci = lax.broadcasted_iota(jnp.int32, (LANES, LANES), 1)
            upto = jnp.where(ri <= ci, 1.0, 0.0)
            seg_end = jnp.dot(padded, upto, preferred_element_type=F32, precision=lax.Precision.HIGHEST)
            segend_o[...] = seg_end
            seg_start[...] = seg_end - padded
            carry[...] = jnp.zeros_like(carry)

    @pl.when(sweep == 1)
    def _():
        ri = lax.broadcasted_iota(jnp.int32, (tm, tm), 0)
        ci = lax.broadcasted_iota(jnp.int32, (tm, tm), 1)
        earlier = jnp.where(ci < ri, 1.0, 0.0).astype(BF16)
        before = (jnp.dot(earlier, onehot.astype(BF16), preferred_element_type=F32)
                  + carry[0:1, :] + seg_start[0:1, :])
        slot = jnp.sum(jnp.where(hit, before, 0.0), axis=1, keepdims=True)
        slot_o[...] = jnp.broadcast_to(slot, (tm, LANES)).astype(jnp.int32)
        carry[...] = carry[...] + jnp.sum(onehot, axis=0, keepdims=True)


def _class_slots(cls, tm):
    t = cls.shape[0]
    return pl.pallas_call(
        _slots_body,
        grid=(2, t // tm),
        in_specs=[pl.BlockSpec((tm, LANES), lambda s, i: (i, 0))],
        out_specs=[pl.BlockSpec((tm, LANES), lambda s, i: (i * s, 0)),
                   pl.BlockSpec((SUB, LANES), lambda s, i: (0, 0))],
        out_shape=[jax.ShapeDtypeStruct((t, LANES), jnp.int32), jax.ShapeDtypeStruct((SUB, LANES), F32)],
        scratch_shapes=[pltpu.VMEM((SUB, LANES), F32), pltpu.VMEM((SUB, LANES), F32)],
        compiler_params=_cparams(("arbitrary", "arbitrary")),
        name="class_slots",
    )(cls)


INVERT_CHUNK = 4096


def _invert_body(dest_ref, src_o):
    i = pl.program_id(0)

    @pl.when(i == 0)
    def _():
        src_o[...] = jnp.zeros_like(src_o)

    def put(r, carry):
        src_o[pl.ds(dest_ref[0, 0, r], 1), :] = jnp.full((1, LANES), i * INVERT_CHUNK + r, jnp.int32)
        return carry

    lax.fori_loop(0, INVERT_CHUNK, put, 0, unroll=8)


def _invert(dest, n_sorted):
    t = dest.shape[0]
    return pl.pallas_call(
        _invert_body,
        grid=(t // INVERT_CHUNK,),
        in_specs=[pl.BlockSpec((1, 1, INVERT_CHUNK), lambda i: (i, 0, 0), memory_space=pltpu.SMEM)],
        out_specs=pl.BlockSpec((n_sorted, LANES), lambda i: (0, 0)),
        out_shape=jax.ShapeDtypeStruct((n_sorted, LANES), jnp.int32),
        compiler_params=_cparams(("arbitrary",)),
        name="invert_slots",
    )(dest.reshape(t // INVERT_CHUNK, 1, INVERT_CHUNK))


def _token_fetches(idx_ref, tiles_hbm, buf, slot, sem, n):
    return [pltpu.make_async_copy(tiles_hbm.at[pl.ds(pl.multiple_of(idx_ref[0, 0, r] * SUB, SUB), SUB)],
                                  buf.at[slot, pl.ds(r * SUB, SUB)], sem.at[slot])
            for r in range(n)]


def _start_all(copies):
    for r, cp in enumerate(copies):
        cp.start(priority=r % 2)


def _tile_fetches(src_ref, rows_hbm, buf, slot, sem):
    return _token_fetches(src_ref, rows_hbm, buf, slot, sem, ROW_TILE)


FETCH_AHEAD = 5


def _experts_body(grp_ref, ea_ref, eb_ref, used_ref, *refs):
    src_refs = refs[0:FETCH_AHEAD + 1]
    rows_hbm, wgu_ref, wd_ref, o_ref, buf, sem = refs[FETCH_AHEAD + 1:]
    j = pl.program_id(0)
    n_buf = FETCH_AHEAD + 1
    slot = j % n_buf
    n_used = used_ref[0]

    for k in range(FETCH_AHEAD):
        @pl.when(jnp.logical_and(j == 0, jnp.logical_or(k == 0, k < n_used)))
        def _(k=k):
            _start_all(_tile_fetches(src_refs[k], rows_hbm, buf, k, sem))

    @pl.when(j + FETCH_AHEAD < n_used)
    def _():
        _start_all(_tile_fetches(src_refs[FETCH_AHEAD], rows_hbm, buf, (j + FETCH_AHEAD) % n_buf, sem))

    @pl.when(jnp.logical_or(j < n_used, j == 0))
    def _():
        for cp in _tile_fetches(src_refs[0], rows_hbm, buf, slot, sem):
            cp.wait()

    @pl.when(j < n_used)
    def _():
        tiles = buf.at[slot]
        x = _unpack_halves(_load_token_words(tiles, ROW_TILE)).astype(BF16)
        wts = lax.bitcast_convert_type(tiles[pl.ds(FEAT_SUB, ROW_TILE, stride=SUB), :], F32)

        def hidden(e, wgt):
            gu = jnp.dot(x, wgu_ref[e], preferred_element_type=F32)
            a = gu[:, 0:D_EXPERT]
            return (a * jax.nn.sigmoid(a) * gu[:, D_EXPERT:2 * D_EXPERT] * wgt).astype(BF16)

        ea, eb = ea_ref[j], eb_ref[j]
        y = (jnp.dot(hidden(ea, wts[:, 0:1]), wd_ref[ea], preferred_element_type=F32)
             + jnp.dot(hidden(eb, wts[:, 1:2]), wd_ref[eb], preferred_element_type=F32))
        _store_token_tiles(o_ref, _pack_halves(y.astype(BF16).astype(F32)), None)

    @pl.when(j >= n_used)
    def _():
        o_ref[...] = jnp.zeros_like(o_ref)


def _experts(grp_t, ea_t, eb_t, n_used, src3, rows, wgu, wd):
    n_tiles = src3.shape[0]
    w_grp = lambda j, grp, ea, eb, nu: (grp[j], 0, 0)
    smem_blk = lambda f: pl.BlockSpec((1, 1, ROW_TILE), f, memory_space=pltpu.SMEM)
    grid_spec = pltpu.PrefetchScalarGridSpec(
        num_scalar_prefetch=4,
        grid=(n_tiles,),
        in_specs=[smem_blk(lambda j, grp, ea, eb, nu, k=k: (jnp.minimum(j + k, n_tiles - 1), 0, 0))
                  for k in range(FETCH_AHEAD + 1)]
        + [pl.BlockSpec(memory_space=pl.ANY),
                  pl.BlockSpec((EXPERTS_PER_GROUP, D_MODEL, 2 * D_EXPERT), w_grp),
                  pl.BlockSpec((EXPERTS_PER_GROUP, D_EXPERT, D_MODEL), w_grp)],
        out_specs=pl.BlockSpec((ROW_TILE * SUB, LANES), lambda j, grp, ea, eb, nu: (j, 0)),
        scratch_shapes=[pltpu.VMEM((FETCH_AHEAD + 1, ROW_TILE * SUB, LANES), U32),
                        pltpu.SemaphoreType.DMA((FETCH_AHEAD + 1,))],
    )
    return pl.pallas_call(
        _experts_body,
        grid_spec=grid_spec,
        out_shape=jax.ShapeDtypeStruct((n_tiles * ROW_TILE * SUB, LANES), U32),
        compiler_params=_cparams(("arbitrary",)),
        name="experts",
    )(grp_t, ea_t, eb_t, n_used, *([src3] * (FETCH_AHEAD + 1)), rows, wgu, wd)


def _row_fetches(dest_ref, ys_hbm, buf, slot, sem):
    return _token_fetches(dest_ref, ys_hbm, buf, slot, sem, MOVE_TILE)


def _combine_body(mod3d, dcur_ref, dnext_ref, ys_hbm, x1_ref, mod_ref, o_ref, buf, sem):
    i = pl.program_id(0)
    slot = i % 2

    @pl.when(i == 0)
    def _():
        _start_all(_row_fetches(dcur_ref, ys_hbm, buf, 0, sem))

    @pl.when(i + 1 < pl.num_programs(0))
    def _():
        _start_all(_row_fetches(dnext_ref, ys_hbm, buf, 1 - slot, sem))

    for cp in _row_fetches(dcur_ref, ys_hbm, buf, slot, sem):
        cp.wait()
    m = mod_ref[0] if mod3d else mod_ref[...]
    o_ref[...] = x1_ref[...] + m[:, 5 * D_MODEL:6 * D_MODEL] * _unpack_halves(_load_token_words(buf.at[slot], MOVE_TILE))


def _combine(dest3, ys, x1, mod, tiles_per_seq):
    steps = dest3.shape[0]
    t = x1.shape[0]
    mod3d = mod.ndim == 3
    row = lambda i: (i, 0)
    if mod3d:
        mod_spec = pl.BlockSpec((1, 1, 6 * D_MODEL), lambda i: (i // tiles_per_seq, 0, 0))
    else:
        mod_spec = pl.BlockSpec((MOVE_TILE, 6 * D_MODEL), row)
    smem_blk = lambda f: pl.BlockSpec((1, 1, MOVE_TILE), f, memory_space=pltpu.SMEM)
    return pl.pallas_call(
        functools.partial(_combine_body, mod3d),
        grid=(steps,),
        in_specs=[smem_blk(lambda i: (i, 0, 0)), smem_blk(lambda i: (jnp.minimum(i + 1, steps - 1), 0, 0)),
                  pl.BlockSpec(memory_space=pl.ANY),
                  pl.BlockSpec((MOVE_TILE, D_MODEL), row), mod_spec],
        out_specs=pl.BlockSpec((MOVE_TILE, D_MODEL), row),
        out_shape=jax.ShapeDtypeStruct((t, D_MODEL), F32),
        scratch_shapes=[pltpu.VMEM((2, MOVE_TILE * SUB, LANES), U32), pltpu.SemaphoreType.DMA((2,))],
        compiler_params=_cparams(("arbitrary",)),
        name="combine",
    )(dest3, dest3, ys, x1, mod)


def _class_expert_tables():
    ea, eb = [], []
    for g in range(N_EXPERT_GROUPS):
        for a in range(EXPERTS_PER_GROUP):
            for b in range(a + 1, EXPERTS_PER_GROUP):
                ea.append(g * EXPERTS_PER_GROUP + a)
                eb.append(g * EXPERTS_PER_GROUP + b)
    return np.asarray(ea, np.int32), np.asarray(eb, np.int32)


def _moe_dispatched(rows, cls, x1, mod, wgu, wd, tiles_per_seq):
    t = x1.shape[0]
    slots, seg_end8 = _class_slots(cls, 1024)
    seg_end = seg_end8[0, 0:N_CLASSES].astype(jnp.int32)
    dest = slots[:, 0]
    dest3 = dest.reshape(t // MOVE_TILE, 1, MOVE_TILE)
    n_sorted = t + N_CLASSES * ROW_TILE
    tile_row0 = jnp.arange(n_sorted // ROW_TILE, dtype=jnp.int32) * ROW_TILE
    tile_cls = jnp.minimum(jnp.sum((seg_end[None, :] <= tile_row0[:, None]).astype(jnp.int32), axis=1),
                           N_CLASSES - 1)
    ea_np, eb_np = _class_expert_tables()
    in_cls = (tile_cls[:, None] == jnp.arange(N_CLASSES, dtype=jnp.int32)[None, :]).astype(jnp.int32)
    ea_t = jnp.sum(in_cls * jnp.asarray(ea_np % EXPERTS_PER_GROUP)[None, :], axis=1)
    eb_t = jnp.sum(in_cls * jnp.asarray(eb_np % EXPERTS_PER_GROUP)[None, :], axis=1)
    grp_t = tile_cls // N_PAIRS
    n_used = (seg_end[N_CLASSES - 1] // ROW_TILE).astype(jnp.int32).reshape(1)
    src = _invert(dest, n_sorted)[:, 0]
    ys = _experts(grp_t, ea_t, eb_t, n_used, src.reshape(n_sorted // ROW_TILE, 1, ROW_TILE), rows, wgu, wd)
    return _combine(dest3, ys, x1, mod, tiles_per_seq)


def _pad_lanes(v, fill):
    n = v.shape[-1]
    return jnp.concatenate([v, jnp.full((LANES - n,), fill, v.dtype)]).reshape(1, LANES)


def kernel(x_prompt, x_sample, c_prompt, c_sample, cache_k, cache_v, cache_kidx, page_table, w_ada, b_ada, norm_mix_g, norm_ffn_g, w_in, q_norm_g, k_norm_g, kidx_norm_g, gm_ln_g, gm_ln_b, gm_spatial_w, gm_spatial_b, w_proj_attn, w_proj_gmlp, w_out, w_router_group, b_router_group, w_router_expert, b_router_expert, w_expert_gate, w_expert_up, w_expert_down):
    depth = w_ada.shape[0]
    assert depth == 1
    l = 0
    bp, sp, _ = x_prompt.shape
    bs, ss, _ = x_sample.shape
    assert ss == 1
    n_pages = page_table.shape[1]
    past = n_pages * PAGE_SIZE
    tp = bp * sp

    w = w_in[l]
    zpad = jnp.zeros((D_MODEL, LANES - IDX_DIM - IDX_HEADS), F32)
    w_pad = jnp.concatenate([w[:, 0:1024], w[:, 1024:1088], w[:, 1088:1092], zpad, w[:, 1092:]], axis=1).astype(BF16)
    seg_np = (np.arange(LANES)[:, None] // HEAD_DIM) == (np.arange(LANES)[None, :] // HEAD_DIM)
    seg = jnp.asarray(seg_np, BF16)
    segki = jnp.asarray(seg_np & (np.arange(LANES)[:, None] < IDX_DIM) & (np.arange(LANES)[None, :] < IDX_DIM), BF16)
    consts = (norm_mix_g[l].reshape(1, D_MODEL),
              jnp.tile(q_norm_g[l], 2).reshape(1, LANES), jnp.tile(k_norm_g[l], 2).reshape(1, LANES),
              _pad_lanes(kidx_norm_g[l], 1.0),
              gm_ln_g[l].reshape(1, GM_WIDTH), gm_ln_b[l].reshape(1, GM_WIDTH), seg, segki)
    wpa = w_proj_attn[l].astype(BF16)
    wpg = w_proj_gmlp[l].astype(BF16)
    wo = w_out[l].astype(BF16)
    wr32 = jnp.concatenate([w_router_expert[l], w_router_group[l],
                            jnp.zeros((D_MODEL, LANES - N_EXPERTS - N_EXPERT_GROUPS), F32)], axis=1)
    wr_hi = wr32.astype(BF16)
    wr_lo = (wr32 - wr_hi.astype(F32)).astype(BF16)
    wr = jnp.concatenate([wr_hi, wr_lo], axis=1)
    br = _pad_lanes(jnp.concatenate([b_router_expert[l], b_router_group[l]]), 0.0)
    wgu = jnp.concatenate([w_expert_gate[l], w_expert_up[l]], axis=2).astype(BF16)
    wd = w_expert_down[l].astype(BF16)
    g2 = norm_ffn_g[l].reshape(1, D_MODEL)

    mod = _adaln(jnp.concatenate([c_prompt, c_sample], axis=0), w_ada[l], b_ada[l])
    mod_p = mod[0:bp].reshape(bp, 1, 6 * D_MODEL)
    mod_s = mod[bp:bp + bs]
    pos = jnp.concatenate([jnp.arange(sp, dtype=jnp.int32),
                           jnp.full((8,), past, jnp.int32)]).astype(F32).reshape(sp + 8, 1)
    tabs = _rope_tables(pos)
    tabs_p = tuple(t[0:sp] for t in tabs)
    tabs_s = tuple(t[sp:sp + 1] for t in tabs)

    tm = 512
    tps = sp // tm
    (q, kb, vb, qi, kib, kiwi, u, vg, ga, gb, kt_f, vt_f, kit_f, vt) = _project(
        x_prompt.reshape(tp, D_MODEL), mod_p, tabs_p, consts, w_pad, tm, tps, BF16)
    topk_p = min(TOPK_MAX, sp // 4)
    r3 = lambda a: a.reshape(bp, sp, a.shape[-1])
    attn_p = _prompt_attention(r3(qi), r3(kiwi), r3(kib), r3(q), r3(kb), vt, topk_p)
    bt = jnp.concatenate([gm_spatial_b[l].T, jnp.zeros((CHUNK, LANES - GM_GROUPS), F32)], axis=1)
    gm_p = _gmlp_prompt(u, vg, gm_spatial_w[l], bt, tm)
    x1_p, rows_p, cls_p = _merge(attn_p, gm_p, ga, gb, x_prompt.reshape(tp, D_MODEL), mod_p,
                                 wpa, wpg, wo, g2, wr, br, tm, tps, True)
    y_p = _moe_dispatched(rows_p, cls_p, x1_p, mod_p, wgu, wd, sp // MOVE_TILE)

    (q_s, kb_s, vb_s, qi_s, kib_s, kiwi_s, u_s, vg_s, ga_s, gb_s, kt_s, vt_s, kit_s, _) = _project(
        x_sample.reshape(bs, D_MODEL), mod_s, tabs_s, consts, w_pad, bs, 1, F32)
    qi8 = jnp.concatenate([qi_s.reshape(bs, IDX_HEADS, IDX_DIM),
                           jnp.zeros((bs, 8 - IDX_HEADS, IDX_DIM), BF16)], axis=1)
    wi8 = jnp.concatenate([kiwi_s[:, IDX_DIM:IDX_DIM + IDX_HEADS],
                           jnp.zeros((bs, 8 - IDX_HEADS), F32)], axis=1).reshape(bs, 8, 1)
    kidx_t = jnp.transpose(cache_kidx[l], (0, 2, 1))
    scores = _sample_scores(page_table, kidx_t, qi8, wi8, kib_s.reshape(bs, 1, IDX_DIM))
    topk_s = min(TOPK_MAX, (past + ss) // 4)
    bias = _sample_select(scores.reshape(bs, past + LANES), topk_s).reshape(bs, 1, past + LANES)
    kvw = N_KV_HEADS * HEAD_DIM
    k_t = jnp.transpose(cache_k[l], (0, 2, 3, 1)).reshape(-1, kvw, PAGE_SIZE)
    v_t = jnp.transpose(cache_v[l], (0, 2, 3, 1)).reshape(-1, kvw, PAGE_SIZE)
    q3 = q_s.reshape(bs, N_HEADS, HEAD_DIM)
    zq = jnp.zeros_like(q3)
    in_first = (jnp.arange(N_HEADS) < N_HEADS // N_KV_HEADS)[None, :, None]
    q2_s = jnp.where(in_first, jnp.concatenate([q3, zq], axis=2), jnp.concatenate([zq, q3], axis=2))
    attn_s = _sample_attention(page_table, k_t, v_t, q2_s, bias,
                               kb_s.reshape(bs, 1, kvw), vb_s.reshape(bs, 1, kvw))
    gd = GM_WIDTH // GM_GROUPS
    w0 = jnp.repeat(gm_spatial_w[l][:, 0, 0], gd).reshape(1, GM_WIDTH)
    b0 = jnp.repeat(gm_spatial_b[l][:, 0], gd).reshape(1, GM_WIDTH)
    gm_s = _gmlp_sample(u_s, vg_s, w0, b0)
    x1_s, h2_s, gate_s = _merge(attn_s.reshape(1, 1, bs, N_HEADS * HEAD_DIM), gm_s, ga_s, gb_s,
                                x_sample.reshape(bs, D_MODEL), mod_s, wpa, wpg, wo, g2, wr, br, bs, 1, False)
    y_s = _moe(h2_s, gate_s, x1_s, mod_s, wgu, wd, bs, 1)

    def rows_kv(a_t, n, s):
        return jnp.transpose(a_t.reshape(n, N_KV_HEADS, HEAD_DIM, s), (0, 3, 1, 2))[None]

    def rows_ki(a_t):
        return jnp.transpose(a_t, (0, 2, 1))[None]

    return (y_p.reshape(bp, sp, D_MODEL), y_s.reshape(bs, ss, D_MODEL),
            rows_kv(kt_f, bp, sp), rows_kv(vt_f, bp, sp), rows_ki(kit_f),
            rows_kv(kt_s, 1, bs).reshape(1, bs, ss, N_KV_HEADS, HEAD_DIM),
            rows_kv(vt_s, 1, bs).reshape(1, bs, ss, N_KV_HEADS, HEAD_DIM),
            rows_ki(kit_s).reshape(1, bs, ss, IDX_DIM), vg_s.reshape(1, bs, ss, GM_WIDTH))
```
